```python
import math, functools
import jax, jax.numpy as jnp
from jax import lax
import numpy as np

D_MODEL = 2048
BATCH = 1
SEQ = 16384
DEPTH = 1
DEC_BATCH = 32
DEC_SEQ = 4
PAST_LEN = 16384
PAGE_SIZE = 128

F32 = jnp.float32
EPS = 1e-6
HEAD_DIM = 128
ATTN_HEADS = D_MODEL // 2 // HEAD_DIM
KV_HEADS = 2
ATTN_WIDTH = ATTN_HEADS * HEAD_DIM
IDX_HEADS = 8
IDX_DIM = 64
TOPK_MAX = 256
Q_BLOCK = 128
ROPE_THETA = 10000.0
GDN_DK = 128
GDN_DV = 128
GDN_HEADS = D_MODEL // 2 // GDN_DV
GDN_WIDTH = GDN_HEADS * GDN_DV
GDN_CONV_CH = 2 * GDN_HEADS * GDN_DK + GDN_HEADS * GDN_DV
CONV_W = 4
CHUNK = 64
MIX_WIDTH = ATTN_WIDTH + GDN_WIDTH
PROJ_SPLITS = (ATTN_WIDTH, KV_HEADS * HEAD_DIM, KV_HEADS * HEAD_DIM, IDX_HEADS * IDX_DIM, IDX_DIM, IDX_HEADS,
               GDN_CONV_CH, GDN_WIDTH, GDN_HEADS, GDN_HEADS)
PROJ_DIM = sum(PROJ_SPLITS)
PEER_HEADS = 8
PEER_KEYS = 128
PEER_EXPERTS = PEER_KEYS * PEER_KEYS
PEER_QDIM = 128
PEER_TOPK = 16
PEER_BLOCK = 128

kernel_name = 'hymba_dsa_gdn_peer_step'


def rmsnorm(x, w):
    xf = x.astype(F32)
    y = xf * lax.rsqrt(jnp.mean(xf * xf, axis=-1, keepdims=True) + EPS)
    return (y * w.astype(F32)).astype(x.dtype)


def l2norm(x):
    xf = x.astype(F32)
    return xf * lax.rsqrt(jnp.sum(xf * xf, axis=-1, keepdims=True) + EPS)


def rope(x, pos):
    half = x.shape[-1] // 2
    inv = ROPE_THETA ** (-jnp.arange(half, dtype=F32) / half)
    ang = pos.astype(F32)[:, None] * inv[None, :]
    cos = jnp.cos(ang)[:, None, :]
    sin = jnp.sin(ang)[:, None, :]
    xf = x.astype(F32)
    x1, x2 = xf[..., :half], xf[..., half:]
    return jnp.concatenate([x1 * cos - x2 * sin, x2 * cos + x1 * sin], axis=-1).astype(x.dtype)


def split_projection(p):
    parts, start = [], 0
    for size in PROJ_SPLITS:
        parts.append(p[..., start:start + size])
        start += size
    return parts


def gather_rows(a, idx):
    return jax.vmap(lambda aa, ii: aa[ii])(a, idx)


def indexer_scores(qi, wi, ki):
    logits = jnp.einsum('nthd,nsd->nths', qi, ki).astype(F32) * IDX_DIM ** -0.5
    w = wi.astype(F32) * IDX_HEADS ** -0.5
    return jnp.einsum('nths,nth->nts', jax.nn.relu(logits), w)


def select_keys(scores, qpos, topk):
    kpos = jnp.arange(scores.shape[-1])
    admissible = kpos[None, None, :] <= qpos[None, :, None]
    _, idx = lax.top_k(jnp.where(admissible, scores, -jnp.inf), topk)
    valid = idx <= qpos[None, :, None]
    return idx, valid


def sparse_attend(q, k_sel, v_sel, valid):
    n, t = q.shape[:2]
    qg = q.reshape(n, t, KV_HEADS, ATTN_HEADS // KV_HEADS, HEAD_DIM)
    s = jnp.einsum('ntkgd,ntjkd->ntkgj', qg, k_sel).astype(F32) * HEAD_DIM ** -0.5
    s = jnp.where(valid[:, :, None, None, :], s, -jnp.inf)
    p = jax.nn.softmax(s, axis=-1).astype(v_sel.dtype)
    o = jnp.einsum('ntkgj,ntjkd->ntkgd', p, v_sel)
    return o.reshape(n, t, ATTN_WIDTH)


def prompt_attention(q, k, v, qi, ki, wi):
    b, s = q.shape[:2]
    topk = min(TOPK_MAX, s // 4)
    qb = min(Q_BLOCK, s)
    nblk = s // qb

    def blk(a):
        return jnp.moveaxis(a.reshape(b, nblk, qb, *a.shape[2:]), 1, 0)

    def one_block(args):
        q_b, qi_b, wi_b, qpos = args
        scores = indexer_scores(qi_b, wi_b, ki)
        idx, valid = select_keys(scores, qpos, topk)
        return sparse_attend(q_b, gather_rows(k, idx), gather_rows(v, idx), valid)

    qpos = jnp.arange(s).reshape(nblk, qb)
    out = lax.map(one_block, (blk(q), blk(qi), blk(wi), qpos))
    return jnp.moveaxis(out, 0, 1).reshape(b, s, ATTN_WIDTH)


def sample_attention(q, k, v, qi, ki, wi, cache_k, cache_v, cache_kidx, page_table, layer):
    nb, t = q.shape[:2]
    past = page_table.shape[1] * PAGE_SIZE
    topk = min(TOPK_MAX, (past + t) // 4)
    ki_past = cache_kidx[layer, page_table].reshape(nb, past, IDX_DIM)
    ki_all = jnp.concatenate([ki_past.astype(ki.dtype), ki], axis=1)
    scores = indexer_scores(qi, wi, ki_all)
    idx, valid = select_keys(scores, past + jnp.arange(t), topk)
    from_cache = (idx < past)[..., None, None]
    cidx = jnp.minimum(idx, past - 1)
    phys_page = page_table[jnp.arange(nb)[:, None, None], cidx // PAGE_SIZE]
    offs = cidx % PAGE_SIZE
    nidx = jnp.clip(idx - past, 0, t - 1)
    k_sel = jnp.where(from_cache, cache_k[layer, phys_page, offs].astype(k.dtype), gather_rows(k, nidx))
    v_sel = jnp.where(from_cache, cache_v[layer, phys_page, offs].astype(v.dtype), gather_rows(v, nidx))
    return sparse_attend(q, k_sel, v_sel, valid)


def short_conv(x, state, w):
    t = x.shape[1]
    xp = jnp.concatenate([state.astype(x.dtype), x], axis=1)
    y = sum(w[j] * xp[:, j:j + t] for j in range(CONV_W))
    return jax.nn.silu(y), xp[:, t:]


def chunk_gated_delta(q, k, v, g, beta, s0):
    n, t, h, dk = q.shape
    pad = (-t) % CHUNK

    def to_chunks(a):
        a = a.astype(F32)
        a = jnp.pad(a, [(0, 0), (0, pad)] + [(0, 0)] * (a.ndim - 2))
        a = jnp.moveaxis(a, 1, 2)
        return a.reshape(n, h, -1, CHUNK, *a.shape[3:])

    q, k, v, g, beta = (to_chunks(a) for a in (q, k, v, g, beta))
    q = q * dk ** -0.5
    kb = k * beta[..., None]
    vb = v * beta[..., None]
    gc = jnp.cumsum(g, axis=-1)
    causal = jnp.tril(jnp.ones((CHUNK, CHUNK), bool))
    strict = jnp.tril(jnp.ones((CHUNK, CHUNK), bool), -1)
    diff = gc[..., :, None] - gc[..., None, :]
    decay = jnp.where(causal, jnp.exp(jnp.where(causal, diff, 0.0)), 0.0)
    a_low = jnp.where(strict, jnp.einsum('nhcid,nhcjd->nhcij', kb, k) * decay, 0.0)
    eye = jnp.eye(CHUNK, dtype=F32)
    tinv = lax.linalg.triangular_solve(eye + a_low, jnp.broadcast_to(eye, a_low.shape),
                                       left_side=True, lower=True, unit_diagonal=True)
    u = jnp.einsum('nhcij,nhcjd->nhcid', tinv, vb)
    w = jnp.einsum('nhcij,nhcjd->nhcid', tinv, kb * jnp.exp(gc)[..., None])
    intra = jnp.where(causal, jnp.einsum('nhcid,nhcjd->nhcij', q, k) * decay, 0.0)

    def step(state, xs):
        qc, kc, uc, wc, gcc, ac = xs
        v_new = uc - jnp.einsum('nhcd,nhde->nhce', wc, state)
        out = (jnp.einsum('nhcd,nhde->nhce', qc * jnp.exp(gcc)[..., None], state)
               + jnp.einsum('nhij,nhje->nhie', ac, v_new))
        g_last = gcc[..., -1:]
        state = (state * jnp.exp(g_last)[..., None]
                 + jnp.einsum('nhcd,nhce->nhde', kc * jnp.exp(g_last - gcc)[..., None], v_new))
        return state, out

    xs = tuple(jnp.moveaxis(a, 2, 0) for a in (q, k, u, w, gc, intra))
    s_final, out = lax.scan(step, s0.astype(F32), xs)
    out = jnp.moveaxis(out, 0, 2).reshape(n, h, -1, v.shape[-1])
    return jnp.moveaxis(out, 1, 2)[:, :t], s_final


def gdn_branch(qkv, z, b, a, conv_state, ssm_state, conv_w, a_log, dt_bias, gdn_norm_w):
    n, t, _ = qkv.shape
    qkv, new_conv = short_conv(qkv, conv_state, conv_w)
    q = l2norm(qkv[..., :GDN_HEADS * GDN_DK].reshape(n, t, GDN_HEADS, GDN_DK))
    k = l2norm(qkv[..., GDN_HEADS * GDN_DK:2 * GDN_HEADS * GDN_DK].reshape(n, t, GDN_HEADS, GDN_DK))
    v = qkv[..., 2 * GDN_HEADS * GDN_DK:].reshape(n, t, GDN_HEADS, GDN_DV)
    beta = jax.nn.sigmoid(b.astype(F32))
    g = -jnp.exp(a_log.astype(F32)) * jax.nn.softplus(a.astype(F32) + dt_bias.astype(F32))
    o, new_ssm = chunk_gated_delta(q, k, v, g, beta, ssm_state)
    o = rmsnorm(o.astype(z.dtype), gdn_norm_w) * jax.nn.silu(z.reshape(n, t, GDN_HEADS, GDN_DV))
    return o.reshape(n, t, GDN_WIDTH), new_conv, new_ssm.astype(ssm_state.dtype)


def peer_ffn(x, wq, sub_keys, w_u, w_v):
    m, d = x.shape
    mp = -(-m // PEER_BLOCK) * PEER_BLOCK
    xp = jnp.pad(x, ((0, mp - m), (0, 0))).reshape(mp // PEER_BLOCK, PEER_BLOCK, d)

    def block(xb):
        qh = (xb @ wq).reshape(PEER_BLOCK, PEER_HEADS, 2, PEER_QDIM // 2)
        s = jnp.einsum('mhcd,hcnd->mhcn', qh, sub_keys).astype(F32)
        top_s, top_i = lax.top_k(s, PEER_TOPK)
        cand = top_s[:, :, 0, :, None] + top_s[:, :, 1, None, :]
        best_s, best_c = lax.top_k(cand.reshape(PEER_BLOCK, PEER_HEADS, PEER_TOPK * PEER_TOPK), PEER_TOPK)
        i1 = jnp.take_along_axis(top_i[:, :, 0], best_c // PEER_TOPK, axis=-1)
        i2 = jnp.take_along_axis(top_i[:, :, 1], best_c % PEER_TOPK, axis=-1)
        expert = i1 * PEER_KEYS + i2
        gate = jax.nn.softmax(best_s, axis=-1)
        act = jax.nn.gelu(jnp.einsum('md,mhkd->mhk', xb, w_u[expert]).astype(F32))
        return jnp.einsum('mhk,mhkd->md', (gate * act).astype(xb.dtype), w_v[expert])

    return lax.map(block, xp).reshape(mp, d)[:m]


def decoder_layer(h, pos, attend, conv_state, ssm_state, lw):
    (norm_mix_w, w_in, conv_w, a_log, dt_bias, gdn_norm_w, w_out, norm_ffn_w,
     peer_wq, peer_sub_keys, peer_u, peer_v) = lw
    n, t, d = h.shape
    hn = rmsnorm(h, norm_mix_w)
    q, k, v, qi, ki, wi, qkv_g, z_g, b_g, a_g = split_projection(hn @ w_in)
    q = rope(q.reshape(n, t, ATTN_HEADS, HEAD_DIM), pos)
    k = rope(k.reshape(n, t, KV_HEADS, HEAD_DIM), pos)
    v = v.reshape(n, t, KV_HEADS, HEAD_DIM)
    qi = rope(qi.reshape(n, t, IDX_HEADS, IDX_DIM), pos)
    ki = rope(ki.reshape(n, t, 1, IDX_DIM), pos)[:, :, 0]
    y_attn = attend(q, k, v, qi, ki, wi)
    y_gdn, new_conv, new_ssm = gdn_branch(qkv_g, z_g, b_g, a_g, conv_state, ssm_state,
                                          conv_w, a_log, dt_bias, gdn_norm_w)
    h = h + jnp.concatenate([y_attn, y_gdn], axis=-1) @ w_out
    f = peer_ffn(rmsnorm(h, norm_ffn_w).reshape(n * t, d), peer_wq, peer_sub_keys, peer_u, peer_v)
    h = h + f.reshape(n, t, d)
    return h, (k, v, ki, new_conv, new_ssm)


def setup_inputs(seed: int = 0) -> dict:
    key = jax.random.key(seed)
    ks = iter(jax.random.split(key, 32))

    def normal(shape, scale):
        return scale * jax.random.normal(next(ks), shape, F32)

    n_pages = PAST_LEN // PAGE_SIZE
    n_used = DEC_BATCH * n_pages
    n_pool = n_used + max(1, n_used // 4)
    perm = jax.random.permutation(next(ks), n_pool)
    page_table = perm[:n_used].reshape(DEC_BATCH, n_pages).astype(jnp.int32)
    dt = jnp.exp(jax.random.uniform(next(ks), (DEPTH, GDN_HEADS), F32, math.log(1e-3), math.log(1e-1)))
    dt_bias = dt + jnp.log(-jnp.expm1(-dt))
    a_log = jnp.log(jax.random.uniform(next(ks), (DEPTH, GDN_HEADS), F32, 1.0, 16.0))
    return {
        'x_prompt': normal((BATCH, SEQ, D_MODEL), 1.0),
        'x_sample': normal((DEC_BATCH, DEC_SEQ, D_MODEL), 1.0),
        'cache_k': normal((DEPTH, n_pool, PAGE_SIZE, KV_HEADS, HEAD_DIM), 1.0),
        'cache_v': normal((DEPTH, n_pool, PAGE_SIZE, KV_HEADS, HEAD_DIM), 1.0),
        'cache_kidx': normal((DEPTH, n_pool, PAGE_SIZE, IDX_DIM), 1.0),
        'page_table': page_table,
        'state_conv': normal((DEPTH, DEC_BATCH, CONV_W - 1, GDN_CONV_CH), 1.0),
        'state_ssm': normal((DEPTH, DEC_BATCH, GDN_HEADS, GDN_DK, GDN_DV), 0.1),
        'norm_mix_w': 1.0 + normal((DEPTH, D_MODEL), 0.01),
        'w_in': normal((DEPTH, D_MODEL, PROJ_DIM), D_MODEL ** -0.5),
        'conv_w': normal((DEPTH, CONV_W, GDN_CONV_CH), CONV_W ** -0.5),
        'a_log': a_log,
        'dt_bias': dt_bias,
        'gdn_norm_w': 1.0 + normal((DEPTH, GDN_DV), 0.01),
        'w_out': normal((DEPTH, MIX_WIDTH, D_MODEL), MIX_WIDTH ** -0.5),
        'norm_ffn_w': 1.0 + normal((DEPTH, D_MODEL), 0.01),
        'peer_wq': normal((DEPTH, D_MODEL, PEER_HEADS * PEER_QDIM), D_MODEL ** -0.5),
        'peer_sub_keys': normal((DEPTH, PEER_HEADS, 2, PEER_KEYS, PEER_QDIM // 2), (PEER_QDIM // 2) ** -0.5),
        'peer_u': normal((DEPTH, PEER_EXPERTS, D_MODEL), D_MODEL ** -0.5),
        'peer_v': normal((DEPTH, PEER_EXPERTS, D_MODEL), PEER_HEADS ** -0.5),
        'norm_final_w': 1.0 + normal((D_MODEL,), 0.01),
    }


def reference(x_prompt, x_sample, cache_k, cache_v, cache_kidx, page_table, state_conv, state_ssm,
              norm_mix_w, w_in, conv_w, a_log, dt_bias, gdn_norm_w, w_out, norm_ffn_w,
              peer_wq, peer_sub_keys, peer_u, peer_v, norm_final_w):
    b, s, _ = x_prompt.shape
    past = page_table.shape[1] * PAGE_SIZE
    t = x_sample.shape[1]
    pos_prompt = jnp.arange(s)
    pos_sample = past + jnp.arange(t)
    hp, hs = x_prompt, x_sample
    new_prompt = [[], [], [], [], []]
    new_sample = [[], [], [], [], []]
    for l in range(DEPTH):
        lw = (norm_mix_w[l], w_in[l], conv_w[l], a_log[l], dt_bias[l], gdn_norm_w[l], w_out[l],
              norm_ffn_w[l], peer_wq[l], peer_sub_keys[l], peer_u[l], peer_v[l])
        conv0 = jnp.zeros((b, CONV_W - 1, GDN_CONV_CH), x_prompt.dtype)
        ssm0 = jnp.zeros((b, GDN_HEADS, GDN_DK, GDN_DV), state_ssm.dtype)
        hp, st_p = decoder_layer(hp, pos_prompt, prompt_attention, conv0, ssm0, lw)
        attend_sample = functools.partial(sample_attention, cache_k=cache_k, cache_v=cache_v,
                                          cache_kidx=cache_kidx, page_table=page_table, layer=l)
        hs, st_s = decoder_layer(hs, pos_sample, attend_sample, state_conv[l], state_ssm[l], lw)
        for lst, a in zip(new_prompt, st_p):
            lst.append(a)
        for lst, a in zip(new_sample, st_s):
            lst.append(a)
    y_prompt = rmsnorm(hp, norm_final_w)
    y_sample = rmsnorm(hs, norm_final_w)
    k_prompt, v_prompt, kidx_prompt, conv_prompt, ssm_prompt = (jnp.stack(a) for a in new_prompt)
    k_sample, v_sample, kidx_sample, conv_sample, ssm_sample = (jnp.stack(a) for a in new_sample)
    return (y_prompt, y_sample, k_prompt, v_prompt, kidx_prompt, conv_prompt, ssm_prompt,
            k_sample, v_sample, kidx_sample, conv_sample, ssm_sample)
```

```python
import functools
import math

import jax
import jax.numpy as jnp
from jax import lax
from jax.experimental import pallas as pl
from jax.experimental.pallas import tpu as pltpu

F32 = jnp.float32
BF16 = jnp.bfloat16
I32 = jnp.int32
EPS = 1e-6
NEG_INF = float("-inf")

HEAD_DIM = 128
KV_HEADS = 2
IDX_HEADS = 8
IDX_DIM = 64
TOPK_MAX = 256
ROPE_THETA = 10000.0
PAGE_SIZE = 128
GDN_DK = 128
GDN_DV = 128
CONV_W = 4
GDN_CHUNK = 64
PEER_HEADS = 8
PEER_KEYS = 128
PEER_QDIM = 128
PEER_TOPK = 16

VMEM_LIMIT_BYTES = 56 * 1024 * 1024


def _cparams(*sem):
    return pltpu.CompilerParams(dimension_semantics=sem, vmem_limit_bytes=VMEM_LIMIT_BYTES)


def _pack_w_in(w_in):
    sizes = (1024, 256, 256, 512, 64, 8, 3072, 1024, 8, 8)
    offs = [0]
    for sz in sizes:
        offs.append(offs[-1] + sz)
    part = lambda i: w_in[:, offs[i]:offs[i + 1]]
    q, k, v, qi, ki, wi, qkv, z, b, a = (part(i) for i in range(10))
    pad = jnp.zeros((w_in.shape[0], 128 - 64 - 24), w_in.dtype)
    return jnp.concatenate([q, qi, k, v, qkv, z, ki, wi, b, a, pad], axis=1).astype(BF16)


def _rope_tables(pos):
    def table(dim):
        half = dim // 2
        inv = ROPE_THETA ** (-jnp.arange(half, dtype=F32) / half)
        ang = pos.astype(F32)[:, None] * inv[None, :]
        cos = jnp.tile(jnp.cos(ang), (1, 128 // half))
        sin = jnp.tile(jnp.concatenate([-jnp.sin(ang), jnp.sin(ang)], axis=1), (1, 128 // dim))
        return cos, sin
    cq, sq = table(HEAD_DIM)
    ci, si = table(IDX_DIM)
    return cq, sq, ci, si


def _norm_matmul_kernel(x_ref, nw_ref, w_ref, o_ref):
    x = x_ref[...]
    y = x * lax.rsqrt(jnp.mean(x * x, axis=-1, keepdims=True) + EPS)
    xn = (y * nw_ref[...]).astype(BF16)
    o_ref[...] = jnp.dot(xn, w_ref[...], preferred_element_type=F32)


def norm_matmul(x, nw, w, *, tm, tn):
    m, d = x.shape
    n = w.shape[1]
    assert m % tm == 0 and n % tn == 0
    return pl.pallas_call(
        _norm_matmul_kernel,
        grid=(n // tn, m // tm),
        in_specs=[
            pl.BlockSpec((tm, d), lambda j, i: (i, 0)),
            pl.BlockSpec((1, d), lambda j, i: (0, 0)),
            pl.BlockSpec((d, tn), lambda j, i: (0, j)),
        ],
        out_specs=pl.BlockSpec((tm, tn), lambda j, i: (i, j)),
        out_shape=jax.ShapeDtypeStruct((m, n), F32),
        compiler_params=_cparams("arbitrary", "arbitrary"),
        name="norm_matmul",
    )(x, nw, w)


def _rope128(x, cos, sin_signed):
    return x * cos + pltpu.roll(x, 64, 1) * sin_signed


def _rope64(x, cos, sin_signed, first_half):
    partner = jnp.where(first_half, pltpu.roll(x, 96, 1), pltpu.roll(x, 32, 1))
    return x * cos + partner * sin_signed


def _rope_split_kernel(q_ref, qi_ref, k_ref, v_ref, aux_ref, cq_ref, sq_ref, ci_ref, si_ref,
                       qb_ref, qib_ref, k32_ref, kb_ref, v32_ref, vb_ref, ki32_ref, kib_ref, auxr_ref):
    cq, sq, ci, si = cq_ref[...], sq_ref[...], ci_ref[...], si_ref[...]
    lane = lax.broadcasted_iota(I32, ci.shape, 1)
    first_half = (lane % IDX_DIM) < (IDX_DIM // 2)
    for h in range(q_ref.shape[1] // HEAD_DIM):
        sl = slice(h * HEAD_DIM, (h + 1) * HEAD_DIM)
        qb_ref[:, sl] = _rope128(q_ref[:, sl], cq, sq).astype(BF16)
    for h in range(k_ref.shape[1] // HEAD_DIM):
        sl = slice(h * HEAD_DIM, (h + 1) * HEAD_DIM)
        kr = _rope128(k_ref[:, sl], cq, sq)
        k32_ref[:, sl] = kr
        kb_ref[:, sl] = kr.astype(BF16)
    for h in range(qi_ref.shape[1] // 128):
        sl = slice(h * 128, (h + 1) * 128)
        qib_ref[:, sl] = _rope64(qi_ref[:, sl], ci, si, first_half).astype(BF16)
    v = v_ref[...]
    v32_ref[...] = v
    vb_ref[...] = v.astype(BF16)
    aux = aux_ref[...]
    kir = _rope64(aux, ci, si, first_half)[:, :IDX_DIM]
    ki32_ref[...] = kir
    kib_ref[...] = kir.astype(BF16)
    auxr_ref[...] = aux[:, IDX_DIM:]


def rope_split(p, cq, sq, ci, si, *, tm):
    m = p.shape[0]
    assert m % tm == 0
    row = lambda w, j: pl.BlockSpec((tm, w), lambda i, j=j: (i, j))
    outs = [
        ((m, 1024), BF16), ((m, 512), BF16), ((m, 256), F32), ((m, 256), BF16), ((m, 256), F32),
        ((m, 256), BF16), ((m, IDX_DIM), F32), ((m, IDX_DIM), BF16), ((m, 128 - IDX_DIM), F32),
    ]
    return pl.pallas_call(
        _rope_split_kernel,
        grid=(m // tm,),
        in_specs=[row(1024, 0), row(512, 2), row(256, 6), row(256, 7), row(128, 48),
                  row(128, 0), row(128, 0), row(128, 0), row(128, 0)],
        out_specs=[pl.BlockSpec((tm, s[1]), lambda i: (i, 0)) for s, _ in outs],
        out_shape=[jax.ShapeDtypeStruct(s, dt) for s, dt in outs],
        compiler_params=_cparams("arbitrary"),
        name="rope_split",
    )(p, p, p, p, p, cq, sq, ci, si)


def _out_proj_kernel(ya_ref, yg_ref, res_ref, wa_ref, wg_ref, nw_ref, wq_ref, h_ref, xn_ref, qp_ref):
    h = res_ref[...] + jnp.dot(ya_ref[...], wa_ref[...], preferred_element_type=F32)
    h = h + jnp.dot(yg_ref[...], wg_ref[...], preferred_element_type=F32)
    h_ref[...] = h
    y = h * lax.rsqrt(jnp.mean(h * h, axis=-1, keepdims=True) + EPS)
    xn = (y * nw_ref[...]).astype(BF16)
    xn_ref[...] = xn
    qp_ref[...] = jnp.dot(xn, wq_ref[...], preferred_element_type=F32)


def out_proj(ya, yg, res, w_out, nw, wq, *, tm):
    m, d = res.shape
    half = ya.shape[1]
    nq = wq.shape[1]
    assert m % tm == 0
    rows = lambda wd: pl.BlockSpec((tm, wd), lambda i: (i, 0))
    const = lambda a: pl.BlockSpec(a.shape, lambda i: (0,) * a.ndim, pipeline_mode=pl.Buffered(1))
    wa, wg = w_out[:half], w_out[half:]
    return pl.pallas_call(
        _out_proj_kernel,
        grid=(m // tm,),
        in_specs=[rows(half), rows(half), rows(d), const(wa), const(wg), const(nw), const(wq)],
        out_specs=[rows(d), rows(d), rows(nq)],
        out_shape=[jax.ShapeDtypeStruct((m, d), F32), jax.ShapeDtypeStruct((m, d), BF16),
                   jax.ShapeDtypeStruct((m, nq), F32)],
        compiler_params=_cparams("arbitrary"),
        name="out_proj",
    )(ya, yg, res, wa, wg, nw, wq)


def _take_top(s, codes, count):
    big = jnp.int32(2 ** 30)
    vals, picks = [], []
    for _ in range(count):
        m = jnp.max(s, axis=0, keepdims=True)
        pick = jnp.min(jnp.where(s == m, codes, big), axis=0, keepdims=True)
        s = jnp.where(codes == pick, NEG_INF, s)
        vals.append(m)
        picks.append(pick)
    return jnp.concatenate(vals, axis=0), jnp.concatenate(picks, axis=0)


def _lookup(table, sel, count):
    out = jnp.zeros(sel.shape, table.dtype)
    for a in range(count):
        out = jnp.where(sel == a, jnp.broadcast_to(table[a:a + 1, :], sel.shape), out)
    return out


def _peer_topk_kernel(q_ref, sk_ref, i1_ref, i2_ref, g_ref):
    tm = q_ref.shape[0]
    kk = PEER_TOPK
    half = PEER_QDIM // 2
    key_codes = lax.broadcasted_iota(I32, (PEER_KEYS, tm), 0)
    top_v, top_i = [], []
    for c in range(2):
        qs = q_ref[:, c * half:(c + 1) * half].astype(BF16)
        s = lax.dot_general(sk_ref[c], qs, (((1,), (1,)), ((), ())), preferred_element_type=F32)
        vals, idx = _take_top(s, key_codes, kk)
        top_v.append(vals)
        top_i.append(idx)
    pieces, codes = [], []
    for a in range(kk):
        nb = kk // (a + 1)
        rows = -(-nb // 8) * 8
        r = lax.broadcasted_iota(I32, (rows, tm), 0)
        cand = top_v[0][a:a + 1, :] + top_v[1][0:rows, :]
        pieces.append(jnp.where(r < nb, cand, NEG_INF))
        codes.append(r + a * kk)
    best_s, best_c = _take_top(jnp.concatenate(pieces, axis=0), jnp.concatenate(codes, axis=0), kk)
    i1_ref[0] = _lookup(top_i[0], lax.shift_right_logical(best_c, 4), kk)
    i2_ref[0] = _lookup(top_i[1], jnp.bitwise_and(best_c, kk - 1), kk)
    e = jnp.exp(best_s - best_s[0:1, :])
    g_ref[0] = e / jnp.sum(e, axis=0, keepdims=True)


def peer_topk(qp, sub_keys, *, tm):
    m = qp.shape[0]
    heads = sub_keys.shape[0]
    assert m % tm == 0 and PEER_TOPK == 16
    sk = sub_keys.reshape(heads * 2, PEER_KEYS, PEER_QDIM // 2).astype(BF16)
    out = pl.BlockSpec((1, PEER_TOPK, tm), lambda i, h: (h, 0, i))
    return pl.pallas_call(
        _peer_topk_kernel,
        grid=(m // tm, heads),
        in_specs=[pl.BlockSpec((tm, PEER_QDIM), lambda i, h: (i, h)),
                  pl.BlockSpec((2, PEER_KEYS, PEER_QDIM // 2), lambda i, h: (h, 0, 0))],
        out_specs=[out, out, out],
        out_shape=[jax.ShapeDtypeStruct((heads, PEER_TOPK, m), I32), jax.ShapeDtypeStruct((heads, PEER_TOPK, m), I32),
                   jax.ShapeDtypeStruct((heads, PEER_TOPK, m), F32)],
        compiler_params=_cparams("arbitrary", "arbitrary"),
        name="peer_topk",
    )(qp, sk)


def _peer_gate_kernel(i1_ref, i2_ref, g_ref, o_ref, tmp_ref):
    tmb = i1_ref.shape[0]
    nk = PEER_KEYS
    sub = lax.broadcasted_iota(I32, (nk, i1_ref.shape[1]), 0)

    def one_token(t, _):
        wide = lambda ref: jnp.broadcast_to(ref[pl.ds(t, 1), :], sub.shape)
        p1 = jnp.where(wide(i1_ref) == sub, wide(g_ref), 0.0).astype(BF16)
        p2 = jnp.where(wide(i2_ref) == sub, 1.0, 0.0).astype(BF16)
        gm = lax.dot_general(p1, p2, (((1,), (1,)), ((), ())), preferred_element_type=F32)
        tmp_ref[pl.ds(pl.multiple_of(t * nk, nk), nk), :] = gm
        return 0

    lax.fori_loop(0, tmb, one_token, 0)
    for i1 in range(nk):
        o_ref[i1] = tmp_ref[pl.ds(i1, tmb, stride=nk), :].astype(o_ref.dtype)


def peer_gates(i1, i2, gate, *, tmb):
    m, slots = i1.shape
    assert m % tmb == 0
    rows = pl.BlockSpec((tmb, slots), lambda i: (i, 0))
    return pl.pallas_call(
        _peer_gate_kernel,
        grid=(m // tmb,),
        in_specs=[rows, rows, rows],
        out_specs=pl.BlockSpec((PEER_KEYS, tmb, PEER_KEYS), lambda i: (0, i, 0)),
        out_shape=jax.ShapeDtypeStruct((PEER_KEYS, m, PEER_KEYS), BF16),
        scratch_shapes=[pltpu.VMEM((tmb * PEER_KEYS, PEER_KEYS), F32)],
        compiler_params=_cparams("arbitrary"),
        name="peer_gates",
    )(i1, i2, gate)


def _gelu_tanh(x):
    return 0.5 * x * (1.0 + jnp.tanh(math.sqrt(2.0 / math.pi) * (x + 0.044715 * (x * x * x))))


def _peer_dense_kernel(xn_ref, wu_ref, wv_ref, g_ref, h_ref, fw_ref, o_ref, acc_ref):
    j = pl.program_id(1)
    ib = g_ref.shape[0]

    @pl.when(j == 0)
    def _():
        acc_ref[...] = jnp.zeros(acc_ref.shape, F32)

    a = lax.dot_general(xn_ref[...], wu_ref[...], (((1,), (1,)), ((), ())), preferred_element_type=F32)
    act = _gelu_tanh(a)
    hm = jnp.concatenate(
        [(g_ref[u].astype(F32) * act[:, u * PEER_KEYS:(u + 1) * PEER_KEYS]).astype(BF16) for u in range(ib)], axis=1)
    acc_ref[...] += jnp.dot(hm, wv_ref[...], preferred_element_type=F32)

    @pl.when(j == pl.num_programs(1) - 1)
    def _():
        y = h_ref[...] + acc_ref[...]
        o_ref[...] = y * lax.rsqrt(jnp.mean(y * y, axis=-1, keepdims=True) + EPS) * fw_ref[...]


def peer_dense(xn, w_u, w_v, gates, h, fw, *, tm, ib):
    m, d = h.shape
    assert m % tm == 0 and PEER_KEYS % ib == 0
    eb = ib * PEER_KEYS
    return pl.pallas_call(
        _peer_dense_kernel,
        grid=(m // tm, PEER_KEYS // ib),
        in_specs=[
            pl.BlockSpec((tm, d), lambda i, j: (i, 0)),
            pl.BlockSpec((eb, d), lambda i, j: (j, 0)),
            pl.BlockSpec((eb, d), lambda i, j: (j, 0)),
            pl.BlockSpec((ib, tm, PEER_KEYS), lambda i, j: (j, i, 0)),
            pl.BlockSpec((tm, d), lambda i, j: (i, 0)),
            pl.BlockSpec((1, d), lambda i, j: (0, 0)),
        ],
        out_specs=pl.BlockSpec((tm, d), lambda i, j: (i, 0)),
        out_shape=jax.ShapeDtypeStruct((m, d), F32),
        scratch_shapes=[pltpu.VMEM((tm, d), F32)],
        compiler_params=_cparams("arbitrary", "arbitrary"),
        name="peer_dense",
    )(xn, w_u, w_v, gates, h, fw)


_HI = lax.Precision.HIGHEST


def _dot_hi(a, b):
    return jnp.dot(a, b, preferred_element_type=F32, precision=_HI)


def _dot_nt_hi(a, b):
    return lax.dot_general(a, b, (((1,), (1,)), ((), ())), preferred_element_type=F32, precision=_HI)


def _dot_tn_hi(a, b):
    return lax.dot_general(a, b, (((0,), (0,)), ((), ())), preferred_element_type=F32, precision=_HI)


def _sigmoid(x):
    return 1.0 / (1.0 + jnp.exp(-x))


def _softplus(x):
    return jnp.maximum(x, 0.0) + jnp.log1p(jnp.exp(-jnp.abs(x)))


def _gdn_kernel(xq_ref, xk_ref, xv_ref, z_ref, aux_ref, at_ref, cs_ref, cw_ref, alog_ref, dtb_ref,
                alogt_ref, dtbt_ref, nw_ref, s0_ref, y_ref, nconv_ref, nssm_ref, xp_ref, st_ref,
                *, chunk, t_total, b_lane, a_lane):
    c = pl.program_id(1)
    nchunks = pl.num_programs(1)
    heads = st_ref.shape[0]
    width = heads * GDN_DK
    tail = CONV_W - 1

    @pl.when(c == 0)
    def _():
        xp_ref[8 - tail:8, :] = cs_ref[0]
        st_ref[...] = s0_ref[0]

    xp_ref[8:8 + chunk, 0:width] = xq_ref[...]
    xp_ref[8:8 + chunk, width:2 * width] = xk_ref[...]
    xp_ref[8:8 + chunk, 2 * width:3 * width] = xv_ref[...]
    conv = cw_ref[tail:tail + 1, :] * xp_ref[8:8 + chunk, :]
    for j in range(tail):
        conv = conv + cw_ref[j:j + 1, :] * xp_ref[8 - tail + j:8 - tail + j + chunk, :]
    conv = conv * _sigmoid(conv)

    padded = t_total % chunk != 0
    ridx = c * chunk + lax.broadcasted_iota(I32, (chunk, 1), 0)
    rvalid = ridx < t_total
    cidx = c * chunk + lax.broadcasted_iota(I32, (1, chunk), 1)
    cvalid = cidx < t_total

    aux = aux_ref[...]
    beta = _sigmoid(aux[:, b_lane:b_lane + heads])
    g = -jnp.exp(alog_ref[...]) * _softplus(aux[:, a_lane:a_lane + heads] + dtb_ref[...])
    gt = -jnp.exp(alogt_ref[...]) * _softplus(at_ref[0] + dtbt_ref[...])
    if padded:
        beta = jnp.where(rvalid, beta, 0.0)
        g = jnp.where(rvalid, g, 0.0)
        gt = jnp.where(cvalid, gt, 0.0)
    ri = lax.broadcasted_iota(I32, (chunk, chunk), 0)
    ci = lax.broadcasted_iota(I32, (chunk, chunk), 1)
    causal = ri >= ci
    strict = ri > ci
    eye = jnp.where(ri == ci, 1.0, 0.0)
    gc = _dot_hi(jnp.where(causal, 1.0, 0.0), g)
    gct = _dot_hi(gt, jnp.where(ri <= ci, 1.0, 0.0))

    for h in range(heads):
        sl = slice(h * GDN_DK, (h + 1) * GDN_DK)
        qh = conv[:, sl]
        kh = conv[:, width + h * GDN_DK:width + (h + 1) * GDN_DK]
        vh = conv[:, 2 * width + h * GDN_DV:2 * width + (h + 1) * GDN_DV]
        qh = qh * lax.rsqrt(jnp.sum(qh * qh, axis=-1, keepdims=True) + EPS) * GDN_DK ** -0.5
        kh = kh * lax.rsqrt(jnp.sum(kh * kh, axis=-1, keepdims=True) + EPS)
        if padded:
            kh = jnp.where(rvalid, kh, 0.0)
            vh = jnp.where(rvalid, vh, 0.0)
        gcol = gc[:, h:h + 1]
        grow = gct[h:h + 1, :]
        bcol = beta[:, h:h + 1]
        decay = jnp.where(causal, jnp.exp(jnp.where(causal, gcol - grow, 0.0)), 0.0)
        kb = kh * bcol
        vb = vh * bcol
        x = -jnp.where(strict, _dot_nt_hi(kb, kh) * decay, 0.0)
        tinv = eye + x
        span = 2
        while span < chunk:
            x = _dot_hi(x, x)
            tinv = tinv + _dot_hi(tinv, x)
            span *= 2
        u = _dot_hi(tinv, vb)
        w = _dot_hi(tinv, kb * jnp.exp(gcol))
        intra = jnp.where(causal, _dot_nt_hi(qh, kh) * decay, 0.0)
        state = st_ref[h]
        v_new = u - _dot_hi(w, state)
        out = _dot_hi(qh * jnp.exp(gcol), state) + _dot_hi(intra, v_new)
        glast = gc[chunk - 1:chunk, h:h + 1]
        st_ref[h] = state * jnp.exp(glast) + _dot_tn_hi(kh * jnp.exp(glast - gcol), v_new)
        zh = z_ref[:, sl]
        on = out * lax.rsqrt(jnp.mean(out * out, axis=-1, keepdims=True) + EPS) * nw_ref[...]
        y_ref[:, sl] = (on * (zh * _sigmoid(zh))).astype(y_ref.dtype)

    last_valid = t_total - (t_total - 1) // chunk * chunk
    @pl.when(c < nchunks - 1)
    def _():
        xp_ref[8 - tail:8, :] = xp_ref[8 + chunk - tail:8 + chunk, :]

    @pl.when(c == nchunks - 1)
    def _():
        nconv_ref[0] = xp_ref[8 + last_valid - tail:8 + last_valid, :]
        nssm_ref[0] = st_ref[...]


def gdn_mixer(p, at, conv_state, conv_w, a_log, dt_bias, norm_w, ssm_state, *, n_seq, t_pad, t_total, chunk):
    heads = ssm_state.shape[1]
    width = heads * GDN_DK
    assert t_pad % chunk == 0 and chunk % 8 == 0
    nch = t_pad // chunk
    rows = lambda wd, j: pl.BlockSpec((chunk, wd), lambda n, c, j=j: (n * nch + c, j))
    const = lambda a: pl.BlockSpec(a.shape, lambda n, c: (0,) * a.ndim)
    alog = a_log.reshape(1, heads)
    dtb = dt_bias.reshape(1, heads)
    nw = norm_w.reshape(1, GDN_DV)
    at = at.reshape(n_seq, heads, nch, chunk).transpose(0, 2, 1, 3).reshape(n_seq * nch, heads, chunk)
    kern = functools.partial(_gdn_kernel, chunk=chunk, t_total=t_total, b_lane=IDX_DIM + 8, a_lane=IDX_DIM + 16)
    return pl.pallas_call(
        kern,
        grid=(n_seq, nch),
        in_specs=[
            rows(width, 2), rows(width, 3), rows(width, 4), rows(width, 5), rows(128, 48),
            pl.BlockSpec((1, heads, chunk), lambda n, c: (n * nch + c, 0, 0)),
            pl.BlockSpec((1, CONV_W - 1, 3 * width), lambda n, c: (n, 0, 0)),
            const(conv_w), const(alog), const(dtb), const(alog.T), const(dtb.T), const(nw),
            pl.BlockSpec((1, heads, GDN_DK, GDN_DV), lambda n, c: (n, 0, 0, 0)),
        ],
        out_specs=[
            pl.BlockSpec((chunk, width), lambda n, c: (n * nch + c, 0)),
            pl.BlockSpec((1, CONV_W - 1, 3 * width), lambda n, c: (n, 0, 0)),
            pl.BlockSpec((1, heads, GDN_DK, GDN_DV), lambda n, c: (n, 0, 0, 0)),
        ],
        out_shape=[
            jax.ShapeDtypeStruct((n_seq * t_pad, width), BF16),
            jax.ShapeDtypeStruct((n_seq, CONV_W - 1, 3 * width), F32),
            jax.ShapeDtypeStruct((n_seq, heads, GDN_DK, GDN_DV), F32),
        ],
        scratch_shapes=[pltpu.VMEM((8 + chunk, 3 * width), F32), pltpu.VMEM((heads, GDN_DK, GDN_DV), F32)],
        compiler_params=_cparams("arbitrary", "arbitrary"),
        name="gdn_mixer",
    )(p, p, p, p, p, at, conv_state, conv_w, alog, dtb, alog.T, dtb.T, nw, ssm_state)


def _lane_fold(x, op):
    out = x[:, :128]
    for j in range(1, x.shape[1] // 128):
        out = op(out, x[:, j * 128:(j + 1) * 128])
    return out


def _select_threshold(load_chunk, nch, ts, rows, topk, lo0, hi0, skip):
    kf = float(topk)
    lane = lax.broadcasted_iota(I32, (rows, 128), 1)
    wide = lambda a: jnp.broadcast_to(a, (rows, 128))

    def count(pred):
        def body(c, acc):
            x = load_chunk(c)
            for j in range(ts // 128):
                kpos = lane + (c * ts + j * 128)
                acc = acc + jnp.where(pred(x[:, j * 128:(j + 1) * 128], kpos), 1.0, 0.0)
            return acc
        acc = lax.fori_loop(0, nch, body, jnp.zeros((rows, 128), F32))
        return jnp.sum(acc, axis=1, keepdims=True)

    def v_cond(st):
        _, _, done, it = st
        return jnp.logical_and(jnp.min(done) < 0.5, it < 4096)

    def v_body(st):
        lo, hi, done, it = st
        mid = jnp.minimum(jnp.maximum(lo * 0.5 + hi * 0.5, lo), hi)
        midw = wide(mid)
        cnt = count(lambda x, kpos: x >= midw)
        active = done < 0.5
        ge = cnt >= kf
        hit = cnt == kf
        collapsed = jnp.logical_or(mid <= lo, mid >= hi)
        new_lo = jnp.where(jnp.logical_and(active, jnp.logical_or(ge, hit)), mid, lo)
        new_hi = jnp.where(jnp.logical_and(active, jnp.logical_and(jnp.logical_not(ge), jnp.logical_not(collapsed))), mid, hi)
        new_hi = jnp.where(jnp.logical_and(active, hit), mid, new_hi)
        new_done = jnp.where(jnp.logical_or(hit, collapsed), 1.0, done)
        return new_lo, new_hi, new_done, it + 1

    done0 = jnp.where(skip, 1.0, 0.0)
    lo, hi, _, _ = lax.while_loop(v_cond, v_body, (lo0, hi0, done0, jnp.int32(0)))
    hiw = wide(hi)
    cnt_hi = count(lambda x, kpos: x >= hiw)
    v = jnp.where(cnt_hi >= kf, hi, lo)
    v = jnp.where(skip, NEG_INF, v)
    vw = wide(v)

    cgt = count(lambda x, kpos: x > vw)
    ceq = count(lambda x, kpos: x == vw)
    need = kf - cgt
    partial = jnp.logical_and(jnp.logical_not(skip), need < ceq)
    span = nch * ts

    def c_cond(st):
        lo_i, hi_i = st
        return jnp.max(jnp.where(partial, hi_i - lo_i, 0)) > 0

    def c_body(st):
        lo_i, hi_i = st
        mid = lax.shift_right_logical(lo_i + hi_i, 1)
        midw = wide(mid)
        cnt = count(lambda x, kpos: jnp.logical_and(x == vw, kpos <= midw))
        ok = cnt >= need
        return jnp.where(ok, lo_i, mid + 1), jnp.where(ok, mid, hi_i)

    zero_i = jnp.zeros((rows, 1), I32)
    _, cut = lax.while_loop(c_cond, c_body, (zero_i, zero_i + (span - 1)))
    cut = jnp.where(partial, cut, span)
    cut = jnp.where(skip, -1, cut)
    return v, cut


def _prompt_attn_kernel(q_ref, qi_ref, wi_ref, kit_ref, kt_ref, v_ref, o_ref,
                        sc_ref, m_ref, l_ref, acc_ref, *, tq, ts, topk):
    i = pl.program_id(0)
    nch = ((i + 1) * tq + ts - 1) // ts
    row = i * tq + lax.broadcasted_iota(I32, (tq, 1), 0)
    heads = q_ref.shape[1] // HEAD_DIM
    wsc = (wi_ref[...] * IDX_HEADS ** -0.5) * IDX_DIM ** -0.5
    qi_h = [qi_ref[:, h * IDX_DIM:(h + 1) * IDX_DIM] for h in range(IDX_HEADS)]
    w_h = [jnp.broadcast_to(wsc[:, h:h + 1], (tq, ts)) for h in range(IDX_HEADS)]

    def scores(c, carry):
        mn, mx = carry
        kic = kit_ref[c]
        acc = jnp.zeros((tq, ts), F32)
        for h in range(IDX_HEADS):
            lg = jnp.dot(qi_h[h], kic, preferred_element_type=F32)
            acc = acc + jnp.maximum(lg, 0.0) * w_h[h]
        kpos = c * ts + lax.broadcasted_iota(I32, (tq, ts), 1)
        adm = kpos <= row
        sc_ref[c] = jnp.where(adm, acc, NEG_INF)
        mn = jnp.minimum(mn, _lane_fold(jnp.where(adm, acc, jnp.inf), jnp.minimum))
        mx = jnp.maximum(mx, _lane_fold(jnp.where(adm, acc, NEG_INF), jnp.maximum))
        return mn, mx

    mn, mx = lax.fori_loop(0, nch, scores,
                           (jnp.full((tq, 128), jnp.inf, F32), jnp.full((tq, 128), NEG_INF, F32)))
    lo0 = jnp.min(mn, axis=1, keepdims=True)
    hi0 = jnp.max(mx, axis=1, keepdims=True)
    skip = row < topk
    v, cut = _select_threshold(lambda c: sc_ref[c], nch, ts, tq, topk, lo0, hi0, skip)

    m_ref[...] = jnp.full(m_ref.shape, -1e30, F32)
    l_ref[...] = jnp.zeros(l_ref.shape, F32)
    acc_ref[...] = jnp.zeros(acc_ref.shape, F32)
    scale = HEAD_DIM ** -0.5
    group = heads // KV_HEADS
    vw = jnp.broadcast_to(v, (tq, ts))
    cutw = jnp.broadcast_to(cut, (tq, ts))

    def attend(c, _):
        x = sc_ref[c]
        kpos = c * ts + lax.broadcasted_iota(I32, (tq, ts), 1)
        sel = jnp.logical_or(x > vw, jnp.logical_and(x == vw, kpos <= cutw))
        bias = jnp.where(sel, 0.0, NEG_INF)
        kc = kt_ref[c]
        vc = v_ref[c]
        for h in range(heads):
            g = h // group
            s = jnp.dot(q_ref[:, h * HEAD_DIM:(h + 1) * HEAD_DIM], kc[g * HEAD_DIM:(g + 1) * HEAD_DIM, :],
                        preferred_element_type=F32) * scale + bias
            m_old = m_ref[h]
            m_new = jnp.maximum(m_old, jnp.max(s, axis=1, keepdims=True))
            alpha = jnp.exp(m_old - m_new)
            p = jnp.exp(s - m_new)
            l_ref[h] = l_ref[h] * alpha + jnp.sum(p, axis=1, keepdims=True)
            acc_ref[h] = acc_ref[h] * alpha + jnp.dot(p.astype(BF16), vc[:, g * HEAD_DIM:(g + 1) * HEAD_DIM],
                                                      preferred_element_type=F32)
            m_ref[h] = m_new
        return 0

    lax.fori_loop(0, nch, attend, 0)
    for h in range(heads):
        o_ref[:, h * HEAD_DIM:(h + 1) * HEAD_DIM] = (acc_ref[h] / l_ref[h]).astype(o_ref.dtype)


SAMPLE_TS = 4 * PAGE_SIZE


def _sample_attn_kernel(pt_ref, q_ref, qi_ref, wi_ref, kin_ref, kn_ref, vn_ref, cki_ref, ck_ref, cv_ref,
                        o_ref, kibuf, kbuf, vbuf, sc_ref, sem_i, sem_k, sem_v, *, n_pages, t_new, topk):
    b = pl.program_id(0)
    ts = SAMPLE_TS
    ppc = ts // PAGE_SIZE
    nch = n_pages // ppc
    past = n_pages * PAGE_SIZE
    rows = qi_ref.shape[1]

    def ki_copy(p):
        return pltpu.make_async_copy(cki_ref.at[0, pt_ref[b, p]], kibuf.at[pl.ds(p * PAGE_SIZE, PAGE_SIZE)], sem_i.at[0])

    def kv_copies(c, slot):
        cps = []
        for j in range(ppc):
            page = pt_ref[b, c * ppc + j]
            dst = pl.ds(j * PAGE_SIZE, PAGE_SIZE)
            cps.append(pltpu.make_async_copy(ck_ref.at[0, page], kbuf.at[slot, dst], sem_k.at[slot]))
            cps.append(pltpu.make_async_copy(cv_ref.at[0, page], vbuf.at[slot, dst], sem_v.at[slot]))
        return cps

    def start_ki(p, _):
        ki_copy(p).start()
        return 0

    def wait_ki(p, _):
        ki_copy(p).wait()
        return 0

    lax.fori_loop(0, n_pages, start_ki, 0)
    for cp in kv_copies(0, 0):
        cp.start()
    lax.fori_loop(0, n_pages, wait_ki, 0)

    wsc = (wi_ref[0] * IDX_HEADS ** -0.5) * IDX_DIM ** -0.5
    qi = qi_ref[0]
    qi_h = [qi[:, h * IDX_DIM:(h + 1) * IDX_DIM] for h in range(IDX_HEADS)]
    w_h = [jnp.broadcast_to(wsc[:, h:h + 1], (rows, ts)) for h in range(IDX_HEADS)]

    def index_scores(kc):
        acc = jnp.zeros((rows, ts), F32)
        for h in range(IDX_HEADS):
            lg = lax.dot_general(qi_h[h], kc, (((1,), (1,)), ((), ())), preferred_element_type=F32)
            acc = acc + jnp.maximum(lg, 0.0) * w_h[h]
        return acc

    def scores(c, carry):
        mn, mx = carry
        acc = index_scores(kibuf[pl.ds(pl.multiple_of(c * ts, ts), ts), :].astype(BF16))
        sc_ref[c] = acc
        return jnp.minimum(mn, _lane_fold(acc, jnp.minimum)), jnp.maximum(mx, _lane_fold(acc, jnp.maximum))

    mn, mx = lax.fori_loop(0, nch, scores,
                           (jnp.full((rows, 128), jnp.inf, F32), jnp.full((rows, 128), NEG_INF, F32)))
    acc = index_scores(kin_ref[0])
    tok = lax.broadcasted_iota(I32, (rows, ts), 0) % t_new
    col = lax.broadcasted_iota(I32, (rows, ts), 1)
    adm = jnp.logical_and(col <= tok, col < t_new)
    sc_ref[nch] = jnp.where(adm, acc, NEG_INF)
    mn = jnp.minimum(mn, _lane_fold(jnp.where(adm, acc, jnp.inf), jnp.minimum))
    mx = jnp.maximum(mx, _lane_fold(jnp.where(adm, acc, NEG_INF), jnp.maximum))
    lo0 = jnp.min(mn, axis=1, keepdims=True)
    hi0 = jnp.max(mx, axis=1, keepdims=True)
    skip = jnp.full((rows, 1), past + 1 <= topk)
    v, cut = _select_threshold(lambda c: sc_ref[c], nch + 1, ts, rows, topk, lo0, hi0, skip)
    vw = jnp.broadcast_to(v, (rows, ts))
    cutw = jnp.broadcast_to(cut, (rows, ts))
    scale = HEAD_DIM ** -0.5

    def bias_of(c):
        x = sc_ref[c]
        kpos = c * ts + lax.broadcasted_iota(I32, (rows, ts), 1)
        sel = jnp.logical_or(x > vw, jnp.logical_and(x == vw, kpos <= cutw))
        return jnp.where(sel, 0.0, NEG_INF)

    def flash(state, g, bias, kg, vg):
        m_old, l_old, a_old = state
        s = lax.dot_general(q_ref[0, g], kg, (((1,), (1,)), ((), ())), preferred_element_type=F32) * scale + bias
        m_new = jnp.maximum(m_old, jnp.max(s, axis=1, keepdims=True))
        alpha = jnp.exp(m_old - m_new)
        p = jnp.exp(s - m_new)
        l_new = l_old * alpha + jnp.sum(p, axis=1, keepdims=True)
        a_new = a_old * alpha + jnp.dot(p.astype(BF16), vg, preferred_element_type=F32)
        return m_new, l_new, a_new

    def attend(c, states):
        slot = c % 2
        for cp in kv_copies(c, slot):
            cp.wait()

        @pl.when(c + 1 < nch)
        def _():
            for cp in kv_copies(c + 1, 1 - slot):
                cp.start()

        bias = bias_of(c)
        return tuple(
            flash(states[g], g, bias, kbuf[slot, :, g, :].astype(BF16), vbuf[slot, :, g, :].astype(BF16))
            for g in range(KV_HEADS))

    init = (jnp.full((rows, 1), -1e30, F32), jnp.zeros((rows, 1), F32), jnp.zeros((rows, HEAD_DIM), F32))
    states = lax.fori_loop(0, nch, attend, (init,) * KV_HEADS)
    bias = bias_of(nch)
    for g in range(KV_HEADS):
        sl = slice(g * HEAD_DIM, (g + 1) * HEAD_DIM)
        _, l_g, a_g = flash(states[g], g, bias, kn_ref[0][:, sl], vn_ref[0][:, sl])
        o_ref[0, g] = (a_g / l_g).astype(o_ref.dtype)


def sample_attention(q, qi, wi, ki_new, k_new, v_new, cache_kidx, cache_k, cache_v, page_table, *, topk):
    nb, t_new, width = q.shape
    heads = width // HEAD_DIM
    group = heads // KV_HEADS
    n_pages = page_table.shape[1]
    ts = SAMPLE_TS
    assert n_pages % (ts // PAGE_SIZE) == 0
    nch = n_pages // (ts // PAGE_SIZE)
    rows = group * t_new
    qg = q.reshape(nb, t_new, KV_HEADS, group, HEAD_DIM).transpose(0, 2, 3, 1, 4).reshape(nb, KV_HEADS, rows, HEAD_DIM)
    qir = jnp.tile(qi, (1, group, 1))
    wir = jnp.tile(wi, (1, group, 1))
    padk = lambda a: jnp.pad(a, ((0, 0), (0, ts - t_new), (0, 0)))
    per_b = lambda a: pl.BlockSpec((1,) + a.shape[1:], lambda b, pt: (b,) + (0,) * (a.ndim - 1))
    ops = (qg, qir, wir, padk(ki_new), padk(k_new), padk(v_new))
    out = pl.pallas_call(
        functools.partial(_sample_attn_kernel, n_pages=n_pages, t_new=t_new, topk=topk),
        grid_spec=pltpu.PrefetchScalarGridSpec(
            num_scalar_prefetch=1,
            grid=(nb,),
            in_specs=[per_b(a) for a in ops] + [pl.BlockSpec(memory_space=pl.ANY)] * 3,
            out_specs=pl.BlockSpec((1, KV_HEADS, rows, HEAD_DIM), lambda b, pt: (b, 0, 0, 0)),
            scratch_shapes=[
                pltpu.VMEM((n_pages * PAGE_SIZE, IDX_DIM), F32),
                pltpu.VMEM((2, ts, KV_HEADS, HEAD_DIM), F32),
                pltpu.VMEM((2, ts, KV_HEADS, HEAD_DIM), F32),
                pltpu.VMEM((nch + 1, rows, ts), F32),
                pltpu.SemaphoreType.DMA((1,)),
                pltpu.SemaphoreType.DMA((2,)),
                pltpu.SemaphoreType.DMA((2,)),
            ]),
        out_shape=jax.ShapeDtypeStruct((nb, KV_HEADS, rows, HEAD_DIM), BF16),
        compiler_params=_cparams("arbitrary"),
        name="sample_attention",
    )(page_table, *ops, cache_kidx, cache_k, cache_v)
    return out.reshape(nb, KV_HEADS, group, t_new, HEAD_DIM).transpose(0, 3, 1, 2, 4).reshape(nb, t_new, width)


def prompt_attention(qb, qib, wi, kib, kb, vb, *, tq, ts, topk):
    s, width = qb.shape
    heads = width // HEAD_DIM
    assert s % tq == 0 and s % ts == 0
    nc = s // ts
    kit = kib.reshape(nc, ts, IDX_DIM).transpose(0, 2, 1)
    kt = kb.reshape(nc, ts, KV_HEADS * HEAD_DIM).transpose(0, 2, 1)
    v3 = vb.reshape(nc, ts, KV_HEADS * HEAD_DIM)
    whole = lambda a: pl.BlockSpec(a.shape, lambda i: (0,) * a.ndim, pipeline_mode=pl.Buffered(1))
    return pl.pallas_call(
        functools.partial(_prompt_attn_kernel, tq=tq, ts=ts, topk=topk),
        grid=(s // tq,),
        in_specs=[
            pl.BlockSpec((tq, width), lambda i: (i, 0)),
            pl.BlockSpec((tq, IDX_HEADS * IDX_DIM), lambda i: (i, 0)),
            pl.BlockSpec((tq, IDX_HEADS), lambda i: (i, 0)),
            whole(kit), whole(kt), whole(v3),
        ],
        out_specs=pl.BlockSpec((tq, width), lambda i: (i, 0)),
        out_shape=jax.ShapeDtypeStruct((s, width), BF16),
        scratch_shapes=[
            pltpu.VMEM((nc, tq, ts), F32),
            pltpu.VMEM((heads, tq, 1), F32),
            pltpu.VMEM((heads, tq, 1), F32),
            pltpu.VMEM((heads, tq, HEAD_DIM), F32),
        ],
        compiler_params=_cparams("arbitrary"),
        name="prompt_attention",
    )(qb, qib, wi, kit, kt, v3)


def _row_tile(m, cap):
    t = cap
    while m % t:
        t //= 2
    return t


def _token_stages(x, attend, gdn, lw):
    (nmw, wp, w_out, nfw, wq, sub_keys, w_u, w_v, fw, tabs) = lw
    m = x.shape[0]
    tm = _row_tile(m, 512)
    p = norm_matmul(x, nmw, wp, tm=tm, tn=wp.shape[1] // 7)
    qb, qib, k32, kb, v32, vb, ki32, kib, auxr = rope_split(p, *tabs, tm=tm)
    wi, a_pre = auxr[:, 0:IDX_HEADS], auxr[:, 2 * IDX_HEADS:3 * IDX_HEADS]
    ya = attend(qb, qib, wi, kib, kb, vb)
    yg, new_conv, new_ssm = gdn(p, a_pre)
    h, xn, qp = out_proj(ya, yg, x, w_out, nfw, wq, tm=tm)
    i1, i2, gate = peer_topk(qp, sub_keys, tm=_row_tile(m, 256))
    tok = lambda a: a.reshape(PEER_HEADS * PEER_TOPK, m).T
    gates = peer_gates(tok(i1), tok(i2), tok(gate), tmb=64)
    y = peer_dense(xn, w_u, w_v, gates, h, fw, tm=tm, ib=4)
    return y, k32, v32, ki32, new_conv, new_ssm


def kernel(x_prompt, x_sample, cache_k, cache_v, cache_kidx, page_table, state_conv, state_ssm, norm_mix_w, w_in,
           conv_w, a_log, dt_bias, gdn_norm_w, w_out, norm_ffn_w, peer_wq, peer_sub_keys, peer_u, peer_v,
           norm_final_w):
    depth = w_in.shape[0]
    b, s, d = x_prompt.shape
    nb, t, _ = x_sample.shape
    assert depth == 1 and b == 1, "single layer, single prompt sequence"
    past = page_table.shape[1] * PAGE_SIZE
    heads_g = state_ssm.shape[2]
    conv_ch = state_conv.shape[-1]

    wp = _pack_w_in(w_in[0])
    shared = (norm_mix_w[0][None], wp, w_out[0].astype(BF16), norm_ffn_w[0][None], peer_wq[0].astype(BF16),
              peer_sub_keys[0], peer_u[0].astype(BF16), peer_v[0].astype(BF16), norm_final_w[None])
    gdn_w = (conv_w[0], a_log[0], dt_bias[0], gdn_norm_w[0])

    def attend_p(qb, qib, wi, kib, kb, vb):
        return prompt_attention(qb, qib, wi, kib, kb, vb, tq=_row_tile(s, 256), ts=_row_tile(s, 512),
                                topk=min(TOPK_MAX, s // 4))

    def gdn_p(p, a_pre):
        conv0 = jnp.zeros((1, CONV_W - 1, conv_ch), F32)
        ssm0 = jnp.zeros((1, heads_g, GDN_DK, GDN_DV), F32)
        t_pad = -(-s // GDN_CHUNK) * GDN_CHUNK
        assert t_pad == s
        return gdn_mixer(p, a_pre.T[None], conv0, *gdn_w, ssm0, n_seq=1, t_pad=s, t_total=s, chunk=GDN_CHUNK)

    tabs_p = _rope_tables(jnp.arange(s))
    y_p, k_p, v_p, ki_p, conv_p, ssm_p = _token_stages(x_prompt.reshape(s, d), attend_p, gdn_p, shared + (tabs_p,))

    m_s = nb * t
    t8 = -(-t // 8) * 8

    def attend_s(qb, qib, wi, kib, kb, vb):
        seq = lambda a: a.reshape(nb, t, a.shape[-1])
        y = sample_attention(seq(qb), seq(qib), seq(wi), seq(kib), seq(kb), seq(vb), cache_kidx, cache_k, cache_v,
                             page_table, topk=min(TOPK_MAX, (past + t) // 4))
        return y.reshape(m_s, y.shape[-1])

    def gdn_s(p, a_pre):
        pad_t = lambda a: jnp.pad(a.reshape(nb, t, a.shape[-1]), ((0, 0), (0, t8 - t), (0, 0)))
        p8 = pad_t(p).reshape(nb * t8, p.shape[-1])
        at = pad_t(a_pre).transpose(0, 2, 1)
        yg, nconv, nssm = gdn_mixer(p8, at, state_conv[0], *gdn_w, state_ssm[0], n_seq=nb, t_pad=t8, t_total=t, chunk=8)
        return yg.reshape(nb, t8, yg.shape[-1])[:, :t].reshape(m_s, yg.shape[-1]), nconv, nssm

    tabs_s = tuple(jnp.tile(a, (nb, 1)) for a in _rope_tables(past + jnp.arange(t)))
    y_s, k_s, v_s, ki_s, conv_s, ssm_s = _token_stages(x_sample.reshape(m_s, d), attend_s, gdn_s, shared + (tabs_s,))

    kv = lambda a, n, tt: a.reshape(1, n, tt, KV_HEADS, HEAD_DIM)
    return (y_p.reshape(b, s, d), y_s.reshape(nb, t, d),
            kv(k_p, b, s), kv(v_p, b, s), ki_p.reshape(1, b, s, IDX_DIM), conv_p[None], ssm_p[None],
            kv(k_s, nb, t), kv(v_s, nb, t), ki_s.reshape(1, nb, t, IDX_DIM), conv_s[None], ssm_s[None])
```

```python
import functools
import math

import jax
import jax.numpy as jnp
from jax import lax
from jax.experimental import pallas as pl
from jax.experimental.pallas import tpu as pltpu

F32 = jnp.float32
BF16 = jnp.bfloat16
I32 = jnp.int32
EPS = 1e-6
NEG_INF = float("-inf")

HEAD_DIM = 128
KV_HEADS = 2
IDX_HEADS = 8
IDX_DIM = 64
TOPK_MAX = 256
ROPE_THETA = 10000.0
PAGE_SIZE = 128
GDN_DK = 128
GDN_DV = 128
CONV_W = 4
GDN_CHUNK = 64
PEER_HEADS = 8
PEER_KEYS = 128
PEER_QDIM = 128
PEER_TOPK = 16

VMEM_LIMIT_BYTES = 56 * 1024 * 1024


def _cparams(*sem):
    return pltpu.CompilerParams(dimension_semantics=sem, vmem_limit_bytes=VMEM_LIMIT_BYTES)


def _pack_w_in(w_in):
    sizes = (1024, 256, 256, 512, 64, 8, 3072, 1024, 8, 8)
    offs = [0]
    for sz in sizes:
        offs.append(offs[-1] + sz)
    part = lambda i: w_in[:, offs[i]:offs[i + 1]]
    q, k, v, qi, ki, wi, qkv, z, b, a = (part(i) for i in range(10))
    pad = jnp.zeros((w_in.shape[0], 128 - 64 - 24), w_in.dtype)
    return jnp.concatenate([q, qi, k, v, qkv, z, ki, wi, b, a, pad], axis=1).astype(BF16)


def _rope_tables(pos):
    def table(dim):
        half = dim // 2
        inv = ROPE_THETA ** (-jnp.arange(half, dtype=F32) / half)
        ang = pos.astype(F32)[:, None] * inv[None, :]
        cos = jnp.tile(jnp.cos(ang), (1, 128 // half))
        sin = jnp.tile(jnp.concatenate([-jnp.sin(ang), jnp.sin(ang)], axis=1), (1, 128 // dim))
        return cos, sin
    cq, sq = table(HEAD_DIM)
    ci, si = table(IDX_DIM)
    return cq, sq, ci, si


def _norm_matmul_kernel(x_ref, nw_ref, w_ref, o_ref):
    x = x_ref[...]
    y = x * lax.rsqrt(jnp.mean(x * x, axis=-1, keepdims=True) + EPS)
    xn = (y * nw_ref[...]).astype(BF16)
    o_ref[...] = jnp.dot(xn, w_ref[...], preferred_element_type=F32)


def norm_matmul(x, nw, w, *, tm, tn):
    m, d = x.shape
    n = w.shape[1]
    assert m % tm == 0 and n % tn == 0
    return pl.pallas_call(
        _norm_matmul_kernel,
        grid=(n // tn, m // tm),
        in_specs=[
            pl.BlockSpec((tm, d), lambda j, i: (i, 0)),
            pl.BlockSpec((1, d), lambda j, i: (0, 0)),
            pl.BlockSpec((d, tn), lambda j, i: (0, j)),
        ],
        out_specs=pl.BlockSpec((tm, tn), lambda j, i: (i, j)),
        out_shape=jax.ShapeDtypeStruct((m, n), F32),
        compiler_params=_cparams("arbitrary", "arbitrary"),
        name="norm_matmul",
    )(x, nw, w)


def _rope128(x, cos, sin_signed):
    return x * cos + pltpu.roll(x, 64, 1) * sin_signed


def _rope64(x, cos, sin_signed, first_half):
    partner = jnp.where(first_half, pltpu.roll(x, 96, 1), pltpu.roll(x, 32, 1))
    return x * cos + partner * sin_signed


def _rope_split_kernel(q_ref, qi_ref, k_ref, v_ref, aux_ref, cq_ref, sq_ref, ci_ref, si_ref,
                       qb_ref, qib_ref, k32_ref, kb_ref, v32_ref, vb_ref, ki32_ref, kib_ref, auxr_ref, kn2_ref):
    cq, sq, ci, si = cq_ref[...], sq_ref[...], ci_ref[...], si_ref[...]
    lane = lax.broadcasted_iota(I32, ci.shape, 1)
    first_half = (lane % IDX_DIM) < (IDX_DIM // 2)
    for h in range(q_ref.shape[1] // HEAD_DIM):
        sl = slice(h * HEAD_DIM, (h + 1) * HEAD_DIM)
        qb_ref[:, sl] = _rope128(q_ref[:, sl], cq, sq).astype(BF16)
    for h in range(k_ref.shape[1] // HEAD_DIM):
        sl = slice(h * HEAD_DIM, (h + 1) * HEAD_DIM)
        kr = _rope128(k_ref[:, sl], cq, sq)
        k32_ref[:, sl] = kr
        krb = kr.astype(BF16)
        kb_ref[:, sl] = krb
        kn2_ref[:, h:h + 1] = jnp.sum(krb.astype(F32) * krb.astype(F32), axis=1, keepdims=True)
    for h in range(qi_ref.shape[1] // 128):
        sl = slice(h * 128, (h + 1) * 128)
        qib_ref[:, sl] = _rope64(qi_ref[:, sl], ci, si, first_half).astype(BF16)
    v = v_ref[...]
    v32_ref[...] = v
    vb_ref[...] = v.astype(BF16)
    aux = aux_ref[...]
    kir = _rope64(aux, ci, si, first_half)[:, :IDX_DIM]
    ki32_ref[...] = kir
    kib_ref[...] = kir.astype(BF16)
    auxr_ref[...] = aux[:, IDX_DIM:]


def rope_split(p, cq, sq, ci, si, *, tm):
    m = p.shape[0]
    assert m % tm == 0
    row = lambda w, j: pl.BlockSpec((tm, w), lambda i, j=j: (i, j))
    outs = [
        ((m, 1024), BF16), ((m, 512), BF16), ((m, 256), F32), ((m, 256), BF16), ((m, 256), F32),
        ((m, 256), BF16), ((m, IDX_DIM), F32), ((m, IDX_DIM), BF16), ((m, 128 - IDX_DIM), F32),
        ((m, KV_HEADS), F32),
    ]
    return pl.pallas_call(
        _rope_split_kernel,
        grid=(m // tm,),
        in_specs=[row(1024, 0), row(512, 2), row(256, 6), row(256, 7), row(128, 48),
                  row(128, 0), row(128, 0), row(128, 0), row(128, 0)],
        out_specs=[pl.BlockSpec((tm, s[1]), lambda i: (i, 0)) for s, _ in outs],
        out_shape=[jax.ShapeDtypeStruct(s, dt) for s, dt in outs],
        compiler_params=_cparams("arbitrary"),
        name="rope_split",
    )(p, p, p, p, p, cq, sq, ci, si)


def _out_proj_kernel(ya_ref, yg_ref, res_ref, wa_ref, wg_ref, nw_ref, wq_ref, h_ref, xn_ref, qp_ref):
    h = res_ref[...] + jnp.dot(ya_ref[...], wa_ref[...], preferred_element_type=F32)
    h = h + jnp.dot(yg_ref[...], wg_ref[...], preferred_element_type=F32)
    h_ref[...] = h
    y = h * lax.rsqrt(jnp.mean(h * h, axis=-1, keepdims=True) + EPS)
    xn = (y * nw_ref[...]).astype(BF16)
    xn_ref[...] = xn
    qp_ref[...] = jnp.dot(xn, wq_ref[...], preferred_element_type=F32)


def out_proj(ya, yg, res, w_out, nw, wq, *, tm):
    m, d = res.shape
    half = ya.shape[1]
    nq = wq.shape[1]
    assert m % tm == 0
    rows = lambda wd: pl.BlockSpec((tm, wd), lambda i: (i, 0))
    const = lambda a: pl.BlockSpec(a.shape, lambda i: (0,) * a.ndim, pipeline_mode=pl.Buffered(1))
    wa, wg = w_out[:half], w_out[half:]
    return pl.pallas_call(
        _out_proj_kernel,
        grid=(m // tm,),
        in_specs=[rows(half), rows(half), rows(d), const(wa), const(wg), const(nw), const(wq)],
        out_specs=[rows(d), rows(d), rows(nq)],
        out_shape=[jax.ShapeDtypeStruct((m, d), F32), jax.ShapeDtypeStruct((m, d), BF16),
                   jax.ShapeDtypeStruct((m, nq), F32)],
        compiler_params=_cparams("arbitrary"),
        name="out_proj",
    )(ya, yg, res, wa, wg, nw, wq)


def _take_top(s, codes, count):
    big = jnp.int32(2 ** 30)
    vals, picks = [], []
    for _ in range(count):
        m = jnp.max(s, axis=0, keepdims=True)
        pick = jnp.min(jnp.where(s == m, codes, big), axis=0, keepdims=True)
        s = jnp.where(codes == pick, NEG_INF, s)
        vals.append(m)
        picks.append(pick)
    return jnp.concatenate(vals, axis=0), jnp.concatenate(picks, axis=0)


def _lookup(table, sel, count):
    out = jnp.zeros(sel.shape, table.dtype)
    for a in range(count):
        out = jnp.where(sel == a, jnp.broadcast_to(table[a:a + 1, :], sel.shape), out)
    return out


def _peer_topk_kernel(q_ref, sk_ref, i1_ref, i2_ref, g_ref):
    tm = q_ref.shape[0]
    kk = PEER_TOPK
    half = PEER_QDIM // 2
    key_codes = lax.broadcasted_iota(I32, (PEER_KEYS, tm), 0)
    top_v, top_i = [], []
    for c in range(2):
        qs = q_ref[:, c * half:(c + 1) * half].astype(BF16)
        s = lax.dot_general(sk_ref[c], qs, (((1,), (1,)), ((), ())), preferred_element_type=F32)
        vals, idx = _take_top(s, key_codes, kk)
        top_v.append(vals)
        top_i.append(idx)
    pieces, codes = [], []
    for a in range(kk):
        nb = kk // (a + 1)
        rows = -(-nb // 8) * 8
        r = lax.broadcasted_iota(I32, (rows, tm), 0)
        cand = top_v[0][a:a + 1, :] + top_v[1][0:rows, :]
        pieces.append(jnp.where(r < nb, cand, NEG_INF))
        codes.append(r + a * kk)
    best_s, best_c = _take_top(jnp.concatenate(pieces, axis=0), jnp.concatenate(codes, axis=0), kk)
    i1_ref[0] = _lookup(top_i[0], lax.shift_right_logical(best_c, 4), kk)
    i2_ref[0] = _lookup(top_i[1], jnp.bitwise_and(best_c, kk - 1), kk)
    e = jnp.exp(best_s - best_s[0:1, :])
    g_ref[0] = e / jnp.sum(e, axis=0, keepdims=True)


def peer_topk(qp, sub_keys, *, tm):
    m = qp.shape[0]
    heads = sub_keys.shape[0]
    assert m % tm == 0 and PEER_TOPK == 16
    sk = sub_keys.reshape(heads * 2, PEER_KEYS, PEER_QDIM // 2).astype(BF16)
    out = pl.BlockSpec((1, PEER_TOPK, tm), lambda i, h: (h, 0, i))
    return pl.pallas_call(
        _peer_topk_kernel,
        grid=(m // tm, heads),
        in_specs=[pl.BlockSpec((tm, PEER_QDIM), lambda i, h: (i, h)),
                  pl.BlockSpec((2, PEER_KEYS, PEER_QDIM // 2), lambda i, h: (h, 0, 0))],
        out_specs=[out, out, out],
        out_shape=[jax.ShapeDtypeStruct((heads, PEER_TOPK, m), I32), jax.ShapeDtypeStruct((heads, PEER_TOPK, m), I32),
                   jax.ShapeDtypeStruct((heads, PEER_TOPK, m), F32)],
        compiler_params=_cparams("arbitrary", "arbitrary"),
        name="peer_topk",
    )(qp, sk)


GATE_TOKENS = 16


def _peer_gate_kernel(i1_ref, i2_ref, g_ref, o_ref):
    tmb = i1_ref.shape[0]
    nk = PEER_KEYS
    sub = lax.broadcasted_iota(I32, (nk, i1_ref.shape[1]), 0)

    def token_block(j, _):
        t0 = pl.multiple_of(j * GATE_TOKENS, GATE_TOKENS)
        per_token = []
        for u in range(GATE_TOKENS):
            wide = lambda ref: jnp.broadcast_to(ref[pl.ds(t0 + u, 1), :], sub.shape)
            p1 = jnp.where(wide(i1_ref) == sub, wide(g_ref), 0.0).astype(BF16)
            p2 = jnp.where(wide(i2_ref) == sub, 1.0, 0.0).astype(BF16)
            gm = lax.dot_general(p1, p2, (((1,), (1,)), ((), ())), preferred_element_type=F32)
            per_token.append(gm.astype(o_ref.dtype))
        block = jnp.stack(per_token, axis=0)
        o_ref[:, pl.ds(t0, GATE_TOKENS), :] = pltpu.einshape("mab->amb", block)
        return 0

    lax.fori_loop(0, tmb // GATE_TOKENS, token_block, 0)


def peer_gates(i1, i2, gate, *, tmb):
    m, slots = i1.shape
    assert m % tmb == 0 and tmb % GATE_TOKENS == 0
    rows = pl.BlockSpec((tmb, slots), lambda i: (i, 0))
    return pl.pallas_call(
        _peer_gate_kernel,
        grid=(m // tmb,),
        in_specs=[rows, rows, rows],
        out_specs=pl.BlockSpec((PEER_KEYS, tmb, PEER_KEYS), lambda i: (0, i, 0)),
        out_shape=jax.ShapeDtypeStruct((PEER_KEYS, m, PEER_KEYS), BF16),
        compiler_params=_cparams("arbitrary"),
        name="peer_gates",
    )(i1, i2, gate)


def _gelu_tanh(x):
    return 0.5 * x * (1.0 + jnp.tanh(math.sqrt(2.0 / math.pi) * (x + 0.044715 * (x * x * x))))


def _peer_dense_kernel(xn_ref, wu_ref, wv_ref, g_ref, h_ref, fw_ref, o_ref, acc_ref):
    j = pl.program_id(1)
    ib = g_ref.shape[0]

    @pl.when(j == 0)
    def _():
        acc_ref[...] = jnp.zeros(acc_ref.shape, F32)

    a = lax.dot_general(xn_ref[...], wu_ref[...], (((1,), (1,)), ((), ())), preferred_element_type=F32)
    act = _gelu_tanh(a)
    hm = jnp.concatenate(
        [(g_ref[u].astype(F32) * act[:, u * PEER_KEYS:(u + 1) * PEER_KEYS]).astype(BF16) for u in range(ib)], axis=1)
    acc_ref[...] += jnp.dot(hm, wv_ref[...], preferred_element_type=F32)

    @pl.when(j == pl.num_programs(1) - 1)
    def _():
        y = h_ref[...] + acc_ref[...]
        o_ref[...] = y * lax.rsqrt(jnp.mean(y * y, axis=-1, keepdims=True) + EPS) * fw_ref[...]


def peer_dense(xn, w_u, w_v, gates, h, fw, *, tm, ib):
    m, d = h.shape
    assert m % tm == 0 and PEER_KEYS % ib == 0
    eb = ib * PEER_KEYS
    return pl.pallas_call(
        _peer_dense_kernel,
        grid=(m // tm, PEER_KEYS // ib),
        in_specs=[
            pl.BlockSpec((tm, d), lambda i, j: (i, 0)),
            pl.BlockSpec((eb, d), lambda i, j: (j, 0)),
            pl.BlockSpec((eb, d), lambda i, j: (j, 0)),
            pl.BlockSpec((ib, tm, PEER_KEYS), lambda i, j: (j, i, 0)),
            pl.BlockSpec((tm, d), lambda i, j: (i, 0)),
            pl.BlockSpec((1, d), lambda i, j: (0, 0)),
        ],
        out_specs=pl.BlockSpec((tm, d), lambda i, j: (i, 0)),
        out_shape=jax.ShapeDtypeStruct((m, d), F32),
        scratch_shapes=[pltpu.VMEM((tm, d), F32)],
        compiler_params=_cparams("arbitrary", "arbitrary"),
        name="peer_dense",
    )(xn, w_u, w_v, gates, h, fw)


_HI = lax.Precision.HIGHEST


def _dot_hi(a, b):
    return jnp.dot(a, b, preferred_element_type=F32, precision=_HI)


_BNN = (((2,), (1,)), ((0,), (0,)))
_BNT = (((2,), (2,)), ((0,), (0,)))
_BTN = (((1,), (1,)), ((0,), (0,)))


def _bdot(a, b, dims=_BNN):
    return lax.dot_general(a.astype(BF16), b.astype(BF16), dims, preferred_element_type=F32)


def _bdot3(a, b, dims=_BNN):
    ah, bh = a.astype(BF16), b.astype(BF16)
    al, bl = (a - ah.astype(F32)).astype(BF16), (b - bh.astype(F32)).astype(BF16)
    dot = lambda x, y: lax.dot_general(x, y, dims, preferred_element_type=F32)
    return dot(ah, bh) + (dot(ah, bl) + dot(al, bh))


def _sigmoid(x):
    return 1.0 / (1.0 + jnp.exp(-x))


def _softplus(x):
    return jnp.maximum(x, 0.0) + jnp.log1p(jnp.exp(-jnp.abs(x)))


def _gdn_kernel(xq_ref, xk_ref, xv_ref, z_ref, aux_ref, at_ref, cs_ref, cw_ref, alog_ref, dtb_ref,
                alogt_ref, dtbt_ref, nw_ref, s0_ref, y_ref, nconv_ref, nssm_ref, xp_ref, st_ref,
                *, chunk, t_total, b_lane, a_lane):
    c = pl.program_id(1)
    nchunks = pl.num_programs(1)
    heads = st_ref.shape[0]
    width = heads * GDN_DK
    tail = CONV_W - 1

    @pl.when(c == 0)
    def _():
        xp_ref[8 - tail:8, :] = cs_ref[0]
        st_ref[...] = s0_ref[0]

    xp_ref[8:8 + chunk, 0:width] = xq_ref[...]
    xp_ref[8:8 + chunk, width:2 * width] = xk_ref[...]
    xp_ref[8:8 + chunk, 2 * width:3 * width] = xv_ref[...]
    conv = cw_ref[tail:tail + 1, :] * xp_ref[8:8 + chunk, :]
    for j in range(tail):
        conv = conv + cw_ref[j:j + 1, :] * xp_ref[8 - tail + j:8 - tail + j + chunk, :]
    conv = conv * _sigmoid(conv)

    padded = t_total % chunk != 0
    ridx = c * chunk + lax.broadcasted_iota(I32, (chunk, 1), 0)
    rvalid = ridx < t_total
    cidx = c * chunk + lax.broadcasted_iota(I32, (1, chunk), 1)
    cvalid = cidx < t_total

    aux = aux_ref[...]
    beta = _sigmoid(aux[:, b_lane:b_lane + heads])
    g = -jnp.exp(alog_ref[...]) * _softplus(aux[:, a_lane:a_lane + heads] + dtb_ref[...])
    gt = -jnp.exp(alogt_ref[...]) * _softplus(at_ref[0] + dtbt_ref[...])
    if padded:
        beta = jnp.where(rvalid, beta, 0.0)
        g = jnp.where(rvalid, g, 0.0)
        gt = jnp.where(cvalid, gt, 0.0)
    ri = lax.broadcasted_iota(I32, (chunk, chunk), 0)
    ci = lax.broadcasted_iota(I32, (chunk, chunk), 1)
    causal = ri >= ci
    strict = ri > ci
    eye = jnp.where(ri == ci, 1.0, 0.0)
    gc = _dot_hi(jnp.where(causal, 1.0, 0.0), g)
    gct = _dot_hi(gt, jnp.where(ri <= ci, 1.0, 0.0))

    per_head = lambda x2d, base: jnp.stack(
        [x2d[:, base + h * GDN_DK:base + (h + 1) * GDN_DK] for h in range(heads)], axis=0)
    q = per_head(conv, 0)
    k = per_head(conv, width)
    v = per_head(conv, 2 * width)
    q = q * lax.rsqrt(jnp.sum(q * q, axis=-1, keepdims=True) + EPS) * GDN_DK ** -0.5
    k = k * lax.rsqrt(jnp.sum(k * k, axis=-1, keepdims=True) + EPS)
    if padded:
        k = jnp.where(rvalid[None], k, 0.0)
        v = jnp.where(rvalid[None], v, 0.0)
    gcol = jnp.stack([gc[:, h:h + 1] for h in range(heads)], axis=0)
    grow = jnp.stack([gct[h:h + 1, :] for h in range(heads)], axis=0)
    bcol = jnp.stack([beta[:, h:h + 1] for h in range(heads)], axis=0)
    decay = jnp.where(causal[None], jnp.exp(jnp.where(causal[None], gcol - grow, 0.0)), 0.0)
    kb = k * bcol
    vb = v * bcol
    x = -jnp.where(strict[None], _bdot(kb, k, _BNT) * decay, 0.0)
    tinv = eye[None] + x
    span = 2
    while span < chunk:
        x = _bdot3(x, x)
        tinv = tinv + _bdot3(tinv, x)
        span *= 2
    u = _bdot(tinv, vb)
    w = _bdot(tinv, kb * jnp.exp(gcol))
    intra = jnp.where(causal[None], _bdot(q, k, _BNT) * decay, 0.0)
    state = st_ref[...]
    v_new = u - _bdot(w, state)
    out = _bdot(q * jnp.exp(gcol), state) + _bdot(intra, v_new)
    glast = gcol[:, chunk - 1:chunk, :]
    st_ref[...] = state * jnp.exp(glast) + _bdot(k * jnp.exp(glast - gcol), v_new, _BTN)
    on = out * lax.rsqrt(jnp.mean(out * out, axis=-1, keepdims=True) + EPS) * nw_ref[...]
    for h in range(heads):
        sl = slice(h * GDN_DV, (h + 1) * GDN_DV)
        zh = z_ref[:, sl]
        y_ref[:, sl] = (on[h] * (zh * _sigmoid(zh))).astype(y_ref.dtype)

    last_valid = t_total - (t_total - 1) // chunk * chunk
    @pl.when(c < nchunks - 1)
    def _():
        xp_ref[8 - tail:8, :] = xp_ref[8 + chunk - tail:8 + chunk, :]

    @pl.when(c == nchunks - 1)
    def _():
        nconv_ref[0] = xp_ref[8 + last_valid - tail:8 + last_valid, :]
        nssm_ref[0] = st_ref[...]


def gdn_mixer(p, at, conv_state, conv_w, a_log, dt_bias, norm_w, ssm_state, *, n_seq, t_pad, t_total, chunk):
    heads = ssm_state.shape[1]
    width = heads * GDN_DK
    assert t_pad % chunk == 0 and chunk % 8 == 0
    nch = t_pad // chunk
    rows = lambda wd, j: pl.BlockSpec((chunk, wd), lambda n, c, j=j: (n * nch + c, j))
    const = lambda a: pl.BlockSpec(a.shape, lambda n, c: (0,) * a.ndim)
    alog = a_log.reshape(1, heads)
    dtb = dt_bias.reshape(1, heads)
    nw = norm_w.reshape(1, GDN_DV)
    at = at.reshape(n_seq, heads, nch, chunk).transpose(0, 2, 1, 3).reshape(n_seq * nch, heads, chunk)
    kern = functools.partial(_gdn_kernel, chunk=chunk, t_total=t_total, b_lane=IDX_DIM + 8, a_lane=IDX_DIM + 16)
    return pl.pallas_call(
        kern,
        grid=(n_seq, nch),
        in_specs=[
            rows(width, 2), rows(width, 3), rows(width, 4), rows(width, 5), rows(128, 48),
            pl.BlockSpec((1, heads, chunk), lambda n, c: (n * nch + c, 0, 0)),
            pl.BlockSpec((1, CONV_W - 1, 3 * width), lambda n, c: (n, 0, 0)),
            const(conv_w), const(alog), const(dtb), const(alog.T), const(dtb.T), const(nw),
            pl.BlockSpec((1, heads, GDN_DK, GDN_DV), lambda n, c: (n, 0, 0, 0)),
        ],
        out_specs=[
            pl.BlockSpec((chunk, width), lambda n, c: (n * nch + c, 0)),
            pl.BlockSpec((1, CONV_W - 1, 3 * width), lambda n, c: (n, 0, 0)),
            pl.BlockSpec((1, heads, GDN_DK, GDN_DV), lambda n, c: (n, 0, 0, 0)),
        ],
        out_shape=[
            jax.ShapeDtypeStruct((n_seq * t_pad, width), BF16),
            jax.ShapeDtypeStruct((n_seq, CONV_W - 1, 3 * width), F32),
            jax.ShapeDtypeStruct((n_seq, heads, GDN_DK, GDN_DV), F32),
        ],
        scratch_shapes=[pltpu.VMEM((8 + chunk, 3 * width), F32), pltpu.VMEM((heads, GDN_DK, GDN_DV), F32)],
        compiler_params=_cparams("arbitrary", "arbitrary"),
        name="gdn_mixer",
    )(p, p, p, p, p, at, conv_state, conv_w, alog, dtb, alog.T, dtb.T, nw, ssm_state)


def _lane_fold(x, op):
    out = x[:, :128]
    for j in range(1, x.shape[1] // 128):
        out = op(out, x[:, j * 128:(j + 1) * 128])
    return out


def _select_threshold(load, nch, ts, rows, topk, lo0, hi0, skip):
    kf = float(topk)
    rb = min(rows, 128)
    assert rows % rb == 0
    lane = lax.broadcasted_iota(I32, (rb, 128), 1)

    def count(pred, *cols):
        parts = []
        for r0 in range(0, rows, rb):
            wide = [jnp.broadcast_to(col[r0:r0 + rb], (rb, 128)) for col in cols]

            def body(c, acc, r0=r0, wide=wide):
                x = load(c, r0, rb)
                for j in range(ts // 128):
                    kpos = lane + (c * ts + j * 128)
                    acc = acc + jnp.where(pred(x[:, j * 128:(j + 1) * 128], kpos, *wide), 1.0, 0.0)
                return acc

            acc = lax.fori_loop(0, nch, body, jnp.zeros((rb, 128), F32))
            parts.append(jnp.sum(acc, axis=1, keepdims=True))
        return parts[0] if len(parts) == 1 else jnp.concatenate(parts, axis=0)

    ge_ = lambda x, kpos, t: x >= t
    gt_ = lambda x, kpos, t: x > t
    eq_ = lambda x, kpos, t: x == t

    zero = jnp.zeros((rows, 1), F32)
    c_pos = count(gt_, zero)
    c_nn = count(ge_, zero)
    at_zero = jnp.logical_and(c_pos < kf, c_nn >= kf)
    lo0 = jnp.where(at_zero, 0.0, jnp.where(c_pos >= kf, jnp.maximum(lo0, 0.0), lo0))
    hi0 = jnp.where(at_zero, 0.0, jnp.where(c_nn < kf, jnp.minimum(hi0, 0.0), hi0))
    skip_search = jnp.logical_or(skip, at_zero)

    def v_cond(st):
        _, _, done, it = st
        return jnp.logical_and(jnp.min(done) < 0.5, it < 4096)

    def v_body(st):
        lo, hi, done, it = st
        mid = jnp.minimum(jnp.maximum(lo * 0.5 + hi * 0.5, lo), hi)
        cnt = count(ge_, mid)
        active = done < 0.5
        ge = cnt >= kf
        hit = cnt == kf
        collapsed = jnp.logical_or(mid <= lo, mid >= hi)
        new_lo = jnp.where(jnp.logical_and(active, ge), mid, lo)
        new_hi = jnp.where(jnp.logical_and(active, jnp.logical_and(jnp.logical_not(ge), jnp.logical_not(collapsed))), mid, hi)
        new_hi = jnp.where(jnp.logical_and(active, hit), mid, new_hi)
        new_done = jnp.where(jnp.logical_or(hit, collapsed), 1.0, done)
        return new_lo, new_hi, new_done, it + 1

    done0 = jnp.where(skip_search, 1.0, 0.0)
    lo, hi, _, _ = lax.while_loop(v_cond, v_body, (lo0, hi0, done0, jnp.int32(0)))
    cnt_hi = count(ge_, hi)
    v = jnp.where(cnt_hi >= kf, hi, lo)
    v = jnp.where(skip, NEG_INF, v)

    cgt = count(gt_, v)
    ceq = count(eq_, v)
    need = kf - cgt
    partial = jnp.logical_and(jnp.logical_not(skip), need < ceq)
    span = nch * ts

    def c_cond(st):
        lo_i, hi_i = st
        return jnp.max(jnp.where(partial, hi_i - lo_i, 0)) > 0

    def c_body(st):
        lo_i, hi_i = st
        mid = lax.shift_right_logical(lo_i + hi_i, 1)
        cnt = count(lambda x, kpos, t, m: jnp.logical_and(x == t, kpos <= m), v, mid)
        ok = cnt >= need
        return jnp.where(ok, lo_i, mid + 1), jnp.where(ok, mid, hi_i)

    zero_i = jnp.zeros((rows, 1), I32)
    _, cut = lax.while_loop(c_cond, c_body, (zero_i, zero_i + (span - 1)))
    cut = jnp.where(partial, cut, span)
    cut = jnp.where(skip, -1, cut)
    return v, cut


def _prompt_attn_kernel(q_ref, qi_ref, wi_ref, kit_ref, kt_ref, v_ref, kn2_ref, o_ref,
                        sc_ref, m_ref, l_ref, acc_ref, *, tq, ts, topk):
    i = pl.program_id(0)
    nch = ((i + 1) * tq + ts - 1) // ts
    row = i * tq + lax.broadcasted_iota(I32, (tq, 1), 0)
    heads = q_ref.shape[1] // HEAD_DIM
    wsc = (wi_ref[...] * IDX_HEADS ** -0.5) * IDX_DIM ** -0.5
    qi_h = [qi_ref[:, h * IDX_DIM:(h + 1) * IDX_DIM] for h in range(IDX_HEADS)]
    w_h = [jnp.broadcast_to(wsc[:, h:h + 1], (tq, ts)) for h in range(IDX_HEADS)]

    def scores(c, carry):
        mn, mx = carry
        kic = kit_ref[c]
        acc = jnp.zeros((tq, ts), F32)
        for h in range(IDX_HEADS):
            lg = jnp.dot(qi_h[h], kic, preferred_element_type=F32)
            acc = acc + jnp.maximum(lg, 0.0) * w_h[h]
        kpos = c * ts + lax.broadcasted_iota(I32, (tq, ts), 1)
        adm = kpos <= row
        sc_ref[c] = jnp.where(adm, acc, NEG_INF)
        mn = jnp.minimum(mn, _lane_fold(jnp.where(adm, acc, jnp.inf), jnp.minimum))
        mx = jnp.maximum(mx, _lane_fold(jnp.where(adm, acc, NEG_INF), jnp.maximum))
        return mn, mx

    mn, mx = lax.fori_loop(0, nch, scores,
                           (jnp.full((tq, 128), jnp.inf, F32), jnp.full((tq, 128), NEG_INF, F32)))
    lo0 = jnp.min(mn, axis=1, keepdims=True)
    hi0 = jnp.max(mx, axis=1, keepdims=True)
    skip = row < topk
    v, cut = _select_threshold(lambda c, r0, nr: sc_ref[c, r0:r0 + nr, :], nch, ts, tq, topk, lo0, hi0, skip)

    scale = HEAD_DIM ** -0.5
    group = heads // KV_HEADS
    vw = jnp.broadcast_to(v, (tq, ts))
    cutw = jnp.broadcast_to(cut, (tq, ts))
    head = lambda h: slice(h * HEAD_DIM, (h + 1) * HEAD_DIM)

    def selected(c):
        x = sc_ref[c]
        kpos = c * ts + lax.broadcasted_iota(I32, (tq, ts), 1)
        return jnp.logical_or(x > vw, jnp.logical_and(x == vw, kpos <= cutw))

    c2 = scale * math.log2(math.e)
    kmax = jnp.sqrt(jnp.max(kn2_ref[...], axis=1, keepdims=True))
    bound = []
    for h in range(heads):
        qh = q_ref[:, head(h)].astype(F32)
        qn = jnp.sqrt(jnp.sum(qh * qh, axis=1, keepdims=True))
        bound.append(jnp.broadcast_to(qn * (kmax[h // group:h // group + 1, :] * c2), (tq, ts)))
    ones = jnp.ones((ts, HEAD_DIM), BF16)
    acc_ref[...] = jnp.zeros(acc_ref.shape, F32)

    def attend_bounded(c, _):
        keep = jnp.where(selected(c), 1.0, 0.0).astype(BF16)
        kc = kt_ref[c]
        vc = v_ref[c]
        for g in range(KV_HEADS):
            v_ones = jnp.concatenate([vc[:, head(g)], ones], axis=1)
            for h in range(g * group, (g + 1) * group):
                s = jnp.dot(q_ref[:, head(h)], kc[head(g), :], preferred_element_type=F32)
                p = jnp.exp2(s * c2 - bound[h]).astype(BF16) * keep
                acc_ref[h] += jnp.dot(p, v_ones, preferred_element_type=F32)
        return 0

    lax.fori_loop(0, nch, attend_bounded, 0)
    lmin = jnp.full((tq, HEAD_DIM), jnp.inf, F32)
    for h in range(heads):
        acc = acc_ref[h]
        lsum = acc[:, HEAD_DIM:]
        lmin = jnp.minimum(lmin, lsum)
        o_ref[:, head(h)] = (acc[:, :HEAD_DIM] / lsum).astype(o_ref.dtype)

    @pl.when(jnp.logical_not(jnp.min(lmin) > 1e-30))
    def _():
        m_ref[...] = jnp.full(m_ref.shape, -1e30, F32)
        l_ref[...] = jnp.zeros(l_ref.shape, F32)
        acc_ref[...] = jnp.zeros(acc_ref.shape, F32)

        def attend_online(c, _):
            bias = jnp.where(selected(c), 0.0, NEG_INF)
            kc = kt_ref[c]
            vc = v_ref[c]
            for h in range(heads):
                g = h // group
                s = jnp.dot(q_ref[:, head(h)], kc[head(g), :], preferred_element_type=F32) * scale + bias
                m_old = m_ref[h]
                m_new = jnp.maximum(m_old, jnp.max(s, axis=1, keepdims=True))
                alpha = jnp.exp(m_old - m_new)
                p = jnp.exp(s - m_new)
                l_ref[h] = l_ref[h] * alpha + jnp.sum(p, axis=1, keepdims=True)
                acc_ref[h, :, :HEAD_DIM] = acc_ref[h, :, :HEAD_DIM] * alpha + jnp.dot(
                    p.astype(BF16), vc[:, head(g)], preferred_element_type=F32)
                m_ref[h] = m_new
            return 0

        lax.fori_loop(0, nch, attend_online, 0)
        for h in range(heads):
            o_ref[:, head(h)] = (acc_ref[h, :, :HEAD_DIM] / l_ref[h]).astype(o_ref.dtype)


SAMPLE_TS = 4 * PAGE_SIZE


def _sample_attn_kernel(pt_ref, q_ref, qi_ref, wi_ref, kin_ref, kn_ref, vn_ref, cki_ref, ck_ref, cv_ref,
                        o_ref, kibuf, kbuf, vbuf, sc_ref, sem_i, sem_k, sem_v, *, n_pages, t_new, topk):
    b = pl.program_id(0)
    ts = SAMPLE_TS
    ppc = ts // PAGE_SIZE
    nch = n_pages // ppc
    past = n_pages * PAGE_SIZE
    rows = q_ref.shape[2]

    def ki_copy(p):
        return pltpu.make_async_copy(cki_ref.at[0, pt_ref[b, p]], kibuf.at[pl.ds(p * PAGE_SIZE, PAGE_SIZE)], sem_i.at[0])

    def kv_copies(c, slot):
        cps = []
        for j in range(ppc):
            page = pt_ref[b, c * ppc + j]
            dst = pl.ds(j * PAGE_SIZE, PAGE_SIZE)
            for g in range(KV_HEADS):
                cps.append(pltpu.make_async_copy(ck_ref.at[0, page, :, g, :], kbuf.at[slot, g, dst], sem_k.at[slot]))
                cps.append(pltpu.make_async_copy(cv_ref.at[0, page, :, g, :], vbuf.at[slot, g, dst], sem_v.at[slot]))
        return cps

    def start_ki(p, _):
        ki_copy(p).start()
        return 0

    def wait_ki(p, _):
        ki_copy(p).wait()
        return 0

    lax.fori_loop(0, n_pages, start_ki, 0)
    for cp in kv_copies(0, 0):
        cp.start()
    lax.fori_loop(0, n_pages, wait_ki, 0)

    wsc = jnp.broadcast_to((wi_ref[0] * IDX_HEADS ** -0.5) * IDX_DIM ** -0.5, (IDX_HEADS * rows, ts))
    qi = qi_ref[0]

    def index_scores(kc):
        lg = lax.dot_general(qi, kc, (((1,), (1,)), ((), ())), preferred_element_type=F32)
        weighted = jnp.maximum(lg, 0.0) * wsc
        acc = weighted[0:rows]
        for h in range(1, IDX_HEADS):
            acc = acc + weighted[h * rows:(h + 1) * rows]
        return acc

    def scores(c, carry):
        mn, mx = carry
        acc = index_scores(kibuf[pl.ds(pl.multiple_of(c * ts, ts), ts), :].astype(BF16))
        sc_ref[c] = acc
        return jnp.minimum(mn, _lane_fold(acc, jnp.minimum)), jnp.maximum(mx, _lane_fold(acc, jnp.maximum))

    mn, mx = lax.fori_loop(0, nch, scores,
                           (jnp.full((rows, 128), jnp.inf, F32), jnp.full((rows, 128), NEG_INF, F32)))
    acc = index_scores(kin_ref[0])
    tok = lax.broadcasted_iota(I32, (rows, ts), 0) % t_new
    col = lax.broadcasted_iota(I32, (rows, ts), 1)
    adm = jnp.logical_and(col <= tok, col < t_new)
    sc_ref[nch] = jnp.where(adm, acc, NEG_INF)
    mn = jnp.minimum(mn, _lane_fold(jnp.where(adm, acc, jnp.inf), jnp.minimum))
    mx = jnp.maximum(mx, _lane_fold(jnp.where(adm, acc, NEG_INF), jnp.maximum))
    lo0 = jnp.min(mn, axis=1, keepdims=True)
    hi0 = jnp.max(mx, axis=1, keepdims=True)
    skip = jnp.full((rows, 1), past + 1 <= topk)
    v, cut = _select_threshold(lambda c, r0, nr: sc_ref[c], nch + 1, ts, rows, topk, lo0, hi0, skip)
    vw = jnp.broadcast_to(v, (rows, ts))
    cutw = jnp.broadcast_to(cut, (rows, ts))
    scale = HEAD_DIM ** -0.5

    def bias_of(c):
        x = sc_ref[c]
        kpos = c * ts + lax.broadcasted_iota(I32, (rows, ts), 1)
        sel = jnp.logical_or(x > vw, jnp.logical_and(x == vw, kpos <= cutw))
        return jnp.where(sel, 0.0, NEG_INF)

    def flash(state, g, bias, kg, vg):
        m_old, l_old, a_old = state
        s = lax.dot_general(q_ref[0, g], kg, (((1,), (1,)), ((), ())), preferred_element_type=F32) * scale + bias
        m_new = jnp.maximum(m_old, jnp.max(s, axis=1, keepdims=True))
        alpha = jnp.exp(m_old - m_new)
        p = jnp.exp(s - m_new)
        l_new = l_old * alpha + jnp.sum(p, axis=1, keepdims=True)
        a_new = a_old * alpha + jnp.dot(p.astype(BF16), vg, preferred_element_type=F32)
        return m_new, l_new, a_new

    def attend(c, states):
        slot = c % 2
        for cp in kv_copies(c, slot):
            cp.wait()

        @pl.when(c + 1 < nch)
        def _():
            for cp in kv_copies(c + 1, 1 - slot):
                cp.start()

        bias = bias_of(c)
        return tuple(
            flash(states[g], g, bias, kbuf[slot, g].astype(BF16), vbuf[slot, g].astype(BF16))
            for g in range(KV_HEADS))

    init = (jnp.full((rows, 1), -1e30, F32), jnp.zeros((rows, 1), F32), jnp.zeros((rows, HEAD_DIM), F32))
    states = lax.fori_loop(0, nch, attend, (init,) * KV_HEADS)
    bias = bias_of(nch)
    for g in range(KV_HEADS):
        sl = slice(g * HEAD_DIM, (g + 1) * HEAD_DIM)
        _, l_g, a_g = flash(states[g], g, bias, kn_ref[0][:, sl], vn_ref[0][:, sl])
        o_ref[0, g] = (a_g / l_g).astype(o_ref.dtype)


def sample_attention(q, qi, wi, ki_new, k_new, v_new, cache_kidx, cache_k, cache_v, page_table, *, topk):
    nb, t_new, width = q.shape
    heads = width // HEAD_DIM
    group = heads // KV_HEADS
    n_pages = page_table.shape[1]
    ts = SAMPLE_TS
    assert n_pages % (ts // PAGE_SIZE) == 0
    nch = n_pages // (ts // PAGE_SIZE)
    rows = group * t_new
    qg = q.reshape(nb, t_new, KV_HEADS, group, HEAD_DIM).transpose(0, 2, 3, 1, 4).reshape(nb, KV_HEADS, rows, HEAD_DIM)
    qir = jnp.tile(qi.reshape(nb, t_new, IDX_HEADS, IDX_DIM).transpose(0, 2, 1, 3), (1, 1, group, 1))
    qir = qir.reshape(nb, IDX_HEADS * rows, IDX_DIM)
    wir = jnp.tile(wi.transpose(0, 2, 1), (1, 1, group)).reshape(nb, IDX_HEADS * rows, 1)
    padk = lambda a: jnp.pad(a, ((0, 0), (0, ts - t_new), (0, 0)))
    per_b = lambda a: pl.BlockSpec((1,) + a.shape[1:], lambda b, pt: (b,) + (0,) * (a.ndim - 1))
    ops = (qg, qir, wir, padk(ki_new), padk(k_new), padk(v_new))
    out = pl.pallas_call(
        functools.partial(_sample_attn_kernel, n_pages=n_pages, t_new=t_new, topk=topk),
        grid_spec=pltpu.PrefetchScalarGridSpec(
            num_scalar_prefetch=1,
            grid=(nb,),
            in_specs=[per_b(a) for a in ops] + [pl.BlockSpec(memory_space=pl.ANY)] * 3,
            out_specs=pl.BlockSpec((1, KV_HEADS, rows, HEAD_DIM), lambda b, pt: (b, 0, 0, 0)),
            scratch_shapes=[
                pltpu.VMEM((n_pages * PAGE_SIZE, IDX_DIM), F32),
                pltpu.VMEM((2, KV_HEADS, ts, HEAD_DIM), F32),
                pltpu.VMEM((2, KV_HEADS, ts, HEAD_DIM), F32),
                pltpu.VMEM((nch + 1, rows, ts), F32),
                pltpu.SemaphoreType.DMA((1,)),
                pltpu.SemaphoreType.DMA((2,)),
                pltpu.SemaphoreType.DMA((2,)),
            ]),
        out_shape=jax.ShapeDtypeStruct((nb, KV_HEADS, rows, HEAD_DIM), BF16),
        compiler_params=_cparams("arbitrary"),
        name="sample_attention",
    )(page_table, *ops, cache_kidx, cache_k, cache_v)
    return out.reshape(nb, KV_HEADS, group, t_new, HEAD_DIM).transpose(0, 3, 1, 2, 4).reshape(nb, t_new, width)


def prompt_attention(qb, qib, wi, kib, kb, vb, kn2, *, tq, ts, topk):
    s, width = qb.shape
    heads = width // HEAD_DIM
    assert s % tq == 0 and s % ts == 0
    nc = s // ts
    kit = kib.reshape(nc, ts, IDX_DIM).transpose(0, 2, 1)
    kt = kb.reshape(nc, ts, KV_HEADS * HEAD_DIM).transpose(0, 2, 1)
    v3 = vb.reshape(nc, ts, KV_HEADS * HEAD_DIM)
    whole = lambda a: pl.BlockSpec(a.shape, lambda i: (0,) * a.ndim, pipeline_mode=pl.Buffered(1))
    return pl.pallas_call(
        functools.partial(_prompt_attn_kernel, tq=tq, ts=ts, topk=topk),
        grid=(s // tq,),
        in_specs=[
            pl.BlockSpec((tq, width), lambda i: (i, 0)),
            pl.BlockSpec((tq, IDX_HEADS * IDX_DIM), lambda i: (i, 0)),
            pl.BlockSpec((tq, IDX_HEADS), lambda i: (i, 0)),
            whole(kit), whole(kt), whole(v3), whole(kn2),
        ],
        out_specs=pl.BlockSpec((tq, width), lambda i: (i, 0)),
        out_shape=jax.ShapeDtypeStruct((s, width), BF16),
        scratch_shapes=[
            pltpu.VMEM((nc, tq, ts), F32),
            pltpu.VMEM((heads, tq, 1), F32),
            pltpu.VMEM((heads, tq, 1), F32),
            pltpu.VMEM((heads, tq, 2 * HEAD_DIM), F32),
        ],
        compiler_params=_cparams("arbitrary"),
        name="prompt_attention",
    )(qb, qib, wi, kit, kt, v3, kn2)


def _row_tile(m, cap):
    t = cap
    while m % t:
        t //= 2
    return t


def _token_stages(x, attend, gdn, lw):
    (nmw, wp, w_out, nfw, wq, sub_keys, w_u, w_v, fw, tabs) = lw
    m = x.shape[0]
    tm = _row_tile(m, 512)
    p = norm_matmul(x, nmw, wp, tm=tm, tn=wp.shape[1] // 7)
    qb, qib, k32, kb, v32, vb, ki32, kib, auxr, kn2 = rope_split(p, *tabs, tm=tm)
    wi, a_pre = auxr[:, 0:IDX_HEADS], auxr[:, 2 * IDX_HEADS:3 * IDX_HEADS]
    ya = attend(qb, qib, wi, kib, kb, vb, kn2)
    yg, new_conv, new_ssm = gdn(p, a_pre)
    h, xn, qp = out_proj(ya, yg, x, w_out, nfw, wq, tm=tm)
    i1, i2, gate = peer_topk(qp, sub_keys, tm=_row_tile(m, 256))
    tok = lambda a: a.reshape(PEER_HEADS * PEER_TOPK, m).T
    gates = peer_gates(tok(i1), tok(i2), tok(gate), tmb=64)
    y = peer_dense(xn, w_u, w_v, gates, h, fw, tm=tm, ib=4)
    return y, k32, v32, ki32, new_conv, new_ssm


def kernel(x_prompt, x_sample, cache_k, cache_v, cache_kidx, page_table, state_conv, state_ssm, norm_mix_w, w_in,
           conv_w, a_log, dt_bias, gdn_norm_w, w_out, norm_ffn_w, peer_wq, peer_sub_keys, peer_u, peer_v,
           norm_final_w):
    depth = w_in.shape[0]
    b, s, d = x_prompt.shape
    nb, t, _ = x_sample.shape
    assert depth == 1 and b == 1, "single layer, single prompt sequence"
    past = page_table.shape[1] * PAGE_SIZE
    heads_g = state_ssm.shape[2]
    conv_ch = state_conv.shape[-1]

    wp = _pack_w_in(w_in[0])
    shared = (norm_mix_w[0][None], wp, w_out[0].astype(BF16), norm_ffn_w[0][None], peer_wq[0].astype(BF16),
              peer_sub_keys[0], peer_u[0].astype(BF16), peer_v[0].astype(BF16), norm_final_w[None])
    gdn_w = (conv_w[0], a_log[0], dt_bias[0], gdn_norm_w[0])

    def attend_p(qb, qib, wi, kib, kb, vb, kn2):
        return prompt_attention(qb, qib, wi, kib, kb, vb, kn2.T, tq=_row_tile(s, 256), ts=_row_tile(s, 512),
                                topk=min(TOPK_MAX, s // 4))

    def gdn_p(p, a_pre):
        conv0 = jnp.zeros((1, CONV_W - 1, conv_ch), F32)
        ssm0 = jnp.zeros((1, heads_g, GDN_DK, GDN_DV), F32)
        t_pad = -(-s // GDN_CHUNK) * GDN_CHUNK
        assert t_pad == s
        return gdn_mixer(p, a_pre.T[None], conv0, *gdn_w, ssm0, n_seq=1, t_pad=s, t_total=s, chunk=GDN_CHUNK)

    tabs_p = _rope_tables(jnp.arange(s))
    y_p, k_p, v_p, ki_p, conv_p, ssm_p = _token_stages(x_prompt.reshape(s, d), attend_p, gdn_p, shared + (tabs_p,))

    m_s = nb * t
    t8 = -(-t // 8) * 8

    def attend_s(qb, qib, wi, kib, kb, vb, kn2):
        seq = lambda a: a.reshape(nb, t, a.shape[-1])
        y = sample_attention(seq(qb), seq(qib), seq(wi), seq(kib), seq(kb), seq(vb), cache_kidx, cache_k, cache_v,
                             page_table, topk=min(TOPK_MAX, (past + t) // 4))
        return y.reshape(m_s, y.shape[-1])

    def gdn_s(p, a_pre):
        pad_t = lambda a: jnp.pad(a.reshape(nb, t, a.shape[-1]), ((0, 0), (0, t8 - t), (0, 0)))
        p8 = pad_t(p).reshape(nb * t8, p.shape[-1])
        at = pad_t(a_pre).transpose(0, 2, 1)
        yg, nconv, nssm = gdn_mixer(p8, at, state_conv[0], *gdn_w, state_ssm[0], n_seq=nb, t_pad=t8, t_total=t, chunk=8)
        return yg.reshape(nb, t8, yg.shape[-1])[:, :t].reshape(m_s, yg.shape[-1]), nconv, nssm

    tabs_s = tuple(jnp.tile(a, (nb, 1)) for a in _rope_tables(past + jnp.arange(t)))
    y_s, k_s, v_s, ki_s, conv_s, ssm_s = _token_stages(x_sample.reshape(m_s, d), attend_s, gdn_s, shared + (tabs_s,))

    kv = lambda a, n, tt: a.reshape(1, n, tt, KV_HEADS, HEAD_DIM)
    return (y_p.reshape(b, s, d), y_s.reshape(nb, t, d),
            kv(k_p, b, s), kv(v_p, b, s), ki_p.reshape(1, b, s, IDX_DIM), conv_p[None], ssm_p[None],
            kv(k_s, nb, t), kv(v_s, nb, t), ki_s.reshape(1, nb, t, IDX_DIM), conv_s[None], ssm_s[None])
```

```python
import functools
import math

import jax
import jax.numpy as jnp
from jax import lax
from jax.experimental import pallas as pl
from jax.experimental.pallas import tpu as pltpu

F32 = jnp.float32
BF16 = jnp.bfloat16
I32 = jnp.int32
EPS = 1e-6
NEG_INF = float("-inf")

HEAD_DIM = 128
KV_HEADS = 2
IDX_HEADS = 8
IDX_DIM = 64
TOPK_MAX = 256
ROPE_THETA = 10000.0
PAGE_SIZE = 128
GDN_DK = 128
GDN_DV = 128
CONV_W = 4
GDN_CHUNK = 64
PEER_HEADS = 8
PEER_KEYS = 128
PEER_QDIM = 128
PEER_TOPK = 16

VMEM_LIMIT_BYTES = 56 * 1024 * 1024


def _cparams(*sem):
    return pltpu.CompilerParams(dimension_semantics=sem, vmem_limit_bytes=VMEM_LIMIT_BYTES)


_PROJ_SRC = (("q", 1024), ("k", 256), ("v", 256), ("qi", 512), ("ki", 64), ("wi", 8), ("qkv", 3072), ("z", 1024),
             ("b", 8), ("a", 8))
_PROJ_DST = ("q", "qi", "k", "v", "qkv", "z", "ki", "wi", "b", "a")
PROJ_PACKED = 6272


def _pack_w_in_kernel(w_ref, o_ref):
    src, off = {}, 0
    for name, width in _PROJ_SRC:
        src[name] = (off, width)
        off += width
    dst = 0
    for name in _PROJ_DST:
        s0, width = src[name]
        o_ref[:, dst:dst + width] = w_ref[:, s0:s0 + width].astype(o_ref.dtype)
        dst += width
    o_ref[:, dst:] = jnp.zeros((o_ref.shape[0], o_ref.shape[1] - dst), o_ref.dtype)


def _pack_w_in(w_in):
    d, n = w_in.shape
    assert n == sum(w for _, w in _PROJ_SRC)
    tr = _row_tile(d, 256)
    return pl.pallas_call(
        _pack_w_in_kernel,
        grid=(d // tr,),
        in_specs=[pl.BlockSpec((tr, n), lambda i: (i, 0))],
        out_specs=pl.BlockSpec((tr, PROJ_PACKED), lambda i: (i, 0)),
        out_shape=jax.ShapeDtypeStruct((d, PROJ_PACKED), BF16),
        compiler_params=_cparams("arbitrary"),
        name="pack_w_in",
    )(w_in)


def _rope_tables(pos):
    def table(dim):
        half = dim // 2
        inv = ROPE_THETA ** (-jnp.arange(half, dtype=F32) / half)
        ang = pos.astype(F32)[:, None] * inv[None, :]
        cos = jnp.tile(jnp.cos(ang), (1, 128 // half))
        sin = jnp.tile(jnp.concatenate([-jnp.sin(ang), jnp.sin(ang)], axis=1), (1, 128 // dim))
        return cos, sin
    cq, sq = table(HEAD_DIM)
    ci, si = table(IDX_DIM)
    return cq, sq, ci, si


def _norm_matmul_kernel(x_ref, nw_ref, w_ref, o_ref):
    x = x_ref[...]
    y = x * lax.rsqrt(jnp.mean(x * x, axis=-1, keepdims=True) + EPS)
    xn = (y * nw_ref[...]).astype(BF16)
    o_ref[...] = jnp.dot(xn, w_ref[...], preferred_element_type=F32)


def norm_matmul(x, nw, w, *, tm, tn):
    m, d = x.shape
    n = w.shape[1]
    assert m % tm == 0 and n % tn == 0
    return pl.pallas_call(
        _norm_matmul_kernel,
        grid=(n // tn, m // tm),
        in_specs=[
            pl.BlockSpec((tm, d), lambda j, i: (i, 0)),
            pl.BlockSpec((1, d), lambda j, i: (0, 0)),
            pl.BlockSpec((d, tn), lambda j, i: (0, j)),
        ],
        out_specs=pl.BlockSpec((tm, tn), lambda j, i: (i, j)),
        out_shape=jax.ShapeDtypeStruct((m, n), F32),
        compiler_params=_cparams("arbitrary", "arbitrary"),
        name="norm_matmul",
    )(x, nw, w)


def _rope128(x, cos, sin_signed):
    return x * cos + pltpu.roll(x, 64, 1) * sin_signed


def _rope64(x, cos, sin_signed, first_half):
    partner = jnp.where(first_half, pltpu.roll(x, 96, 1), pltpu.roll(x, 32, 1))
    return x * cos + partner * sin_signed


def _rope_split_kernel(q_ref, qi_ref, k_ref, v_ref, aux_ref, cq_ref, sq_ref, ci_ref, si_ref,
                       qb_ref, qib_ref, k32_ref, kb_ref, v32_ref, vb_ref, ki32_ref, kib_ref, auxr_ref, kn2_ref):
    cq, sq, ci, si = cq_ref[...], sq_ref[...], ci_ref[...], si_ref[...]
    lane = lax.broadcasted_iota(I32, ci.shape, 1)
    first_half = (lane % IDX_DIM) < (IDX_DIM // 2)
    for h in range(q_ref.shape[1] // HEAD_DIM):
        sl = slice(h * HEAD_DIM, (h + 1) * HEAD_DIM)
        qb_ref[:, sl] = _rope128(q_ref[:, sl], cq, sq).astype(BF16)
    for h in range(k_ref.shape[1] // HEAD_DIM):
        sl = slice(h * HEAD_DIM, (h + 1) * HEAD_DIM)
        kr = _rope128(k_ref[:, sl], cq, sq)
        k32_ref[:, sl] = kr
        krb = kr.astype(BF16)
        kb_ref[:, sl] = krb
        kn2_ref[:, h:h + 1] = jnp.sum(krb.astype(F32) * krb.astype(F32), axis=1, keepdims=True)
    for h in range(qi_ref.shape[1] // 128):
        sl = slice(h * 128, (h + 1) * 128)
        qib_ref[:, sl] = _rope64(qi_ref[:, sl], ci, si, first_half).astype(BF16)
    v = v_ref[...]
    v32_ref[...] = v
    vb_ref[...] = v.astype(BF16)
    aux = aux_ref[...]
    kir = _rope64(aux, ci, si, first_half)[:, :IDX_DIM]
    ki32_ref[...] = kir
    kib_ref[...] = kir.astype(BF16)
    auxr_ref[...] = aux[:, IDX_DIM:]


def rope_split(p, cq, sq, ci, si, *, tm):
    m = p.shape[0]
    assert m % tm == 0
    row = lambda w, j: pl.BlockSpec((tm, w), lambda i, j=j: (i, j))
    outs = [
        ((m, 1024), BF16), ((m, 512), BF16), ((m, 256), F32), ((m, 256), BF16), ((m, 256), F32),
        ((m, 256), BF16), ((m, IDX_DIM), F32), ((m, IDX_DIM), BF16), ((m, 128 - IDX_DIM), F32),
        ((m, KV_HEADS), F32),
    ]
    return pl.pallas_call(
        _rope_split_kernel,
        grid=(m // tm,),
        in_specs=[row(1024, 0), row(512, 2), row(256, 6), row(256, 7), row(128, 48),
                  row(128, 0), row(128, 0), row(128, 0), row(128, 0)],
        out_specs=[pl.BlockSpec((tm, s[1]), lambda i: (i, 0)) for s, _ in outs],
        out_shape=[jax.ShapeDtypeStruct(s, dt) for s, dt in outs],
        compiler_params=_cparams("arbitrary"),
        name="rope_split",
    )(p, p, p, p, p, cq, sq, ci, si)


def _out_proj_kernel(ya_ref, yg_ref, res_ref, wa_ref, wg_ref, nw_ref, wq_ref, h_ref, xn_ref, qp_ref):
    h = res_ref[...] + jnp.dot(ya_ref[...], wa_ref[...], preferred_element_type=F32)
    h = h + jnp.dot(yg_ref[...], wg_ref[...], preferred_element_type=F32)
    h_ref[...] = h
    y = h * lax.rsqrt(jnp.mean(h * h, axis=-1, keepdims=True) + EPS)
    xn = (y * nw_ref[...]).astype(BF16)
    xn_ref[...] = xn
    qp_ref[...] = jnp.dot(xn, wq_ref[...], preferred_element_type=F32)


def out_proj(ya, yg, res, w_out, nw, wq, *, tm):
    m, d = res.shape
    half = ya.shape[1]
    nq = wq.shape[1]
    assert m % tm == 0
    rows = lambda wd: pl.BlockSpec((tm, wd), lambda i: (i, 0))
    const = lambda a: pl.BlockSpec(a.shape, lambda i: (0,) * a.ndim, pipeline_mode=pl.Buffered(1))
    wa, wg = w_out[:half], w_out[half:]
    return pl.pallas_call(
        _out_proj_kernel,
        grid=(m // tm,),
        in_specs=[rows(half), rows(half), rows(d), const(wa), const(wg), const(nw), const(wq)],
        out_specs=[rows(d), rows(d), rows(nq)],
        out_shape=[jax.ShapeDtypeStruct((m, d), F32), jax.ShapeDtypeStruct((m, d), BF16),
                   jax.ShapeDtypeStruct((m, nq), F32)],
        compiler_params=_cparams("arbitrary"),
        name="out_proj",
    )(ya, yg, res, wa, wg, nw, wq)


def _take_top(s, codes, count):
    big = jnp.int32(2 ** 30)
    vals, picks = [], []
    for _ in range(count):
        m = jnp.max(s, axis=0, keepdims=True)
        pick = jnp.min(jnp.where(s == m, codes, big), axis=0, keepdims=True)
        s = jnp.where(codes == pick, NEG_INF, s)
        vals.append(m)
        picks.append(pick)
    return jnp.concatenate(vals, axis=0), jnp.concatenate(picks, axis=0)


def _lookup(table, sel, count):
    out = jnp.zeros(sel.shape, table.dtype)
    for a in range(count):
        out = jnp.where(sel == a, jnp.broadcast_to(table[a:a + 1, :], sel.shape), out)
    return out


def _peer_topk_kernel(q_ref, sk_ref, i1_ref, i2_ref, g_ref):
    tm = q_ref.shape[0]
    kk = PEER_TOPK
    half = PEER_QDIM // 2
    key_codes = lax.broadcasted_iota(I32, (PEER_KEYS, tm), 0)
    top_v, top_i = [], []
    for c in range(2):
        qs = q_ref[:, c * half:(c + 1) * half].astype(BF16)
        s = lax.dot_general(sk_ref[c], qs, (((1,), (1,)), ((), ())), preferred_element_type=F32)
        vals, idx = _take_top(s, key_codes, kk)
        top_v.append(vals)
        top_i.append(idx)
    pieces, codes = [], []
    for a in range(kk):
        nb = kk // (a + 1)
        rows = -(-nb // 8) * 8
        r = lax.broadcasted_iota(I32, (rows, tm), 0)
        cand = top_v[0][a:a + 1, :] + top_v[1][0:rows, :]
        pieces.append(jnp.where(r < nb, cand, NEG_INF))
        codes.append(r + a * kk)
    best_s, best_c = _take_top(jnp.concatenate(pieces, axis=0), jnp.concatenate(codes, axis=0), kk)
    i1_ref[0] = _lookup(top_i[0], lax.shift_right_logical(best_c, 4), kk)
    i2_ref[0] = _lookup(top_i[1], jnp.bitwise_and(best_c, kk - 1), kk)
    e = jnp.exp(best_s - best_s[0:1, :])
    g_ref[0] = e / jnp.sum(e, axis=0, keepdims=True)


def peer_topk(qp, sub_keys, *, tm):
    m = qp.shape[0]
    heads = sub_keys.shape[0]
    assert m % tm == 0 and PEER_TOPK == 16
    sk = sub_keys.reshape(heads * 2, PEER_KEYS, PEER_QDIM // 2).astype(BF16)
    out = pl.BlockSpec((1, PEER_TOPK, tm), lambda i, h: (h, 0, i))
    return pl.pallas_call(
        _peer_topk_kernel,
        grid=(m // tm, heads),
        in_specs=[pl.BlockSpec((tm, PEER_QDIM), lambda i, h: (i, h)),
                  pl.BlockSpec((2, PEER_KEYS, PEER_QDIM // 2), lambda i, h: (h, 0, 0))],
        out_specs=[out, out, out],
        out_shape=[jax.ShapeDtypeStruct((heads, PEER_TOPK, m), I32), jax.ShapeDtypeStruct((heads, PEER_TOPK, m), I32),
                   jax.ShapeDtypeStruct((heads, PEER_TOPK, m), F32)],
        compiler_params=_cparams("arbitrary", "arbitrary"),
        name="peer_topk",
    )(qp, sk)


GATE_TOKENS = 16


def _peer_gate_kernel(i1_ref, i2_ref, g_ref, o_ref):
    tmb = i1_ref.shape[0]
    nk = PEER_KEYS
    sub = lax.broadcasted_iota(I32, (nk, i1_ref.shape[1]), 0)

    def token_block(j, _):
        t0 = pl.multiple_of(j * GATE_TOKENS, GATE_TOKENS)
        per_token = []
        for u in range(GATE_TOKENS):
            wide = lambda ref: jnp.broadcast_to(ref[pl.ds(t0 + u, 1), :], sub.shape)
            p1 = jnp.where(wide(i1_ref) == sub, wide(g_ref), 0.0).astype(BF16)
            p2 = jnp.where(wide(i2_ref) == sub, 1.0, 0.0).astype(BF16)
            gm = lax.dot_general(p1, p2, (((1,), (1,)), ((), ())), preferred_element_type=F32)
            per_token.append(gm.astype(o_ref.dtype))
        block = jnp.stack(per_token, axis=0)
        o_ref[:, pl.ds(t0, GATE_TOKENS), :] = pltpu.einshape("mab->amb", block)
        return 0

    lax.fori_loop(0, tmb // GATE_TOKENS, token_block, 0)


def peer_gates(i1, i2, gate, *, tmb):
    m, slots = i1.shape
    assert m % tmb == 0 and tmb % GATE_TOKENS == 0
    rows = pl.BlockSpec((tmb, slots), lambda i: (i, 0))
    return pl.pallas_call(
        _peer_gate_kernel,
        grid=(m // tmb,),
        in_specs=[rows, rows, rows],
        out_specs=pl.BlockSpec((PEER_KEYS, tmb, PEER_KEYS), lambda i: (0, i, 0)),
        out_shape=jax.ShapeDtypeStruct((PEER_KEYS, m, PEER_KEYS), BF16),
        compiler_params=_cparams("arbitrary"),
        name="peer_gates",
    )(i1, i2, gate)


def _gelu_tanh(x):
    return 0.5 * x * (1.0 + jnp.tanh(math.sqrt(2.0 / math.pi) * (x + 0.044715 * (x * x * x))))


def _peer_dense_kernel(xn_ref, wu_ref, wv_ref, g_ref, o_ref):
    j = pl.program_id(1)
    ib = g_ref.shape[0]
    a = lax.dot_general(xn_ref[...], wu_ref[...], (((1,), (1,)), ((), ())), preferred_element_type=F32)
    act = _gelu_tanh(a)
    hm = jnp.concatenate(
        [(g_ref[u].astype(F32) * act[:, u * PEER_KEYS:(u + 1) * PEER_KEYS]).astype(BF16) for u in range(ib)], axis=1)
    part = jnp.dot(hm, wv_ref[...], preferred_element_type=F32)

    @pl.when(j == 0)
    def _():
        o_ref[...] = part

    @pl.when(j > 0)
    def _():
        o_ref[...] += part


def peer_dense(xn, w_u, w_v, gates, *, tm, ib):
    m, d = xn.shape
    assert m % tm == 0 and PEER_KEYS % ib == 0
    eb = ib * PEER_KEYS
    return pl.pallas_call(
        _peer_dense_kernel,
        grid=(m // tm, PEER_KEYS // ib),
        in_specs=[
            pl.BlockSpec((tm, d), lambda i, j: (i, 0)),
            pl.BlockSpec((eb, d), lambda i, j: (j, 0)),
            pl.BlockSpec((eb, d), lambda i, j: (j, 0)),
            pl.BlockSpec((ib, tm, PEER_KEYS), lambda i, j: (j, i, 0)),
        ],
        out_specs=pl.BlockSpec((tm, d), lambda i, j: (i, 0)),
        out_shape=jax.ShapeDtypeStruct((m, d), F32),
        compiler_params=_cparams("arbitrary", "arbitrary"),
        name="peer_dense",
    )(xn, w_u, w_v, gates)


def _residual_norm_kernel(h_ref, f_ref, w_ref, o_ref):
    y = h_ref[...] + f_ref[...]
    o_ref[...] = y * lax.rsqrt(jnp.mean(y * y, axis=-1, keepdims=True) + EPS) * w_ref[...]


def residual_norm(h, f, w, *, tm):
    m, d = h.shape
    assert m % tm == 0
    rows = pl.BlockSpec((tm, d), lambda i: (i, 0))
    return pl.pallas_call(
        _residual_norm_kernel,
        grid=(m // tm,),
        in_specs=[rows, rows, pl.BlockSpec((1, d), lambda i: (0, 0))],
        out_specs=rows,
        out_shape=jax.ShapeDtypeStruct((m, d), F32),
        compiler_params=_cparams("arbitrary"),
        name="residual_norm",
    )(h, f, w)


_HI = lax.Precision.HIGHEST


def _dot_hi(a, b):
    return jnp.dot(a, b, preferred_element_type=F32, precision=_HI)


_BNN = (((2,), (1,)), ((0,), (0,)))
_BNT = (((2,), (2,)), ((0,), (0,)))
_BTN = (((1,), (1,)), ((0,), (0,)))


def _bdot(a, b, dims=_BNN):
    return lax.dot_general(a.astype(BF16), b.astype(BF16), dims, preferred_element_type=F32)


def _bdot3(a, b, dims=_BNN):
    ah, bh = a.astype(BF16), b.astype(BF16)
    al, bl = (a - ah.astype(F32)).astype(BF16), (b - bh.astype(F32)).astype(BF16)
    dot = lambda x, y: lax.dot_general(x, y, dims, preferred_element_type=F32)
    return dot(ah, bh) + (dot(ah, bl) + dot(al, bh))


def _sigmoid(x):
    return 1.0 / (1.0 + jnp.exp(-x))


def _softplus(x):
    return jnp.maximum(x, 0.0) + jnp.log1p(jnp.exp(-jnp.abs(x)))


def _gdn_kernel(xq_ref, xk_ref, xv_ref, z_ref, aux_ref, at_ref, cs_ref, cw_ref, alog_ref, dtb_ref,
                alogt_ref, dtbt_ref, nw_ref, s0_ref, y_ref, nconv_ref, nssm_ref, xp_ref, st_ref,
                *, chunk, t_total, b_lane, a_lane):
    c = pl.program_id(1)
    nchunks = pl.num_programs(1)
    heads = st_ref.shape[0]
    width = heads * GDN_DK
    tail = CONV_W - 1

    @pl.when(c == 0)
    def _():
        xp_ref[8 - tail:8, :] = cs_ref[0]
        st_ref[...] = s0_ref[0]

    xp_ref[8:8 + chunk, 0:width] = xq_ref[...]
    xp_ref[8:8 + chunk, width:2 * width] = xk_ref[...]
    xp_ref[8:8 + chunk, 2 * width:3 * width] = xv_ref[...]
    conv = cw_ref[tail:tail + 1, :] * xp_ref[8:8 + chunk, :]
    for j in range(tail):
        conv = conv + cw_ref[j:j + 1, :] * xp_ref[8 - tail + j:8 - tail + j + chunk, :]
    conv = conv * _sigmoid(conv)

    padded = t_total % chunk != 0
    ridx = c * chunk + lax.broadcasted_iota(I32, (chunk, 1), 0)
    rvalid = ridx < t_total
    cidx = c * chunk + lax.broadcasted_iota(I32, (1, chunk), 1)
    cvalid = cidx < t_total

    aux = aux_ref[...]
    beta = _sigmoid(aux[:, b_lane:b_lane + heads])
    g = -jnp.exp(alog_ref[...]) * _softplus(aux[:, a_lane:a_lane + heads] + dtb_ref[...])
    gt = -jnp.exp(alogt_ref[...]) * _softplus(at_ref[0] + dtbt_ref[...])
    if padded:
        beta = jnp.where(rvalid, beta, 0.0)
        g = jnp.where(rvalid, g, 0.0)
        gt = jnp.where(cvalid, gt, 0.0)
    ri = lax.broadcasted_iota(I32, (chunk, chunk), 0)
    ci = lax.broadcasted_iota(I32, (chunk, chunk), 1)
    causal = ri >= ci
    strict = ri > ci
    eye = jnp.where(ri == ci, 1.0, 0.0)
    gc = _dot_hi(jnp.where(causal, 1.0, 0.0), g)
    gct = _dot_hi(gt, jnp.where(ri <= ci, 1.0, 0.0))

    per_head = lambda x2d, base: jnp.stack(
        [x2d[:, base + h * GDN_DK:base + (h + 1) * GDN_DK] for h in range(heads)], axis=0)
    q = per_head(conv, 0)
    k = per_head(conv, width)
    v = per_head(conv, 2 * width)
    q = q * lax.rsqrt(jnp.sum(q * q, axis=-1, keepdims=True) + EPS) * GDN_DK ** -0.5
    k = k * lax.rsqrt(jnp.sum(k * k, axis=-1, keepdims=True) + EPS)
    if padded:
        k = jnp.where(rvalid[None], k, 0.0)
        v = jnp.where(rvalid[None], v, 0.0)
    gcol = jnp.stack([gc[:, h:h + 1] for h in range(heads)], axis=0)
    grow = jnp.stack([gct[h:h + 1, :] for h in range(heads)], axis=0)
    bcol = jnp.stack([beta[:, h:h + 1] for h in range(heads)], axis=0)
    decay = jnp.where(causal[None], jnp.exp(jnp.where(causal[None], gcol - grow, 0.0)), 0.0)
    kb = k * bcol
    vb = v * bcol
    x = -jnp.where(strict[None], _bdot(kb, k, _BNT) * decay, 0.0)
    tinv = eye[None] + x
    span = 2
    while span < chunk:
        x = _bdot3(x, x)
        tinv = tinv + _bdot3(tinv, x)
        span *= 2
    u = _bdot(tinv, vb)
    w = _bdot(tinv, kb * jnp.exp(gcol))
    intra = jnp.where(causal[None], _bdot(q, k, _BNT) * decay, 0.0)
    state = st_ref[...]
    v_new = u - _bdot(w, state)
    out = _bdot(q * jnp.exp(gcol), state) + _bdot(intra, v_new)
    glast = gcol[:, chunk - 1:chunk, :]
    st_ref[...] = state * jnp.exp(glast) + _bdot(k * jnp.exp(glast - gcol), v_new, _BTN)
    on = out * lax.rsqrt(jnp.mean(out * out, axis=-1, keepdims=True) + EPS) * nw_ref[...]
    for h in range(heads):
        sl = slice(h * GDN_DV, (h + 1) * GDN_DV)
        zh = z_ref[:, sl]
        y_ref[:, sl] = (on[h] * (zh * _sigmoid(zh))).astype(y_ref.dtype)

    last_valid = t_total - (t_total - 1) // chunk * chunk
    @pl.when(c < nchunks - 1)
    def _():
        xp_ref[8 - tail:8, :] = xp_ref[8 + chunk - tail:8 + chunk, :]

    @pl.when(c == nchunks - 1)
    def _():
        nconv_ref[0] = xp_ref[8 + last_valid - tail:8 + last_valid, :]
        nssm_ref[0] = st_ref[...]


def gdn_mixer(p, at, conv_state, conv_w, a_log, dt_bias, norm_w, ssm_state, *, n_seq, t_pad, t_total, chunk):
    heads = ssm_state.shape[1]
    width = heads * GDN_DK
    assert t_pad % chunk == 0 and chunk % 8 == 0
    nch = t_pad // chunk
    rows = lambda wd, j: pl.BlockSpec((chunk, wd), lambda n, c, j=j: (n * nch + c, j))
    const = lambda a: pl.BlockSpec(a.shape, lambda n, c: (0,) * a.ndim)
    alog = a_log.reshape(1, heads)
    dtb = dt_bias.reshape(1, heads)
    nw = norm_w.reshape(1, GDN_DV)
    at = at.reshape(n_seq, heads, nch, chunk).transpose(0, 2, 1, 3).reshape(n_seq * nch, heads, chunk)
    kern = functools.partial(_gdn_kernel, chunk=chunk, t_total=t_total, b_lane=IDX_DIM + 8, a_lane=IDX_DIM + 16)
    return pl.pallas_call(
        kern,
        grid=(n_seq, nch),
        in_specs=[
            rows(width, 2), rows(width, 3), rows(width, 4), rows(width, 5), rows(128, 48),
            pl.BlockSpec((1, heads, chunk), lambda n, c: (n * nch + c, 0, 0)),
            pl.BlockSpec((1, CONV_W - 1, 3 * width), lambda n, c: (n, 0, 0)),
            const(conv_w), const(alog), const(dtb), const(alog.T), const(dtb.T), const(nw),
            pl.BlockSpec((1, heads, GDN_DK, GDN_DV), lambda n, c: (n, 0, 0, 0)),
        ],
        out_specs=[
            pl.BlockSpec((chunk, width), lambda n, c: (n * nch + c, 0)),
            pl.BlockSpec((1, CONV_W - 1, 3 * width), lambda n, c: (n, 0, 0)),
            pl.BlockSpec((1, heads, GDN_DK, GDN_DV), lambda n, c: (n, 0, 0, 0)),
        ],
        out_shape=[
            jax.ShapeDtypeStruct((n_seq * t_pad, width), BF16),
            jax.ShapeDtypeStruct((n_seq, CONV_W - 1, 3 * width), F32),
            jax.ShapeDtypeStruct((n_seq, heads, GDN_DK, GDN_DV), F32),
        ],
        scratch_shapes=[pltpu.VMEM((8 + chunk, 3 * width), F32), pltpu.VMEM((heads, GDN_DK, GDN_DV), F32)],
        compiler_params=_cparams("arbitrary", "arbitrary"),
        name="gdn_mixer",
    )(p, p, p, p, p, at, conv_state, conv_w, alog, dtb, alog.T, dtb.T, nw, ssm_state)


def _lane_fold(x, op):
    out = x[:, :128]
    for j in range(1, x.shape[1] // 128):
        out = op(out, x[:, j * 128:(j + 1) * 128])
    return out


def _select_threshold(load, nch, nch_max, ts, rows, topk, lo0, hi0, skip):
    kf = float(topk)
    rb = min(rows, 128)
    assert rows % rb == 0
    assert nch_max * (ts // 128) <= 256
    lane = lax.broadcasted_iota(I32, (rb, 128), 1)
    ones = jnp.ones((128, 128), BF16)
    rep = lambda col: jnp.broadcast_to(col, (rows, 128))
    lo0, hi0, skip = rep(lo0), rep(hi0), rep(skip)

    def count(pred, *cols):
        parts = []
        for r0 in range(0, rows, rb):
            wide = [col[r0:r0 + rb] for col in cols]

            def body(c, acc, r0=r0, wide=wide):
                x = load(c, r0, rb)
                for j in range(ts // 128):
                    kpos = lane + (c * ts + j * 128)
                    acc = acc + jnp.where(pred(x[:, j * 128:(j + 1) * 128], kpos, *wide), 1.0, 0.0)
                return acc

            acc = lax.fori_loop(0, nch, body, jnp.zeros((rb, 128), F32))
            parts.append(jnp.dot(acc.astype(BF16), ones, preferred_element_type=F32))
        return parts[0] if len(parts) == 1 else jnp.concatenate(parts, axis=0)

    ge_ = lambda x, kpos, t: x >= t
    gt_ = lambda x, kpos, t: x > t
    eq_ = lambda x, kpos, t: x == t

    zero = jnp.zeros((rows, 128), F32)
    c_pos = count(gt_, zero)
    c_nn = count(ge_, zero)
    at_zero = jnp.logical_and(c_pos < kf, c_nn >= kf)
    lo0 = jnp.where(at_zero, 0.0, jnp.where(c_pos >= kf, jnp.maximum(lo0, 0.0), lo0))
    hi0 = jnp.where(at_zero, 0.0, jnp.where(c_nn < kf, jnp.minimum(hi0, 0.0), hi0))
    skip_search = jnp.logical_or(skip, at_zero)

    def v_cond(st):
        _, _, done, it = st
        return jnp.logical_and(jnp.min(done) < 0.5, it < 4096)

    def v_body(st):
        lo, hi, done, it = st
        mid = jnp.minimum(jnp.maximum(lo * 0.5 + hi * 0.5, lo), hi)
        cnt = count(ge_, mid)
        active = done < 0.5
        ge = cnt >= kf
        hit = cnt == kf
        collapsed = jnp.logical_or(mid <= lo, mid >= hi)
        new_lo = jnp.where(jnp.logical_and(active, ge), mid, lo)
        new_hi = jnp.where(jnp.logical_and(active, jnp.logical_and(jnp.logical_not(ge), jnp.logical_not(collapsed))), mid, hi)
        new_hi = jnp.where(jnp.logical_and(active, hit), mid, new_hi)
        new_done = jnp.where(jnp.logical_or(hit, collapsed), 1.0, done)
        return new_lo, new_hi, new_done, it + 1

    done0 = jnp.where(skip_search, 1.0, 0.0)
    lo, hi, _, _ = lax.while_loop(v_cond, v_body, (lo0, hi0, done0, jnp.int32(0)))
    cnt_hi = count(ge_, hi)
    v = jnp.where(cnt_hi >= kf, hi, lo)
    v = jnp.where(skip, NEG_INF, v)

    cgt = count(gt_, v)
    ceq = count(eq_, v)
    need = kf - cgt
    partial = jnp.logical_and(jnp.logical_not(skip), need < ceq)
    span = nch * ts

    def c_cond(st):
        lo_i, hi_i = st
        return jnp.max(jnp.where(partial, hi_i - lo_i, 0)) > 0

    def c_body(st):
        lo_i, hi_i = st
        mid = lax.shift_right_logical(lo_i + hi_i, 1)
        cnt = count(lambda x, kpos, t, m: jnp.logical_and(x == t, kpos <= m), v, mid)
        ok = cnt >= need
        return jnp.where(ok, lo_i, mid + 1), jnp.where(ok, mid, hi_i)

    zero_i = jnp.zeros((rows, 128), I32)
    _, cut = lax.while_loop(c_cond, c_body, (zero_i, zero_i + (span - 1)))
    cut = jnp.where(partial, cut, span)
    cut = jnp.where(skip, -1, cut)
    return v, cut


def _prompt_attn_kernel(q_ref, qi_ref, wi_ref, kit_ref, kt_ref, v_ref, kn2_ref, o_ref,
                        sc_ref, m_ref, l_ref, acc_ref, *, tq, ts, topk):
    i = pl.program_id(0)
    nch = ((i + 1) * tq + ts - 1) // ts
    row = i * tq + lax.broadcasted_iota(I32, (tq, 1), 0)
    heads = q_ref.shape[1] // HEAD_DIM
    wsc = (wi_ref[...] * IDX_HEADS ** -0.5) * IDX_DIM ** -0.5
    qi_h = [qi_ref[:, h * IDX_DIM:(h + 1) * IDX_DIM] for h in range(IDX_HEADS)]
    w_h = [jnp.broadcast_to(wsc[:, h:h + 1], (tq, ts)) for h in range(IDX_HEADS)]

    def scores(c, carry):
        mn, mx = carry
        kic = kit_ref[c]
        acc = jnp.zeros((tq, ts), F32)
        for h in range(IDX_HEADS):
            lg = jnp.dot(qi_h[h], kic, preferred_element_type=F32)
            acc = acc + jnp.maximum(lg, 0.0) * w_h[h]
        kpos = c * ts + lax.broadcasted_iota(I32, (tq, ts), 1)
        adm = kpos <= row
        sc_ref[c] = jnp.where(adm, acc, NEG_INF)
        mn = jnp.minimum(mn, _lane_fold(jnp.where(adm, acc, jnp.inf), jnp.minimum))
        mx = jnp.maximum(mx, _lane_fold(jnp.where(adm, acc, NEG_INF), jnp.maximum))
        return mn, mx

    mn, mx = lax.fori_loop(0, nch, scores,
                           (jnp.full((tq, 128), jnp.inf, F32), jnp.full((tq, 128), NEG_INF, F32)))
    lo0 = jnp.min(mn, axis=1, keepdims=True)
    hi0 = jnp.max(mx, axis=1, keepdims=True)
    skip = row < topk
    v, cut = _select_threshold(lambda c, r0, nr: sc_ref[c, r0:r0 + nr, :], nch, sc_ref.shape[0], ts, tq, topk,
                               lo0, hi0, skip)

    scale = HEAD_DIM ** -0.5
    group = heads // KV_HEADS
    vw = jnp.concatenate([v] * (ts // 128), axis=1)
    cutw = jnp.concatenate([cut] * (ts // 128), axis=1)
    head = lambda h: slice(h * HEAD_DIM, (h + 1) * HEAD_DIM)

    def selected(c):
        x = sc_ref[c]
        kpos = c * ts + lax.broadcasted_iota(I32, (tq, ts), 1)
        return jnp.logical_or(x > vw, jnp.logical_and(x == vw, kpos <= cutw))

    c2 = scale * math.log2(math.e)
    kmax = jnp.sqrt(jnp.max(kn2_ref[...], axis=1, keepdims=True))
    bound = []
    for h in range(heads):
        qh = q_ref[:, head(h)].astype(F32)
        qn = jnp.sqrt(jnp.sum(qh * qh, axis=1, keepdims=True))
        bound.append(jnp.broadcast_to(qn * (kmax[h // group:h // group + 1, :] * c2), (tq, ts)))
    ones = jnp.ones((ts, HEAD_DIM), BF16)
    acc_ref[...] = jnp.zeros(acc_ref.shape, F32)

    def attend_bounded(c, _):
        keep = jnp.where(selected(c), 1.0, 0.0).astype(BF16)
        kc = kt_ref[c]
        vc = v_ref[c]
        for g in range(KV_HEADS):
            v_ones = jnp.concatenate([vc[:, head(g)], ones], axis=1)
            for h in range(g * group, (g + 1) * group):
                s = jnp.dot(q_ref[:, head(h)], kc[head(g), :], preferred_element_type=F32)
                p = jnp.exp2(s * c2 - bound[h]).astype(BF16) * keep
                acc_ref[h] += jnp.dot(p, v_ones, preferred_element_type=F32)
        return 0

    lax.fori_loop(0, nch, attend_bounded, 0)
    lmin = jnp.full((tq, HEAD_DIM), jnp.inf, F32)
    for h in range(heads):
        acc = acc_ref[h]
        lsum = acc[:, HEAD_DIM:]
        lmin = jnp.minimum(lmin, lsum)
        o_ref[:, head(h)] = (acc[:, :HEAD_DIM] / lsum).astype(o_ref.dtype)

    @pl.when(jnp.logical_not(jnp.min(lmin) > 1e-30))
    def _():
        m_ref[...] = jnp.full(m_ref.shape, -1e30, F32)
        l_ref[...] = jnp.zeros(l_ref.shape, F32)
        acc_ref[...] = jnp.zeros(acc_ref.shape, F32)

        def attend_online(c, _):
            bias = jnp.where(selected(c), 0.0, NEG_INF)
            kc = kt_ref[c]
            vc = v_ref[c]
            for h in range(heads):
                g = h // group
                s = jnp.dot(q_ref[:, head(h)], kc[head(g), :], preferred_element_type=F32) * scale + bias
                m_old = m_ref[h]
                m_new = jnp.maximum(m_old, jnp.max(s, axis=1, keepdims=True))
                alpha = jnp.exp(m_old - m_new)
                p = jnp.exp(s - m_new)
                l_ref[h] = l_ref[h] * alpha + jnp.sum(p, axis=1, keepdims=True)
                acc_ref[h, :, :HEAD_DIM] = acc_ref[h, :, :HEAD_DIM] * alpha + jnp.dot(
                    p.astype(BF16), vc[:, head(g)], preferred_element_type=F32)
                m_ref[h] = m_new
            return 0

        lax.fori_loop(0, nch, attend_online, 0)
        for h in range(heads):
            o_ref[:, head(h)] = (acc_ref[h, :, :HEAD_DIM] / l_ref[h]).astype(o_ref.dtype)


SAMPLE_TS = 4 * PAGE_SIZE
SAMPLE_RING = 8


def _sample_attn_kernel(pt_ref, q_ref, qi_ref, wi_ref, kin_ref, kn_ref, vn_ref, cki_ref, ck_ref, cv_ref,
                        o_ref, kibuf, kbuf, vbuf, sc_ref, sem_i, sem_k, sem_v, *, n_pages, t_new, topk):
    b = pl.program_id(0)
    ts = SAMPLE_TS
    ppc = ts // PAGE_SIZE
    nch = n_pages // ppc
    past = n_pages * PAGE_SIZE
    rows = q_ref.shape[2]

    def ki_copy(p):
        return pltpu.make_async_copy(cki_ref.at[0, pt_ref[b, p]], kibuf.at[pl.ds(p * PAGE_SIZE, PAGE_SIZE)], sem_i.at[0])

    def kv_copies(c, slot):
        cps = []
        for j in range(ppc):
            page = pt_ref[b, c * ppc + j]
            dst = pl.ds(j * PAGE_SIZE, PAGE_SIZE)
            for g in range(KV_HEADS):
                cps.append(pltpu.make_async_copy(ck_ref.at[0, page, :, g, :], kbuf.at[slot, g, dst], sem_k.at[slot]))
                cps.append(pltpu.make_async_copy(cv_ref.at[0, page, :, g, :], vbuf.at[slot, g, dst], sem_v.at[slot]))
        return cps

    def start_ki(p, _):
        ki_copy(p).start()
        return 0

    def wait_ki(p, _):
        ki_copy(p).wait()
        return 0

    lax.fori_loop(0, n_pages, start_ki, 0)
    ring = kbuf.shape[0]
    for c0 in range(min(ring - 1, nch)):
        for cp in kv_copies(c0, c0):
            cp.start()
    lax.fori_loop(0, n_pages, wait_ki, 0)

    wsc = jnp.broadcast_to((wi_ref[0] * IDX_HEADS ** -0.5) * IDX_DIM ** -0.5, (IDX_HEADS * rows, ts))
    qi = qi_ref[0]

    def index_scores(kc):
        lg = lax.dot_general(qi, kc, (((1,), (1,)), ((), ())), preferred_element_type=F32)
        weighted = jnp.maximum(lg, 0.0) * wsc
        acc = weighted[0:rows]
        for h in range(1, IDX_HEADS):
            acc = acc + weighted[h * rows:(h + 1) * rows]
        return acc

    def scores(c, carry):
        mn, mx = carry
        acc = index_scores(kibuf[pl.ds(pl.multiple_of(c * ts, ts), ts), :].astype(BF16))
        sc_ref[c] = acc
        return jnp.minimum(mn, _lane_fold(acc, jnp.minimum)), jnp.maximum(mx, _lane_fold(acc, jnp.maximum))

    mn, mx = lax.fori_loop(0, nch, scores,
                           (jnp.full((rows, 128), jnp.inf, F32), jnp.full((rows, 128), NEG_INF, F32)))
    acc = index_scores(kin_ref[0])
    tok = lax.broadcasted_iota(I32, (rows, ts), 0) % t_new
    col = lax.broadcasted_iota(I32, (rows, ts), 1)
    adm = jnp.logical_and(col <= tok, col < t_new)
    sc_ref[nch] = jnp.where(adm, acc, NEG_INF)
    mn = jnp.minimum(mn, _lane_fold(jnp.where(adm, acc, jnp.inf), jnp.minimum))
    mx = jnp.maximum(mx, _lane_fold(jnp.where(adm, acc, NEG_INF), jnp.maximum))
    lo0 = jnp.min(mn, axis=1, keepdims=True)
    hi0 = jnp.max(mx, axis=1, keepdims=True)
    skip = jnp.full((rows, 1), past + 1 <= topk)
    v, cut = _select_threshold(lambda c, r0, nr: sc_ref[c], nch + 1, nch + 1, ts, rows, topk, lo0, hi0, skip)
    vw = jnp.concatenate([v] * (ts // 128), axis=1)
    cutw = jnp.concatenate([cut] * (ts // 128), axis=1)
    scale = HEAD_DIM ** -0.5

    def bias_of(c):
        x = sc_ref[c]
        kpos = c * ts + lax.broadcasted_iota(I32, (rows, ts), 1)
        sel = jnp.logical_or(x > vw, jnp.logical_and(x == vw, kpos <= cutw))
        return jnp.where(sel, 0.0, NEG_INF)

    def flash(state, g, bias, kg, vg):
        m_old, l_old, a_old = state
        s = lax.dot_general(q_ref[0, g], kg, (((1,), (1,)), ((), ())), preferred_element_type=F32) * scale + bias
        m_new = jnp.maximum(m_old, jnp.max(s, axis=1, keepdims=True))
        alpha = jnp.exp(m_old - m_new)
        p = jnp.exp(s - m_new)
        l_new = l_old * alpha + jnp.sum(p, axis=1, keepdims=True)
        a_new = a_old * alpha + jnp.dot(p.astype(BF16), vg, preferred_element_type=F32)
        return m_new, l_new, a_new

    def attend(c, states):
        slot = lax.rem(c, ring)
        for cp in kv_copies(c, slot):
            cp.wait()

        @pl.when(c + ring - 1 < nch)
        def _():
            for cp in kv_copies(c + ring - 1, lax.rem(c + ring - 1, ring)):
                cp.start()

        bias = bias_of(c)
        return tuple(
            flash(states[g], g, bias, kbuf[slot, g].astype(BF16), vbuf[slot, g].astype(BF16))
            for g in range(KV_HEADS))

    init = (jnp.full((rows, 1), -1e30, F32), jnp.zeros((rows, 1), F32), jnp.zeros((rows, HEAD_DIM), F32))
    states = lax.fori_loop(0, nch, attend, (init,) * KV_HEADS)
    bias = bias_of(nch)
    for g in range(KV_HEADS):
        sl = slice(g * HEAD_DIM, (g + 1) * HEAD_DIM)
        _, l_g, a_g = flash(states[g], g, bias, kn_ref[0][:, sl], vn_ref[0][:, sl])
        o_ref[0, g] = (a_g / l_g).astype(o_ref.dtype)


def sample_attention(q, qi, wi, ki_new, k_new, v_new, cache_kidx, cache_k, cache_v, page_table, *, topk):
    nb, t_new, width = q.shape
    heads = width // HEAD_DIM
    group = heads // KV_HEADS
    n_pages = page_table.shape[1]
    ts = SAMPLE_TS
    assert n_pages % (ts // PAGE_SIZE) == 0
    nch = n_pages // (ts // PAGE_SIZE)
    rows = group * t_new
    qg = q.reshape(nb, t_new, KV_HEADS, group, HEAD_DIM).transpose(0, 2, 3, 1, 4).reshape(nb, KV_HEADS, rows, HEAD_DIM)
    qir = jnp.tile(qi.reshape(nb, t_new, IDX_HEADS, IDX_DIM).transpose(0, 2, 1, 3), (1, 1, group, 1))
    qir = qir.reshape(nb, IDX_HEADS * rows, IDX_DIM)
    wir = jnp.tile(wi.transpose(0, 2, 1), (1, 1, group)).reshape(nb, IDX_HEADS * rows, 1)
    padk = lambda a: jnp.pad(a, ((0, 0), (0, ts - t_new), (0, 0)))
    per_b = lambda a: pl.BlockSpec((1,) + a.shape[1:], lambda b, pt: (b,) + (0,) * (a.ndim - 1))
    ops = (qg, qir, wir, padk(ki_new), padk(k_new), padk(v_new))
    out = pl.pallas_call(
        functools.partial(_sample_attn_kernel, n_pages=n_pages, t_new=t_new, topk=topk),
        grid_spec=pltpu.PrefetchScalarGridSpec(
            num_scalar_prefetch=1,
            grid=(nb,),
            in_specs=[per_b(a) for a in ops] + [pl.BlockSpec(memory_space=pl.ANY)] * 3,
            out_specs=pl.BlockSpec((1, KV_HEADS, rows, HEAD_DIM), lambda b, pt: (b, 0, 0, 0)),
            scratch_shapes=[
                pltpu.VMEM((n_pages * PAGE_SIZE, IDX_DIM), F32),
                pltpu.VMEM((SAMPLE_RING, KV_HEADS, ts, HEAD_DIM), F32),
                pltpu.VMEM((SAMPLE_RING, KV_HEADS, ts, HEAD_DIM), F32),
                pltpu.VMEM((nch + 1, rows, ts), F32),
                pltpu.SemaphoreType.DMA((1,)),
                pltpu.SemaphoreType.DMA((SAMPLE_RING,)),
                pltpu.SemaphoreType.DMA((SAMPLE_RING,)),
            ]),
        out_shape=jax.ShapeDtypeStruct((nb, KV_HEADS, rows, HEAD_DIM), BF16),
        compiler_params=_cparams("arbitrary"),
        name="sample_attention",
    )(page_table, *ops, cache_kidx, cache_k, cache_v)
    return out.reshape(nb, KV_HEADS, group, t_new, HEAD_DIM).transpose(0, 3, 1, 2, 4).reshape(nb, t_new, width)


def prompt_attention(qb, qib, wi, kib, kb, vb, kn2, *, tq, ts, topk):
    s, width = qb.shape
    heads = width // HEAD_DIM
    assert s % tq == 0 and s % ts == 0
    nc = s // ts
    kit = kib.reshape(nc, ts, IDX_DIM).transpose(0, 2, 1)
    kt = kb.reshape(nc, ts, KV_HEADS * HEAD_DIM).transpose(0, 2, 1)
    v3 = vb.reshape(nc, ts, KV_HEADS * HEAD_DIM)
    whole = lambda a: pl.BlockSpec(a.shape, lambda i: (0,) * a.ndim, pipeline_mode=pl.Buffered(1))
    return pl.pallas_call(
        functools.partial(_prompt_attn_kernel, tq=tq, ts=ts, topk=topk),
        grid=(s // tq,),
        in_specs=[
            pl.BlockSpec((tq, width), lambda i: (i, 0)),
            pl.BlockSpec((tq, IDX_HEADS * IDX_DIM), lambda i: (i, 0)),
            pl.BlockSpec((tq, IDX_HEADS), lambda i: (i, 0)),
            whole(kit), whole(kt), whole(v3), whole(kn2),
        ],
        out_specs=pl.BlockSpec((tq, width), lambda i: (i, 0)),
        out_shape=jax.ShapeDtypeStruct((s, width), BF16),
        scratch_shapes=[
            pltpu.VMEM((nc, tq, ts), F32),
            pltpu.VMEM((heads, tq, 1), F32),
            pltpu.VMEM((heads, tq, 1), F32),
            pltpu.VMEM((heads, tq, 2 * HEAD_DIM), F32),
        ],
        compiler_params=_cparams("arbitrary"),
        name="prompt_attention",
    )(qb, qib, wi, kit, kt, v3, kn2)


def _row_tile(m, cap):
    t = cap
    while m % t:
        t //= 2
    return t


def _token_stages(x, attend, gdn, lw):
    (nmw, wp, w_out, nfw, wq, sub_keys, w_u, w_v, fw, tabs) = lw
    m = x.shape[0]
    tm = _row_tile(m, 512)
    p = norm_matmul(x, nmw, wp, tm=tm, tn=wp.shape[1] // 7)
    qb, qib, k32, kb, v32, vb, ki32, kib, auxr, kn2 = rope_split(p, *tabs, tm=tm)
    wi, a_pre = auxr[:, 0:IDX_HEADS], auxr[:, 2 * IDX_HEADS:3 * IDX_HEADS]
    ya = attend(qb, qib, wi, kib, kb, vb, kn2)
    yg, new_conv, new_ssm = gdn(p, a_pre)
    h, xn, qp = out_proj(ya, yg, x, w_out, nfw, wq, tm=tm)
    i1, i2, gate = peer_topk(qp, sub_keys, tm=_row_tile(m, 256))
    tok = lambda a: a.reshape(PEER_HEADS * PEER_TOPK, m).T
    gates = peer_gates(tok(i1), tok(i2), tok(gate), tmb=64)
    f = peer_dense(xn, w_u, w_v, gates, tm=_row_tile(m, 1024), ib=8)
    y = residual_norm(h, f, fw, tm=tm)
    return y, k32, v32, ki32, new_conv, new_ssm


def kernel(x_prompt, x_sample, cache_k, cache_v, cache_kidx, page_table, state_conv, state_ssm, norm_mix_w, w_in,
           conv_w, a_log, dt_bias, gdn_norm_w, w_out, norm_ffn_w, peer_wq, peer_sub_keys, peer_u, peer_v,
           norm_final_w):
    depth = w_in.shape[0]
    b, s, d = x_prompt.shape
    nb, t, _ = x_sample.shape
    assert depth == 1 and b == 1, "single layer, single prompt sequence"
    past = page_table.shape[1] * PAGE_SIZE
    heads_g = state_ssm.shape[2]
    conv_ch = state_conv.shape[-1]

    wp = _pack_w_in(w_in[0])
    shared = (norm_mix_w[0][None], wp, w_out[0].astype(BF16), norm_ffn_w[0][None], peer_wq[0].astype(BF16),
              peer_sub_keys[0], peer_u[0].astype(BF16), peer_v[0].astype(BF16), norm_final_w[None])
    gdn_w = (conv_w[0], a_log[0], dt_bias[0], gdn_norm_w[0])

    def attend_p(qb, qib, wi, kib, kb, vb, kn2):
        return prompt_attention(qb, qib, wi, kib, kb, vb, kn2.T, tq=_row_tile(s, 256), ts=_row_tile(s, 512),
                                topk=min(TOPK_MAX, s // 4))

    def gdn_p(p, a_pre):
        conv0 = jnp.zeros((1, CONV_W - 1, conv_ch), F32)
        ssm0 = jnp.zeros((1, heads_g, GDN_DK, GDN_DV), F32)
        t_pad = -(-s // GDN_CHUNK) * GDN_CHUNK
        assert t_pad == s
        return gdn_mixer(p, a_pre.T[None], conv0, *gdn_w, ssm0, n_seq=1, t_pad=s, t_total=s, chunk=GDN_CHUNK)

    tabs_p = _rope_tables(jnp.arange(s))
    y_p, k_p, v_p, ki_p, conv_p, ssm_p = _token_stages(x_prompt.reshape(s, d), attend_p, gdn_p, shared + (tabs_p,))

    m_s = nb * t
    t8 = -(-t // 8) * 8

    def attend_s(qb, qib, wi, kib, kb, vb, kn2):
        seq = lambda a: a.reshape(nb, t, a.shape[-1])
        y = sample_attention(seq(qb), seq(qib), seq(wi), seq(kib), seq(kb), seq(vb), cache_kidx, cache_k, cache_v,
                             page_table, topk=min(TOPK_MAX, (past + t) // 4))
        return y.reshape(m_s, y.shape[-1])

    def gdn_s(p, a_pre):
        pad_t = lambda a: jnp.pad(a.reshape(nb, t, a.shape[-1]), ((0, 0), (0, t8 - t), (0, 0)))
        p8 = pad_t(p).reshape(nb * t8, p.shape[-1])
        at = pad_t(a_pre).transpose(0, 2, 1)
        yg, nconv, nssm = gdn_mixer(p8, at, state_conv[0], *gdn_w, state_ssm[0], n_seq=nb, t_pad=t8, t_total=t, chunk=8)
        return yg.reshape(nb, t8, yg.shape[-1])[:, :t].reshape(m_s, yg.shape[-1]), nconv, nssm

    tabs_s = tuple(jnp.tile(a, (nb, 1)) for a in _rope_tables(past + jnp.arange(t)))
    y_s, k_s, v_s, ki_s, conv_s, ssm_s = _token_stages(x_sample.reshape(m_s, d), attend_s, gdn_s, shared + (tabs_s,))

    kv = lambda a, n, tt: a.reshape(1, n, tt, KV_HEADS, HEAD_DIM)
    return (y_p.reshape(b, s, d), y_s.reshape(nb, t, d),
            kv(k_p, b, s), kv(v_p, b, s), ki_p.reshape(1, b, s, IDX_DIM), conv_p[None], ssm_p[None],
            kv(k_s, nb, t), kv(v_s, nb, t), ki_s.reshape(1, nb, t, IDX_DIM), conv_s[None], ssm_s[None])
```

```python
import functools
import math

import jax
import jax.numpy as jnp
from jax import lax
from jax.experimental import pallas as pl
from jax.experimental.pallas import tpu as pltpu

F32 = jnp.float32
BF16 = jnp.bfloat16
I32 = jnp.int32
EPS = 1e-6
NEG_INF = float("-inf")

HEAD_DIM = 128
KV_HEADS = 2
IDX_HEADS = 8
IDX_DIM = 64
TOPK_MAX = 256
ROPE_THETA = 10000.0
PAGE_SIZE = 128
GDN_DK = 128
GDN_DV = 128
CONV_W = 4
GDN_CHUNK = 64
PEER_HEADS = 8
PEER_KEYS = 128
PEER_QDIM = 128
PEER_TOPK = 16

VMEM_LIMIT_BYTES = 56 * 1024 * 1024


def _cparams(*sem):
    return pltpu.CompilerParams(dimension_semantics=sem, vmem_limit_bytes=VMEM_LIMIT_BYTES)


_PROJ_SRC = (("q", 1024), ("k", 256), ("v", 256), ("qi", 512), ("ki", 64), ("wi", 8), ("qkv", 3072), ("z", 1024),
             ("b", 8), ("a", 8))
_PROJ_DST = ("q", "qi", "k", "v", "qkv", "z", "ki", "wi", "b", "a")
PROJ_PACKED = 6272


def _pack_w_in_kernel(w_ref, o_ref):
    src, off = {}, 0
    for name, width in _PROJ_SRC:
        src[name] = (off, width)
        off += width
    dst = 0
    for name in _PROJ_DST:
        s0, width = src[name]
        o_ref[:, dst:dst + width] = w_ref[:, s0:s0 + width].astype(o_ref.dtype)
        dst += width
    o_ref[:, dst:] = jnp.zeros((o_ref.shape[0], o_ref.shape[1] - dst), o_ref.dtype)


def _pack_w_in(w_in):
    d, n = w_in.shape
    assert n == sum(w for _, w in _PROJ_SRC)
    tr = _row_tile(d, 256)
    return pl.pallas_call(
        _pack_w_in_kernel,
        grid=(d // tr,),
        in_specs=[pl.BlockSpec((tr, n), lambda i: (i, 0))],
        out_specs=pl.BlockSpec((tr, PROJ_PACKED), lambda i: (i, 0)),
        out_shape=jax.ShapeDtypeStruct((d, PROJ_PACKED), BF16),
        compiler_params=_cparams("arbitrary"),
        name="pack_w_in",
    )(w_in)


def _rope_tables(pos):
    def table(dim):
        half = dim // 2
        inv = ROPE_THETA ** (-jnp.arange(half, dtype=F32) / half)
        ang = pos.astype(F32)[:, None] * inv[None, :]
        cos = jnp.tile(jnp.cos(ang), (1, 128 // half))
        sin = jnp.tile(jnp.concatenate([-jnp.sin(ang), jnp.sin(ang)], axis=1), (1, 128 // dim))
        return cos, sin
    cq, sq = table(HEAD_DIM)
    ci, si = table(IDX_DIM)
    return cq, sq, ci, si


def _norm_matmul_kernel(x_ref, nw_ref, w_ref, o_ref):
    x = x_ref[...]
    y = x * lax.rsqrt(jnp.mean(x * x, axis=-1, keepdims=True) + EPS)
    xn = (y * nw_ref[...]).astype(BF16)
    o_ref[...] = jnp.dot(xn, w_ref[...], preferred_element_type=F32)


def norm_matmul(x, nw, w, *, tm, tn):
    m, d = x.shape
    n = w.shape[1]
    assert m % tm == 0 and n % tn == 0
    return pl.pallas_call(
        _norm_matmul_kernel,
        grid=(n // tn, m // tm),
        in_specs=[
            pl.BlockSpec((tm, d), lambda j, i: (i, 0)),
            pl.BlockSpec((1, d), lambda j, i: (0, 0)),
            pl.BlockSpec((d, tn), lambda j, i: (0, j)),
        ],
        out_specs=pl.BlockSpec((tm, tn), lambda j, i: (i, j)),
        out_shape=jax.ShapeDtypeStruct((m, n), F32),
        compiler_params=_cparams("arbitrary", "arbitrary"),
        name="norm_matmul",
    )(x, nw, w)


def _rope128(x, cos, sin_signed):
    return x * cos + pltpu.roll(x, 64, 1) * sin_signed


def _rope64(x, cos, sin_signed, first_half):
    partner = jnp.where(first_half, pltpu.roll(x, 96, 1), pltpu.roll(x, 32, 1))
    return x * cos + partner * sin_signed


def _rope_split_kernel(q_ref, qi_ref, k_ref, v_ref, aux_ref, cq_ref, sq_ref, ci_ref, si_ref,
                       qb_ref, qib_ref, k32_ref, kb_ref, v32_ref, vb_ref, ki32_ref, kib_ref, auxr_ref, kn2_ref):
    cq, sq, ci, si = cq_ref[...], sq_ref[...], ci_ref[...], si_ref[...]
    lane = lax.broadcasted_iota(I32, ci.shape, 1)
    first_half = (lane % IDX_DIM) < (IDX_DIM // 2)
    for h in range(q_ref.shape[1] // HEAD_DIM):
        sl = slice(h * HEAD_DIM, (h + 1) * HEAD_DIM)
        qb_ref[:, sl] = _rope128(q_ref[:, sl], cq, sq).astype(BF16)
    for h in range(k_ref.shape[1] // HEAD_DIM):
        sl = slice(h * HEAD_DIM, (h + 1) * HEAD_DIM)
        kr = _rope128(k_ref[:, sl], cq, sq)
        k32_ref[:, sl] = kr
        krb = kr.astype(BF16)
        kb_ref[:, sl] = krb
        kn2_ref[:, h:h + 1] = jnp.sum(krb.astype(F32) * krb.astype(F32), axis=1, keepdims=True)
    for h in range(qi_ref.shape[1] // 128):
        sl = slice(h * 128, (h + 1) * 128)
        qib_ref[:, sl] = _rope64(qi_ref[:, sl], ci, si, first_half).astype(BF16)
    v = v_ref[...]
    v32_ref[...] = v
    vb_ref[...] = v.astype(BF16)
    aux = aux_ref[...]
    kir = _rope64(aux, ci, si, first_half)[:, :IDX_DIM]
    ki32_ref[...] = kir
    kib_ref[...] = kir.astype(BF16)
    auxr_ref[...] = aux[:, IDX_DIM:]


def rope_split(p, cq, sq, ci, si, *, tm):
    m = p.shape[0]
    assert m % tm == 0
    row = lambda w, j: pl.BlockSpec((tm, w), lambda i, j=j: (i, j))
    outs = [
        ((m, 1024), BF16), ((m, 512), BF16), ((m, 256), F32), ((m, 256), BF16), ((m, 256), F32),
        ((m, 256), BF16), ((m, IDX_DIM), F32), ((m, IDX_DIM), BF16), ((m, 128 - IDX_DIM), F32),
        ((m, KV_HEADS), F32),
    ]
    return pl.pallas_call(
        _rope_split_kernel,
        grid=(m // tm,),
        in_specs=[row(1024, 0), row(512, 2), row(256, 6), row(256, 7), row(128, 48),
                  row(128, 0), row(128, 0), row(128, 0), row(128, 0)],
        out_specs=[pl.BlockSpec((tm, s[1]), lambda i: (i, 0)) for s, _ in outs],
        out_shape=[jax.ShapeDtypeStruct(s, dt) for s, dt in outs],
        compiler_params=_cparams("arbitrary"),
        name="rope_split",
    )(p, p, p, p, p, cq, sq, ci, si)


def _out_proj_kernel(ya_ref, yg_ref, res_ref, wa_ref, wg_ref, nw_ref, wq_ref, h_ref, xn_ref, qp_ref):
    h = res_ref[...] + jnp.dot(ya_ref[...], wa_ref[...], preferred_element_type=F32)
    h = h + jnp.dot(yg_ref[...], wg_ref[...], preferred_element_type=F32)
    h_ref[...] = h
    y = h * lax.rsqrt(jnp.mean(h * h, axis=-1, keepdims=True) + EPS)
    xn = (y * nw_ref[...]).astype(BF16)
    xn_ref[...] = xn
    qp_ref[...] = jnp.dot(xn, wq_ref[...], preferred_element_type=F32)


def out_proj(ya, yg, res, w_out, nw, wq, *, tm):
    m, d = res.shape
    half = ya.shape[1]
    nq = wq.shape[1]
    assert m % tm == 0
    rows = lambda wd: pl.BlockSpec((tm, wd), lambda i: (i, 0))
    const = lambda a: pl.BlockSpec(a.shape, lambda i: (0,) * a.ndim, pipeline_mode=pl.Buffered(1))
    wa, wg = w_out[:half], w_out[half:]
    return pl.pallas_call(
        _out_proj_kernel,
        grid=(m // tm,),
        in_specs=[rows(half), rows(half), rows(d), const(wa), const(wg), const(nw), const(wq)],
        out_specs=[rows(d), rows(d), rows(nq)],
        out_shape=[jax.ShapeDtypeStruct((m, d), F32), jax.ShapeDtypeStruct((m, d), BF16),
                   jax.ShapeDtypeStruct((m, nq), F32)],
        compiler_params=_cparams("arbitrary"),
        name="out_proj",
    )(ya, yg, res, wa, wg, nw, wq)


def _take_top(s, codes, count):
    big = jnp.int32(2 ** 30)
    vals, picks = [], []
    for _ in range(count):
        m = jnp.max(s, axis=0, keepdims=True)
        pick = jnp.min(jnp.where(s == m, codes, big), axis=0, keepdims=True)
        s = jnp.where(codes == pick, NEG_INF, s)
        vals.append(m)
        picks.append(pick)
    return jnp.concatenate(vals, axis=0), jnp.concatenate(picks, axis=0)


def _lookup(table, sel, count):
    out = jnp.zeros(sel.shape, table.dtype)
    for a in range(count):
        out = jnp.where(sel == a, jnp.broadcast_to(table[a:a + 1, :], sel.shape), out)
    return out


def _peer_topk_kernel(q_ref, sk_ref, i1_ref, i2_ref, g_ref):
    tm = q_ref.shape[0]
    kk = PEER_TOPK
    half = PEER_QDIM // 2
    key_codes = lax.broadcasted_iota(I32, (PEER_KEYS, tm), 0)
    top_v, top_i = [], []
    for c in range(2):
        qs = q_ref[:, c * half:(c + 1) * half].astype(BF16)
        s = lax.dot_general(sk_ref[c], qs, (((1,), (1,)), ((), ())), preferred_element_type=F32)
        vals, idx = _take_top(s, key_codes, kk)
        top_v.append(vals)
        top_i.append(idx)
    pieces, codes = [], []
    for a in range(kk):
        nb = kk // (a + 1)
        rows = -(-nb // 8) * 8
        r = lax.broadcasted_iota(I32, (rows, tm), 0)
        cand = top_v[0][a:a + 1, :] + top_v[1][0:rows, :]
        pieces.append(jnp.where(r < nb, cand, NEG_INF))
        codes.append(r + a * kk)
    best_s, best_c = _take_top(jnp.concatenate(pieces, axis=0), jnp.concatenate(codes, axis=0), kk)
    i1_ref[0] = _lookup(top_i[0], lax.shift_right_logical(best_c, 4), kk)
    i2_ref[0] = _lookup(top_i[1], jnp.bitwise_and(best_c, kk - 1), kk)
    e = jnp.exp(best_s - best_s[0:1, :])
    g_ref[0] = e / jnp.sum(e, axis=0, keepdims=True)


def peer_topk(qp, sub_keys, *, tm):
    m = qp.shape[0]
    heads = sub_keys.shape[0]
    assert m % tm == 0 and PEER_TOPK == 16
    sk = sub_keys.reshape(heads * 2, PEER_KEYS, PEER_QDIM // 2).astype(BF16)
    out = pl.BlockSpec((1, PEER_TOPK, tm), lambda i, h: (h, 0, i))
    return pl.pallas_call(
        _peer_topk_kernel,
        grid=(m // tm, heads),
        in_specs=[pl.BlockSpec((tm, PEER_QDIM), lambda i, h: (i, h)),
                  pl.BlockSpec((2, PEER_KEYS, PEER_QDIM // 2), lambda i, h: (h, 0, 0))],
        out_specs=[out, out, out],
        out_shape=[jax.ShapeDtypeStruct((heads, PEER_TOPK, m), I32), jax.ShapeDtypeStruct((heads, PEER_TOPK, m), I32),
                   jax.ShapeDtypeStruct((heads, PEER_TOPK, m), F32)],
        compiler_params=_cparams("arbitrary", "arbitrary"),
        name="peer_topk",
    )(qp, sk)


GATE_TOKENS = 16


def _peer_gate_kernel(i1_ref, i2_ref, g_ref, o_ref):
    tmb = i1_ref.shape[0]
    nk = PEER_KEYS
    sub = lax.broadcasted_iota(I32, (nk, i1_ref.shape[1]), 0)

    def token_block(j, _):
        t0 = pl.multiple_of(j * GATE_TOKENS, GATE_TOKENS)
        per_token = []
        for u in range(GATE_TOKENS):
            wide = lambda ref: jnp.broadcast_to(ref[pl.ds(t0 + u, 1), :], sub.shape)
            p1 = jnp.where(wide(i1_ref) == sub, wide(g_ref), 0.0).astype(BF16)
            p2 = jnp.where(wide(i2_ref) == sub, 1.0, 0.0).astype(BF16)
            gm = lax.dot_general(p1, p2, (((1,), (1,)), ((), ())), preferred_element_type=F32)
            per_token.append(gm.astype(o_ref.dtype))
        block = jnp.stack(per_token, axis=0)
        o_ref[:, pl.ds(t0, GATE_TOKENS), :] = pltpu.einshape("mab->amb", block)
        return 0

    lax.fori_loop(0, tmb // GATE_TOKENS, token_block, 0)


def peer_gates(i1, i2, gate, *, tmb):
    m, slots = i1.shape
    assert m % tmb == 0 and tmb % GATE_TOKENS == 0
    rows = pl.BlockSpec((tmb, slots), lambda i: (i, 0))
    return pl.pallas_call(
        _peer_gate_kernel,
        grid=(m // tmb,),
        in_specs=[rows, rows, rows],
        out_specs=pl.BlockSpec((PEER_KEYS, tmb, PEER_KEYS), lambda i: (0, i, 0)),
        out_shape=jax.ShapeDtypeStruct((PEER_KEYS, m, PEER_KEYS), BF16),
        compiler_params=_cparams("arbitrary"),
        name="peer_gates",
    )(i1, i2, gate)


def _gelu_tanh(x):
    return 0.5 * x * (1.0 + jnp.tanh(math.sqrt(2.0 / math.pi) * (x + 0.044715 * (x * x * x))))


def _peer_dense_kernel(xn_ref, wu_ref, wv_ref, g_ref, o_ref):
    j = pl.program_id(1)
    ib = g_ref.shape[0]
    a = lax.dot_general(xn_ref[...], wu_ref[...], (((1,), (1,)), ((), ())), preferred_element_type=F32)
    act = _gelu_tanh(a)
    hm = jnp.concatenate(
        [(g_ref[u].astype(F32) * act[:, u * PEER_KEYS:(u + 1) * PEER_KEYS]).astype(BF16) for u in range(ib)], axis=1)
    part = jnp.dot(hm, wv_ref[...], preferred_element_type=F32)

    @pl.when(j == 0)
    def _():
        o_ref[...] = part

    @pl.when(j > 0)
    def _():
        o_ref[...] += part


def peer_dense(xn, w_u, w_v, gates, *, tm, ib):
    m, d = xn.shape
    assert m % tm == 0 and PEER_KEYS % ib == 0
    eb = ib * PEER_KEYS
    return pl.pallas_call(
        _peer_dense_kernel,
        grid=(m // tm, PEER_KEYS // ib),
        in_specs=[
            pl.BlockSpec((tm, d), lambda i, j: (i, 0)),
            pl.BlockSpec((eb, d), lambda i, j: (j, 0)),
            pl.BlockSpec((eb, d), lambda i, j: (j, 0)),
            pl.BlockSpec((ib, tm, PEER_KEYS), lambda i, j: (j, i, 0)),
        ],
        out_specs=pl.BlockSpec((tm, d), lambda i, j: (i, 0)),
        out_shape=jax.ShapeDtypeStruct((m, d), F32),
        compiler_params=_cparams("arbitrary", "arbitrary"),
        name="peer_dense",
    )(xn, w_u, w_v, gates)


def _residual_norm_kernel(h_ref, f_ref, w_ref, o_ref):
    y = h_ref[...] + f_ref[...]
    o_ref[...] = y * lax.rsqrt(jnp.mean(y * y, axis=-1, keepdims=True) + EPS) * w_ref[...]


def residual_norm(h, f, w, *, tm):
    m, d = h.shape
    assert m % tm == 0
    rows = pl.BlockSpec((tm, d), lambda i: (i, 0))
    return pl.pallas_call(
        _residual_norm_kernel,
        grid=(m // tm,),
        in_specs=[rows, rows, pl.BlockSpec((1, d), lambda i: (0, 0))],
        out_specs=rows,
        out_shape=jax.ShapeDtypeStruct((m, d), F32),
        compiler_params=_cparams("arbitrary"),
        name="residual_norm",
    )(h, f, w)


_HI = lax.Precision.HIGHEST


def _dot_hi(a, b):
    return jnp.dot(a, b, preferred_element_type=F32, precision=_HI)


_BNN = (((2,), (1,)), ((0,), (0,)))
_BNT = (((2,), (2,)), ((0,), (0,)))
_BTN = (((1,), (1,)), ((0,), (0,)))


def _bdot(a, b, dims=_BNN):
    return lax.dot_general(a.astype(BF16), b.astype(BF16), dims, preferred_element_type=F32)


def _bdot3(a, b, dims=_BNN):
    ah, bh = a.astype(BF16), b.astype(BF16)
    al, bl = (a - ah.astype(F32)).astype(BF16), (b - bh.astype(F32)).astype(BF16)
    dot = lambda x, y: lax.dot_general(x, y, dims, preferred_element_type=F32)
    return dot(ah, bh) + (dot(ah, bl) + dot(al, bh))


def _sigmoid(x):
    return 1.0 / (1.0 + jnp.exp(-x))


def _softplus(x):
    return jnp.maximum(x, 0.0) + jnp.log1p(jnp.exp(-jnp.abs(x)))


def _gdn_kernel(xq_ref, xk_ref, xv_ref, z_ref, aux_ref, at_ref, cs_ref, cw_ref, alog_ref, dtb_ref,
                alogt_ref, dtbt_ref, nw_ref, s0_ref, y_ref, nconv_ref, nssm_ref, xp_ref, st_ref,
                *, chunk, t_total, b_lane, a_lane):
    c = pl.program_id(1)
    nchunks = pl.num_programs(1)
    heads = st_ref.shape[0]
    width = heads * GDN_DK
    tail = CONV_W - 1

    @pl.when(c == 0)
    def _():
        xp_ref[8 - tail:8, :] = cs_ref[0]
        st_ref[...] = s0_ref[0]

    xp_ref[8:8 + chunk, 0:width] = xq_ref[...]
    xp_ref[8:8 + chunk, width:2 * width] = xk_ref[...]
    xp_ref[8:8 + chunk, 2 * width:3 * width] = xv_ref[...]
    conv = cw_ref[tail:tail + 1, :] * xp_ref[8:8 + chunk, :]
    for j in range(tail):
        conv = conv + cw_ref[j:j + 1, :] * xp_ref[8 - tail + j:8 - tail + j + chunk, :]
    conv = conv * _sigmoid(conv)

    padded = t_total % chunk != 0
    ridx = c * chunk + lax.broadcasted_iota(I32, (chunk, 1), 0)
    rvalid = ridx < t_total
    cidx = c * chunk + lax.broadcasted_iota(I32, (1, chunk), 1)
    cvalid = cidx < t_total

    aux = aux_ref[...]
    beta = _sigmoid(aux[:, b_lane:b_lane + heads])
    g = -jnp.exp(alog_ref[...]) * _softplus(aux[:, a_lane:a_lane + heads] + dtb_ref[...])
    gt = -jnp.exp(alogt_ref[...]) * _softplus(at_ref[0] + dtbt_ref[...])
    if padded:
        beta = jnp.where(rvalid, beta, 0.0)
        g = jnp.where(rvalid, g, 0.0)
        gt = jnp.where(cvalid, gt, 0.0)
    ri = lax.broadcasted_iota(I32, (chunk, chunk), 0)
    ci = lax.broadcasted_iota(I32, (chunk, chunk), 1)
    causal = ri >= ci
    strict = ri > ci
    eye = jnp.where(ri == ci, 1.0, 0.0)
    gc = _dot_hi(jnp.where(causal, 1.0, 0.0), g)
    gct = _dot_hi(gt, jnp.where(ri <= ci, 1.0, 0.0))

    per_head = lambda x2d, base: jnp.stack(
        [x2d[:, base + h * GDN_DK:base + (h + 1) * GDN_DK] for h in range(heads)], axis=0)
    q = per_head(conv, 0)
    k = per_head(conv, width)
    v = per_head(conv, 2 * width)
    q = q * lax.rsqrt(jnp.sum(q * q, axis=-1, keepdims=True) + EPS) * GDN_DK ** -0.5
    k = k * lax.rsqrt(jnp.sum(k * k, axis=-1, keepdims=True) + EPS)
    if padded:
        k = jnp.where(rvalid[None], k, 0.0)
        v = jnp.where(rvalid[None], v, 0.0)
    gcol = jnp.stack([gc[:, h:h + 1] for h in range(heads)], axis=0)
    grow = jnp.stack([gct[h:h + 1, :] for h in range(heads)], axis=0)
    bcol = jnp.stack([beta[:, h:h + 1] for h in range(heads)], axis=0)
    decay = jnp.where(causal[None], jnp.exp(jnp.where(causal[None], gcol - grow, 0.0)), 0.0)
    kb = k * bcol
    vb = v * bcol
    x = -jnp.where(strict[None], _bdot(kb, k, _BNT) * decay, 0.0)
    tinv = eye[None] + x
    span = 2
    while span < chunk:
        x = _bdot3(x, x)
        tinv = tinv + _bdot3(tinv, x)
        span *= 2
    u = _bdot(tinv, vb)
    w = _bdot(tinv, kb * jnp.exp(gcol))
    intra = jnp.where(causal[None], _bdot(q, k, _BNT) * decay, 0.0)
    state = st_ref[...]
    v_new = u - _bdot(w, state)
    out = _bdot(q * jnp.exp(gcol), state) + _bdot(intra, v_new)
    glast = gcol[:, chunk - 1:chunk, :]
    st_ref[...] = state * jnp.exp(glast) + _bdot(k * jnp.exp(glast - gcol), v_new, _BTN)
    on = out * lax.rsqrt(jnp.mean(out * out, axis=-1, keepdims=True) + EPS) * nw_ref[...]
    for h in range(heads):
        sl = slice(h * GDN_DV, (h + 1) * GDN_DV)
        zh = z_ref[:, sl]
        y_ref[:, sl] = (on[h] * (zh * _sigmoid(zh))).astype(y_ref.dtype)

    last_valid = t_total - (t_total - 1) // chunk * chunk
    @pl.when(c < nchunks - 1)
    def _():
        xp_ref[8 - tail:8, :] = xp_ref[8 + chunk - tail:8 + chunk, :]

    @pl.when(c == nchunks - 1)
    def _():
        nconv_ref[0] = xp_ref[8 + last_valid - tail:8 + last_valid, :]
        nssm_ref[0] = st_ref[...]


def gdn_mixer(p, at, conv_state, conv_w, a_log, dt_bias, norm_w, ssm_state, *, n_seq, t_pad, t_total, chunk):
    heads = ssm_state.shape[1]
    width = heads * GDN_DK
    assert t_pad % chunk == 0 and chunk % 8 == 0
    nch = t_pad // chunk
    rows = lambda wd, j: pl.BlockSpec((chunk, wd), lambda n, c, j=j: (n * nch + c, j))
    const = lambda a: pl.BlockSpec(a.shape, lambda n, c: (0,) * a.ndim)
    alog = a_log.reshape(1, heads)
    dtb = dt_bias.reshape(1, heads)
    nw = norm_w.reshape(1, GDN_DV)
    at = at.reshape(n_seq, heads, nch, chunk).transpose(0, 2, 1, 3).reshape(n_seq * nch, heads, chunk)
    kern = functools.partial(_gdn_kernel, chunk=chunk, t_total=t_total, b_lane=IDX_DIM + 8, a_lane=IDX_DIM + 16)
    return pl.pallas_call(
        kern,
        grid=(n_seq, nch),
        in_specs=[
            rows(width, 2), rows(width, 3), rows(width, 4), rows(width, 5), rows(128, 48),
            pl.BlockSpec((1, heads, chunk), lambda n, c: (n * nch + c, 0, 0)),
            pl.BlockSpec((1, CONV_W - 1, 3 * width), lambda n, c: (n, 0, 0)),
            const(conv_w), const(alog), const(dtb), const(alog.T), const(dtb.T), const(nw),
            pl.BlockSpec((1, heads, GDN_DK, GDN_DV), lambda n, c: (n, 0, 0, 0)),
        ],
        out_specs=[
            pl.BlockSpec((chunk, width), lambda n, c: (n * nch + c, 0)),
            pl.BlockSpec((1, CONV_W - 1, 3 * width), lambda n, c: (n, 0, 0)),
            pl.BlockSpec((1, heads, GDN_DK, GDN_DV), lambda n, c: (n, 0, 0, 0)),
        ],
        out_shape=[
            jax.ShapeDtypeStruct((n_seq * t_pad, width), BF16),
            jax.ShapeDtypeStruct((n_seq, CONV_W - 1, 3 * width), F32),
            jax.ShapeDtypeStruct((n_seq, heads, GDN_DK, GDN_DV), F32),
        ],
        scratch_shapes=[pltpu.VMEM((8 + chunk, 3 * width), F32), pltpu.VMEM((heads, GDN_DK, GDN_DV), F32)],
        compiler_params=_cparams("arbitrary", "arbitrary"),
        name="gdn_mixer",
    )(p, p, p, p, p, at, conv_state, conv_w, alog, dtb, alog.T, dtb.T, nw, ssm_state)


def _lane_fold(x, op):
    out = x[:, :128]
    for j in range(1, x.shape[1] // 128):
        out = op(out, x[:, j * 128:(j + 1) * 128])
    return out


def _select_threshold(load, nch, nch_max, ts, rows, topk, lo0, hi0, skip):
    kf = float(topk)
    rb = min(rows, 128)
    assert rows % rb == 0
    assert nch_max * (ts // 128) <= 256
    lane = lax.broadcasted_iota(I32, (rb, 128), 1)
    on_lanes = rb == 128
    if on_lanes:
        state_shape = (8, rows)
        ones = jnp.ones((8, 128), BF16)
        to_state = lambda blk: jnp.transpose(jnp.broadcast_to(blk, (128, 128)))[0:8]
        to_rows = lambda st, r0: jnp.transpose(jnp.broadcast_to(st[0:1, r0:r0 + 128], (128, 128)))
        row_sums = lambda acc: lax.dot_general(ones, acc.astype(BF16), (((1,), (1,)), ((), ())),
                                               preferred_element_type=F32)
        join_axis = 1
    else:
        state_shape = (rows, 128)
        ones = jnp.ones((128, 128), BF16)
        to_state = lambda blk: jnp.broadcast_to(blk, (rb, 128))
        to_rows = lambda st, r0: st[r0:r0 + rb]
        row_sums = lambda acc: jnp.dot(acc.astype(BF16), ones, preferred_element_type=F32)
        join_axis = 0
    join = lambda parts: parts[0] if len(parts) == 1 else jnp.concatenate(parts, axis=join_axis)
    rep = lambda col: join([to_state(col[r0:r0 + rb].astype(F32)) for r0 in range(0, rows, rb)])
    lo0, hi0, skip = rep(lo0), rep(hi0), rep(skip) > 0.5

    def count(pred, *cols):
        blocks = range(0, rows, rb)
        wides = [[to_rows(col, r0) for col in cols] for r0 in blocks]
        accs = []
        for r0, wide in zip(blocks, wides):
            def body(c, acc, r0=r0, wide=wide):
                x = load(c, r0, rb)
                for j in range(ts // 128):
                    kpos = lane + (c * ts + j * 128)
                    acc = acc + jnp.where(pred(x[:, j * 128:(j + 1) * 128], kpos, *wide), 1.0, 0.0)
                return acc

            accs.append(lax.fori_loop(0, nch, body, jnp.zeros((rb, 128), F32)))
        return join([row_sums(acc) for acc in accs])

    ge_ = lambda x, kpos, t: x >= t
    gt_ = lambda x, kpos, t: x > t
    eq_ = lambda x, kpos, t: x == t

    zero = jnp.zeros(state_shape, F32)
    c_pos = count(gt_, zero)
    c_nn = count(ge_, zero)
    at_zero = jnp.logical_and(c_pos < kf, c_nn >= kf)
    lo0 = jnp.where(at_zero, 0.0, jnp.where(c_pos >= kf, jnp.maximum(lo0, 0.0), lo0))
    hi0 = jnp.where(at_zero, 0.0, jnp.where(c_nn < kf, jnp.minimum(hi0, 0.0), hi0))
    skip_search = jnp.logical_or(skip, at_zero)

    def v_cond(st):
        _, _, done, it = st
        return jnp.logical_and(jnp.min(done) < 0.5, it < 4096)

    def v_body(st):
        lo, hi, done, it = st
        mid = jnp.minimum(jnp.maximum(lo * 0.5 + hi * 0.5, lo), hi)
        cnt = count(ge_, mid)
        active = done < 0.5
        ge = cnt >= kf
        hit = cnt == kf
        collapsed = jnp.logical_or(mid <= lo, mid >= hi)
        new_lo = jnp.where(jnp.logical_and(active, ge), mid, lo)
        new_hi = jnp.where(jnp.logical_and(active, jnp.logical_and(jnp.logical_not(ge), jnp.logical_not(collapsed))), mid, hi)
        new_hi = jnp.where(jnp.logical_and(active, hit), mid, new_hi)
        new_done = jnp.where(jnp.logical_or(hit, collapsed), 1.0, done)
        return new_lo, new_hi, new_done, it + 1

    done0 = jnp.where(skip_search, 1.0, 0.0)
    lo, hi, _, _ = lax.while_loop(v_cond, v_body, (lo0, hi0, done0, jnp.int32(0)))
    cnt_hi = count(ge_, hi)
    v = jnp.where(cnt_hi >= kf, hi, lo)
    v = jnp.where(skip, NEG_INF, v)

    cgt = count(gt_, v)
    ceq = count(eq_, v)
    need = kf - cgt
    partial = jnp.logical_and(jnp.logical_not(skip), need < ceq)
    span = (nch * ts).astype(F32) if hasattr(nch, "astype") else float(nch * ts)

    def c_cond(st):
        lo_i, hi_i = st
        return jnp.max(jnp.where(partial, hi_i - lo_i, 0.0)) > 0.0

    def c_body(st):
        lo_i, hi_i = st
        mid = jnp.floor((lo_i + hi_i) * 0.5)
        cnt = count(lambda x, kpos, t, m: jnp.logical_and(x == t, kpos.astype(F32) <= m), v, mid)
        ok = cnt >= need
        return jnp.where(ok, lo_i, mid + 1.0), jnp.where(ok, mid, hi_i)

    _, cut = lax.while_loop(c_cond, c_body, (zero, zero + (span - 1.0)))
    cut = jnp.where(partial, cut, span)
    cut = jnp.where(skip, -1.0, cut)
    rows_of = lambda st: jnp.concatenate([to_rows(st, r0) for r0 in range(0, rows, rb)], axis=0)
    return rows_of(v), rows_of(cut).astype(I32)


def _prompt_attn_kernel(q_ref, qi_ref, wi_ref, kit_ref, kt_ref, v_ref, kn2_ref, o_ref,
                        sc_ref, m_ref, l_ref, acc_ref, *, tq, ts, topk):
    i = pl.program_id(0)
    nch = ((i + 1) * tq + ts - 1) // ts
    row = i * tq + lax.broadcasted_iota(I32, (tq, 1), 0)
    heads = q_ref.shape[1] // HEAD_DIM
    wsc = (wi_ref[...] * IDX_HEADS ** -0.5) * IDX_DIM ** -0.5
    qi_h = [qi_ref[:, h * IDX_DIM:(h + 1) * IDX_DIM] for h in range(IDX_HEADS)]
    w_h = [jnp.broadcast_to(wsc[:, h:h + 1], (tq, ts)) for h in range(IDX_HEADS)]

    def scores(c, carry):
        mn, mx = carry
        kic = kit_ref[c]
        acc = jnp.zeros((tq, ts), F32)
        for h in range(IDX_HEADS):
            lg = jnp.dot(qi_h[h], kic, preferred_element_type=F32)
            acc = acc + jnp.maximum(lg, 0.0) * w_h[h]
        kpos = c * ts + lax.broadcasted_iota(I32, (tq, ts), 1)
        adm = kpos <= row
        sc_ref[c] = jnp.where(adm, acc, NEG_INF)
        mn = jnp.minimum(mn, _lane_fold(jnp.where(adm, acc, jnp.inf), jnp.minimum))
        mx = jnp.maximum(mx, _lane_fold(jnp.where(adm, acc, NEG_INF), jnp.maximum))
        return mn, mx

    mn, mx = lax.fori_loop(0, nch, scores,
                           (jnp.full((tq, 128), jnp.inf, F32), jnp.full((tq, 128), NEG_INF, F32)))
    lo0 = jnp.min(mn, axis=1, keepdims=True)
    hi0 = jnp.max(mx, axis=1, keepdims=True)
    skip = row < topk
    v, cut = _select_threshold(lambda c, r0, nr: sc_ref[c, r0:r0 + nr, :], nch, sc_ref.shape[0], ts, tq, topk,
                               lo0, hi0, skip)

    scale = HEAD_DIM ** -0.5
    group = heads // KV_HEADS
    vw = jnp.concatenate([v] * (ts // 128), axis=1)
    cutw = jnp.concatenate([cut] * (ts // 128), axis=1)
    head = lambda h: slice(h * HEAD_DIM, (h + 1) * HEAD_DIM)

    def selected(c):
        x = sc_ref[c]
        kpos = c * ts + lax.broadcasted_iota(I32, (tq, ts), 1)
        return jnp.logical_or(x > vw, jnp.logical_and(x == vw, kpos <= cutw))

    c2 = scale * math.log2(math.e)
    kmax = jnp.sqrt(jnp.max(kn2_ref[...], axis=1, keepdims=True))
    bound = []
    for h in range(heads):
        qh = q_ref[:, head(h)].astype(F32)
        qn = jnp.sqrt(jnp.sum(qh * qh, axis=1, keepdims=True))
        bound.append(jnp.broadcast_to(qn * (kmax[h // group:h // group + 1, :] * c2), (tq, ts)))
    ones = jnp.ones((ts, HEAD_DIM), BF16)
    acc_ref[...] = jnp.zeros(acc_ref.shape, F32)

    def attend_bounded(c, _):
        keep = jnp.where(selected(c), 1.0, 0.0).astype(BF16)
        kc = kt_ref[c]
        vc = v_ref[c]
        for g in range(KV_HEADS):
            v_ones = jnp.concatenate([vc[:, head(g)], ones], axis=1)
            for h in range(g * group, (g + 1) * group):
                s = jnp.dot(q_ref[:, head(h)], kc[head(g), :], preferred_element_type=F32)
                p = jnp.exp2(s * c2 - bound[h]).astype(BF16) * keep
                acc_ref[h] += jnp.dot(p, v_ones, preferred_element_type=F32)
        return 0

    lax.fori_loop(0, nch, attend_bounded, 0)
    lmin = jnp.full((tq, HEAD_DIM), jnp.inf, F32)
    for h in range(heads):
        acc = acc_ref[h]
        lsum = acc[:, HEAD_DIM:]
        lmin = jnp.minimum(lmin, lsum)
        o_ref[:, head(h)] = (acc[:, :HEAD_DIM] / lsum).astype(o_ref.dtype)

    @pl.when(jnp.logical_not(jnp.min(lmin) > 1e-30))
    def _():
        m_ref[...] = jnp.full(m_ref.shape, -1e30, F32)
        l_ref[...] = jnp.zeros(l_ref.shape, F32)
        acc_ref[...] = jnp.zeros(acc_ref.shape, F32)

        def attend_online(c, _):
            bias = jnp.where(selected(c), 0.0, NEG_INF)
            kc = kt_ref[c]
            vc = v_ref[c]
            for h in range(heads):
                g = h // group
                s = jnp.dot(q_ref[:, head(h)], kc[head(g), :], preferred_element_type=F32) * scale + bias
                m_old = m_ref[h]
                m_new = jnp.maximum(m_old, jnp.max(s, axis=1, keepdims=True))
                alpha = jnp.exp(m_old - m_new)
                p = jnp.exp(s - m_new)
                l_ref[h] = l_ref[h] * alpha + jnp.sum(p, axis=1, keepdims=True)
                acc_ref[h, :, :HEAD_DIM] = acc_ref[h, :, :HEAD_DIM] * alpha + jnp.dot(
                    p.astype(BF16), vc[:, head(g)], preferred_element_type=F32)
                m_ref[h] = m_new
            return 0

        lax.fori_loop(0, nch, attend_online, 0)
        for h in range(heads):
            o_ref[:, head(h)] = (acc_ref[h, :, :HEAD_DIM] / l_ref[h]).astype(o_ref.dtype)


SAMPLE_TS = 4 * PAGE_SIZE
SAMPLE_RING = 8


def _sample_attn_kernel(pt_ref, q_ref, qi_ref, wi_ref, kin_ref, kn_ref, vn_ref, cki_ref, ck_ref, cv_ref,
                        o_ref, kibuf, kbuf, vbuf, sc_ref, sem_i, sem_k, sem_v, *, n_pages, t_new, topk):
    b = pl.program_id(0)
    ts = SAMPLE_TS
    ppc = ts // PAGE_SIZE
    nch = n_pages // ppc
    past = n_pages * PAGE_SIZE
    rows = q_ref.shape[2]

    def ki_copy(p):
        dst = kibuf.at[:, pl.ds(pl.multiple_of(p * PAGE_SIZE, PAGE_SIZE), PAGE_SIZE)]
        return pltpu.make_async_copy(cki_ref.at[0, pt_ref[b, p]], dst, sem_i.at[0])

    def kv_copies(c, slot):
        cps = []
        for j in range(ppc):
            page = pt_ref[b, c * ppc + j]
            dst = pl.ds(j * PAGE_SIZE, PAGE_SIZE)
            for g in range(KV_HEADS):
                cps.append(pltpu.make_async_copy(ck_ref.at[0, page, :, g, :], kbuf.at[slot, g, dst], sem_k.at[slot]))
                cps.append(pltpu.make_async_copy(cv_ref.at[0, page, :, g, :], vbuf.at[slot, g, dst], sem_v.at[slot]))
        return cps

    def start_ki(p, _):
        ki_copy(p).start()
        return 0

    def wait_ki(p, _):
        ki_copy(p).wait()
        return 0

    lax.fori_loop(0, n_pages, start_ki, 0)
    ring = kbuf.shape[0]
    lanes = next(n for n in (4, 2, 1) if nch % n == 0)
    ahead = ring - lanes
    assert ahead >= 1
    for c0 in range(min(ahead, nch)):
        for cp in kv_copies(c0, c0):
            cp.start()
    lax.fori_loop(0, n_pages, wait_ki, 0)

    wcol = (wi_ref[0] * IDX_HEADS ** -0.5) * IDX_DIM ** -0.5
    qi = qi_ref[0]

    def index_scores(kct):
        width = kct.shape[1]
        lg = jnp.dot(qi, kct, preferred_element_type=F32)
        weighted = jnp.maximum(lg, 0.0) * jnp.broadcast_to(wcol, (IDX_HEADS * rows, width))
        acc = weighted[0:rows]
        for h in range(1, IDX_HEADS):
            acc = acc + weighted[h * rows:(h + 1) * rows]
        return acc

    def scores(i, carry):
        mn, mx = carry
        span = lanes * ts
        acc = index_scores(kibuf[:, pl.ds(pl.multiple_of(i * span, span), span)].astype(BF16))
        sc_ref[i] = acc
        return jnp.minimum(mn, _lane_fold(acc, jnp.minimum)), jnp.maximum(mx, _lane_fold(acc, jnp.maximum))

    mn, mx = lax.fori_loop(0, nch // lanes, scores,
                           (jnp.full((rows, 128), jnp.inf, F32), jnp.full((rows, 128), NEG_INF, F32)))
    acc = index_scores(kin_ref[0])
    tok = lax.broadcasted_iota(I32, (rows, ts), 0) % t_new
    col = lax.broadcasted_iota(I32, (rows, ts), 1)
    adm = jnp.logical_and(col <= tok, col < t_new)
    span = lanes * ts
    ngrp = nch // lanes + 1
    sc_ref[ngrp - 1] = jnp.concatenate(
        [jnp.where(adm, acc, NEG_INF), jnp.full((rows, span - ts), NEG_INF, F32)], axis=1) if lanes > 1 else jnp.where(
            adm, acc, NEG_INF)
    mn = jnp.minimum(mn, _lane_fold(jnp.where(adm, acc, jnp.inf), jnp.minimum))
    mx = jnp.maximum(mx, _lane_fold(jnp.where(adm, acc, NEG_INF), jnp.maximum))
    lo0 = jnp.min(mn, axis=1, keepdims=True)
    hi0 = jnp.max(mx, axis=1, keepdims=True)
    skip = jnp.full((rows, 1), past + 1 <= topk)
    v, cut = _select_threshold(lambda c, r0, nr: sc_ref[c], ngrp, ngrp, span, rows, topk, lo0, hi0, skip)
    vw = jnp.concatenate([v] * (span // 128), axis=1)
    cutw = jnp.concatenate([cut] * (span // 128), axis=1)
    scale = HEAD_DIM ** -0.5

    def bias_of(c):
        x = sc_ref[c]
        kpos = c * span + lax.broadcasted_iota(I32, (rows, span), 1)
        sel = jnp.logical_or(x > vw, jnp.logical_and(x == vw, kpos <= cutw))
        return jnp.where(sel, 0.0, NEG_INF)

    def flash(state, parts):
        m_old, l_old, a_old = state
        s = jnp.concatenate(
            [lax.dot_general(q_ref[0, g], kg, (((1,), (1,)), ((), ())), preferred_element_type=F32) * scale + bias
             for g, bias, kg, _ in parts], axis=0)
        m_new = jnp.maximum(m_old, jnp.max(s, axis=1, keepdims=True))
        alpha = jnp.exp(m_old - m_new)
        p = jnp.exp(s - m_new)
        l_new = l_old * alpha + jnp.sum(p, axis=1, keepdims=True)
        pb = p.astype(BF16)
        pv = jnp.concatenate(
            [jnp.dot(pb[n * rows:(n + 1) * rows], vg, preferred_element_type=F32) for n, (_, _, _, vg) in enumerate(parts)],
            axis=0)
        return m_new, l_new, a_old * alpha + pv

    def receive(c):
        for cp in kv_copies(c, lax.rem(c, ring)):
            cp.wait()

        @pl.when(c + ahead < nch)
        def _():
            for cp in kv_copies(c + ahead, lax.rem(c + ahead, ring)):
                cp.start()

    def attend(i, state):
        for u in range(lanes):
            receive(i * lanes + u)
        parts = []
        bias_grp = bias_of(i)
        for u in range(lanes):
            c = i * lanes + u
            slot = lax.rem(c, ring)
            bias = bias_grp[:, u * ts:(u + 1) * ts]
            parts += [(g, bias, kbuf[slot, g].astype(BF16), vbuf[slot, g].astype(BF16)) for g in range(KV_HEADS)]
        return flash(state, parts)

    pieces = lanes * KV_HEADS
    init = (jnp.full((pieces * rows, 1), -1e30, F32), jnp.zeros((pieces * rows, 1), F32),
            jnp.zeros((pieces * rows, HEAD_DIM), F32))
    m_all, l_all, a_all = lax.fori_loop(0, nch // lanes, attend, init)

    def merge(a, b):
        m = jnp.maximum(a[0], b[0])
        fa, fb = jnp.exp(a[0] - m), jnp.exp(b[0] - m)
        return m, a[1] * fa + b[1] * fb, a[2] * fa + b[2] * fb

    per_lane = KV_HEADS * rows
    state = tuple(x[0:per_lane] for x in (m_all, l_all, a_all))
    for u in range(1, lanes):
        state = merge(state, tuple(x[u * per_lane:(u + 1) * per_lane] for x in (m_all, l_all, a_all)))
    bias = bias_of(ngrp - 1)[:, :ts]
    new_parts = [(g, bias, kn_ref[0][:, g * HEAD_DIM:(g + 1) * HEAD_DIM], vn_ref[0][:, g * HEAD_DIM:(g + 1) * HEAD_DIM])
                 for g in range(KV_HEADS)]
    _, l_fin, a_fin = flash(state, new_parts)
    out = a_fin / l_fin
    for g in range(KV_HEADS):
        o_ref[0, g] = out[g * rows:(g + 1) * rows].astype(o_ref.dtype)


def sample_attention(q, qi, wi, ki_new, k_new, v_new, cache_kidx, cache_k, cache_v, page_table, *, topk):
    nb, t_new, width = q.shape
    heads = width // HEAD_DIM
    group = heads // KV_HEADS
    n_pages = page_table.shape[1]
    ts = SAMPLE_TS
    assert n_pages % (ts // PAGE_SIZE) == 0
    nch = n_pages // (ts // PAGE_SIZE)
    lanes = next(n for n in (4, 2, 1) if nch % n == 0)
    rows = group * t_new
    qg = q.reshape(nb, t_new, KV_HEADS, group, HEAD_DIM).transpose(0, 2, 3, 1, 4).reshape(nb, KV_HEADS, rows, HEAD_DIM)
    qir = jnp.tile(qi.reshape(nb, t_new, IDX_HEADS, IDX_DIM).transpose(0, 2, 1, 3), (1, 1, group, 1))
    qir = qir.reshape(nb, IDX_HEADS * rows, IDX_DIM)
    wir = jnp.tile(wi.transpose(0, 2, 1), (1, 1, group)).reshape(nb, IDX_HEADS * rows, 1)
    padk = lambda a: jnp.pad(a, ((0, 0), (0, ts - t_new), (0, 0)))
    per_b = lambda a: pl.BlockSpec((1,) + a.shape[1:], lambda b, pt: (b,) + (0,) * (a.ndim - 1))
    ops = (qg, qir, wir, padk(ki_new).transpose(0, 2, 1), padk(k_new), padk(v_new))
    cache_kidx_t = cache_kidx.transpose(0, 1, 3, 2)
    out = pl.pallas_call(
        functools.partial(_sample_attn_kernel, n_pages=n_pages, t_new=t_new, topk=topk),
        grid_spec=pltpu.PrefetchScalarGridSpec(
            num_scalar_prefetch=1,
            grid=(nb,),
            in_specs=[per_b(a) for a in ops] + [pl.BlockSpec(memory_space=pl.ANY)] * 3,
            out_specs=pl.BlockSpec((1, KV_HEADS, rows, HEAD_DIM), lambda b, pt: (b, 0, 0, 0)),
            scratch_shapes=[
                pltpu.VMEM((IDX_DIM, n_pages * PAGE_SIZE), F32),
                pltpu.VMEM((SAMPLE_RING, KV_HEADS, ts, HEAD_DIM), F32),
                pltpu.VMEM((SAMPLE_RING, KV_HEADS, ts, HEAD_DIM), F32),
                pltpu.VMEM((nch // lanes + 1, rows, lanes * ts), F32),
                pltpu.SemaphoreType.DMA((1,)),
                pltpu.SemaphoreType.DMA((SAMPLE_RING,)),
                pltpu.SemaphoreType.DMA((SAMPLE_RING,)),
            ]),
        out_shape=jax.ShapeDtypeStruct((nb, KV_HEADS, rows, HEAD_DIM), BF16),
        compiler_params=_cparams("arbitrary"),
        name="sample_attention",
    )(page_table, *ops, cache_kidx_t, cache_k, cache_v)
    return out.reshape(nb, KV_HEADS, group, t_new, HEAD_DIM).transpose(0, 3, 1, 2, 4).reshape(nb, t_new, width)


def prompt_attention(qb, qib, wi, kib, kb, vb, kn2, *, tq, ts, topk):
    s, width = qb.shape
    heads = width // HEAD_DIM
    assert s % tq == 0 and s % ts == 0
    nc = s // ts
    kit = kib.reshape(nc, ts, IDX_DIM).transpose(0, 2, 1)
    kt = kb.reshape(nc, ts, KV_HEADS * HEAD_DIM).transpose(0, 2, 1)
    v3 = vb.reshape(nc, ts, KV_HEADS * HEAD_DIM)
    whole = lambda a: pl.BlockSpec(a.shape, lambda i: (0,) * a.ndim, pipeline_mode=pl.Buffered(1))
    return pl.pallas_call(
        functools.partial(_prompt_attn_kernel, tq=tq, ts=ts, topk=topk),
        grid=(s // tq,),
        in_specs=[
            pl.BlockSpec((tq, width), lambda i: (i, 0)),
            pl.BlockSpec((tq, IDX_HEADS * IDX_DIM), lambda i: (i, 0)),
            pl.BlockSpec((tq, IDX_HEADS), lambda i: (i, 0)),
            whole(kit), whole(kt), whole(v3), whole(kn2),
        ],
        out_specs=pl.BlockSpec((tq, width), lambda i: (i, 0)),
        out_shape=jax.ShapeDtypeStruct((s, width), BF16),
        scratch_shapes=[
            pltpu.VMEM((nc, tq, ts), F32),
            pltpu.VMEM((heads, tq, 1), F32),
            pltpu.VMEM((heads, tq, 1), F32),
            pltpu.VMEM((heads, tq, 2 * HEAD_DIM), F32),
        ],
        compiler_params=_cparams("arbitrary"),
        name="prompt_attention",
    )(qb, qib, wi, kit, kt, v3, kn2)


def _row_tile(m, cap):
    t = cap
    while m % t:
        t //= 2
    return t


def _token_stages(x, attend, gdn, lw):
    (nmw, wp, w_out, nfw, wq, sub_keys, w_u, w_v, fw, tabs) = lw
    m = x.shape[0]
    tm = _row_tile(m, 512)
    p = norm_matmul(x, nmw, wp, tm=tm, tn=wp.shape[1] // 7)
    qb, qib, k32, kb, v32, vb, ki32, kib, auxr, kn2 = rope_split(p, *tabs, tm=tm)
    wi, a_pre = auxr[:, 0:IDX_HEADS], auxr[:, 2 * IDX_HEADS:3 * IDX_HEADS]
    ya = attend(qb, qib, wi, kib, kb, vb, kn2)
    yg, new_conv, new_ssm = gdn(p, a_pre)
    h, xn, qp = out_proj(ya, yg, x, w_out, nfw, wq, tm=tm)
    i1, i2, gate = peer_topk(qp, sub_keys, tm=_row_tile(m, 256))
    tok = lambda a: a.reshape(PEER_HEADS * PEER_TOPK, m).T
    gates = peer_gates(tok(i1), tok(i2), tok(gate), tmb=64)
    f = peer_dense(xn, w_u, w_v, gates, tm=_row_tile(m, 1024), ib=8)
    y = residual_norm(h, f, fw, tm=tm)
    return y, k32, v32, ki32, new_conv, new_ssm


def kernel(x_prompt, x_sample, cache_k, cache_v, cache_kidx, page_table, state_conv, state_ssm, norm_mix_w, w_in,
           conv_w, a_log, dt_bias, gdn_norm_w, w_out, norm_ffn_w, peer_wq, peer_sub_keys, peer_u, peer_v,
           norm_final_w):
    depth = w_in.shape[0]
    b, s, d = x_prompt.shape
    nb, t, _ = x_sample.shape
    assert depth == 1 and b == 1, "single layer, single prompt sequence"
    past = page_table.shape[1] * PAGE_SIZE
    heads_g = state_ssm.shape[2]
    conv_ch = state_conv.shape[-1]

    wp = _pack_w_in(w_in[0])
    shared = (norm_mix_w[0][None], wp, w_out[0].astype(BF16), norm_ffn_w[0][None], peer_wq[0].astype(BF16),
              peer_sub_keys[0], peer_u[0].astype(BF16), peer_v[0].astype(BF16), norm_final_w[None])
    gdn_w = (conv_w[0], a_log[0], dt_bias[0], gdn_norm_w[0])

    def attend_p(qb, qib, wi, kib, kb, vb, kn2):
        return prompt_attention(qb, qib, wi, kib, kb, vb, kn2.T, tq=_row_tile(s, 256), ts=_row_tile(s, 512),
                                topk=min(TOPK_MAX, s // 4))

    def gdn_p(p, a_pre):
        conv0 = jnp.zeros((1, CONV_W - 1, conv_ch), F32)
        ssm0 = jnp.zeros((1, heads_g, GDN_DK, GDN_DV), F32)
        t_pad = -(-s // GDN_CHUNK) * GDN_CHUNK
        assert t_pad == s
        return gdn_mixer(p, a_pre.T[None], conv0, *gdn_w, ssm0, n_seq=1, t_pad=s, t_total=s, chunk=GDN_CHUNK)

    tabs_p = _rope_tables(jnp.arange(s))
    y_p, k_p, v_p, ki_p, conv_p, ssm_p = _token_stages(x_prompt.reshape(s, d), attend_p, gdn_p, shared + (tabs_p,))

    m_s = nb * t
    t8 = -(-t // 8) * 8

    def attend_s(qb, qib, wi, kib, kb, vb, kn2):
        seq = lambda a: a.reshape(nb, t, a.shape[-1])
        y = sample_attention(seq(qb), seq(qib), seq(wi), seq(kib), seq(kb), seq(vb), cache_kidx, cache_k, cache_v,
                             page_table, topk=min(TOPK_MAX, (past + t) // 4))
        return y.reshape(m_s, y.shape[-1])

    def gdn_s(p, a_pre):
        pad_t = lambda a: jnp.pad(a.reshape(nb, t, a.shape[-1]), ((0, 0), (0, t8 - t), (0, 0)))
        p8 = pad_t(p).reshape(nb * t8, p.shape[-1])
        at = pad_t(a_pre).transpose(0, 2, 1)
        yg, nconv, nssm = gdn_mixer(p8, at, state_conv[0], *gdn_w, state_ssm[0], n_seq=nb, t_pad=t8, t_total=t, chunk=8)
        return yg.reshape(nb, t8, yg.shape[-1])[:, :t].reshape(m_s, yg.shape[-1]), nconv, nssm

    tabs_s = tuple(jnp.tile(a, (nb, 1)) for a in _rope_tables(past + jnp.arange(t)))
    y_s, k_s, v_s, ki_s, conv_s, ssm_s = _token_stages(x_sample.reshape(m_s, d), attend_s, gdn_s, shared + (tabs_s,))

    kv = lambda a, n, tt: a.reshape(1, n, tt, KV_HEADS, HEAD_DIM)
    return (y_p.reshape(b, s, d), y_s.reshape(nb, t, d),
            kv(k_p, b, s), kv(v_p, b, s), ki_p.reshape(1, b, s, IDX_DIM), conv_p[None], ssm_p[None],
            kv(k_s, nb, t), kv(v_s, nb, t), ki_s.reshape(1, nb, t, IDX_DIM), conv_s[None], ssm_s[None])
```

```python
import functools
import math

import jax
import jax.numpy as jnp
from jax import lax
from jax.experimental import pallas as pl
from jax.experimental.pallas import tpu as pltpu

F32 = jnp.float32
BF16 = jnp.bfloat16
I32 = jnp.int32
EPS = 1e-6
NEG_INF = float("-inf")

HEAD_DIM = 128
KV_HEADS = 2
IDX_HEADS = 8
IDX_DIM = 64
TOPK_MAX = 256
ROPE_THETA = 10000.0
PAGE_SIZE = 128
GDN_DK = 128
GDN_DV = 128
CONV_W = 4
GDN_CHUNK = 64
PEER_HEADS = 8
PEER_KEYS = 128
PEER_QDIM = 128
PEER_TOPK = 16

VMEM_LIMIT_BYTES = 56 * 1024 * 1024


def _cparams(*sem):
    return pltpu.CompilerParams(dimension_semantics=sem, vmem_limit_bytes=VMEM_LIMIT_BYTES)


_PROJ_SRC = (("q", 1024), ("k", 256), ("v", 256), ("qi", 512), ("ki", 64), ("wi", 8), ("qkv", 3072), ("z", 1024),
             ("b", 8), ("a", 8))
_PROJ_DST = ("q", "qi", "k", "v", "qkv", "z", "ki", "wi", "b", "a")
PROJ_PACKED = 6400


def _pack_w_in_kernel(w_ref, o_ref):
    src, off = {}, 0
    for name, width in _PROJ_SRC:
        src[name] = (off, width)
        off += width
    dst = 0
    for name in _PROJ_DST:
        s0, width = src[name]
        o_ref[:, dst:dst + width] = w_ref[:, s0:s0 + width].astype(o_ref.dtype)
        dst += width
    o_ref[:, dst:] = jnp.zeros((o_ref.shape[0], o_ref.shape[1] - dst), o_ref.dtype)


def _pack_w_in(w_in):
    d, n = w_in.shape
    assert n == sum(w for _, w in _PROJ_SRC)
    tr = _row_tile(d, 256)
    return pl.pallas_call(
        _pack_w_in_kernel,
        grid=(d // tr,),
        in_specs=[pl.BlockSpec((tr, n), lambda i: (i, 0))],
        out_specs=pl.BlockSpec((tr, PROJ_PACKED), lambda i: (i, 0)),
        out_shape=jax.ShapeDtypeStruct((d, PROJ_PACKED), BF16),
        compiler_params=_cparams("arbitrary"),
        name="pack_w_in",
    )(w_in)


def _rope_tables(pos):
    def table(dim):
        half = dim // 2
        inv = ROPE_THETA ** (-jnp.arange(half, dtype=F32) / half)
        ang = pos.astype(F32)[:, None] * inv[None, :]
        cos = jnp.tile(jnp.cos(ang), (1, 128 // half))
        sin = jnp.tile(jnp.concatenate([-jnp.sin(ang), jnp.sin(ang)], axis=1), (1, 128 // dim))
        return cos, sin
    cq, sq = table(HEAD_DIM)
    ci, si = table(IDX_DIM)
    return cq, sq, ci, si


def _norm_matmul_kernel(x_ref, nw_ref, w_ref, o_ref):
    x = x_ref[...]
    y = x * lax.rsqrt(jnp.mean(x * x, axis=-1, keepdims=True) + EPS)
    xn = (y * nw_ref[...]).astype(BF16)
    o_ref[...] = jnp.dot(xn, w_ref[...], preferred_element_type=F32)


def norm_matmul(x, nw, w, *, tm, tn):
    m, d = x.shape
    n = w.shape[1]
    assert m % tm == 0 and n % tn == 0
    return pl.pallas_call(
        _norm_matmul_kernel,
        grid=(n // tn, m // tm),
        in_specs=[
            pl.BlockSpec((tm, d), lambda j, i: (i, 0)),
            pl.BlockSpec((1, d), lambda j, i: (0, 0)),
            pl.BlockSpec((d, tn), lambda j, i: (0, j)),
        ],
        out_specs=pl.BlockSpec((tm, tn), lambda j, i: (i, j)),
        out_shape=jax.ShapeDtypeStruct((m, n), F32),
        compiler_params=_cparams("arbitrary", "arbitrary"),
        name="norm_matmul",
    )(x, nw, w)


def _rope128(x, cos, sin_signed):
    return x * cos + pltpu.roll(x, 64, 1) * sin_signed


def _rope64(x, cos, sin_signed, first_half):
    partner = jnp.where(first_half, pltpu.roll(x, 96, 1), pltpu.roll(x, 32, 1))
    return x * cos + partner * sin_signed


def _rope_split_kernel(q_ref, qi_ref, k_ref, v_ref, aux_ref, cq_ref, sq_ref, ci_ref, si_ref,
                       qb_ref, qib_ref, k32_ref, kb_ref, v32_ref, vb_ref, ki32_ref, kib_ref, auxr_ref, kn2_ref):
    cq, sq, ci, si = cq_ref[...], sq_ref[...], ci_ref[...], si_ref[...]
    lane = lax.broadcasted_iota(I32, ci.shape, 1)
    first_half = (lane % IDX_DIM) < (IDX_DIM // 2)
    for h in range(q_ref.shape[1] // HEAD_DIM):
        sl = slice(h * HEAD_DIM, (h + 1) * HEAD_DIM)
        qb_ref[:, sl] = _rope128(q_ref[:, sl], cq, sq).astype(BF16)
    for h in range(k_ref.shape[1] // HEAD_DIM):
        sl = slice(h * HEAD_DIM, (h + 1) * HEAD_DIM)
        kr = _rope128(k_ref[:, sl], cq, sq)
        k32_ref[:, sl] = kr
        krb = kr.astype(BF16)
        kb_ref[:, sl] = krb
        kn2_ref[:, h:h + 1] = jnp.sum(krb.astype(F32) * krb.astype(F32), axis=1, keepdims=True)
    for h in range(qi_ref.shape[1] // 128):
        sl = slice(h * 128, (h + 1) * 128)
        qib_ref[:, sl] = _rope64(qi_ref[:, sl], ci, si, first_half).astype(BF16)
    v = v_ref[...]
    v32_ref[...] = v
    vb_ref[...] = v.astype(BF16)
    aux = aux_ref[...]
    kir = _rope64(aux, ci, si, first_half)[:, :IDX_DIM]
    ki32_ref[...] = kir
    kib_ref[...] = kir.astype(BF16)
    auxr_ref[...] = aux[:, IDX_DIM:]


def rope_split(p, cq, sq, ci, si, *, tm):
    m = p.shape[0]
    assert m % tm == 0
    row = lambda w, j: pl.BlockSpec((tm, w), lambda i, j=j: (i, j))
    outs = [
        ((m, 1024), BF16), ((m, 512), BF16), ((m, 256), F32), ((m, 256), BF16), ((m, 256), F32),
        ((m, 256), BF16), ((m, IDX_DIM), F32), ((m, IDX_DIM), BF16), ((m, 128 - IDX_DIM), F32),
        ((m, KV_HEADS), F32),
    ]
    return pl.pallas_call(
        _rope_split_kernel,
        grid=(m // tm,),
        in_specs=[row(1024, 0), row(512, 2), row(256, 6), row(256, 7), row(128, 48),
                  row(128, 0), row(128, 0), row(128, 0), row(128, 0)],
        out_specs=[pl.BlockSpec((tm, s[1]), lambda i: (i, 0)) for s, _ in outs],
        out_shape=[jax.ShapeDtypeStruct(s, dt) for s, dt in outs],
        compiler_params=_cparams("arbitrary"),
        name="rope_split",
    )(p, p, p, p, p, cq, sq, ci, si)


def _out_proj_kernel(ya_ref, yg_ref, res_ref, wa_ref, wg_ref, nw_ref, wq_ref, h_ref, xn_ref, qp_ref):
    h = res_ref[...] + jnp.dot(ya_ref[...], wa_ref[...], preferred_element_type=F32)
    h = h + jnp.dot(yg_ref[...], wg_ref[...], preferred_element_type=F32)
    h_ref[...] = h
    y = h * lax.rsqrt(jnp.mean(h * h, axis=-1, keepdims=True) + EPS)
    xn = (y * nw_ref[...]).astype(BF16)
    xn_ref[...] = xn
    qp_ref[...] = jnp.dot(xn, wq_ref[...], preferred_element_type=F32)


def out_proj(ya, yg, res, w_out, nw, wq, *, tm):
    m, d = res.shape
    half = ya.shape[1]
    nq = wq.shape[1]
    assert m % tm == 0
    rows = lambda wd: pl.BlockSpec((tm, wd), lambda i: (i, 0))
    const = lambda a: pl.BlockSpec(a.shape, lambda i: (0,) * a.ndim, pipeline_mode=pl.Buffered(1))
    wa, wg = w_out[:half], w_out[half:]
    return pl.pallas_call(
        _out_proj_kernel,
        grid=(m // tm,),
        in_specs=[rows(half), rows(half), rows(d), const(wa), const(wg), const(nw), const(wq)],
        out_specs=[rows(d), rows(d), rows(nq)],
        out_shape=[jax.ShapeDtypeStruct((m, d), F32), jax.ShapeDtypeStruct((m, d), BF16),
                   jax.ShapeDtypeStruct((m, nq), F32)],
        compiler_params=_cparams("arbitrary"),
        name="out_proj",
    )(ya, yg, res, wa, wg, nw, wq)


def _take_top(s, codes, count):
    big = jnp.int32(2 ** 30)
    vals, picks = [], []
    for _ in range(count):
        m = jnp.max(s, axis=0, keepdims=True)
        pick = jnp.min(jnp.where(s == m, codes, big), axis=0, keepdims=True)
        s = jnp.where(codes == pick, NEG_INF, s)
        vals.append(m)
        picks.append(pick)
    return jnp.concatenate(vals, axis=0), jnp.concatenate(picks, axis=0)


def _lookup(table, sel, count):
    out = jnp.zeros(sel.shape, table.dtype)
    for a in range(count):
        out = jnp.where(sel == a, jnp.broadcast_to(table[a:a + 1, :], sel.shape), out)
    return out


def _peer_topk_kernel(q_ref, sk_ref, i1_ref, i2_ref, g_ref):
    tm = q_ref.shape[0]
    kk = PEER_TOPK
    half = PEER_QDIM // 2
    key_codes = lax.broadcasted_iota(I32, (PEER_KEYS, tm), 0)
    top_v, top_i = [], []
    for c in range(2):
        qs = q_ref[:, c * half:(c + 1) * half].astype(BF16)
        s = lax.dot_general(sk_ref[c], qs, (((1,), (1,)), ((), ())), preferred_element_type=F32)
        vals, idx = _take_top(s, key_codes, kk)
        top_v.append(vals)
        top_i.append(idx)
    pieces, codes = [], []
    for a in range(kk):
        nb = kk // (a + 1)
        rows = -(-nb // 8) * 8
        r = lax.broadcasted_iota(I32, (rows, tm), 0)
        cand = top_v[0][a:a + 1, :] + top_v[1][0:rows, :]
        pieces.append(jnp.where(r < nb, cand, NEG_INF))
        codes.append(r + a * kk)
    best_s, best_c = _take_top(jnp.concatenate(pieces, axis=0), jnp.concatenate(codes, axis=0), kk)
    i1_ref[0] = _lookup(top_i[0], lax.shift_right_logical(best_c, 4), kk)
    i2_ref[0] = _lookup(top_i[1], jnp.bitwise_and(best_c, kk - 1), kk)
    e = jnp.exp(best_s - best_s[0:1, :])
    g_ref[0] = e / jnp.sum(e, axis=0, keepdims=True)


def peer_topk(qp, sub_keys, *, tm):
    m = qp.shape[0]
    heads = sub_keys.shape[0]
    assert m % tm == 0 and PEER_TOPK == 16
    sk = sub_keys.reshape(heads * 2, PEER_KEYS, PEER_QDIM // 2).astype(BF16)
    out = pl.BlockSpec((1, PEER_TOPK, tm), lambda i, h: (h, 0, i))
    return pl.pallas_call(
        _peer_topk_kernel,
        grid=(m // tm, heads),
        in_specs=[pl.BlockSpec((tm, PEER_QDIM), lambda i, h: (i, h)),
                  pl.BlockSpec((2, PEER_KEYS, PEER_QDIM // 2), lambda i, h: (h, 0, 0))],
        out_specs=[out, out, out],
        out_shape=[jax.ShapeDtypeStruct((heads, PEER_TOPK, m), I32), jax.ShapeDtypeStruct((heads, PEER_TOPK, m), I32),
                   jax.ShapeDtypeStruct((heads, PEER_TOPK, m), F32)],
        compiler_params=_cparams("arbitrary", "arbitrary"),
        name="peer_topk",
    )(qp, sk)


GATE_TOKENS = 16


def _peer_gate_kernel(i1_ref, i2_ref, g_ref, o_ref):
    tmb = i1_ref.shape[0]
    nk = PEER_KEYS
    sub = lax.broadcasted_iota(I32, (nk, i1_ref.shape[1]), 0)

    def token_block(j, _):
        t0 = pl.multiple_of(j * GATE_TOKENS, GATE_TOKENS)
        per_token = []
        for u in range(GATE_TOKENS):
            wide = lambda ref: jnp.broadcast_to(ref[pl.ds(t0 + u, 1), :], sub.shape)
            p1 = jnp.where(wide(i1_ref) == sub, wide(g_ref), 0.0).astype(BF16)
            p2 = jnp.where(wide(i2_ref) == sub, 1.0, 0.0).astype(BF16)
            gm = lax.dot_general(p1, p2, (((1,), (1,)), ((), ())), preferred_element_type=F32)
            per_token.append(gm.astype(o_ref.dtype))
        block = jnp.stack(per_token, axis=0)
        o_ref[:, pl.ds(t0, GATE_TOKENS), :] = pltpu.einshape("mab->amb", block)
        return 0

    lax.fori_loop(0, tmb // GATE_TOKENS, token_block, 0)


def peer_gates(i1, i2, gate, *, tmb):
    m, slots = i1.shape
    assert m % tmb == 0 and tmb % GATE_TOKENS == 0
    rows = pl.BlockSpec((tmb, slots), lambda i: (i, 0))
    return pl.pallas_call(
        _peer_gate_kernel,
        grid=(m // tmb,),
        in_specs=[rows, rows, rows],
        out_specs=pl.BlockSpec((PEER_KEYS, tmb, PEER_KEYS), lambda i: (0, i, 0)),
        out_shape=jax.ShapeDtypeStruct((PEER_KEYS, m, PEER_KEYS), BF16),
        compiler_params=_cparams("arbitrary"),
        name="peer_gates",
    )(i1, i2, gate)


def _gelu_tanh(x):
    return 0.5 * x * (1.0 + jnp.tanh(math.sqrt(2.0 / math.pi) * (x + 0.044715 * (x * x * x))))


def _peer_dense_kernel(xn_ref, wu_ref, wv_ref, g_ref, o_ref):
    j = pl.program_id(1)
    ib = g_ref.shape[0]
    a = lax.dot_general(xn_ref[...], wu_ref[...], (((1,), (1,)), ((), ())), preferred_element_type=F32)
    act = _gelu_tanh(a)
    hm = jnp.concatenate(
        [(g_ref[u].astype(F32) * act[:, u * PEER_KEYS:(u + 1) * PEER_KEYS]).astype(BF16) for u in range(ib)], axis=1)
    part = jnp.dot(hm, wv_ref[...], preferred_element_type=F32)

    @pl.when(j == 0)
    def _():
        o_ref[...] = part

    @pl.when(j > 0)
    def _():
        o_ref[...] += part


def peer_dense(xn, w_u, w_v, gates, *, tm, ib):
    m, d = xn.shape
    assert m % tm == 0 and PEER_KEYS % ib == 0
    eb = ib * PEER_KEYS
    return pl.pallas_call(
        _peer_dense_kernel,
        grid=(m // tm, PEER_KEYS // ib),
        in_specs=[
            pl.BlockSpec((tm, d), lambda i, j: (i, 0)),
            pl.BlockSpec((eb, d), lambda i, j: (j, 0)),
            pl.BlockSpec((eb, d), lambda i, j: (j, 0)),
            pl.BlockSpec((ib, tm, PEER_KEYS), lambda i, j: (j, i, 0)),
        ],
        out_specs=pl.BlockSpec((tm, d), lambda i, j: (i, 0)),
        out_shape=jax.ShapeDtypeStruct((m, d), F32),
        compiler_params=_cparams("arbitrary", "arbitrary"),
        name="peer_dense",
    )(xn, w_u, w_v, gates)


def _residual_norm_kernel(h_ref, f_ref, w_ref, o_ref):
    y = h_ref[...] + f_ref[...]
    o_ref[...] = y * lax.rsqrt(jnp.mean(y * y, axis=-1, keepdims=True) + EPS) * w_ref[...]


def residual_norm(h, f, w, *, tm):
    m, d = h.shape
    assert m % tm == 0
    rows = pl.BlockSpec((tm, d), lambda i: (i, 0))
    return pl.pallas_call(
        _residual_norm_kernel,
        grid=(m // tm,),
        in_specs=[rows, rows, pl.BlockSpec((1, d), lambda i: (0, 0))],
        out_specs=rows,
        out_shape=jax.ShapeDtypeStruct((m, d), F32),
        compiler_params=_cparams("arbitrary"),
        name="residual_norm",
    )(h, f, w)


_HI = lax.Precision.HIGHEST


def _dot_hi(a, b):
    return jnp.dot(a, b, preferred_element_type=F32, precision=_HI)


_BNN = (((2,), (1,)), ((0,), (0,)))
_BNT = (((2,), (2,)), ((0,), (0,)))
_BTN = (((1,), (1,)), ((0,), (0,)))


def _bdot(a, b, dims=_BNN):
    return lax.dot_general(a.astype(BF16), b.astype(BF16), dims, preferred_element_type=F32)


def _bdot3(a, b, dims=_BNN):
    ah, bh = a.astype(BF16), b.astype(BF16)
    al, bl = (a - ah.astype(F32)).astype(BF16), (b - bh.astype(F32)).astype(BF16)
    dot = lambda x, y: lax.dot_general(x, y, dims, preferred_element_type=F32)
    return dot(ah, bh) + (dot(ah, bl) + dot(al, bh))


def _sigmoid(x):
    return 1.0 / (1.0 + jnp.exp(-x))


def _softplus(x):
    return jnp.maximum(x, 0.0) + jnp.log1p(jnp.exp(-jnp.abs(x)))


def _gdn_kernel(xq_ref, xk_ref, xv_ref, z_ref, aux_ref, at_ref, cs_ref, cw_ref, alog_ref, dtb_ref,
                alogt_ref, dtbt_ref, nw_ref, s0_ref, y_ref, nconv_ref, nssm_ref, xp_ref, st_ref,
                *, chunk, sub, t_total, b_lane, a_lane):
    c = pl.program_id(1)
    nchunks = pl.num_programs(1)
    heads = st_ref.shape[0]
    width = heads * GDN_DK
    tail = CONV_W - 1

    @pl.when(c == 0)
    def _():
        xp_ref[8 - tail:8, :] = cs_ref[0]
        st_ref[...] = s0_ref[0]

    xp_ref[8:8 + chunk, 0:width] = xq_ref[...]
    xp_ref[8:8 + chunk, width:2 * width] = xk_ref[...]
    xp_ref[8:8 + chunk, 2 * width:3 * width] = xv_ref[...]
    conv = cw_ref[tail:tail + 1, :] * xp_ref[8:8 + chunk, :]
    for j in range(tail):
        conv = conv + cw_ref[j:j + 1, :] * xp_ref[8 - tail + j:8 - tail + j + chunk, :]
    conv = conv * _sigmoid(conv)

    padded = t_total % chunk != 0
    ridx = c * chunk + lax.broadcasted_iota(I32, (chunk, 1), 0)
    rvalid = ridx < t_total
    cidx = c * chunk + lax.broadcasted_iota(I32, (1, chunk), 1)
    cvalid = cidx < t_total

    aux = aux_ref[...]
    beta = _sigmoid(aux[:, b_lane:b_lane + heads])
    g = -jnp.exp(alog_ref[...]) * _softplus(aux[:, a_lane:a_lane + heads] + dtb_ref[...])
    gt = -jnp.exp(alogt_ref[...]) * _softplus(at_ref[0] + dtbt_ref[...])
    if padded:
        beta = jnp.where(rvalid, beta, 0.0)
        g = jnp.where(rvalid, g, 0.0)
        gt = jnp.where(cvalid, gt, 0.0)
    n_sub = chunk // sub
    rf = lax.broadcasted_iota(I32, (chunk, chunk), 0)
    cf = lax.broadcasted_iota(I32, (chunk, chunk), 1)
    same = (rf // sub) == (cf // sub)
    gc = _dot_hi(jnp.where(jnp.logical_and(same, rf >= cf), 1.0, 0.0), g)
    gct = _dot_hi(gt, jnp.where(jnp.logical_and(same, rf <= cf), 1.0, 0.0))
    ri = lax.broadcasted_iota(I32, (sub, sub), 0)
    ci = lax.broadcasted_iota(I32, (sub, sub), 1)
    causal = ri >= ci
    strict = ri > ci
    eye = jnp.where(ri == ci, 1.0, 0.0)

    per_head = lambda x2d, base: jnp.stack(
        [x2d[:, base + h * GDN_DK:base + (h + 1) * GDN_DK] for h in range(heads)], axis=0)
    q = per_head(conv, 0)
    k = per_head(conv, width)
    v = per_head(conv, 2 * width)
    q = q * lax.rsqrt(jnp.sum(q * q, axis=-1, keepdims=True) + EPS) * GDN_DK ** -0.5
    k = k * lax.rsqrt(jnp.sum(k * k, axis=-1, keepdims=True) + EPS)
    if padded:
        k = jnp.where(rvalid[None], k, 0.0)
        v = jnp.where(rvalid[None], v, 0.0)
    gcol = jnp.stack([gc[:, h:h + 1] for h in range(heads)], axis=0)
    grow = jnp.stack([gct[h:h + 1, :] for h in range(heads)], axis=0)
    bcol = jnp.stack([beta[:, h:h + 1] for h in range(heads)], axis=0)
    by_rows = lambda a: jnp.concatenate([a[:, j * sub:(j + 1) * sub] for j in range(n_sub)], axis=0)
    q, k, v, gcol, bcol = (by_rows(a) for a in (q, k, v, gcol, bcol))
    grow = jnp.concatenate([grow[:, :, j * sub:(j + 1) * sub] for j in range(n_sub)], axis=0)
    decay = jnp.where(causal[None], jnp.exp(jnp.where(causal[None], gcol - grow, 0.0)), 0.0)
    kb = k * bcol
    vb = v * bcol
    x = -jnp.where(strict[None], _bdot(kb, k, _BNT) * decay, 0.0)
    tinv = eye[None] + x
    span = 2
    while span < sub:
        x = _bdot3(x, x)
        tinv = tinv + _bdot3(tinv, x)
        span *= 2
    u = _bdot(tinv, vb)
    w = _bdot(tinv, kb * jnp.exp(gcol))
    intra = jnp.where(causal[None], _bdot(q, k, _BNT) * decay, 0.0)
    qg = q * jnp.exp(gcol)
    glast = gcol[:, sub - 1:sub, :]
    kd = k * jnp.exp(glast - gcol)
    state = st_ref[...]
    outs = []
    for j in range(n_sub):
        sl = slice(j * heads, (j + 1) * heads)
        v_new = u[sl] - _bdot(w[sl], state)
        outs.append(_bdot(qg[sl], state) + _bdot(intra[sl], v_new))
        state = state * jnp.exp(glast[sl]) + _bdot(kd[sl], v_new, _BTN)
    st_ref[...] = state
    out = outs[0] if n_sub == 1 else jnp.concatenate(outs, axis=1)
    on = out * lax.rsqrt(jnp.mean(out * out, axis=-1, keepdims=True) + EPS) * nw_ref[...]
    for h in range(heads):
        sl = slice(h * GDN_DV, (h + 1) * GDN_DV)
        zh = z_ref[:, sl]
        y_ref[:, sl] = (on[h] * (zh * _sigmoid(zh))).astype(y_ref.dtype)

    last_valid = t_total - (t_total - 1) // chunk * chunk
    @pl.when(c < nchunks - 1)
    def _():
        xp_ref[8 - tail:8, :] = xp_ref[8 + chunk - tail:8 + chunk, :]

    @pl.when(c == nchunks - 1)
    def _():
        nconv_ref[0] = xp_ref[8 + last_valid - tail:8 + last_valid, :]
        nssm_ref[0] = st_ref[...]


def gdn_mixer(p, at, conv_state, conv_w, a_log, dt_bias, norm_w, ssm_state, *, n_seq, t_pad, t_total, chunk, sub):
    heads = ssm_state.shape[1]
    width = heads * GDN_DK
    assert t_pad % chunk == 0 and chunk % sub == 0 and sub % 8 == 0
    nch = t_pad // chunk
    rows = lambda wd, j: pl.BlockSpec((chunk, wd), lambda n, c, j=j: (n * nch + c, j))
    const = lambda a: pl.BlockSpec(a.shape, lambda n, c: (0,) * a.ndim)
    alog = a_log.reshape(1, heads)
    dtb = dt_bias.reshape(1, heads)
    nw = norm_w.reshape(1, GDN_DV)
    at = at.reshape(n_seq, heads, nch, chunk).transpose(0, 2, 1, 3).reshape(n_seq * nch, heads, chunk)
    kern = functools.partial(_gdn_kernel, chunk=chunk, sub=sub, t_total=t_total, b_lane=IDX_DIM + 8,
                             a_lane=IDX_DIM + 16)
    return pl.pallas_call(
        kern,
        grid=(n_seq, nch),
        in_specs=[
            rows(width, 2), rows(width, 3), rows(width, 4), rows(width, 5), rows(128, 48),
            pl.BlockSpec((1, heads, chunk), lambda n, c: (n * nch + c, 0, 0)),
            pl.BlockSpec((1, CONV_W - 1, 3 * width), lambda n, c: (n, 0, 0)),
            const(conv_w), const(alog), const(dtb), const(alog.T), const(dtb.T), const(nw),
            pl.BlockSpec((1, heads, GDN_DK, GDN_DV), lambda n, c: (n, 0, 0, 0)),
        ],
        out_specs=[
            pl.BlockSpec((chunk, width), lambda n, c: (n * nch + c, 0)),
            pl.BlockSpec((1, CONV_W - 1, 3 * width), lambda n, c: (n, 0, 0)),
            pl.BlockSpec((1, heads, GDN_DK, GDN_DV), lambda n, c: (n, 0, 0, 0)),
        ],
        out_shape=[
            jax.ShapeDtypeStruct((n_seq * t_pad, width), BF16),
            jax.ShapeDtypeStruct((n_seq, CONV_W - 1, 3 * width), F32),
            jax.ShapeDtypeStruct((n_seq, heads, GDN_DK, GDN_DV), F32),
        ],
        scratch_shapes=[pltpu.VMEM((8 + chunk, 3 * width), F32), pltpu.VMEM((heads, GDN_DK, GDN_DV), F32)],
        compiler_params=_cparams("arbitrary", "arbitrary"),
        name="gdn_mixer",
    )(p, p, p, p, p, at, conv_state, conv_w, alog, dtb, alog.T, dtb.T, nw, ssm_state)


def _lane_fold(x, op):
    out = x[:, :128]
    for j in range(1, x.shape[1] // 128):
        out = op(out, x[:, j * 128:(j + 1) * 128])
    return out


def _select_threshold(load, nch, nch_max, ts, rows, topk, lo0, hi0, skip):
    kf = float(topk)
    rb = min(rows, 128)
    assert rows % rb == 0
    assert nch_max * (ts // 128) <= 256
    lane = lax.broadcasted_iota(I32, (rb, 128), 1)
    on_lanes = rb == 128
    if on_lanes:
        state_shape = (8, rows)
        ones = jnp.ones((8, 128), BF16)
        to_state = lambda blk: jnp.transpose(jnp.broadcast_to(blk, (128, 128)))[0:8]
        to_rows = lambda st, r0: jnp.transpose(jnp.broadcast_to(st[0:1, r0:r0 + 128], (128, 128)))
        row_sums = lambda acc: lax.dot_general(ones, acc.astype(BF16), (((1,), (1,)), ((), ())),
                                               preferred_element_type=F32)
        join_axis = 1
    else:
        state_shape = (rows, 128)
        ones = jnp.ones((128, 128), BF16)
        to_state = lambda blk: jnp.broadcast_to(blk, (rb, 128))
        to_rows = lambda st, r0: st[r0:r0 + rb]
        row_sums = lambda acc: jnp.dot(acc.astype(BF16), ones, preferred_element_type=F32)
        join_axis = 0
    join = lambda parts: parts[0] if len(parts) == 1 else jnp.concatenate(parts, axis=join_axis)
    rep = lambda col: join([to_state(col[r0:r0 + rb].astype(F32)) for r0 in range(0, rows, rb)])
    lo0, hi0, skip = rep(lo0), rep(hi0), rep(skip) > 0.5

    def count(pred, *cols):
        blocks = range(0, rows, rb)
        wides = [[to_rows(col, r0) for col in cols] for r0 in blocks]
        accs = []
        for r0, wide in zip(blocks, wides):
            def body(c, acc, r0=r0, wide=wide):
                x = load(c, r0, rb)
                for j in range(ts // 128):
                    kpos = lane + (c * ts + j * 128)
                    acc = acc + jnp.where(pred(x[:, j * 128:(j + 1) * 128], kpos, *wide), 1.0, 0.0)
                return acc

            accs.append(lax.fori_loop(0, nch, body, jnp.zeros((rb, 128), F32)))
        return join([row_sums(acc) for acc in accs])

    ge_ = lambda x, kpos, t: x >= t
    gt_ = lambda x, kpos, t: x > t
    eq_ = lambda x, kpos, t: x == t

    zero = jnp.zeros(state_shape, F32)
    c_pos = count(gt_, zero)
    c_nn = count(ge_, zero)
    at_zero = jnp.logical_and(c_pos < kf, c_nn >= kf)
    lo0 = jnp.where(at_zero, 0.0, jnp.where(c_pos >= kf, jnp.maximum(lo0, 0.0), lo0))
    hi0 = jnp.where(at_zero, 0.0, jnp.where(c_nn < kf, jnp.minimum(hi0, 0.0), hi0))
    skip_search = jnp.logical_or(skip, at_zero)

    def v_cond(st):
        _, _, done, it = st
        return jnp.logical_and(jnp.min(done) < 0.5, it < 4096)

    def v_body(st):
        lo, hi, done, it = st
        mid = jnp.minimum(jnp.maximum(lo * 0.5 + hi * 0.5, lo), hi)
        cnt = count(ge_, mid)
        active = done < 0.5
        ge = cnt >= kf
        hit = cnt == kf
        collapsed = jnp.logical_or(mid <= lo, mid >= hi)
        new_lo = jnp.where(jnp.logical_and(active, ge), mid, lo)
        new_hi = jnp.where(jnp.logical_and(active, jnp.logical_and(jnp.logical_not(ge), jnp.logical_not(collapsed))), mid, hi)
        new_hi = jnp.where(jnp.logical_and(active, hit), mid, new_hi)
        new_done = jnp.where(jnp.logical_or(hit, collapsed), 1.0, done)
        return new_lo, new_hi, new_done, it + 1

    done0 = jnp.where(skip_search, 1.0, 0.0)
    lo, hi, _, _ = lax.while_loop(v_cond, v_body, (lo0, hi0, done0, jnp.int32(0)))
    cnt_hi = count(ge_, hi)
    v = jnp.where(cnt_hi >= kf, hi, lo)
    v = jnp.where(skip, NEG_INF, v)

    cgt = count(gt_, v)
    ceq = count(eq_, v)
    need = kf - cgt
    partial = jnp.logical_and(jnp.logical_not(skip), need < ceq)
    span = (nch * ts).astype(F32) if hasattr(nch, "astype") else float(nch * ts)

    def c_cond(st):
        lo_i, hi_i = st
        return jnp.max(jnp.where(partial, hi_i - lo_i, 0.0)) > 0.0

    def c_body(st):
        lo_i, hi_i = st
        mid = jnp.floor((lo_i + hi_i) * 0.5)
        cnt = count(lambda x, kpos, t, m: jnp.logical_and(x == t, kpos.astype(F32) <= m), v, mid)
        ok = cnt >= need
        return jnp.where(ok, lo_i, mid + 1.0), jnp.where(ok, mid, hi_i)

    _, cut = lax.while_loop(c_cond, c_body, (zero, zero + (span - 1.0)))
    cut = jnp.where(partial, cut, span)
    cut = jnp.where(skip, -1.0, cut)
    rows_of = lambda st: jnp.concatenate([to_rows(st, r0) for r0 in range(0, rows, rb)], axis=0)
    return rows_of(v), rows_of(cut).astype(I32)


def _prompt_attn_kernel(q_ref, qi_ref, wi_ref, kit_ref, kt_ref, v_ref, kn2_ref, o_ref,
                        sc_ref, m_ref, l_ref, acc_ref, *, tq, ts, topk):
    i = pl.program_id(0)
    nch = ((i + 1) * tq + ts - 1) // ts
    row = i * tq + lax.broadcasted_iota(I32, (tq, 1), 0)
    heads = q_ref.shape[1] // HEAD_DIM
    wsc = (wi_ref[...] * IDX_HEADS ** -0.5) * IDX_DIM ** -0.5
    qi_h = [qi_ref[:, h * IDX_DIM:(h + 1) * IDX_DIM] for h in range(IDX_HEADS)]
    w_h = [jnp.broadcast_to(wsc[:, h:h + 1], (tq, ts)) for h in range(IDX_HEADS)]

    def scores(c, carry):
        mn, mx = carry
        kic = kit_ref[c]
        acc = jnp.zeros((tq, ts), F32)
        for h in range(IDX_HEADS):
            lg = jnp.dot(qi_h[h], kic, preferred_element_type=F32)
            acc = acc + jnp.maximum(lg, 0.0) * w_h[h]
        kpos = c * ts + lax.broadcasted_iota(I32, (tq, ts), 1)
        adm = kpos <= row
        sc_ref[c] = jnp.where(adm, acc, NEG_INF)
        mn = jnp.minimum(mn, _lane_fold(jnp.where(adm, acc, jnp.inf), jnp.minimum))
        mx = jnp.maximum(mx, _lane_fold(jnp.where(adm, acc, NEG_INF), jnp.maximum))
        return mn, mx

    mn, mx = lax.fori_loop(0, nch, scores,
                           (jnp.full((tq, 128), jnp.inf, F32), jnp.full((tq, 128), NEG_INF, F32)))
    lo0 = jnp.min(mn, axis=1, keepdims=True)
    hi0 = jnp.max(mx, axis=1, keepdims=True)
    skip = row < topk
    v, cut = _select_threshold(lambda c, r0, nr: sc_ref[c, r0:r0 + nr, :], nch, sc_ref.shape[0], ts, tq, topk,
                               lo0, hi0, skip)

    scale = HEAD_DIM ** -0.5
    group = heads // KV_HEADS
    vw = jnp.concatenate([v] * (ts // 128), axis=1)
    cutw = jnp.concatenate([cut] * (ts // 128), axis=1)
    head = lambda h: slice(h * HEAD_DIM, (h + 1) * HEAD_DIM)

    def selected(c):
        x = sc_ref[c]
        kpos = c * ts + lax.broadcasted_iota(I32, (tq, ts), 1)
        return jnp.logical_or(x > vw, jnp.logical_and(x == vw, kpos <= cutw))

    c2 = scale * math.log2(math.e)
    kmax = jnp.sqrt(jnp.max(kn2_ref[...], axis=1, keepdims=True))
    bound = []
    for h in range(heads):
        qh = q_ref[:, head(h)].astype(F32)
        qn = jnp.sqrt(jnp.sum(qh * qh, axis=1, keepdims=True))
        bound.append(jnp.broadcast_to(qn * (kmax[h // group:h // group + 1, :] * c2), (tq, ts)))
    ones = jnp.ones((ts, HEAD_DIM), BF16)
    acc_ref[...] = jnp.zeros(acc_ref.shape, F32)

    def attend_bounded(c, _):
        keep = jnp.where(selected(c), 1.0, 0.0).astype(BF16)
        kc = kt_ref[c]
        vc = v_ref[c]
        for g in range(KV_HEADS):
            v_ones = jnp.concatenate([vc[:, head(g)], ones], axis=1)
            for h in range(g * group, (g + 1) * group):
                s = jnp.dot(q_ref[:, head(h)], kc[head(g), :], preferred_element_type=F32)
                p = jnp.exp2(s * c2 - bound[h]).astype(BF16) * keep
                acc_ref[h] += jnp.dot(p, v_ones, preferred_element_type=F32)
        return 0

    lax.fori_loop(0, nch, attend_bounded, 0)
    lmin = jnp.full((tq, HEAD_DIM), jnp.inf, F32)
    for h in range(heads):
        acc = acc_ref[h]
        lsum = acc[:, HEAD_DIM:]
        lmin = jnp.minimum(lmin, lsum)
        o_ref[:, head(h)] = (acc[:, :HEAD_DIM] / lsum).astype(o_ref.dtype)

    @pl.when(jnp.logical_not(jnp.min(lmin) > 1e-30))
    def _():
        m_ref[...] = jnp.full(m_ref.shape, -1e30, F32)
        l_ref[...] = jnp.zeros(l_ref.shape, F32)
        acc_ref[...] = jnp.zeros(acc_ref.shape, F32)

        def attend_online(c, _):
            bias = jnp.where(selected(c), 0.0, NEG_INF)
            kc = kt_ref[c]
            vc = v_ref[c]
            for h in range(heads):
                g = h // group
                s = jnp.dot(q_ref[:, head(h)], kc[head(g), :], preferred_element_type=F32) * scale + bias
                m_old = m_ref[h]
                m_new = jnp.maximum(m_old, jnp.max(s, axis=1, keepdims=True))
                alpha = jnp.exp(m_old - m_new)
                p = jnp.exp(s - m_new)
                l_ref[h] = l_ref[h] * alpha + jnp.sum(p, axis=1, keepdims=True)
                acc_ref[h, :, :HEAD_DIM] = acc_ref[h, :, :HEAD_DIM] * alpha + jnp.dot(
                    p.astype(BF16), vc[:, head(g)], preferred_element_type=F32)
                m_ref[h] = m_new
            return 0

        lax.fori_loop(0, nch, attend_online, 0)
        for h in range(heads):
            o_ref[:, head(h)] = (acc_ref[h, :, :HEAD_DIM] / l_ref[h]).astype(o_ref.dtype)


SAMPLE_TS = 4 * PAGE_SIZE
SAMPLE_RING = 8


def _sample_attn_kernel(pt_ref, q_ref, qi_ref, wi_ref, kin_ref, kn_ref, vn_ref, cki_ref, ck_ref, cv_ref,
                        o_ref, kibuf, kbuf, vbuf, sc_ref, sem_i, sem_k, sem_v, *, n_pages, t_new, topk):
    b = pl.program_id(0)
    ts = SAMPLE_TS
    ppc = ts // PAGE_SIZE
    nch = n_pages // ppc
    past = n_pages * PAGE_SIZE
    rows = q_ref.shape[2]

    def ki_copy(p):
        dst = kibuf.at[:, pl.ds(pl.multiple_of(p * PAGE_SIZE, PAGE_SIZE), PAGE_SIZE)]
        return pltpu.make_async_copy(cki_ref.at[0, pt_ref[b, p]], dst, sem_i.at[0])

    def kv_copies(c, slot):
        cps = []
        for j in range(ppc):
            page = pt_ref[b, c * ppc + j]
            dst = pl.ds(j * PAGE_SIZE, PAGE_SIZE)
            for g in range(KV_HEADS):
                cps.append(pltpu.make_async_copy(ck_ref.at[0, page, :, g, :], kbuf.at[slot, g, dst], sem_k.at[slot]))
                cps.append(pltpu.make_async_copy(cv_ref.at[0, page, :, g, :], vbuf.at[slot, g, dst], sem_v.at[slot]))
        return cps

    def start_ki(p, _):
        ki_copy(p).start()
        return 0

    def wait_ki(p, _):
        ki_copy(p).wait()
        return 0

    lax.fori_loop(0, n_pages, start_ki, 0)
    ring = kbuf.shape[0]
    lanes = next(n for n in (4, 2, 1) if nch % n == 0)
    ahead = ring - lanes
    assert ahead >= 1
    for c0 in range(min(ahead, nch)):
        for cp in kv_copies(c0, c0):
            cp.start()
    lax.fori_loop(0, n_pages, wait_ki, 0)

    wcol = (wi_ref[0] * IDX_HEADS ** -0.5) * IDX_DIM ** -0.5
    qi = qi_ref[0]

    def index_scores(kct):
        width = kct.shape[1]
        lg = jnp.dot(qi, kct, preferred_element_type=F32)
        weighted = jnp.maximum(lg, 0.0) * jnp.broadcast_to(wcol, (IDX_HEADS * rows, width))
        acc = weighted[0:rows]
        for h in range(1, IDX_HEADS):
            acc = acc + weighted[h * rows:(h + 1) * rows]
        return acc

    def scores(i, carry):
        mn, mx = carry
        span = lanes * ts
        acc = index_scores(kibuf[:, pl.ds(pl.multiple_of(i * span, span), span)].astype(BF16))
        sc_ref[i] = acc
        return jnp.minimum(mn, _lane_fold(acc, jnp.minimum)), jnp.maximum(mx, _lane_fold(acc, jnp.maximum))

    mn, mx = lax.fori_loop(0, nch // lanes, scores,
                           (jnp.full((rows, 128), jnp.inf, F32), jnp.full((rows, 128), NEG_INF, F32)))
    acc = index_scores(kin_ref[0])
    tok = lax.broadcasted_iota(I32, (rows, ts), 0) % t_new
    col = lax.broadcasted_iota(I32, (rows, ts), 1)
    adm = jnp.logical_and(col <= tok, col < t_new)
    span = lanes * ts
    ngrp = nch // lanes + 1
    sc_ref[ngrp - 1] = jnp.concatenate(
        [jnp.where(adm, acc, NEG_INF), jnp.full((rows, span - ts), NEG_INF, F32)], axis=1) if lanes > 1 else jnp.where(
            adm, acc, NEG_INF)
    mn = jnp.minimum(mn, _lane_fold(jnp.where(adm, acc, jnp.inf), jnp.minimum))
    mx = jnp.maximum(mx, _lane_fold(jnp.where(adm, acc, NEG_INF), jnp.maximum))
    lo0 = jnp.min(mn, axis=1, keepdims=True)
    hi0 = jnp.max(mx, axis=1, keepdims=True)
    skip = jnp.full((rows, 1), past + 1 <= topk)
    v, cut = _select_threshold(lambda c, r0, nr: sc_ref[c], ngrp, ngrp, span, rows, topk, lo0, hi0, skip)
    vw = jnp.concatenate([v] * (span // 128), axis=1)
    cutw = jnp.concatenate([cut] * (span // 128), axis=1)
    scale = HEAD_DIM ** -0.5

    def bias_of(c):
        x = sc_ref[c]
        kpos = c * span + lax.broadcasted_iota(I32, (rows, span), 1)
        sel = jnp.logical_or(x > vw, jnp.logical_and(x == vw, kpos <= cutw))
        return jnp.where(sel, 0.0, NEG_INF)

    def flash(state, parts):
        m_old, l_old, a_old = state
        s = jnp.concatenate(
            [lax.dot_general(q_ref[0, g], kg, (((1,), (1,)), ((), ())), preferred_element_type=F32) * scale + bias
             for g, bias, kg, _ in parts], axis=0)
        m_new = jnp.maximum(m_old, jnp.max(s, axis=1, keepdims=True))
        alpha = jnp.exp(m_old - m_new)
        p = jnp.exp(s - m_new)
        l_new = l_old * alpha + jnp.sum(p, axis=1, keepdims=True)
        pb = p.astype(BF16)
        pv = jnp.concatenate(
            [jnp.dot(pb[n * rows:(n + 1) * rows], vg, preferred_element_type=F32) for n, (_, _, _, vg) in enumerate(parts)],
            axis=0)
        return m_new, l_new, a_old * alpha + pv

    def receive(c):
        for cp in kv_copies(c, lax.rem(c, ring)):
            cp.wait()

        @pl.when(c + ahead < nch)
        def _():
            for cp in kv_copies(c + ahead, lax.rem(c + ahead, ring)):
                cp.start()

    def attend(i, state):
        for u in range(lanes):
            receive(i * lanes + u)
        parts = []
        bias_grp = bias_of(i)
        for u in range(lanes):
            c = i * lanes + u
            slot = lax.rem(c, ring)
            bias = bias_grp[:, u * ts:(u + 1) * ts]
            parts += [(g, bias, kbuf[slot, g].astype(BF16), vbuf[slot, g].astype(BF16)) for g in range(KV_HEADS)]
        return flash(state, parts)

    pieces = lanes * KV_HEADS
    init = (jnp.full((pieces * rows, 1), -1e30, F32), jnp.zeros((pieces * rows, 1), F32),
            jnp.zeros((pieces * rows, HEAD_DIM), F32))
    m_all, l_all, a_all = lax.fori_loop(0, nch // lanes, attend, init)

    def merge(a, b):
        m = jnp.maximum(a[0], b[0])
        fa, fb = jnp.exp(a[0] - m), jnp.exp(b[0] - m)
        return m, a[1] * fa + b[1] * fb, a[2] * fa + b[2] * fb

    per_lane = KV_HEADS * rows
    state = tuple(x[0:per_lane] for x in (m_all, l_all, a_all))
    for u in range(1, lanes):
        state = merge(state, tuple(x[u * per_lane:(u + 1) * per_lane] for x in (m_all, l_all, a_all)))
    bias = bias_of(ngrp - 1)[:, :ts]
    new_parts = [(g, bias, kn_ref[0][:, g * HEAD_DIM:(g + 1) * HEAD_DIM], vn_ref[0][:, g * HEAD_DIM:(g + 1) * HEAD_DIM])
                 for g in range(KV_HEADS)]
    _, l_fin, a_fin = flash(state, new_parts)
    out = a_fin / l_fin
    for g in range(KV_HEADS):
        o_ref[0, g] = out[g * rows:(g + 1) * rows].astype(o_ref.dtype)


def sample_attention(q, qi, wi, ki_new, k_new, v_new, cache_kidx, cache_k, cache_v, page_table, *, topk):
    nb, t_new, width = q.shape
    heads = width // HEAD_DIM
    group = heads // KV_HEADS
    n_pages = page_table.shape[1]
    ts = SAMPLE_TS
    assert n_pages % (ts // PAGE_SIZE) == 0
    nch = n_pages // (ts // PAGE_SIZE)
    lanes = next(n for n in (4, 2, 1) if nch % n == 0)
    rows = group * t_new
    qg = q.reshape(nb, t_new, KV_HEADS, group, HEAD_DIM).transpose(0, 2, 3, 1, 4).reshape(nb, KV_HEADS, rows, HEAD_DIM)
    qir = jnp.tile(qi.reshape(nb, t_new, IDX_HEADS, IDX_DIM).transpose(0, 2, 1, 3), (1, 1, group, 1))
    qir = qir.reshape(nb, IDX_HEADS * rows, IDX_DIM)
    wir = jnp.tile(wi.transpose(0, 2, 1), (1, 1, group)).reshape(nb, IDX_HEADS * rows, 1)
    padk = lambda a: jnp.pad(a, ((0, 0), (0, ts - t_new), (0, 0)))
    per_b = lambda a: pl.BlockSpec((1,) + a.shape[1:], lambda b, pt: (b,) + (0,) * (a.ndim - 1))
    ops = (qg, qir, wir, padk(ki_new).transpose(0, 2, 1), padk(k_new), padk(v_new))
    cache_kidx_t = cache_kidx.transpose(0, 1, 3, 2)
    out = pl.pallas_call(
        functools.partial(_sample_attn_kernel, n_pages=n_pages, t_new=t_new, topk=topk),
        grid_spec=pltpu.PrefetchScalarGridSpec(
            num_scalar_prefetch=1,
            grid=(nb,),
            in_specs=[per_b(a) for a in ops] + [pl.BlockSpec(memory_space=pl.ANY)] * 3,
            out_specs=pl.BlockSpec((1, KV_HEADS, rows, HEAD_DIM), lambda b, pt: (b, 0, 0, 0)),
            scratch_shapes=[
                pltpu.VMEM((IDX_DIM, n_pages * PAGE_SIZE), F32),
                pltpu.VMEM((SAMPLE_RING, KV_HEADS, ts, HEAD_DIM), F32),
                pltpu.VMEM((SAMPLE_RING, KV_HEADS, ts, HEAD_DIM), F32),
                pltpu.VMEM((nch // lanes + 1, rows, lanes * ts), F32),
                pltpu.SemaphoreType.DMA((1,)),
                pltpu.SemaphoreType.DMA((SAMPLE_RING,)),
                pltpu.SemaphoreType.DMA((SAMPLE_RING,)),
            ]),
        out_shape=jax.ShapeDtypeStruct((nb, KV_HEADS, rows, HEAD_DIM), BF16),
        compiler_params=_cparams("arbitrary"),
        name="sample_attention",
    )(page_table, *ops, cache_kidx_t, cache_k, cache_v)
    return out.reshape(nb, KV_HEADS, group, t_new, HEAD_DIM).transpose(0, 3, 1, 2, 4).reshape(nb, t_new, width)


def prompt_attention(qb, qib, wi, kib, kb, vb, kn2, *, tq, ts, topk):
    s, width = qb.shape
    heads = width // HEAD_DIM
    assert s % tq == 0 and s % ts == 0
    nc = s // ts
    kit = kib.reshape(nc, ts, IDX_DIM).transpose(0, 2, 1)
    kt = kb.reshape(nc, ts, KV_HEADS * HEAD_DIM).transpose(0, 2, 1)
    v3 = vb.reshape(nc, ts, KV_HEADS * HEAD_DIM)
    whole = lambda a: pl.BlockSpec(a.shape, lambda i: (0,) * a.ndim, pipeline_mode=pl.Buffered(1))
    return pl.pallas_call(
        functools.partial(_prompt_attn_kernel, tq=tq, ts=ts, topk=topk),
        grid=(s // tq,),
        in_specs=[
            pl.BlockSpec((tq, width), lambda i: (i, 0)),
            pl.BlockSpec((tq, IDX_HEADS * IDX_DIM), lambda i: (i, 0)),
            pl.BlockSpec((tq, IDX_HEADS), lambda i: (i, 0)),
            whole(kit), whole(kt), whole(v3), whole(kn2),
        ],
        out_specs=pl.BlockSpec((tq, width), lambda i: (i, 0)),
        out_shape=jax.ShapeDtypeStruct((s, width), BF16),
        scratch_shapes=[
            pltpu.VMEM((nc, tq, ts), F32),
            pltpu.VMEM((heads, tq, 1), F32),
            pltpu.VMEM((heads, tq, 1), F32),
            pltpu.VMEM((heads, tq, 2 * HEAD_DIM), F32),
        ],
        compiler_params=_cparams("arbitrary"),
        name="prompt_attention",
    )(qb, qib, wi, kit, kt, v3, kn2)


def _row_tile(m, cap):
    t = cap
    while m % t:
        t //= 2
    return t


def _token_stages(x, attend, gdn, lw):
    (nmw, wp, w_out, nfw, wq, sub_keys, w_u, w_v, fw, tabs) = lw
    m = x.shape[0]
    tm = _row_tile(m, 512)
    p = norm_matmul(x, nmw, wp, tm=tm, tn=wp.shape[1] // 2)
    qb, qib, k32, kb, v32, vb, ki32, kib, auxr, kn2 = rope_split(p, *tabs, tm=tm)
    wi, a_pre = auxr[:, 0:IDX_HEADS], auxr[:, 2 * IDX_HEADS:3 * IDX_HEADS]
    ya = attend(qb, qib, wi, kib, kb, vb, kn2)
    yg, new_conv, new_ssm = gdn(p, a_pre)
    h, xn, qp = out_proj(ya, yg, x, w_out, nfw, wq, tm=tm)
    i1, i2, gate = peer_topk(qp, sub_keys, tm=_row_tile(m, 256))
    tok = lambda a: a.reshape(PEER_HEADS * PEER_TOPK, m).T
    gates = peer_gates(tok(i1), tok(i2), tok(gate), tmb=64)
    f = peer_dense(xn, w_u, w_v, gates, tm=_row_tile(m, 1024), ib=8)
    y = residual_norm(h, f, fw, tm=tm)
    return y, k32, v32, ki32, new_conv, new_ssm


def kernel(x_prompt, x_sample, cache_k, cache_v, cache_kidx, page_table, state_conv, state_ssm, norm_mix_w, w_in,
           conv_w, a_log, dt_bias, gdn_norm_w, w_out, norm_ffn_w, peer_wq, peer_sub_keys, peer_u, peer_v,
           norm_final_w):
    depth = w_in.shape[0]
    b, s, d = x_prompt.shape
    nb, t, _ = x_sample.shape
    assert depth == 1 and b == 1, "single layer, single prompt sequence"
    past = page_table.shape[1] * PAGE_SIZE
    heads_g = state_ssm.shape[2]
    conv_ch = state_conv.shape[-1]

    wp = _pack_w_in(w_in[0])
    shared = (norm_mix_w[0][None], wp, w_out[0].astype(BF16), norm_ffn_w[0][None], peer_wq[0].astype(BF16),
              peer_sub_keys[0], peer_u[0].astype(BF16), peer_v[0].astype(BF16), norm_final_w[None])
    gdn_w = (conv_w[0], a_log[0], dt_bias[0], gdn_norm_w[0])

    def attend_p(qb, qib, wi, kib, kb, vb, kn2):
        return prompt_attention(qb, qib, wi, kib, kb, vb, kn2.T, tq=_row_tile(s, 256), ts=_row_tile(s, 512),
                                topk=min(TOPK_MAX, s // 4))

    def gdn_p(p, a_pre):
        conv0 = jnp.zeros((1, CONV_W - 1, conv_ch), F32)
        ssm0 = jnp.zeros((1, heads_g, GDN_DK, GDN_DV), F32)
        block = 2 * GDN_CHUNK if s % (2 * GDN_CHUNK) == 0 else GDN_CHUNK
        assert s % block == 0
        return gdn_mixer(p, a_pre.T[None], conv0, *gdn_w, ssm0, n_seq=1, t_pad=s, t_total=s, chunk=block,
                         sub=GDN_CHUNK)

    tabs_p = _rope_tables(jnp.arange(s))
    y_p, k_p, v_p, ki_p, conv_p, ssm_p = _token_stages(x_prompt.reshape(s, d), attend_p, gdn_p, shared + (tabs_p,))

    m_s = nb * t
    t8 = -(-t // 8) * 8

    def attend_s(qb, qib, wi, kib, kb, vb, kn2):
        seq = lambda a: a.reshape(nb, t, a.shape[-1])
        y = sample_attention(seq(qb), seq(qib), seq(wi), seq(kib), seq(kb), seq(vb), cache_kidx, cache_k, cache_v,
                             page_table, topk=min(TOPK_MAX, (past + t) // 4))
        return y.reshape(m_s, y.shape[-1])

    def gdn_s(p, a_pre):
        pad_t = lambda a: jnp.pad(a.reshape(nb, t, a.shape[-1]), ((0, 0), (0, t8 - t), (0, 0)))
        p8 = pad_t(p).reshape(nb * t8, p.shape[-1])
        at = pad_t(a_pre).transpose(0, 2, 1)
        yg, nconv, nssm = gdn_mixer(p8, at, state_conv[0], *gdn_w, state_ssm[0], n_seq=nb, t_pad=t8, t_total=t,
                                    chunk=t8, sub=8)
        return yg.reshape(nb, t8, yg.shape[-1])[:, :t].reshape(m_s, yg.shape[-1]), nconv, nssm

    tabs_s = tuple(jnp.tile(a, (nb, 1)) for a in _rope_tables(past + jnp.arange(t)))
    y_s, k_s, v_s, ki_s, conv_s, ssm_s = _token_stages(x_sample.reshape(m_s, d), attend_s, gdn_s, shared + (tabs_s,))

    kv = lambda a, n, tt: a.reshape(1, n, tt, KV_HEADS, HEAD_DIM)
    return (y_p.reshape(b, s, d), y_s.reshape(nb, t, d),
            kv(k_p, b, s), kv(v_p, b, s), ki_p.reshape(1, b, s, IDX_DIM), conv_p[None], ssm_p[None],
            kv(k_s, nb, t), kv(v_s, nb, t), ki_s.reshape(1, nb, t, IDX_DIM), conv_s[None], ssm_s[None])
```

```python
import functools
import math

import jax
import jax.numpy as jnp
from jax import lax
from jax.experimental import pallas as pl
from jax.experimental.pallas import tpu as pltpu

F32 = jnp.float32
BF16 = jnp.bfloat16
I32 = jnp.int32
EPS = 1e-6
NEG_INF = float("-inf")

HEAD_DIM = 128
KV_HEADS = 2
IDX_HEADS = 8
IDX_DIM = 64
TOPK_MAX = 256
ROPE_THETA = 10000.0
PAGE_SIZE = 128
GDN_DK = 128
GDN_DV = 128
CONV_W = 4
GDN_CHUNK = 64
PEER_HEADS = 8
PEER_KEYS = 128
PEER_QDIM = 128
PEER_TOPK = 16

VMEM_LIMIT_BYTES = 56 * 1024 * 1024


def _cparams(*sem):
    return pltpu.CompilerParams(dimension_semantics=sem, vmem_limit_bytes=VMEM_LIMIT_BYTES)


_PROJ_SRC = (("q", 1024), ("k", 256), ("v", 256), ("qi", 512), ("ki", 64), ("wi", 8), ("qkv", 3072), ("z", 1024),
             ("b", 8), ("a", 8))
_PROJ_DST = ("q", "qi", "k", "v", "qkv", "z", "ki", "wi", "b", "a")
PROJ_PACKED = 6400


def _pack_w_in_kernel(w_ref, o_ref):
    src, off = {}, 0
    for name, width in _PROJ_SRC:
        src[name] = (off, width)
        off += width
    dst = 0
    for name in _PROJ_DST:
        s0, width = src[name]
        o_ref[:, dst:dst + width] = w_ref[:, s0:s0 + width].astype(o_ref.dtype)
        dst += width
    o_ref[:, dst:] = jnp.zeros((o_ref.shape[0], o_ref.shape[1] - dst), o_ref.dtype)


def _pack_w_in(w_in):
    d, n = w_in.shape
    assert n == sum(w for _, w in _PROJ_SRC)
    tr = _row_tile(d, 256)
    return pl.pallas_call(
        _pack_w_in_kernel,
        grid=(d // tr,),
        in_specs=[pl.BlockSpec((tr, n), lambda i: (i, 0))],
        out_specs=pl.BlockSpec((tr, PROJ_PACKED), lambda i: (i, 0)),
        out_shape=jax.ShapeDtypeStruct((d, PROJ_PACKED), BF16),
        compiler_params=_cparams("arbitrary"),
        name="pack_w_in",
    )(w_in)


def _rope_tables(pos):
    def table(dim):
        half = dim // 2
        inv = ROPE_THETA ** (-jnp.arange(half, dtype=F32) / half)
        ang = pos.astype(F32)[:, None] * inv[None, :]
        cos = jnp.tile(jnp.cos(ang), (1, 128 // half))
        sin = jnp.tile(jnp.concatenate([-jnp.sin(ang), jnp.sin(ang)], axis=1), (1, 128 // dim))
        return cos, sin
    cq, sq = table(HEAD_DIM)
    ci, si = table(IDX_DIM)
    return cq, sq, ci, si


def _norm_matmul_kernel(x_ref, nw_ref, w_ref, o_ref):
    x = x_ref[...]
    y = x * lax.rsqrt(jnp.mean(x * x, axis=-1, keepdims=True) + EPS)
    xn = (y * nw_ref[...]).astype(BF16)
    o_ref[...] = jnp.dot(xn, w_ref[...], preferred_element_type=F32)


def norm_matmul(x, nw, w, *, tm, tn):
    m, d = x.shape
    n = w.shape[1]
    assert m % tm == 0 and n % tn == 0
    return pl.pallas_call(
        _norm_matmul_kernel,
        grid=(n // tn, m // tm),
        in_specs=[
            pl.BlockSpec((tm, d), lambda j, i: (i, 0)),
            pl.BlockSpec((1, d), lambda j, i: (0, 0)),
            pl.BlockSpec((d, tn), lambda j, i: (0, j)),
        ],
        out_specs=pl.BlockSpec((tm, tn), lambda j, i: (i, j)),
        out_shape=jax.ShapeDtypeStruct((m, n), F32),
        compiler_params=_cparams("arbitrary", "arbitrary"),
        name="norm_matmul",
    )(x, nw, w)


def _rope128(x, cos, sin_signed):
    return x * cos + pltpu.roll(x, 64, 1) * sin_signed


def _rope64(x, cos, sin_signed, first_half):
    partner = jnp.where(first_half, pltpu.roll(x, 96, 1), pltpu.roll(x, 32, 1))
    return x * cos + partner * sin_signed


def _rope_split_kernel(q_ref, qi_ref, k_ref, v_ref, aux_ref, cq_ref, sq_ref, ci_ref, si_ref,
                       qb_ref, qib_ref, k32_ref, kb_ref, v32_ref, vb_ref, ki32_ref, kib_ref, auxr_ref, kn2_ref):
    cq, sq, ci, si = cq_ref[...], sq_ref[...], ci_ref[...], si_ref[...]
    lane = lax.broadcasted_iota(I32, ci.shape, 1)
    first_half = (lane % IDX_DIM) < (IDX_DIM // 2)
    for h in range(q_ref.shape[1] // HEAD_DIM):
        sl = slice(h * HEAD_DIM, (h + 1) * HEAD_DIM)
        qb_ref[:, sl] = _rope128(q_ref[:, sl], cq, sq).astype(BF16)
    for h in range(k_ref.shape[1] // HEAD_DIM):
        sl = slice(h * HEAD_DIM, (h + 1) * HEAD_DIM)
        kr = _rope128(k_ref[:, sl], cq, sq)
        k32_ref[:, sl] = kr
        krb = kr.astype(BF16)
        kb_ref[:, sl] = krb
        kn2_ref[:, h:h + 1] = jnp.sum(krb.astype(F32) * krb.astype(F32), axis=1, keepdims=True)
    for h in range(qi_ref.shape[1] // 128):
        sl = slice(h * 128, (h + 1) * 128)
        qib_ref[:, sl] = _rope64(qi_ref[:, sl], ci, si, first_half).astype(BF16)
    v = v_ref[...]
    v32_ref[...] = v
    vb_ref[...] = v.astype(BF16)
    aux = aux_ref[...]
    kir = _rope64(aux, ci, si, first_half)[:, :IDX_DIM]
    ki32_ref[...] = kir
    kib_ref[...] = kir.astype(BF16)
    auxr_ref[...] = aux[:, IDX_DIM:]


def rope_split(p, cq, sq, ci, si, *, tm):
    m = p.shape[0]
    assert m % tm == 0
    row = lambda w, j: pl.BlockSpec((tm, w), lambda i, j=j: (i, j))
    outs = [
        ((m, 1024), BF16), ((m, 512), BF16), ((m, 256), F32), ((m, 256), BF16), ((m, 256), F32),
        ((m, 256), BF16), ((m, IDX_DIM), F32), ((m, IDX_DIM), BF16), ((m, 128 - IDX_DIM), F32),
        ((m, KV_HEADS), F32),
    ]
    return pl.pallas_call(
        _rope_split_kernel,
        grid=(m // tm,),
        in_specs=[row(1024, 0), row(512, 2), row(256, 6), row(256, 7), row(128, 48),
                  row(128, 0), row(128, 0), row(128, 0), row(128, 0)],
        out_specs=[pl.BlockSpec((tm, s[1]), lambda i: (i, 0)) for s, _ in outs],
        out_shape=[jax.ShapeDtypeStruct(s, dt) for s, dt in outs],
        compiler_params=_cparams("arbitrary"),
        name="rope_split",
    )(p, p, p, p, p, cq, sq, ci, si)


def _out_proj_kernel(ya_ref, yg_ref, res_ref, wa_ref, wg_ref, nw_ref, wq_ref, h_ref, xn_ref, qp_ref):
    h = res_ref[...] + jnp.dot(ya_ref[...], wa_ref[...], preferred_element_type=F32)
    h = h + jnp.dot(yg_ref[...], wg_ref[...], preferred_element_type=F32)
    h_ref[...] = h
    y = h * lax.rsqrt(jnp.mean(h * h, axis=-1, keepdims=True) + EPS)
    xn = (y * nw_ref[...]).astype(BF16)
    xn_ref[...] = xn
    qp_ref[...] = jnp.dot(xn, wq_ref[...], preferred_element_type=F32)


def out_proj(ya, yg, res, w_out, nw, wq, *, tm):
    m, d = res.shape
    half = ya.shape[1]
    nq = wq.shape[1]
    assert m % tm == 0
    rows = lambda wd: pl.BlockSpec((tm, wd), lambda i: (i, 0))
    const = lambda a: pl.BlockSpec(a.shape, lambda i: (0,) * a.ndim, pipeline_mode=pl.Buffered(1))
    wa, wg = w_out[:half], w_out[half:]
    return pl.pallas_call(
        _out_proj_kernel,
        grid=(m // tm,),
        in_specs=[rows(half), rows(half), rows(d), const(wa), const(wg), const(nw), const(wq)],
        out_specs=[rows(d), rows(d), rows(nq)],
        out_shape=[jax.ShapeDtypeStruct((m, d), F32), jax.ShapeDtypeStruct((m, d), BF16),
                   jax.ShapeDtypeStruct((m, nq), F32)],
        compiler_params=_cparams("arbitrary"),
        name="out_proj",
    )(ya, yg, res, wa, wg, nw, wq)


def _take_top(s, codes, count):
    big = jnp.int32(2 ** 30)
    vals, picks = [], []
    for _ in range(count):
        m = jnp.max(s, axis=0, keepdims=True)
        pick = jnp.min(jnp.where(s == m, codes, big), axis=0, keepdims=True)
        s = jnp.where(codes == pick, NEG_INF, s)
        vals.append(m)
        picks.append(pick)
    return jnp.concatenate(vals, axis=0), jnp.concatenate(picks, axis=0)


def _lookup(table, sel, count):
    out = jnp.zeros(sel.shape, table.dtype)
    for a in range(count):
        out = jnp.where(sel == a, jnp.broadcast_to(table[a:a + 1, :], sel.shape), out)
    return out


def _peer_topk_kernel(q_ref, sk_ref, i1_ref, i2_ref, g_ref):
    i1_ref[0], i2_ref[0], g_ref[0] = _peer_topk_tile(q_ref[...], sk_ref)


def _peer_topk_tile(q, sk_ref):
    tm = q.shape[0]
    kk = PEER_TOPK
    half = PEER_QDIM // 2
    key_codes = lax.broadcasted_iota(I32, (PEER_KEYS, tm), 0)
    top_v, top_i = [], []
    for c in range(2):
        qs = q[:, c * half:(c + 1) * half].astype(BF16)
        s = lax.dot_general(sk_ref[c], qs, (((1,), (1,)), ((), ())), preferred_element_type=F32)
        vals, idx = _take_top(s, key_codes, kk)
        top_v.append(vals)
        top_i.append(idx)
    pieces, codes = [], []

    def add_piece(cand, a_of_row, b_of_row):
        ok = (a_of_row + 1) * (b_of_row + 1) <= kk
        pieces.append(jnp.where(ok, cand, NEG_INF))
        codes.append(a_of_row * kk + b_of_row)

    split = 4
    for a in range(split):
        rows = -(-(kk // (a + 1)) // 8) * 8
        r = lax.broadcasted_iota(I32, (rows, tm), 0)
        add_piece(top_v[0][a:a + 1, :] + top_v[1][0:rows, :], jnp.full((rows, tm), a, I32), r)
    r8 = lax.broadcasted_iota(I32, (8, tm), 0)
    for b in range(kk // (split + 1)):
        cand = top_v[0][0:8, :] + top_v[1][b:b + 1, :]
        add_piece(jnp.where(r8 >= split, cand, NEG_INF), r8, jnp.full((8, tm), b, I32))
    add_piece(top_v[0][8:kk, :] + top_v[1][0:1, :], r8 + 8, jnp.zeros((8, tm), I32))
    best_s, best_c = _take_top(jnp.concatenate(pieces, axis=0), jnp.concatenate(codes, axis=0), kk)
    i1 = _lookup(top_i[0], lax.shift_right_logical(best_c, 4), kk)
    i2 = _lookup(top_i[1], jnp.bitwise_and(best_c, kk - 1), kk)
    e = jnp.exp(best_s - best_s[0:1, :])
    return i1, i2, e / jnp.sum(e, axis=0, keepdims=True)


def peer_topk(qp, sub_keys, *, tm):
    m = qp.shape[0]
    heads = sub_keys.shape[0]
    assert m % tm == 0 and PEER_TOPK == 16
    sk = sub_keys.reshape(heads * 2, PEER_KEYS, PEER_QDIM // 2).astype(BF16)
    out = pl.BlockSpec((1, PEER_TOPK, tm), lambda i, h: (h, 0, i))
    return pl.pallas_call(
        _peer_topk_kernel,
        grid=(m // tm, heads),
        in_specs=[pl.BlockSpec((tm, PEER_QDIM), lambda i, h: (i, h)),
                  pl.BlockSpec((2, PEER_KEYS, PEER_QDIM // 2), lambda i, h: (h, 0, 0))],
        out_specs=[out, out, out],
        out_shape=[jax.ShapeDtypeStruct((heads, PEER_TOPK, m), I32), jax.ShapeDtypeStruct((heads, PEER_TOPK, m), I32),
                   jax.ShapeDtypeStruct((heads, PEER_TOPK, m), F32)],
        compiler_params=_cparams("arbitrary", "arbitrary"),
        name="peer_topk",
    )(qp, sk)


GATE_TOKENS = 32


def _peer_gate_kernel(i1_ref, i2_ref, g_ref, o_ref):
    tmb = i1_ref.shape[0]
    nk = PEER_KEYS
    sub = lax.broadcasted_iota(I32, (nk, i1_ref.shape[1]), 0)

    def token_block(j, _):
        t0 = pl.multiple_of(j * GATE_TOKENS, GATE_TOKENS)
        per_token = []
        for u in range(GATE_TOKENS):
            wide = lambda ref: jnp.broadcast_to(ref[pl.ds(t0 + u, 1), :], sub.shape)
            p1 = jnp.where(wide(i1_ref) == sub, wide(g_ref), 0.0).astype(BF16)
            p2 = jnp.where(wide(i2_ref) == sub, 1.0, 0.0).astype(BF16)
            gm = lax.dot_general(p1, p2, (((1,), (1,)), ((), ())), preferred_element_type=F32)
            per_token.append(gm.astype(o_ref.dtype))
        block = jnp.stack(per_token, axis=0)
        o_ref[:, pl.ds(t0, GATE_TOKENS), :] = pltpu.einshape("mab->amb", block)
        return 0

    lax.fori_loop(0, tmb // GATE_TOKENS, token_block, 0)


def peer_gates(i1, i2, gate, *, tmb):
    m, slots = i1.shape
    assert m % tmb == 0 and tmb % GATE_TOKENS == 0
    rows = pl.BlockSpec((tmb, slots), lambda i: (i, 0))
    return pl.pallas_call(
        _peer_gate_kernel,
        grid=(m // tmb,),
        in_specs=[rows, rows, rows],
        out_specs=pl.BlockSpec((PEER_KEYS, tmb, PEER_KEYS), lambda i: (0, i, 0)),
        out_shape=jax.ShapeDtypeStruct((PEER_KEYS, m, PEER_KEYS), BF16),
        compiler_params=_cparams("arbitrary"),
        name="peer_gates",
    )(i1, i2, gate)


def _gelu_tanh(x):
    return 0.5 * x * (1.0 + jnp.tanh(math.sqrt(2.0 / math.pi) * (x + 0.044715 * (x * x * x))))


def _peer_dense_kernel(xn_ref, wu_ref, wv_ref, g_ref, o_ref):
    j = pl.program_id(1)
    ib = g_ref.shape[0]
    a = lax.dot_general(xn_ref[...], wu_ref[...], (((1,), (1,)), ((), ())), preferred_element_type=F32)
    act = _gelu_tanh(a)
    hm = jnp.concatenate(
        [(g_ref[u].astype(F32) * act[:, u * PEER_KEYS:(u + 1) * PEER_KEYS]).astype(BF16) for u in range(ib)], axis=1)
    part = jnp.dot(hm, wv_ref[...], preferred_element_type=F32)

    @pl.when(j == 0)
    def _():
        o_ref[...] = part

    @pl.when(j > 0)
    def _():
        o_ref[...] += part


def peer_dense(xn, w_u, w_v, gates, *, tm, ib):
    m, d = xn.shape
    assert m % tm == 0 and PEER_KEYS % ib == 0
    eb = ib * PEER_KEYS
    return pl.pallas_call(
        _peer_dense_kernel,
        grid=(m // tm, PEER_KEYS // ib),
        in_specs=[
            pl.BlockSpec((tm, d), lambda i, j: (i, 0)),
            pl.BlockSpec((eb, d), lambda i, j: (j, 0)),
            pl.BlockSpec((eb, d), lambda i, j: (j, 0)),
            pl.BlockSpec((ib, tm, PEER_KEYS), lambda i, j: (j, i, 0)),
        ],
        out_specs=pl.BlockSpec((tm, d), lambda i, j: (i, 0)),
        out_shape=jax.ShapeDtypeStruct((m, d), F32),
        compiler_params=_cparams("arbitrary", "arbitrary"),
        name="peer_dense",
    )(xn, w_u, w_v, gates)


def _residual_norm_kernel(h_ref, f_ref, w_ref, o_ref):
    y = h_ref[...] + f_ref[...]
    o_ref[...] = y * lax.rsqrt(jnp.mean(y * y, axis=-1, keepdims=True) + EPS) * w_ref[...]


def residual_norm(h, f, w, *, tm):
    m, d = h.shape
    assert m % tm == 0
    rows = pl.BlockSpec((tm, d), lambda i: (i, 0))
    return pl.pallas_call(
        _residual_norm_kernel,
        grid=(m // tm,),
        in_specs=[rows, rows, pl.BlockSpec((1, d), lambda i: (0, 0))],
        out_specs=rows,
        out_shape=jax.ShapeDtypeStruct((m, d), F32),
        compiler_params=_cparams("arbitrary"),
        name="residual_norm",
    )(h, f, w)


_HI = lax.Precision.HIGHEST


def _dot_hi(a, b):
    return jnp.dot(a, b, preferred_element_type=F32, precision=_HI)


_BNN = (((2,), (1,)), ((0,), (0,)))
_BNT = (((2,), (2,)), ((0,), (0,)))
_BTN = (((1,), (1,)), ((0,), (0,)))


def _bdot(a, b, dims=_BNN):
    return lax.dot_general(a.astype(BF16), b.astype(BF16), dims, preferred_element_type=F32)


def _bdot3(a, b, dims=_BNN):
    ah, bh = a.astype(BF16), b.astype(BF16)
    al, bl = (a - ah.astype(F32)).astype(BF16), (b - bh.astype(F32)).astype(BF16)
    dot = lambda x, y: lax.dot_general(x, y, dims, preferred_element_type=F32)
    return dot(ah, bh) + (dot(ah, bl) + dot(al, bh))


def _sigmoid(x):
    return 1.0 / (1.0 + jnp.exp(-x))


def _softplus(x):
    return jnp.maximum(x, 0.0) + jnp.log1p(jnp.exp(-jnp.abs(x)))


def _gdn_kernel(xq_ref, xk_ref, xv_ref, z_ref, aux_ref, at_ref, cs_ref, cw_ref, alog_ref, dtb_ref,
                alogt_ref, dtbt_ref, nw_ref, s0_ref, y_ref, nconv_ref, nssm_ref, xp_ref, st_ref,
                *, chunk, sub, t_total, b_lane, a_lane):
    c = pl.program_id(1)
    nchunks = pl.num_programs(1)
    heads = st_ref.shape[0]
    width = heads * GDN_DK
    tail = CONV_W - 1

    @pl.when(c == 0)
    def _():
        xp_ref[8 - tail:8, :] = cs_ref[0]
        st_ref[...] = s0_ref[0]

    xp_ref[8:8 + chunk, 0:width] = xq_ref[...]
    xp_ref[8:8 + chunk, width:2 * width] = xk_ref[...]
    xp_ref[8:8 + chunk, 2 * width:3 * width] = xv_ref[...]
    conv = cw_ref[tail:tail + 1, :] * xp_ref[8:8 + chunk, :]
    for j in range(tail):
        conv = conv + cw_ref[j:j + 1, :] * xp_ref[8 - tail + j:8 - tail + j + chunk, :]
    conv = conv * _sigmoid(conv)

    padded = t_total % chunk != 0
    ridx = c * chunk + lax.broadcasted_iota(I32, (chunk, 1), 0)
    rvalid = ridx < t_total
    cidx = c * chunk + lax.broadcasted_iota(I32, (1, chunk), 1)
    cvalid = cidx < t_total

    aux = aux_ref[...]
    beta = _sigmoid(aux[:, b_lane:b_lane + heads])
    g = -jnp.exp(alog_ref[...]) * _softplus(aux[:, a_lane:a_lane + heads] + dtb_ref[...])
    gt = -jnp.exp(alogt_ref[...]) * _softplus(at_ref[0] + dtbt_ref[...])
    if padded:
        beta = jnp.where(rvalid, beta, 0.0)
        g = jnp.where(rvalid, g, 0.0)
        gt = jnp.where(cvalid, gt, 0.0)
    n_sub = chunk // sub
    rf = lax.broadcasted_iota(I32, (chunk, chunk), 0)
    cf = lax.broadcasted_iota(I32, (chunk, chunk), 1)
    same = (rf // sub) == (cf // sub)
    gc = _dot_hi(jnp.where(jnp.logical_and(same, rf >= cf), 1.0, 0.0), g)
    gct = _dot_hi(gt, jnp.where(jnp.logical_and(same, rf <= cf), 1.0, 0.0))
    ri = lax.broadcasted_iota(I32, (sub, sub), 0)
    ci = lax.broadcasted_iota(I32, (sub, sub), 1)
    causal = ri >= ci
    strict = ri > ci
    eye = jnp.where(ri == ci, 1.0, 0.0)

    per_head = lambda x2d, base: jnp.stack(
        [x2d[:, base + h * GDN_DK:base + (h + 1) * GDN_DK] for h in range(heads)], axis=0)
    q = per_head(conv, 0)
    k = per_head(conv, width)
    v = per_head(conv, 2 * width)
    q = q * lax.rsqrt(jnp.sum(q * q, axis=-1, keepdims=True) + EPS) * GDN_DK ** -0.5
    k = k * lax.rsqrt(jnp.sum(k * k, axis=-1, keepdims=True) + EPS)
    if padded:
        k = jnp.where(rvalid[None], k, 0.0)
        v = jnp.where(rvalid[None], v, 0.0)
    gcol = jnp.stack([gc[:, h:h + 1] for h in range(heads)], axis=0)
    grow = jnp.stack([gct[h:h + 1, :] for h in range(heads)], axis=0)
    bcol = jnp.stack([beta[:, h:h + 1] for h in range(heads)], axis=0)
    by_rows = lambda a: jnp.concatenate([a[:, j * sub:(j + 1) * sub] for j in range(n_sub)], axis=0)
    q, k, v, gcol, bcol = (by_rows(a) for a in (q, k, v, gcol, bcol))
    grow = jnp.concatenate([grow[:, :, j * sub:(j + 1) * sub] for j in range(n_sub)], axis=0)
    decay = jnp.where(causal[None], jnp.exp(jnp.where(causal[None], gcol - grow, 0.0)), 0.0)
    kb = k * bcol
    vb = v * bcol
    x = -jnp.where(strict[None], _bdot(kb, k, _BNT) * decay, 0.0)
    tinv = eye[None] + x
    span = 2
    while span < sub:
        x = _bdot3(x, x)
        tinv = tinv + _bdot3(tinv, x)
        span *= 2
    u = _bdot(tinv, vb)
    w = _bdot(tinv, kb * jnp.exp(gcol))
    intra = jnp.where(causal[None], _bdot(q, k, _BNT) * decay, 0.0)
    qg = q * jnp.exp(gcol)
    glast = gcol[:, sub - 1:sub, :]
    kd = k * jnp.exp(glast - gcol)
    state = st_ref[...]
    outs = []
    for j in range(n_sub):
        sl = slice(j * heads, (j + 1) * heads)
        v_new = u[sl] - _bdot(w[sl], state)
        outs.append(_bdot(qg[sl], state) + _bdot(intra[sl], v_new))
        state = state * jnp.exp(glast[sl]) + _bdot(kd[sl], v_new, _BTN)
    st_ref[...] = state
    out = outs[0] if n_sub == 1 else jnp.concatenate(outs, axis=1)
    on = out * lax.rsqrt(jnp.mean(out * out, axis=-1, keepdims=True) + EPS) * nw_ref[...]
    for h in range(heads):
        sl = slice(h * GDN_DV, (h + 1) * GDN_DV)
        zh = z_ref[:, sl]
        y_ref[:, sl] = (on[h] * (zh * _sigmoid(zh))).astype(y_ref.dtype)

    last_valid = t_total - (t_total - 1) // chunk * chunk
    @pl.when(c < nchunks - 1)
    def _():
        xp_ref[8 - tail:8, :] = xp_ref[8 + chunk - tail:8 + chunk, :]

    @pl.when(c == nchunks - 1)
    def _():
        nconv_ref[0] = xp_ref[8 + last_valid - tail:8 + last_valid, :]
        nssm_ref[0] = st_ref[...]


def gdn_mixer(p, at, conv_state, conv_w, a_log, dt_bias, norm_w, ssm_state, *, n_seq, t_pad, t_total, chunk, sub):
    heads = ssm_state.shape[1]
    width = heads * GDN_DK
    assert t_pad % chunk == 0 and chunk % sub == 0 and sub % 8 == 0
    nch = t_pad // chunk
    rows = lambda wd, j: pl.BlockSpec((chunk, wd), lambda n, c, j=j: (n * nch + c, j))
    const = lambda a: pl.BlockSpec(a.shape, lambda n, c: (0,) * a.ndim)
    alog = a_log.reshape(1, heads)
    dtb = dt_bias.reshape(1, heads)
    nw = norm_w.reshape(1, GDN_DV)
    at = at.reshape(n_seq, heads, nch, chunk).transpose(0, 2, 1, 3).reshape(n_seq * nch, heads, chunk)
    kern = functools.partial(_gdn_kernel, chunk=chunk, sub=sub, t_total=t_total, b_lane=IDX_DIM + 8,
                             a_lane=IDX_DIM + 16)
    return pl.pallas_call(
        kern,
        grid=(n_seq, nch),
        in_specs=[
            rows(width, 2), rows(width, 3), rows(width, 4), rows(width, 5), rows(128, 48),
            pl.BlockSpec((1, heads, chunk), lambda n, c: (n * nch + c, 0, 0)),
            pl.BlockSpec((1, CONV_W - 1, 3 * width), lambda n, c: (n, 0, 0)),
            const(conv_w), const(alog), const(dtb), const(alog.T), const(dtb.T), const(nw),
            pl.BlockSpec((1, heads, GDN_DK, GDN_DV), lambda n, c: (n, 0, 0, 0)),
        ],
        out_specs=[
            pl.BlockSpec((chunk, width), lambda n, c: (n * nch + c, 0)),
            pl.BlockSpec((1, CONV_W - 1, 3 * width), lambda n, c: (n, 0, 0)),
            pl.BlockSpec((1, heads, GDN_DK, GDN_DV), lambda n, c: (n, 0, 0, 0)),
        ],
        out_shape=[
            jax.ShapeDtypeStruct((n_seq * t_pad, width), BF16),
            jax.ShapeDtypeStruct((n_seq, CONV_W - 1, 3 * width), F32),
            jax.ShapeDtypeStruct((n_seq, heads, GDN_DK, GDN_DV), F32),
        ],
        scratch_shapes=[pltpu.VMEM((8 + chunk, 3 * width), F32), pltpu.VMEM((heads, GDN_DK, GDN_DV), F32)],
        compiler_params=_cparams("arbitrary", "arbitrary"),
        name="gdn_mixer",
    )(p, p, p, p, p, at, conv_state, conv_w, alog, dtb, alog.T, dtb.T, nw, ssm_state)


def _lane_fold(x, op):
    out = x[:, :128]
    for j in range(1, x.shape[1] // 128):
        out = op(out, x[:, j * 128:(j + 1) * 128])
    return out


def _select_threshold(load, nch, nch_max, ts, rows, topk, lo0, hi0, skip):
    kf = float(topk)
    rb = min(rows, 128)
    assert rows % rb == 0
    assert nch_max * (ts // 128) <= 256
    lane = lax.broadcasted_iota(I32, (rb, 128), 1)
    on_lanes = rb == 128
    if on_lanes:
        state_shape = (8, rows)
        ones = jnp.ones((8, 128), BF16)
        to_state = lambda blk: jnp.transpose(jnp.broadcast_to(blk, (128, 128)))[0:8]
        to_rows = lambda st, r0: jnp.transpose(jnp.broadcast_to(st[0:1, r0:r0 + 128], (128, 128)))
        row_sums = lambda acc: lax.dot_general(ones, acc.astype(BF16), (((1,), (1,)), ((), ())),
                                               preferred_element_type=F32)
        join_axis = 1
    else:
        state_shape = (rows, 128)
        ones = jnp.ones((128, 128), BF16)
        to_state = lambda blk: jnp.broadcast_to(blk, (rb, 128))
        to_rows = lambda st, r0: st[r0:r0 + rb]
        row_sums = lambda acc: jnp.dot(acc.astype(BF16), ones, preferred_element_type=F32)
        join_axis = 0
    join = lambda parts: parts[0] if len(parts) == 1 else jnp.concatenate(parts, axis=join_axis)
    rep = lambda col: join([to_state(col[r0:r0 + rb].astype(F32)) for r0 in range(0, rows, rb)])
    lo0, hi0, skip = rep(lo0), rep(hi0), rep(skip) > 0.5

    def count(pred, *cols):
        blocks = range(0, rows, rb)
        wides = [[to_rows(col, r0) for col in cols] for r0 in blocks]
        accs = []
        for r0, wide in zip(blocks, wides):
            def body(c, acc, r0=r0, wide=wide):
                x = load(c, r0, rb)
                for j in range(ts // 128):
                    kpos = lane + (c * ts + j * 128)
                    acc = acc + jnp.where(pred(x[:, j * 128:(j + 1) * 128], kpos, *wide), 1.0, 0.0)
                return acc

            accs.append(lax.fori_loop(0, nch, body, jnp.zeros((rb, 128), F32)))
        return join([row_sums(acc) for acc in accs])

    ge_ = lambda x, kpos, t: x >= t
    gt_ = lambda x, kpos, t: x > t
    eq_ = lambda x, kpos, t: x == t

    zero = jnp.zeros(state_shape, F32)
    c_pos = count(gt_, zero)
    c_nn = count(ge_, zero)
    at_zero = jnp.logical_and(c_pos < kf, c_nn >= kf)
    lo0 = jnp.where(at_zero, 0.0, jnp.where(c_pos >= kf, jnp.maximum(lo0, 0.0), lo0))
    hi0 = jnp.where(at_zero, 0.0, jnp.where(c_nn < kf, jnp.minimum(hi0, 0.0), hi0))
    skip_search = jnp.logical_or(skip, at_zero)

    def v_cond(st):
        _, _, done, it = st
        return jnp.logical_and(jnp.min(done) < 0.5, it < 4096)

    def v_body(st):
        lo, hi, done, it = st
        mid = jnp.minimum(jnp.maximum(lo * 0.5 + hi * 0.5, lo), hi)
        cnt = count(ge_, mid)
        active = done < 0.5
        ge = cnt >= kf
        hit = cnt == kf
        collapsed = jnp.logical_or(mid <= lo, mid >= hi)
        new_lo = jnp.where(jnp.logical_and(active, ge), mid, lo)
        new_hi = jnp.where(jnp.logical_and(active, jnp.logical_and(jnp.logical_not(ge), jnp.logical_not(collapsed))), mid, hi)
        new_hi = jnp.where(jnp.logical_and(active, hit), mid, new_hi)
        new_done = jnp.where(jnp.logical_or(hit, collapsed), 1.0, done)
        return new_lo, new_hi, new_done, it + 1

    done0 = jnp.where(skip_search, 1.0, 0.0)
    lo, hi, _, _ = lax.while_loop(v_cond, v_body, (lo0, hi0, done0, jnp.int32(0)))
    cnt_hi = count(ge_, hi)
    v = jnp.where(cnt_hi >= kf, hi, lo)
    v = jnp.where(skip, NEG_INF, v)

    cgt = count(gt_, v)
    ceq = count(eq_, v)
    need = kf - cgt
    partial = jnp.logical_and(jnp.logical_not(skip), need < ceq)
    span = (nch * ts).astype(F32) if hasattr(nch, "astype") else float(nch * ts)

    def c_cond(st):
        lo_i, hi_i = st
        return jnp.max(jnp.where(partial, hi_i - lo_i, 0.0)) > 0.0

    def c_body(st):
        lo_i, hi_i = st
        mid = jnp.floor((lo_i + hi_i) * 0.5)
        cnt = count(lambda x, kpos, t, m: jnp.logical_and(x == t, kpos.astype(F32) <= m), v, mid)
        ok = cnt >= need
        return jnp.where(ok, lo_i, mid + 1.0), jnp.where(ok, mid, hi_i)

    _, cut = lax.while_loop(c_cond, c_body, (zero, zero + (span - 1.0)))
    cut = jnp.where(partial, cut, span)
    cut = jnp.where(skip, -1.0, cut)
    rows_of = lambda st: jnp.concatenate([to_rows(st, r0) for r0 in range(0, rows, rb)], axis=0)
    return rows_of(v), rows_of(cut).astype(I32)


def _prompt_attn_kernel(q_ref, qi_ref, wi_ref, kit_ref, kt_ref, v_ref, kn2_ref, o_ref,
                        sc_ref, m_ref, l_ref, acc_ref, *, tq, ts, topk):
    i = pl.program_id(0)
    nch = ((i + 1) * tq + ts - 1) // ts
    row = i * tq + lax.broadcasted_iota(I32, (tq, 1), 0)
    heads = q_ref.shape[1] // HEAD_DIM
    wsc = (wi_ref[...] * IDX_HEADS ** -0.5) * IDX_DIM ** -0.5
    qi_h = [qi_ref[:, h * IDX_DIM:(h + 1) * IDX_DIM] for h in range(IDX_HEADS)]
    w_h = [jnp.broadcast_to(wsc[:, h:h + 1], (tq, ts)) for h in range(IDX_HEADS)]

    def scores(c, carry):
        mn, mx = carry
        kic = kit_ref[c]
        acc = jnp.zeros((tq, ts), F32)
        for h in range(IDX_HEADS):
            lg = jnp.dot(qi_h[h], kic, preferred_element_type=F32)
            acc = acc + jnp.maximum(lg, 0.0) * w_h[h]
        kpos = c * ts + lax.broadcasted_iota(I32, (tq, ts), 1)
        adm = kpos <= row
        sc_ref[c] = jnp.where(adm, acc, NEG_INF)
        mn = jnp.minimum(mn, _lane_fold(jnp.where(adm, acc, jnp.inf), jnp.minimum))
        mx = jnp.maximum(mx, _lane_fold(jnp.where(adm, acc, NEG_INF), jnp.maximum))
        return mn, mx

    mn, mx = lax.fori_loop(0, nch, scores,
                           (jnp.full((tq, 128), jnp.inf, F32), jnp.full((tq, 128), NEG_INF, F32)))
    lo0 = jnp.min(mn, axis=1, keepdims=True)
    hi0 = jnp.max(mx, axis=1, keepdims=True)
    skip = row < topk
    v, cut = _select_threshold(lambda c, r0, nr: sc_ref[c, r0:r0 + nr, :], nch, sc_ref.shape[0], ts, tq, topk,
                               lo0, hi0, skip)

    scale = HEAD_DIM ** -0.5
    group = heads // KV_HEADS
    vw = jnp.concatenate([v] * (ts // 128), axis=1)
    cutw = jnp.concatenate([cut] * (ts // 128), axis=1)
    head = lambda h: slice(h * HEAD_DIM, (h + 1) * HEAD_DIM)

    def selected(c):
        x = sc_ref[c]
        kpos = c * ts + lax.broadcasted_iota(I32, (tq, ts), 1)
        return jnp.logical_or(x > vw, jnp.logical_and(x == vw, kpos <= cutw))

    c2 = scale * math.log2(math.e)
    kmax = jnp.sqrt(jnp.max(kn2_ref[...], axis=1, keepdims=True))
    bound = []
    for h in range(heads):
        qh = q_ref[:, head(h)].astype(F32)
        qn = jnp.sqrt(jnp.sum(qh * qh, axis=1, keepdims=True))
        bound.append(jnp.broadcast_to(qn * (kmax[h // group:h // group + 1, :] * c2), (tq, ts)))
    ones = jnp.ones((ts, HEAD_DIM), BF16)
    acc_ref[...] = jnp.zeros(acc_ref.shape, F32)

    def attend_bounded(c, _):
        keep = jnp.where(selected(c), 1.0, 0.0).astype(BF16)
        kc = kt_ref[c]
        vc = v_ref[c]
        for g in range(KV_HEADS):
            v_ones = jnp.concatenate([vc[:, head(g)], ones], axis=1)
            for h in range(g * group, (g + 1) * group):
                s = jnp.dot(q_ref[:, head(h)], kc[head(g), :], preferred_element_type=F32)
                p = jnp.exp2(s * c2 - bound[h]).astype(BF16) * keep
                acc_ref[h] += jnp.dot(p, v_ones, preferred_element_type=F32)
        return 0

    lax.fori_loop(0, nch, attend_bounded, 0)
    lmin = jnp.full((tq, HEAD_DIM), jnp.inf, F32)
    for h in range(heads):
        acc = acc_ref[h]
        lsum = acc[:, HEAD_DIM:]
        lmin = jnp.minimum(lmin, lsum)
        o_ref[:, head(h)] = (acc[:, :HEAD_DIM] / lsum).astype(o_ref.dtype)

    @pl.when(jnp.logical_not(jnp.min(lmin) > 1e-30))
    def _():
        m_ref[...] = jnp.full(m_ref.shape, -1e30, F32)
        l_ref[...] = jnp.zeros(l_ref.shape, F32)
        acc_ref[...] = jnp.zeros(acc_ref.shape, F32)

        def attend_online(c, _):
            bias = jnp.where(selected(c), 0.0, NEG_INF)
            kc = kt_ref[c]
            vc = v_ref[c]
            for h in range(heads):
                g = h // group
                s = jnp.dot(q_ref[:, head(h)], kc[head(g), :], preferred_element_type=F32) * scale + bias
                m_old = m_ref[h]
                m_new = jnp.maximum(m_old, jnp.max(s, axis=1, keepdims=True))
                alpha = jnp.exp(m_old - m_new)
                p = jnp.exp(s - m_new)
                l_ref[h] = l_ref[h] * alpha + jnp.sum(p, axis=1, keepdims=True)
                acc_ref[h, :, :HEAD_DIM] = acc_ref[h, :, :HEAD_DIM] * alpha + jnp.dot(
                    p.astype(BF16), vc[:, head(g)], preferred_element_type=F32)
                m_ref[h] = m_new
            return 0

        lax.fori_loop(0, nch, attend_online, 0)
        for h in range(heads):
            o_ref[:, head(h)] = (acc_ref[h, :, :HEAD_DIM] / l_ref[h]).astype(o_ref.dtype)


SAMPLE_TS = 4 * PAGE_SIZE
SAMPLE_RING = 8


def _sample_attn_kernel(pt_ref, q_ref, qi_ref, wi_ref, kin_ref, kn_ref, vn_ref, cki_ref, ck_ref, cv_ref,
                        o_ref, kibuf, kbuf, vbuf, sc_ref, sem_i, sem_k, sem_v, *, n_pages, t_new, topk):
    b = pl.program_id(0)
    ts = SAMPLE_TS
    ppc = ts // PAGE_SIZE
    nch = n_pages // ppc
    past = n_pages * PAGE_SIZE
    rows = q_ref.shape[2]

    def ki_copy(p):
        dst = kibuf.at[:, pl.ds(pl.multiple_of(p * PAGE_SIZE, PAGE_SIZE), PAGE_SIZE)]
        return pltpu.make_async_copy(cki_ref.at[0, pt_ref[b, p]], dst, sem_i.at[0])

    def kv_copies(c, slot):
        cps = []
        for j in range(ppc):
            page = pt_ref[b, c * ppc + j]
            dst = pl.ds(j * PAGE_SIZE, PAGE_SIZE)
            for g in range(KV_HEADS):
                cps.append(pltpu.make_async_copy(ck_ref.at[0, page, :, g, :], kbuf.at[slot, g, dst], sem_k.at[slot]))
                cps.append(pltpu.make_async_copy(cv_ref.at[0, page, :, g, :], vbuf.at[slot, g, dst], sem_v.at[slot]))
        return cps

    def start_ki(p, _):
        ki_copy(p).start()
        return 0

    def wait_ki(p, _):
        ki_copy(p).wait()
        return 0

    lax.fori_loop(0, n_pages, start_ki, 0)
    ring = kbuf.shape[0]
    lanes = next(n for n in (4, 2, 1) if nch % n == 0)
    ahead = ring - lanes
    assert ahead >= 1
    for c0 in range(min(ahead, nch)):
        for cp in kv_copies(c0, c0):
            cp.start()
    lax.fori_loop(0, n_pages, wait_ki, 0)

    wcol = (wi_ref[0] * IDX_HEADS ** -0.5) * IDX_DIM ** -0.5
    qi = qi_ref[0]

    def index_scores(kct):
        width = kct.shape[1]
        lg = jnp.dot(qi, kct, preferred_element_type=F32)
        weighted = jnp.maximum(lg, 0.0) * jnp.broadcast_to(wcol, (IDX_HEADS * rows, width))
        acc = weighted[0:rows]
        for h in range(1, IDX_HEADS):
            acc = acc + weighted[h * rows:(h + 1) * rows]
        return acc

    def scores(i, carry):
        mn, mx = carry
        span = lanes * ts
        acc = index_scores(kibuf[:, pl.ds(pl.multiple_of(i * span, span), span)].astype(BF16))
        sc_ref[i] = acc
        return jnp.minimum(mn, _lane_fold(acc, jnp.minimum)), jnp.maximum(mx, _lane_fold(acc, jnp.maximum))

    mn, mx = lax.fori_loop(0, nch // lanes, scores,
                           (jnp.full((rows, 128), jnp.inf, F32), jnp.full((rows, 128), NEG_INF, F32)))
    acc = index_scores(kin_ref[0])
    tok = lax.broadcasted_iota(I32, (rows, ts), 0) % t_new
    col = lax.broadcasted_iota(I32, (rows, ts), 1)
    adm = jnp.logical_and(col <= tok, col < t_new)
    span = lanes * ts
    ngrp = nch // lanes + 1
    sc_ref[ngrp - 1] = jnp.concatenate(
        [jnp.where(adm, acc, NEG_INF), jnp.full((rows, span - ts), NEG_INF, F32)], axis=1) if lanes > 1 else jnp.where(
            adm, acc, NEG_INF)
    mn = jnp.minimum(mn, _lane_fold(jnp.where(adm, acc, jnp.inf), jnp.minimum))
    mx = jnp.maximum(mx, _lane_fold(jnp.where(adm, acc, NEG_INF), jnp.maximum))
    lo0 = jnp.min(mn, axis=1, keepdims=True)
    hi0 = jnp.max(mx, axis=1, keepdims=True)
    skip = jnp.full((rows, 1), past + 1 <= topk)
    v, cut = _select_threshold(lambda c, r0, nr: sc_ref[c], ngrp, ngrp, span, rows, topk, lo0, hi0, skip)
    vw = jnp.concatenate([v] * (span // 128), axis=1)
    cutw = jnp.concatenate([cut] * (span // 128), axis=1)
    scale = HEAD_DIM ** -0.5

    def bias_of(c):
        x = sc_ref[c]
        kpos = c * span + lax.broadcasted_iota(I32, (rows, span), 1)
        sel = jnp.logical_or(x > vw, jnp.logical_and(x == vw, kpos <= cutw))
        return jnp.where(sel, 0.0, NEG_INF)

    def flash(state, parts):
        m_old, l_old, a_old = state
        s = jnp.concatenate(
            [lax.dot_general(q_ref[0, g], kg, (((1,), (1,)), ((), ())), preferred_element_type=F32) * scale + bias
             for g, bias, kg, _ in parts], axis=0)
        m_new = jnp.maximum(m_old, jnp.max(s, axis=1, keepdims=True))
        alpha = jnp.exp(m_old - m_new)
        p = jnp.exp(s - m_new)
        l_new = l_old * alpha + jnp.sum(p, axis=1, keepdims=True)
        pb = p.astype(BF16)
        pv = jnp.concatenate(
            [jnp.dot(pb[n * rows:(n + 1) * rows], vg, preferred_element_type=F32) for n, (_, _, _, vg) in enumerate(parts)],
            axis=0)
        return m_new, l_new, a_old * alpha + pv

    def receive(c):
        for cp in kv_copies(c, lax.rem(c, ring)):
            cp.wait()

        @pl.when(c + ahead < nch)
        def _():
            for cp in kv_copies(c + ahead, lax.rem(c + ahead, ring)):
                cp.start()

    def attend(i, state):
        for u in range(lanes):
            receive(i * lanes + u)
        parts = []
        bias_grp = bias_of(i)
        for u in range(lanes):
            c = i * lanes + u
            slot = lax.rem(c, ring)
            bias = bias_grp[:, u * ts:(u + 1) * ts]
            parts += [(g, bias, kbuf[slot, g].astype(BF16), vbuf[slot, g].astype(BF16)) for g in range(KV_HEADS)]
        return flash(state, parts)

    pieces = lanes * KV_HEADS
    init = (jnp.full((pieces * rows, 1), -1e30, F32), jnp.zeros((pieces * rows, 1), F32),
            jnp.zeros((pieces * rows, HEAD_DIM), F32))
    m_all, l_all, a_all = lax.fori_loop(0, nch // lanes, attend, init)

    def merge(a, b):
        m = jnp.maximum(a[0], b[0])
        fa, fb = jnp.exp(a[0] - m), jnp.exp(b[0] - m)
        return m, a[1] * fa + b[1] * fb, a[2] * fa + b[2] * fb

    per_lane = KV_HEADS * rows
    state = tuple(x[0:per_lane] for x in (m_all, l_all, a_all))
    for u in range(1, lanes):
        state = merge(state, tuple(x[u * per_lane:(u + 1) * per_lane] for x in (m_all, l_all, a_all)))
    bias = bias_of(ngrp - 1)[:, :ts]
    new_parts = [(g, bias, kn_ref[0][:, g * HEAD_DIM:(g + 1) * HEAD_DIM], vn_ref[0][:, g * HEAD_DIM:(g + 1) * HEAD_DIM])
                 for g in range(KV_HEADS)]
    _, l_fin, a_fin = flash(state, new_parts)
    out = a_fin / l_fin
    for g in range(KV_HEADS):
        o_ref[0, g] = out[g * rows:(g + 1) * rows].astype(o_ref.dtype)


def sample_attention(q, qi, wi, ki_new, k_new, v_new, cache_kidx, cache_k, cache_v, page_table, *, topk):
    nb, t_new, width = q.shape
    heads = width // HEAD_DIM
    group = heads // KV_HEADS
    n_pages = page_table.shape[1]
    ts = SAMPLE_TS
    assert n_pages % (ts // PAGE_SIZE) == 0
    nch = n_pages // (ts // PAGE_SIZE)
    lanes = next(n for n in (4, 2, 1) if nch % n == 0)
    rows = group * t_new
    qg = q.reshape(nb, t_new, KV_HEADS, group, HEAD_DIM).transpose(0, 2, 3, 1, 4).reshape(nb, KV_HEADS, rows, HEAD_DIM)
    qir = jnp.tile(qi.reshape(nb, t_new, IDX_HEADS, IDX_DIM).transpose(0, 2, 1, 3), (1, 1, group, 1))
    qir = qir.reshape(nb, IDX_HEADS * rows, IDX_DIM)
    wir = jnp.tile(wi.transpose(0, 2, 1), (1, 1, group)).reshape(nb, IDX_HEADS * rows, 1)
    padk = lambda a: jnp.pad(a, ((0, 0), (0, ts - t_new), (0, 0)))
    per_b = lambda a: pl.BlockSpec((1,) + a.shape[1:], lambda b, pt: (b,) + (0,) * (a.ndim - 1))
    ops = (qg, qir, wir, padk(ki_new).transpose(0, 2, 1), padk(k_new), padk(v_new))
    cache_kidx_t = cache_kidx.transpose(0, 1, 3, 2)
    out = pl.pallas_call(
        functools.partial(_sample_attn_kernel, n_pages=n_pages, t_new=t_new, topk=topk),
        grid_spec=pltpu.PrefetchScalarGridSpec(
            num_scalar_prefetch=1,
            grid=(nb,),
            in_specs=[per_b(a) for a in ops] + [pl.BlockSpec(memory_space=pl.ANY)] * 3,
            out_specs=pl.BlockSpec((1, KV_HEADS, rows, HEAD_DIM), lambda b, pt: (b, 0, 0, 0)),
            scratch_shapes=[
                pltpu.VMEM((IDX_DIM, n_pages * PAGE_SIZE), F32),
                pltpu.VMEM((SAMPLE_RING, KV_HEADS, ts, HEAD_DIM), F32),
                pltpu.VMEM((SAMPLE_RING, KV_HEADS, ts, HEAD_DIM), F32),
                pltpu.VMEM((nch // lanes + 1, rows, lanes * ts), F32),
                pltpu.SemaphoreType.DMA((1,)),
                pltpu.SemaphoreType.DMA((SAMPLE_RING,)),
                pltpu.SemaphoreType.DMA((SAMPLE_RING,)),
            ]),
        out_shape=jax.ShapeDtypeStruct((nb, KV_HEADS, rows, HEAD_DIM), BF16),
        compiler_params=_cparams("arbitrary"),
        name="sample_attention",
    )(page_table, *ops, cache_kidx_t, cache_k, cache_v)
    return out.reshape(nb, KV_HEADS, group, t_new, HEAD_DIM).transpose(0, 3, 1, 2, 4).reshape(nb, t_new, width)


def prompt_attention(qb, qib, wi, kib, kb, vb, kn2, *, tq, ts, topk):
    s, width = qb.shape
    heads = width // HEAD_DIM
    assert s % tq == 0 and s % ts == 0
    nc = s // ts
    kit = kib.reshape(nc, ts, IDX_DIM).transpose(0, 2, 1)
    kt = kb.reshape(nc, ts, KV_HEADS * HEAD_DIM).transpose(0, 2, 1)
    v3 = vb.reshape(nc, ts, KV_HEADS * HEAD_DIM)
    whole = lambda a: pl.BlockSpec(a.shape, lambda i: (0,) * a.ndim, pipeline_mode=pl.Buffered(1))
    return pl.pallas_call(
        functools.partial(_prompt_attn_kernel, tq=tq, ts=ts, topk=topk),
        grid=(s // tq,),
        in_specs=[
            pl.BlockSpec((tq, width), lambda i: (i, 0)),
            pl.BlockSpec((tq, IDX_HEADS * IDX_DIM), lambda i: (i, 0)),
            pl.BlockSpec((tq, IDX_HEADS), lambda i: (i, 0)),
            whole(kit), whole(kt), whole(v3), whole(kn2),
        ],
        out_specs=pl.BlockSpec((tq, width), lambda i: (i, 0)),
        out_shape=jax.ShapeDtypeStruct((s, width), BF16),
        scratch_shapes=[
            pltpu.VMEM((nc, tq, ts), F32),
            pltpu.VMEM((heads, tq, 1), F32),
            pltpu.VMEM((heads, tq, 1), F32),
            pltpu.VMEM((heads, tq, 2 * HEAD_DIM), F32),
        ],
        compiler_params=_cparams("arbitrary"),
        name="prompt_attention",
    )(qb, qib, wi, kit, kt, v3, kn2)


def _row_tile(m, cap):
    t = cap
    while m % t:
        t //= 2
    return t


def _token_stages(x, attend, gdn, lw):
    (nmw, wp, w_out, nfw, wq, sub_keys, w_u, w_v, fw, tabs) = lw
    m = x.shape[0]
    tm = _row_tile(m, 512)
    p = norm_matmul(x, nmw, wp, tm=tm, tn=wp.shape[1] // 2)
    qb, qib, k32, kb, v32, vb, ki32, kib, auxr, kn2 = rope_split(p, *tabs, tm=tm)
    wi, a_pre = auxr[:, 0:IDX_HEADS], auxr[:, 2 * IDX_HEADS:3 * IDX_HEADS]
    ya = attend(qb, qib, wi, kib, kb, vb, kn2)
    yg, new_conv, new_ssm = gdn(p, a_pre)
    h, xn, qp = out_proj(ya, yg, x, w_out, nfw, wq, tm=tm)
    i1, i2, gate = peer_topk(qp, sub_keys, tm=_row_tile(m, 256))
    tok = lambda a: a.reshape(PEER_HEADS * PEER_TOPK, m).T
    gates = peer_gates(tok(i1), tok(i2), tok(gate), tmb=64)
    f = peer_dense(xn, w_u, w_v, gates, tm=_row_tile(m, 1024), ib=8)
    y = residual_norm(h, f, fw, tm=tm)
    return y, k32, v32, ki32, new_conv, new_ssm


def kernel(x_prompt, x_sample, cache_k, cache_v, cache_kidx, page_table, state_conv, state_ssm, norm_mix_w, w_in,
           conv_w, a_log, dt_bias, gdn_norm_w, w_out, norm_ffn_w, peer_wq, peer_sub_keys, peer_u, peer_v,
           norm_final_w):
    depth = w_in.shape[0]
    b, s, d = x_prompt.shape
    nb, t, _ = x_sample.shape
    assert depth == 1 and b == 1, "single layer, single prompt sequence"
    past = page_table.shape[1] * PAGE_SIZE
    heads_g = state_ssm.shape[2]
    conv_ch = state_conv.shape[-1]

    wp = _pack_w_in(w_in[0])
    shared = (norm_mix_w[0][None], wp, w_out[0].astype(BF16), norm_ffn_w[0][None], peer_wq[0].astype(BF16),
              peer_sub_keys[0], peer_u[0].astype(BF16), peer_v[0].astype(BF16), norm_final_w[None])
    gdn_w = (conv_w[0], a_log[0], dt_bias[0], gdn_norm_w[0])

    def attend_p(qb, qib, wi, kib, kb, vb, kn2):
        return prompt_attention(qb, qib, wi, kib, kb, vb, kn2.T, tq=_row_tile(s, 256), ts=_row_tile(s, 512),
                                topk=min(TOPK_MAX, s // 4))

    def gdn_p(p, a_pre):
        conv0 = jnp.zeros((1, CONV_W - 1, conv_ch), F32)
        ssm0 = jnp.zeros((1, heads_g, GDN_DK, GDN_DV), F32)
        block = 2 * GDN_CHUNK if s % (2 * GDN_CHUNK) == 0 else GDN_CHUNK
        assert s % block == 0
        return gdn_mixer(p, a_pre.T[None], conv0, *gdn_w, ssm0, n_seq=1, t_pad=s, t_total=s, chunk=block,
                         sub=GDN_CHUNK)

    tabs_p = _rope_tables(jnp.arange(s))
    y_p, k_p, v_p, ki_p, conv_p, ssm_p = _token_stages(x_prompt.reshape(s, d), attend_p, gdn_p, shared + (tabs_p,))

    m_s = nb * t
    t8 = -(-t // 8) * 8

    def attend_s(qb, qib, wi, kib, kb, vb, kn2):
        seq = lambda a: a.reshape(nb, t, a.shape[-1])
        y = sample_attention(seq(qb), seq(qib), seq(wi), seq(kib), seq(kb), seq(vb), cache_kidx, cache_k, cache_v,
                             page_table, topk=min(TOPK_MAX, (past + t) // 4))
        return y.reshape(m_s, y.shape[-1])

    def gdn_s(p, a_pre):
        pad_t = lambda a: jnp.pad(a.reshape(nb, t, a.shape[-1]), ((0, 0), (0, t8 - t), (0, 0)))
        p8 = pad_t(p).reshape(nb * t8, p.shape[-1])
        at = pad_t(a_pre).transpose(0, 2, 1)
        yg, nconv, nssm = gdn_mixer(p8, at, state_conv[0], *gdn_w, state_ssm[0], n_seq=nb, t_pad=t8, t_total=t,
                                    chunk=t8, sub=8)
        return yg.reshape(nb, t8, yg.shape[-1])[:, :t].reshape(m_s, yg.shape[-1]), nconv, nssm

    tabs_s = tuple(jnp.tile(a, (nb, 1)) for a in _rope_tables(past + jnp.arange(t)))
    y_s, k_s, v_s, ki_s, conv_s, ssm_s = _token_stages(x_sample.reshape(m_s, d), attend_s, gdn_s, shared + (tabs_s,))

    kv = lambda a, n, tt: a.reshape(1, n, tt, KV_HEADS, HEAD_DIM)
    return (y_p.reshape(b, s, d), y_s.reshape(nb, t, d),
            kv(k_p, b, s), kv(v_p, b, s), ki_p.reshape(1, b, s, IDX_DIM), conv_p[None], ssm_p[None],
            kv(k_s, nb, t), kv(v_s, nb, t), ki_s.reshape(1, nb, t, IDX_DIM), conv_s[None], ssm_s[None])
```

```python
import functools
import math

import jax
import jax.numpy as jnp
from jax import lax
from jax.experimental import pallas as pl
from jax.experimental.pallas import tpu as pltpu

F32 = jnp.float32
BF16 = jnp.bfloat16
I32 = jnp.int32
EPS = 1e-6
NEG_INF = float("-inf")

HEAD_DIM = 128
KV_HEADS = 2
IDX_HEADS = 8
IDX_DIM = 64
TOPK_MAX = 256
ROPE_THETA = 10000.0
PAGE_SIZE = 128
GDN_DK = 128
GDN_DV = 128
CONV_W = 4
GDN_CHUNK = 64
PEER_HEADS = 8
PEER_KEYS = 128
PEER_QDIM = 128
PEER_TOPK = 16

VMEM_LIMIT_BYTES = 56 * 1024 * 1024


def _cparams(*sem):
    return pltpu.CompilerParams(dimension_semantics=sem, vmem_limit_bytes=VMEM_LIMIT_BYTES)


_PROJ_SRC = (("q", 1024), ("k", 256), ("v", 256), ("qi", 512), ("ki", 64), ("wi", 8), ("qkv", 3072), ("z", 1024),
             ("b", 8), ("a", 8))
_PROJ_DST = ("q", "qi", "k", "v", "qkv", "z", "ki", "wi", "b", "a")
PROJ_PACKED = 6400


def _pack_w_in_kernel(w_ref, o_ref):
    src, off = {}, 0
    for name, width in _PROJ_SRC:
        src[name] = (off, width)
        off += width
    dst = 0
    for name in _PROJ_DST:
        s0, width = src[name]
        o_ref[:, dst:dst + width] = w_ref[:, s0:s0 + width].astype(o_ref.dtype)
        dst += width
    o_ref[:, dst:] = jnp.zeros((o_ref.shape[0], o_ref.shape[1] - dst), o_ref.dtype)


def _pack_w_in(w_in):
    d, n = w_in.shape
    assert n == sum(w for _, w in _PROJ_SRC)
    tr = _row_tile(d, 256)
    return pl.pallas_call(
        _pack_w_in_kernel,
        grid=(d // tr,),
        in_specs=[pl.BlockSpec((tr, n), lambda i: (i, 0))],
        out_specs=pl.BlockSpec((tr, PROJ_PACKED), lambda i: (i, 0)),
        out_shape=jax.ShapeDtypeStruct((d, PROJ_PACKED), BF16),
        compiler_params=_cparams("arbitrary"),
        name="pack_w_in",
    )(w_in)


def _rope_tables(pos):
    def table(dim):
        half = dim // 2
        inv = ROPE_THETA ** (-jnp.arange(half, dtype=F32) / half)
        ang = pos.astype(F32)[:, None] * inv[None, :]
        cos = jnp.tile(jnp.cos(ang), (1, 128 // half))
        sin = jnp.tile(jnp.concatenate([-jnp.sin(ang), jnp.sin(ang)], axis=1), (1, 128 // dim))
        return cos, sin
    cq, sq = table(HEAD_DIM)
    ci, si = table(IDX_DIM)
    return cq, sq, ci, si


def _norm_matmul_kernel(x_ref, nw_ref, w_ref, o_ref):
    x = x_ref[...]
    y = x * lax.rsqrt(jnp.mean(x * x, axis=-1, keepdims=True) + EPS)
    xn = (y * nw_ref[...]).astype(BF16)
    o_ref[...] = jnp.dot(xn, w_ref[...], preferred_element_type=F32)


def norm_matmul(x, nw, w, *, tm, tn):
    m, d = x.shape
    n = w.shape[1]
    assert m % tm == 0 and n % tn == 0
    return pl.pallas_call(
        _norm_matmul_kernel,
        grid=(n // tn, m // tm),
        in_specs=[
            pl.BlockSpec((tm, d), lambda j, i: (i, 0)),
            pl.BlockSpec((1, d), lambda j, i: (0, 0)),
            pl.BlockSpec((d, tn), lambda j, i: (0, j)),
        ],
        out_specs=pl.BlockSpec((tm, tn), lambda j, i: (i, j)),
        out_shape=jax.ShapeDtypeStruct((m, n), F32),
        compiler_params=_cparams("arbitrary", "arbitrary"),
        name="norm_matmul",
    )(x, nw, w)


def _rope128(x, cos, sin_signed):
    return x * cos + pltpu.roll(x, 64, 1) * sin_signed


def _rope64(x, cos, sin_signed, first_half):
    partner = jnp.where(first_half, pltpu.roll(x, 96, 1), pltpu.roll(x, 32, 1))
    return x * cos + partner * sin_signed


def _rope_split_kernel(q_ref, qi_ref, k_ref, v_ref, aux_ref, cq_ref, sq_ref, ci_ref, si_ref,
                       qb_ref, qib_ref, k32_ref, kb_ref, v32_ref, vb_ref, ki32_ref, kib_ref, auxr_ref, kn2_ref):
    cq, sq, ci, si = cq_ref[...], sq_ref[...], ci_ref[...], si_ref[...]
    lane = lax.broadcasted_iota(I32, ci.shape, 1)
    first_half = (lane % IDX_DIM) < (IDX_DIM // 2)
    for h in range(q_ref.shape[1] // HEAD_DIM):
        sl = slice(h * HEAD_DIM, (h + 1) * HEAD_DIM)
        qb_ref[:, sl] = _rope128(q_ref[:, sl], cq, sq).astype(BF16)
    for h in range(k_ref.shape[1] // HEAD_DIM):
        sl = slice(h * HEAD_DIM, (h + 1) * HEAD_DIM)
        kr = _rope128(k_ref[:, sl], cq, sq)
        k32_ref[:, sl] = kr
        krb = kr.astype(BF16)
        kb_ref[:, sl] = krb
        kn2_ref[:, h:h + 1] = jnp.sum(krb.astype(F32) * krb.astype(F32), axis=1, keepdims=True)
    for h in range(qi_ref.shape[1] // 128):
        sl = slice(h * 128, (h + 1) * 128)
        qib_ref[:, sl] = _rope64(qi_ref[:, sl], ci, si, first_half).astype(BF16)
    v = v_ref[...]
    v32_ref[...] = v
    vb_ref[...] = v.astype(BF16)
    aux = aux_ref[...]
    kir = _rope64(aux, ci, si, first_half)[:, :IDX_DIM]
    ki32_ref[...] = kir
    kib_ref[...] = kir.astype(BF16)
    auxr_ref[...] = aux[:, IDX_DIM:]


def rope_split(p, cq, sq, ci, si, *, tm):
    m = p.shape[0]
    assert m % tm == 0
    row = lambda w, j: pl.BlockSpec((tm, w), lambda i, j=j: (i, j))
    outs = [
        ((m, 1024), BF16), ((m, 512), BF16), ((m, 256), F32), ((m, 256), BF16), ((m, 256), F32),
        ((m, 256), BF16), ((m, IDX_DIM), F32), ((m, IDX_DIM), BF16), ((m, 128 - IDX_DIM), F32),
        ((m, KV_HEADS), F32),
    ]
    return pl.pallas_call(
        _rope_split_kernel,
        grid=(m // tm,),
        in_specs=[row(1024, 0), row(512, 2), row(256, 6), row(256, 7), row(128, 48),
                  row(128, 0), row(128, 0), row(128, 0), row(128, 0)],
        out_specs=[pl.BlockSpec((tm, s[1]), lambda i: (i, 0)) for s, _ in outs],
        out_shape=[jax.ShapeDtypeStruct(s, dt) for s, dt in outs],
        compiler_params=_cparams("arbitrary"),
        name="rope_split",
    )(p, p, p, p, p, cq, sq, ci, si)


def _out_proj_kernel(ya_ref, yg_ref, res_ref, wa_ref, wg_ref, nw_ref, wq_ref, h_ref, xn_ref, qp_ref):
    h = res_ref[...] + jnp.dot(ya_ref[...], wa_ref[...], preferred_element_type=F32)
    h = h + jnp.dot(yg_ref[...], wg_ref[...], preferred_element_type=F32)
    h_ref[...] = h
    y = h * lax.rsqrt(jnp.mean(h * h, axis=-1, keepdims=True) + EPS)
    xn = (y * nw_ref[...]).astype(BF16)
    xn_ref[...] = xn
    qp_ref[...] = jnp.dot(xn, wq_ref[...], preferred_element_type=F32)


def out_proj(ya, yg, res, w_out, nw, wq, *, tm):
    m, d = res.shape
    half = ya.shape[1]
    nq = wq.shape[1]
    assert m % tm == 0
    rows = lambda wd: pl.BlockSpec((tm, wd), lambda i: (i, 0))
    const = lambda a: pl.BlockSpec(a.shape, lambda i: (0,) * a.ndim, pipeline_mode=pl.Buffered(1))
    wa, wg = w_out[:half], w_out[half:]
    return pl.pallas_call(
        _out_proj_kernel,
        grid=(m // tm,),
        in_specs=[rows(half), rows(half), rows(d), const(wa), const(wg), const(nw), const(wq)],
        out_specs=[rows(d), rows(d), rows(nq)],
        out_shape=[jax.ShapeDtypeStruct((m, d), F32), jax.ShapeDtypeStruct((m, d), BF16),
                   jax.ShapeDtypeStruct((m, nq), F32)],
        compiler_params=_cparams("arbitrary"),
        name="out_proj",
    )(ya, yg, res, wa, wg, nw, wq)


def _take_top(s, codes, count):
    big = jnp.int32(2 ** 30)
    vals, picks = [], []
    for _ in range(count):
        m = jnp.max(s, axis=0, keepdims=True)
        pick = jnp.min(jnp.where(s == m, codes, big), axis=0, keepdims=True)
        s = jnp.where(codes == pick, NEG_INF, s)
        vals.append(m)
        picks.append(pick)
    return jnp.concatenate(vals, axis=0), jnp.concatenate(picks, axis=0)


def _lookup(table, sel, count):
    out = jnp.zeros(sel.shape, table.dtype)
    for a in range(count):
        out = jnp.where(sel == a, jnp.broadcast_to(table[a:a + 1, :], sel.shape), out)
    return out


def _peer_topk_kernel(q_ref, sk_ref, i1_ref, i2_ref, g_ref):
    i1_ref[0], i2_ref[0], g_ref[0] = _peer_topk_tile(q_ref[...], sk_ref)


def _peer_topk_tile(q, sk_ref):
    tm = q.shape[0]
    kk = PEER_TOPK
    half = PEER_QDIM // 2
    key_codes = lax.broadcasted_iota(I32, (PEER_KEYS, tm), 0)
    top_v, top_i = [], []
    for c in range(2):
        qs = q[:, c * half:(c + 1) * half].astype(BF16)
        s = lax.dot_general(sk_ref[c], qs, (((1,), (1,)), ((), ())), preferred_element_type=F32)
        vals, idx = _take_top(s, key_codes, kk)
        top_v.append(vals)
        top_i.append(idx)
    pieces, codes = [], []

    def add_piece(cand, a_of_row, b_of_row):
        ok = (a_of_row + 1) * (b_of_row + 1) <= kk
        pieces.append(jnp.where(ok, cand, NEG_INF))
        codes.append(a_of_row * kk + b_of_row)

    split = 4
    for a in range(split):
        rows = -(-(kk // (a + 1)) // 8) * 8
        r = lax.broadcasted_iota(I32, (rows, tm), 0)
        add_piece(top_v[0][a:a + 1, :] + top_v[1][0:rows, :], jnp.full((rows, tm), a, I32), r)
    r8 = lax.broadcasted_iota(I32, (8, tm), 0)
    for b in range(kk // (split + 1)):
        cand = top_v[0][0:8, :] + top_v[1][b:b + 1, :]
        add_piece(jnp.where(r8 >= split, cand, NEG_INF), r8, jnp.full((8, tm), b, I32))
    add_piece(top_v[0][8:kk, :] + top_v[1][0:1, :], r8 + 8, jnp.zeros((8, tm), I32))
    best_s, best_c = _take_top(jnp.concatenate(pieces, axis=0), jnp.concatenate(codes, axis=0), kk)
    i1 = _lookup(top_i[0], lax.shift_right_logical(best_c, 4), kk)
    i2 = _lookup(top_i[1], jnp.bitwise_and(best_c, kk - 1), kk)
    e = jnp.exp(best_s - best_s[0:1, :])
    return i1, i2, e / jnp.sum(e, axis=0, keepdims=True)


def peer_topk(qp, sub_keys, *, tm):
    m = qp.shape[0]
    heads = sub_keys.shape[0]
    assert m % tm == 0 and PEER_TOPK == 16
    sk = sub_keys.reshape(heads * 2, PEER_KEYS, PEER_QDIM // 2).astype(BF16)
    out = pl.BlockSpec((1, PEER_TOPK, tm), lambda i, h: (h, 0, i))
    return pl.pallas_call(
        _peer_topk_kernel,
        grid=(m // tm, heads),
        in_specs=[pl.BlockSpec((tm, PEER_QDIM), lambda i, h: (i, h)),
                  pl.BlockSpec((2, PEER_KEYS, PEER_QDIM // 2), lambda i, h: (h, 0, 0))],
        out_specs=[out, out, out],
        out_shape=[jax.ShapeDtypeStruct((heads, PEER_TOPK, m), I32), jax.ShapeDtypeStruct((heads, PEER_TOPK, m), I32),
                   jax.ShapeDtypeStruct((heads, PEER_TOPK, m), F32)],
        compiler_params=_cparams("arbitrary", "arbitrary"),
        name="peer_topk",
    )(qp, sk)


GATE_TOKENS = 32


def _peer_gate_kernel(i1_ref, i2_ref, g_ref, o_ref):
    tmb = i1_ref.shape[0]
    nk = PEER_KEYS
    sub = lax.broadcasted_iota(I32, (nk, i1_ref.shape[1]), 0)

    def token_block(j, _):
        t0 = pl.multiple_of(j * GATE_TOKENS, GATE_TOKENS)
        per_token = []
        for u in range(GATE_TOKENS):
            wide = lambda ref: jnp.broadcast_to(ref[pl.ds(t0 + u, 1), :], sub.shape)
            p1 = jnp.where(wide(i1_ref) == sub, wide(g_ref), 0.0).astype(BF16)
            p2 = jnp.where(wide(i2_ref) == sub, 1.0, 0.0).astype(BF16)
            gm = lax.dot_general(p1, p2, (((1,), (1,)), ((), ())), preferred_element_type=F32)
            per_token.append(gm.astype(o_ref.dtype))
        block = jnp.stack(per_token, axis=0)
        o_ref[:, pl.ds(t0, GATE_TOKENS), :] = pltpu.einshape("mab->amb", block)
        return 0

    lax.fori_loop(0, tmb // GATE_TOKENS, token_block, 0)


def peer_gates(i1, i2, gate, *, tmb):
    m, slots = i1.shape
    assert m % tmb == 0 and tmb % GATE_TOKENS == 0
    rows = pl.BlockSpec((tmb, slots), lambda i: (i, 0))
    return pl.pallas_call(
        _peer_gate_kernel,
        grid=(m // tmb,),
        in_specs=[rows, rows, rows],
        out_specs=pl.BlockSpec((PEER_KEYS, tmb, PEER_KEYS), lambda i: (0, i, 0)),
        out_shape=jax.ShapeDtypeStruct((PEER_KEYS, m, PEER_KEYS), BF16),
        compiler_params=_cparams("arbitrary"),
        name="peer_gates",
    )(i1, i2, gate)


def _gelu_tanh(x):
    return 0.5 * x * (1.0 + jnp.tanh(math.sqrt(2.0 / math.pi) * (x + 0.044715 * (x * x * x))))


def _peer_dense_kernel(xn_ref, wu_ref, wv_ref, g_ref, o_ref):
    j = pl.program_id(1)
    ib = g_ref.shape[0]
    a = lax.dot_general(xn_ref[...], wu_ref[...], (((1,), (1,)), ((), ())), preferred_element_type=F32)
    act = _gelu_tanh(a)
    hm = jnp.concatenate(
        [(g_ref[u].astype(F32) * act[:, u * PEER_KEYS:(u + 1) * PEER_KEYS]).astype(BF16) for u in range(ib)], axis=1)
    part = jnp.dot(hm, wv_ref[...], preferred_element_type=F32)

    @pl.when(j == 0)
    def _():
        o_ref[...] = part

    @pl.when(j > 0)
    def _():
        o_ref[...] += part


def peer_dense(xn, w_u, w_v, gates, *, tm, ib):
    m, d = xn.shape
    assert m % tm == 0 and PEER_KEYS % ib == 0
    eb = ib * PEER_KEYS
    return pl.pallas_call(
        _peer_dense_kernel,
        grid=(m // tm, PEER_KEYS // ib),
        in_specs=[
            pl.BlockSpec((tm, d), lambda i, j: (i, 0)),
            pl.BlockSpec((eb, d), lambda i, j: (j, 0)),
            pl.BlockSpec((eb, d), lambda i, j: (j, 0)),
            pl.BlockSpec((ib, tm, PEER_KEYS), lambda i, j: (j, i, 0)),
        ],
        out_specs=pl.BlockSpec((tm, d), lambda i, j: (i, 0)),
        out_shape=jax.ShapeDtypeStruct((m, d), F32),
        compiler_params=_cparams("arbitrary", "arbitrary"),
        name="peer_dense",
    )(xn, w_u, w_v, gates)


def _residual_norm_kernel(h_ref, f_ref, w_ref, o_ref):
    y = h_ref[...] + f_ref[...]
    o_ref[...] = y * lax.rsqrt(jnp.mean(y * y, axis=-1, keepdims=True) + EPS) * w_ref[...]


def residual_norm(h, f, w, *, tm):
    m, d = h.shape
    assert m % tm == 0
    rows = pl.BlockSpec((tm, d), lambda i: (i, 0))
    return pl.pallas_call(
        _residual_norm_kernel,
        grid=(m // tm,),
        in_specs=[rows, rows, pl.BlockSpec((1, d), lambda i: (0, 0))],
        out_specs=rows,
        out_shape=jax.ShapeDtypeStruct((m, d), F32),
        compiler_params=_cparams("arbitrary"),
        name="residual_norm",
    )(h, f, w)


_HI = lax.Precision.HIGHEST


def _dot_hi(a, b):
    return jnp.dot(a, b, preferred_element_type=F32, precision=_HI)


_BNN = (((2,), (1,)), ((0,), (0,)))
_BNT = (((2,), (2,)), ((0,), (0,)))
_BTN = (((1,), (1,)), ((0,), (0,)))


def _bdot(a, b, dims=_BNN):
    return lax.dot_general(a.astype(BF16), b.astype(BF16), dims, preferred_element_type=F32)


def _bdot3(a, b, dims=_BNN):
    ah, bh = a.astype(BF16), b.astype(BF16)
    al, bl = (a - ah.astype(F32)).astype(BF16), (b - bh.astype(F32)).astype(BF16)
    dot = lambda x, y: lax.dot_general(x, y, dims, preferred_element_type=F32)
    return dot(ah, bh) + (dot(ah, bl) + dot(al, bh))


def _sigmoid(x):
    return 1.0 / (1.0 + jnp.exp(-x))


def _softplus(x):
    return jnp.maximum(x, 0.0) + jnp.log1p(jnp.exp(-jnp.abs(x)))


def _gdn_kernel(xq_ref, xk_ref, xv_ref, z_ref, aux_ref, at_ref, cs_ref, cw_ref, alog_ref, dtb_ref,
                alogt_ref, dtbt_ref, nw_ref, s0_ref, y_ref, nconv_ref, nssm_ref, xp_ref, st_ref,
                *, chunk, sub, t_total, b_lane, a_lane):
    c = pl.program_id(1)
    nchunks = pl.num_programs(1)
    heads = st_ref.shape[0]
    width = heads * GDN_DK
    tail = CONV_W - 1

    @pl.when(c == 0)
    def _():
        xp_ref[8 - tail:8, :] = cs_ref[0]
        st_ref[...] = s0_ref[0]

    xp_ref[8:8 + chunk, 0:width] = xq_ref[...]
    xp_ref[8:8 + chunk, width:2 * width] = xk_ref[...]
    xp_ref[8:8 + chunk, 2 * width:3 * width] = xv_ref[...]
    conv = cw_ref[tail:tail + 1, :] * xp_ref[8:8 + chunk, :]
    for j in range(tail):
        conv = conv + cw_ref[j:j + 1, :] * xp_ref[8 - tail + j:8 - tail + j + chunk, :]
    conv = conv * _sigmoid(conv)

    padded = t_total % chunk != 0
    ridx = c * chunk + lax.broadcasted_iota(I32, (chunk, 1), 0)
    rvalid = ridx < t_total
    cidx = c * chunk + lax.broadcasted_iota(I32, (1, chunk), 1)
    cvalid = cidx < t_total

    aux = aux_ref[...]
    beta = _sigmoid(aux[:, b_lane:b_lane + heads])
    g = -jnp.exp(alog_ref[...]) * _softplus(aux[:, a_lane:a_lane + heads] + dtb_ref[...])
    gt = -jnp.exp(alogt_ref[...]) * _softplus(at_ref[0] + dtbt_ref[...])
    if padded:
        beta = jnp.where(rvalid, beta, 0.0)
        g = jnp.where(rvalid, g, 0.0)
        gt = jnp.where(cvalid, gt, 0.0)
    n_sub = chunk // sub
    rf = lax.broadcasted_iota(I32, (chunk, chunk), 0)
    cf = lax.broadcasted_iota(I32, (chunk, chunk), 1)
    same = (rf // sub) == (cf // sub)
    gc = _dot_hi(jnp.where(jnp.logical_and(same, rf >= cf), 1.0, 0.0), g)
    gct = _dot_hi(gt, jnp.where(jnp.logical_and(same, rf <= cf), 1.0, 0.0))
    ri = lax.broadcasted_iota(I32, (sub, sub), 0)
    ci = lax.broadcasted_iota(I32, (sub, sub), 1)
    causal = ri >= ci
    strict = ri > ci
    eye = jnp.where(ri == ci, 1.0, 0.0)

    per_head = lambda x2d, base: jnp.stack(
        [x2d[:, base + h * GDN_DK:base + (h + 1) * GDN_DK] for h in range(heads)], axis=0)
    q = per_head(conv, 0)
    k = per_head(conv, width)
    v = per_head(conv, 2 * width)
    q = q * lax.rsqrt(jnp.sum(q * q, axis=-1, keepdims=True) + EPS) * GDN_DK ** -0.5
    k = k * lax.rsqrt(jnp.sum(k * k, axis=-1, keepdims=True) + EPS)
    if padded:
        k = jnp.where(rvalid[None], k, 0.0)
        v = jnp.where(rvalid[None], v, 0.0)
    gcol = jnp.stack([gc[:, h:h + 1] for h in range(heads)], axis=0)
    grow = jnp.stack([gct[h:h + 1, :] for h in range(heads)], axis=0)
    bcol = jnp.stack([beta[:, h:h + 1] for h in range(heads)], axis=0)
    by_rows = lambda a: jnp.concatenate([a[:, j * sub:(j + 1) * sub] for j in range(n_sub)], axis=0)
    q, k, v, gcol, bcol = (by_rows(a) for a in (q, k, v, gcol, bcol))
    grow = jnp.concatenate([grow[:, :, j * sub:(j + 1) * sub] for j in range(n_sub)], axis=0)
    decay = jnp.where(causal[None], jnp.exp(jnp.where(causal[None], gcol - grow, 0.0)), 0.0)
    kb = k * bcol
    vb = v * bcol
    x = -jnp.where(strict[None], _bdot(kb, k, _BNT) * decay, 0.0)
    tinv = eye[None] + x
    span = 2
    while span < sub:
        x = _bdot3(x, x)
        tinv = tinv + _bdot3(tinv, x)
        span *= 2
    u = _bdot(tinv, vb)
    w = _bdot(tinv, kb * jnp.exp(gcol))
    intra = jnp.where(causal[None], _bdot(q, k, _BNT) * decay, 0.0)
    qg = q * jnp.exp(gcol)
    glast = gcol[:, sub - 1:sub, :]
    kd = k * jnp.exp(glast - gcol)
    state = st_ref[...]
    outs = []
    for j in range(n_sub):
        sl = slice(j * heads, (j + 1) * heads)
        v_new = u[sl] - _bdot(w[sl], state)
        outs.append(_bdot(qg[sl], state) + _bdot(intra[sl], v_new))
        state = state * jnp.exp(glast[sl]) + _bdot(kd[sl], v_new, _BTN)
    st_ref[...] = state
    out = outs[0] if n_sub == 1 else jnp.concatenate(outs, axis=1)
    on = out * lax.rsqrt(jnp.mean(out * out, axis=-1, keepdims=True) + EPS) * nw_ref[...]
    for h in range(heads):
        sl = slice(h * GDN_DV, (h + 1) * GDN_DV)
        zh = z_ref[:, sl]
        y_ref[:, sl] = (on[h] * (zh * _sigmoid(zh))).astype(y_ref.dtype)

    last_valid = t_total - (t_total - 1) // chunk * chunk
    @pl.when(c < nchunks - 1)
    def _():
        xp_ref[8 - tail:8, :] = xp_ref[8 + chunk - tail:8 + chunk, :]

    @pl.when(c == nchunks - 1)
    def _():
        nconv_ref[0] = xp_ref[8 + last_valid - tail:8 + last_valid, :]
        nssm_ref[0] = st_ref[...]


def gdn_mixer(p, at, conv_state, conv_w, a_log, dt_bias, norm_w, ssm_state, *, n_seq, t_pad, t_total, chunk, sub):
    heads = ssm_state.shape[1]
    width = heads * GDN_DK
    assert t_pad % chunk == 0 and chunk % sub == 0 and sub % 8 == 0
    nch = t_pad // chunk
    rows = lambda wd, j: pl.BlockSpec((chunk, wd), lambda n, c, j=j: (n * nch + c, j))
    const = lambda a: pl.BlockSpec(a.shape, lambda n, c: (0,) * a.ndim)
    alog = a_log.reshape(1, heads)
    dtb = dt_bias.reshape(1, heads)
    nw = norm_w.reshape(1, GDN_DV)
    at = at.reshape(n_seq, heads, nch, chunk).transpose(0, 2, 1, 3).reshape(n_seq * nch, heads, chunk)
    kern = functools.partial(_gdn_kernel, chunk=chunk, sub=sub, t_total=t_total, b_lane=IDX_DIM + 8,
                             a_lane=IDX_DIM + 16)
    return pl.pallas_call(
        kern,
        grid=(n_seq, nch),
        in_specs=[
            rows(width, 2), rows(width, 3), rows(width, 4), rows(width, 5), rows(128, 48),
            pl.BlockSpec((1, heads, chunk), lambda n, c: (n * nch + c, 0, 0)),
            pl.BlockSpec((1, CONV_W - 1, 3 * width), lambda n, c: (n, 0, 0)),
            const(conv_w), const(alog), const(dtb), const(alog.T), const(dtb.T), const(nw),
            pl.BlockSpec((1, heads, GDN_DK, GDN_DV), lambda n, c: (n, 0, 0, 0)),
        ],
        out_specs=[
            pl.BlockSpec((chunk, width), lambda n, c: (n * nch + c, 0)),
            pl.BlockSpec((1, CONV_W - 1, 3 * width), lambda n, c: (n, 0, 0)),
            pl.BlockSpec((1, heads, GDN_DK, GDN_DV), lambda n, c: (n, 0, 0, 0)),
        ],
        out_shape=[
            jax.ShapeDtypeStruct((n_seq * t_pad, width), BF16),
            jax.ShapeDtypeStruct((n_seq, CONV_W - 1, 3 * width), F32),
            jax.ShapeDtypeStruct((n_seq, heads, GDN_DK, GDN_DV), F32),
        ],
        scratch_shapes=[pltpu.VMEM((8 + chunk, 3 * width), F32), pltpu.VMEM((heads, GDN_DK, GDN_DV), F32)],
        compiler_params=_cparams("arbitrary", "arbitrary"),
        name="gdn_mixer",
    )(p, p, p, p, p, at, conv_state, conv_w, alog, dtb, alog.T, dtb.T, nw, ssm_state)


def _lane_fold(x, op):
    out = x[:, :128]
    for j in range(1, x.shape[1] // 128):
        out = op(out, x[:, j * 128:(j + 1) * 128])
    return out


def _select_threshold(load, nch, nch_max, ts, rows, topk, lo0, hi0, skip):
    kf = float(topk)
    rb = min(rows, 128)
    assert rows % rb == 0
    assert nch_max * (ts // 128) <= 256
    lane = lax.broadcasted_iota(I32, (rb, 128), 1)
    on_lanes = rb == 128
    if on_lanes:
        state_shape = (8, rows)
        ones = jnp.ones((8, 128), BF16)
        to_state = lambda blk: jnp.transpose(jnp.broadcast_to(blk, (128, 128)))[0:8]
        to_rows = lambda st, r0: jnp.transpose(jnp.broadcast_to(st[0:1, r0:r0 + 128], (128, 128)))
        row_sums = lambda acc: lax.dot_general(ones, acc.astype(BF16), (((1,), (1,)), ((), ())),
                                               preferred_element_type=F32)
        join_axis = 1
    else:
        state_shape = (rows, 128)
        ones = jnp.ones((128, 128), BF16)
        to_state = lambda blk: jnp.broadcast_to(blk, (rb, 128))
        to_rows = lambda st, r0: st[r0:r0 + rb]
        row_sums = lambda acc: jnp.dot(acc.astype(BF16), ones, preferred_element_type=F32)
        join_axis = 0
    join = lambda parts: parts[0] if len(parts) == 1 else jnp.concatenate(parts, axis=join_axis)
    rep = lambda col: join([to_state(col[r0:r0 + rb].astype(F32)) for r0 in range(0, rows, rb)])
    lo0, hi0, skip = rep(lo0), rep(hi0), rep(skip) > 0.5

    def count(pred, *cols):
        blocks = range(0, rows, rb)
        wides = [[to_rows(col, r0) for col in cols] for r0 in blocks]
        accs = []
        for r0, wide in zip(blocks, wides):
            def body(c, acc, r0=r0, wide=wide):
                x = load(c, r0, rb)
                for j in range(ts // 128):
                    kpos = lane + (c * ts + j * 128)
                    acc = acc + jnp.where(pred(x[:, j * 128:(j + 1) * 128], kpos, *wide), 1.0, 0.0)
                return acc

            accs.append(lax.fori_loop(0, nch, body, jnp.zeros((rb, 128), F32)))
        return join([row_sums(acc) for acc in accs])

    ge_ = lambda x, kpos, t: x >= t
    gt_ = lambda x, kpos, t: x > t
    eq_ = lambda x, kpos, t: x == t

    zero = jnp.zeros(state_shape, F32)
    span = (nch * ts).astype(F32) if hasattr(nch, "astype") else float(nch * ts)

    def search(state, limit):
        def cond(st):
            _, _, done, _, it = st
            return jnp.logical_and(jnp.min(done) < 0.5, it < limit)

        def body(st):
            lo, hi, done, hit_any, it = st
            mid = jnp.minimum(jnp.maximum(lo * 0.5 + hi * 0.5, lo), hi)
            cnt = count(ge_, mid)
            active = done < 0.5
            ge = jnp.logical_and(active, cnt >= kf)
            hit = jnp.logical_and(active, cnt == kf)
            collapsed = jnp.logical_or(mid <= lo, mid >= hi)
            new_lo = jnp.where(ge, mid, lo)
            new_hi = jnp.where(jnp.logical_and(active, jnp.logical_not(jnp.logical_or(ge, collapsed))), mid, hi)
            new_hi = jnp.where(hit, mid, new_hi)
            new_done = jnp.where(jnp.logical_or(hit, jnp.logical_and(active, collapsed)), 1.0, done)
            return new_lo, new_hi, new_done, jnp.where(hit, 1.0, hit_any), it + 1

        lo, hi, done, hit_any, _ = lax.while_loop(cond, body, state + (jnp.int32(0),))
        return lo, hi, done, hit_any

    c_pos = count(gt_, zero)
    c_nn = count(ge_, zero)
    at_zero = jnp.logical_and(jnp.logical_not(skip), jnp.logical_and(c_pos < kf, c_nn >= kf))
    lo0 = jnp.where(at_zero, 0.0, jnp.where(c_pos >= kf, jnp.maximum(lo0, 0.0), lo0))
    hi0 = jnp.where(at_zero, 0.0, jnp.where(c_nn < kf, jnp.minimum(hi0, 0.0), hi0))
    lo, hi, _, hit_any = search((lo0, hi0, jnp.where(jnp.logical_or(skip, at_zero), 1.0, 0.0), zero), 4096)
    hit = hit_any > 0.5

    def count_collapsed(_):
        cnt_hi = count(ge_, hi)
        vc = jnp.where(cnt_hi >= kf, hi, lo)
        return vc, count(gt_, vc), count(eq_, vc)

    settled = jnp.logical_or(jnp.logical_or(skip, at_zero), hit)
    vc, cgt_c, ceq_c = lax.cond(jnp.min(jnp.where(settled, 1.0, 0.0)) > 0.5,
                                lambda _: (hi, zero, zero), count_collapsed, None)
    v = jnp.where(skip, NEG_INF, jnp.where(at_zero, 0.0, jnp.where(hit, hi, vc)))
    cgt = jnp.where(at_zero, c_pos, jnp.where(hit, kf, cgt_c))
    ceq = jnp.where(at_zero, c_nn - c_pos, jnp.where(hit, 0.0, ceq_c))
    need = kf - cgt
    partial = jnp.logical_and(jnp.logical_not(skip), need < ceq)
    rows_of = lambda st: jnp.concatenate([to_rows(st, r0) for r0 in range(0, rows, rb)], axis=0)
    cut_all = rows_of(jnp.where(skip, -1.0, zero + span))
    blocks = range(0, rows, rb)
    tile_ts = lambda a: jnp.concatenate([a] * (ts // 128), axis=1)

    def cut_by_prefix(_):
        nchp = -(-nch_max // 8) * 8
        crow = lax.broadcasted_iota(I32, (nchp, rb), 0)
        tables = []
        for r0 in blocks:
            vt = tile_ts(to_rows(v, r0))

            def per_chunk(c, tab, r0=r0, vt=vt):
                ties = jnp.where(load(c, r0, rb) == vt, 1.0, 0.0)
                cnt = row_sums(_lane_fold(ties, jnp.add))
                return jnp.where(crow == c, jnp.concatenate([cnt] * (nchp // 8), axis=0), tab)

            tables.append(lax.fori_loop(0, nch, per_chunk, jnp.zeros((nchp, rb), F32)))
        tab = join(tables)
        ci = lax.broadcasted_iota(I32, (nchp, nchp), 0)
        cj = lax.broadcasted_iota(I32, (nchp, nchp), 1)
        upto = _dot_hi(jnp.where(ci >= cj, 1.0, 0.0), tab)
        before = upto < jnp.broadcast_to(need[0:1], tab.shape)
        cstar = jnp.sum(jnp.where(before, 1.0, 0.0), axis=0, keepdims=True)
        left = need[0:1] - jnp.sum(jnp.where(before, tab, 0.0), axis=0, keepdims=True)
        cstar, left = jnp.broadcast_to(cstar, state_shape), jnp.broadcast_to(left, state_shape)
        ki = lax.broadcasted_iota(I32, (ts, ts), 0)
        kj = lax.broadcasted_iota(I32, (ts, ts), 1)
        upper = jnp.where(ki <= kj, 1.0, 0.0).astype(BF16)
        cuts = []
        for r0 in blocks:
            vt, ct = tile_ts(to_rows(v, r0)), tile_ts(to_rows(cstar, r0))

            def crossing(c, m, r0=r0, vt=vt, ct=ct):
                here = jnp.logical_and(load(c, r0, rb) == vt, ct == c.astype(F32))
                return m + jnp.where(here, 1.0, 0.0)

            mask = lax.fori_loop(0, nch, crossing, jnp.zeros((rb, ts), F32))
            seen = jnp.dot(mask.astype(BF16), upper, preferred_element_type=F32)
            inside = jnp.sum(jnp.where(seen < tile_ts(to_rows(left, r0)), 1.0, 0.0), axis=1, keepdims=True)
            cuts.append(to_rows(cstar, r0) * float(ts) + jnp.broadcast_to(inside, (rb, 128)))
        return jnp.where(rows_of(jnp.where(partial, 1.0, 0.0)) > 0.5, jnp.concatenate(cuts, axis=0), cut_all)

    def cut_by_bisection(_):
        def c_cond(st):
            lo_i, hi_i = st
            return jnp.max(jnp.where(partial, hi_i - lo_i, 0.0)) > 0.0

        def c_body(st):
            lo_i, hi_i = st
            mid = jnp.floor((lo_i + hi_i) * 0.5)
            cnt = count(lambda x, kpos, t, m: jnp.logical_and(x == t, kpos.astype(F32) <= m), v, mid)
            ok = cnt >= need
            return jnp.where(ok, lo_i, mid + 1.0), jnp.where(ok, mid, hi_i)

        _, cut = lax.while_loop(c_cond, c_body, (zero, zero + (span - 1.0)))
        return jnp.where(rows_of(jnp.where(partial, 1.0, 0.0)) > 0.5, rows_of(cut), cut_all)

    any_partial = jnp.max(jnp.where(partial, 1.0, 0.0)) > 0.5
    cut = lax.cond(any_partial, cut_by_prefix if on_lanes else cut_by_bisection, lambda _: cut_all, None)
    return rows_of(v), cut.astype(I32)


def _prompt_attn_kernel(q_ref, qi_ref, wi_ref, kit_ref, kt_ref, v_ref, kn2_ref, o_ref,
                        sc_ref, m_ref, l_ref, acc_ref, *, tq, ts, topk):
    i = pl.program_id(0)
    nch = ((i + 1) * tq + ts - 1) // ts
    row = i * tq + lax.broadcasted_iota(I32, (tq, 1), 0)
    heads = q_ref.shape[1] // HEAD_DIM
    wsc = (wi_ref[...] * IDX_HEADS ** -0.5) * IDX_DIM ** -0.5
    qi_h = [qi_ref[:, h * IDX_DIM:(h + 1) * IDX_DIM] for h in range(IDX_HEADS)]
    w_h = [jnp.broadcast_to(wsc[:, h:h + 1], (tq, ts)) for h in range(IDX_HEADS)]

    def scores(c, carry):
        mn, mx = carry
        kic = kit_ref[c]
        acc = jnp.zeros((tq, ts), F32)
        for h in range(IDX_HEADS):
            lg = jnp.dot(qi_h[h], kic, preferred_element_type=F32)
            acc = acc + jnp.maximum(lg, 0.0) * w_h[h]
        kpos = c * ts + lax.broadcasted_iota(I32, (tq, ts), 1)
        adm = kpos <= row
        sc_ref[c] = jnp.where(adm, acc, NEG_INF)
        mn = jnp.minimum(mn, _lane_fold(jnp.where(adm, acc, jnp.inf), jnp.minimum))
        mx = jnp.maximum(mx, _lane_fold(jnp.where(adm, acc, NEG_INF), jnp.maximum))
        return mn, mx

    mn, mx = lax.fori_loop(0, nch, scores,
                           (jnp.full((tq, 128), jnp.inf, F32), jnp.full((tq, 128), NEG_INF, F32)))
    lo0 = jnp.min(mn, axis=1, keepdims=True)
    hi0 = jnp.max(mx, axis=1, keepdims=True)
    skip = row < topk
    v, cut = _select_threshold(lambda c, r0, nr: sc_ref[c, r0:r0 + nr, :], nch, sc_ref.shape[0], ts, tq, topk,
                               lo0, hi0, skip)

    scale = HEAD_DIM ** -0.5
    group = heads // KV_HEADS
    vw = jnp.concatenate([v] * (ts // 128), axis=1)
    cutw = jnp.concatenate([cut] * (ts // 128), axis=1)
    head = lambda h: slice(h * HEAD_DIM, (h + 1) * HEAD_DIM)

    def selected(c):
        x = sc_ref[c]
        kpos = c * ts + lax.broadcasted_iota(I32, (tq, ts), 1)
        return jnp.logical_or(x > vw, jnp.logical_and(x == vw, kpos <= cutw))

    c2 = scale * math.log2(math.e)
    kmax = jnp.sqrt(jnp.max(kn2_ref[...], axis=1, keepdims=True))
    bound = []
    for h in range(heads):
        qh = q_ref[:, head(h)].astype(F32)
        qn = jnp.sqrt(jnp.sum(qh * qh, axis=1, keepdims=True))
        bound.append(jnp.broadcast_to(qn * (kmax[h // group:h // group + 1, :] * c2), (tq, ts)))
    ones = jnp.ones((ts, HEAD_DIM), BF16)
    acc_ref[...] = jnp.zeros(acc_ref.shape, F32)

    def attend_bounded(c, _):
        keep = jnp.where(selected(c), 1.0, 0.0).astype(BF16)
        kc = kt_ref[c]
        vc = v_ref[c]
        for g in range(KV_HEADS):
            v_ones = jnp.concatenate([vc[:, head(g)], ones], axis=1)
            for h in range(g * group, (g + 1) * group):
                s = jnp.dot(q_ref[:, head(h)], kc[head(g), :], preferred_element_type=F32)
                p = jnp.exp2(s * c2 - bound[h]).astype(BF16) * keep
                acc_ref[h] += jnp.dot(p, v_ones, preferred_element_type=F32)
        return 0

    lax.fori_loop(0, nch, attend_bounded, 0)
    lmin = jnp.full((tq, HEAD_DIM), jnp.inf, F32)
    for h in range(heads):
        acc = acc_ref[h]
        lsum = acc[:, HEAD_DIM:]
        lmin = jnp.minimum(lmin, lsum)
        o_ref[:, head(h)] = (acc[:, :HEAD_DIM] / lsum).astype(o_ref.dtype)

    @pl.when(jnp.logical_not(jnp.min(lmin) > 1e-30))
    def _():
        m_ref[...] = jnp.full(m_ref.shape, -1e30, F32)
        l_ref[...] = jnp.zeros(l_ref.shape, F32)
        acc_ref[...] = jnp.zeros(acc_ref.shape, F32)

        def attend_online(c, _):
            bias = jnp.where(selected(c), 0.0, NEG_INF)
            kc = kt_ref[c]
            vc = v_ref[c]
            for h in range(heads):
                g = h // group
                s = jnp.dot(q_ref[:, head(h)], kc[head(g), :], preferred_element_type=F32) * scale + bias
                m_old = m_ref[h]
                m_new = jnp.maximum(m_old, jnp.max(s, axis=1, keepdims=True))
                alpha = jnp.exp(m_old - m_new)
                p = jnp.exp(s - m_new)
                l_ref[h] = l_ref[h] * alpha + jnp.sum(p, axis=1, keepdims=True)
                acc_ref[h, :, :HEAD_DIM] = acc_ref[h, :, :HEAD_DIM] * alpha + jnp.dot(
                    p.astype(BF16), vc[:, head(g)], preferred_element_type=F32)
                m_ref[h] = m_new
            return 0

        lax.fori_loop(0, nch, attend_online, 0)
        for h in range(heads):
            o_ref[:, head(h)] = (acc_ref[h, :, :HEAD_DIM] / l_ref[h]).astype(o_ref.dtype)


SAMPLE_TS = 4 * PAGE_SIZE
SAMPLE_RING = 8


def _sample_attn_kernel(pt_ref, q_ref, qi_ref, wi_ref, kin_ref, kn_ref, vn_ref, cki_ref, ck_ref, cv_ref,
                        o_ref, kibuf, kbuf, vbuf, sc_ref, sem_i, sem_k, sem_v, *, n_pages, t_new, topk):
    b = pl.program_id(0)
    ts = SAMPLE_TS
    ppc = ts // PAGE_SIZE
    nch = n_pages // ppc
    past = n_pages * PAGE_SIZE
    rows = q_ref.shape[2]

    def ki_copy(p):
        dst = kibuf.at[:, pl.ds(pl.multiple_of(p * PAGE_SIZE, PAGE_SIZE), PAGE_SIZE)]
        return pltpu.make_async_copy(cki_ref.at[0, pt_ref[b, p]], dst, sem_i.at[0])

    def kv_copies(c, slot):
        cps = []
        for j in range(ppc):
            page = pt_ref[b, c * ppc + j]
            dst = pl.ds(j * PAGE_SIZE, PAGE_SIZE)
            for g in range(KV_HEADS):
                cps.append(pltpu.make_async_copy(ck_ref.at[0, page, :, g, :], kbuf.at[slot, g, dst], sem_k.at[slot]))
                cps.append(pltpu.make_async_copy(cv_ref.at[0, page, :, g, :], vbuf.at[slot, g, dst], sem_v.at[slot]))
        return cps

    def start_ki(p, _):
        ki_copy(p).start()
        return 0

    def wait_ki(p, _):
        ki_copy(p).wait()
        return 0

    lax.fori_loop(0, n_pages, start_ki, 0)
    ring = kbuf.shape[0]
    lanes = next(n for n in (4, 2, 1) if nch % n == 0)
    ahead = ring - lanes
    assert ahead >= 1
    for c0 in range(min(ahead, nch)):
        for cp in kv_copies(c0, c0):
            cp.start()
    lax.fori_loop(0, n_pages, wait_ki, 0)

    wcol = (wi_ref[0] * IDX_HEADS ** -0.5) * IDX_DIM ** -0.5
    qi = qi_ref[0]

    def index_scores(kct):
        width = kct.shape[1]
        lg = jnp.dot(qi, kct, preferred_element_type=F32)
        weighted = jnp.maximum(lg, 0.0) * jnp.broadcast_to(wcol, (IDX_HEADS * rows, width))
        acc = weighted[0:rows]
        for h in range(1, IDX_HEADS):
            acc = acc + weighted[h * rows:(h + 1) * rows]
        return acc

    def scores(i, carry):
        mn, mx = carry
        span = lanes * ts
        acc = index_scores(kibuf[:, pl.ds(pl.multiple_of(i * span, span), span)].astype(BF16))
        sc_ref[i] = acc
        return jnp.minimum(mn, _lane_fold(acc, jnp.minimum)), jnp.maximum(mx, _lane_fold(acc, jnp.maximum))

    mn, mx = lax.fori_loop(0, nch // lanes, scores,
                           (jnp.full((rows, 128), jnp.inf, F32), jnp.full((rows, 128), NEG_INF, F32)))
    acc = index_scores(kin_ref[0])
    tok = lax.broadcasted_iota(I32, (rows, ts), 0) % t_new
    col = lax.broadcasted_iota(I32, (rows, ts), 1)
    adm = jnp.logical_and(col <= tok, col < t_new)
    span = lanes * ts
    ngrp = nch // lanes + 1
    sc_ref[ngrp - 1] = jnp.concatenate(
        [jnp.where(adm, acc, NEG_INF), jnp.full((rows, span - ts), NEG_INF, F32)], axis=1) if lanes > 1 else jnp.where(
            adm, acc, NEG_INF)
    mn = jnp.minimum(mn, _lane_fold(jnp.where(adm, acc, jnp.inf), jnp.minimum))
    mx = jnp.maximum(mx, _lane_fold(jnp.where(adm, acc, NEG_INF), jnp.maximum))
    lo0 = jnp.min(mn, axis=1, keepdims=True)
    hi0 = jnp.max(mx, axis=1, keepdims=True)
    skip = jnp.full((rows, 1), past + 1 <= topk)
    v, cut = _select_threshold(lambda c, r0, nr: sc_ref[c], ngrp, ngrp, span, rows, topk, lo0, hi0, skip)
    vw = jnp.concatenate([v] * (span // 128), axis=1)
    cutw = jnp.concatenate([cut] * (span // 128), axis=1)
    scale = HEAD_DIM ** -0.5

    def bias_of(c):
        x = sc_ref[c]
        kpos = c * span + lax.broadcasted_iota(I32, (rows, span), 1)
        sel = jnp.logical_or(x > vw, jnp.logical_and(x == vw, kpos <= cutw))
        return jnp.where(sel, 0.0, NEG_INF)

    def flash(state, parts):
        m_old, l_old, a_old = state
        s = jnp.concatenate(
            [lax.dot_general(q_ref[0, g], kg, (((1,), (1,)), ((), ())), preferred_element_type=F32) * scale + bias
             for g, bias, kg, _ in parts], axis=0)
        m_new = jnp.maximum(m_old, jnp.max(s, axis=1, keepdims=True))
        alpha = jnp.exp(m_old - m_new)
        p = jnp.exp(s - m_new)
        l_new = l_old * alpha + jnp.sum(p, axis=1, keepdims=True)
        pb = p.astype(BF16)
        pv = jnp.concatenate(
            [jnp.dot(pb[n * rows:(n + 1) * rows], vg, preferred_element_type=F32) for n, (_, _, _, vg) in enumerate(parts)],
            axis=0)
        return m_new, l_new, a_old * alpha + pv

    def receive(c):
        for cp in kv_copies(c, lax.rem(c, ring)):
            cp.wait()

        @pl.when(c + ahead < nch)
        def _():
            for cp in kv_copies(c + ahead, lax.rem(c + ahead, ring)):
                cp.start()

    def attend(i, state):
        for u in range(lanes):
            receive(i * lanes + u)
        parts = []
        bias_grp = bias_of(i)
        for u in range(lanes):
            c = i * lanes + u
            slot = lax.rem(c, ring)
            bias = bias_grp[:, u * ts:(u + 1) * ts]
            parts += [(g, bias, kbuf[slot, g].astype(BF16), vbuf[slot, g].astype(BF16)) for g in range(KV_HEADS)]
        return flash(state, parts)

    pieces = lanes * KV_HEADS
    init = (jnp.full((pieces * rows, 1), -1e30, F32), jnp.zeros((pieces * rows, 1), F32),
            jnp.zeros((pieces * rows, HEAD_DIM), F32))
    m_all, l_all, a_all = lax.fori_loop(0, nch // lanes, attend, init)

    def merge(a, b):
        m = jnp.maximum(a[0], b[0])
        fa, fb = jnp.exp(a[0] - m), jnp.exp(b[0] - m)
        return m, a[1] * fa + b[1] * fb, a[2] * fa + b[2] * fb

    per_lane = KV_HEADS * rows
    state = tuple(x[0:per_lane] for x in (m_all, l_all, a_all))
    for u in range(1, lanes):
        state = merge(state, tuple(x[u * per_lane:(u + 1) * per_lane] for x in (m_all, l_all, a_all)))
    bias = bias_of(ngrp - 1)[:, :ts]
    new_parts = [(g, bias, kn_ref[0][:, g * HEAD_DIM:(g + 1) * HEAD_DIM], vn_ref[0][:, g * HEAD_DIM:(g + 1) * HEAD_DIM])
                 for g in range(KV_HEADS)]
    _, l_fin, a_fin = flash(state, new_parts)
    out = a_fin / l_fin
    for g in range(KV_HEADS):
        o_ref[0, g] = out[g * rows:(g + 1) * rows].astype(o_ref.dtype)


def sample_attention(q, qi, wi, ki_new, k_new, v_new, cache_kidx, cache_k, cache_v, page_table, *, topk):
    nb, t_new, width = q.shape
    heads = width // HEAD_DIM
    group = heads // KV_HEADS
    n_pages = page_table.shape[1]
    ts = SAMPLE_TS
    assert n_pages % (ts // PAGE_SIZE) == 0
    nch = n_pages // (ts // PAGE_SIZE)
    lanes = next(n for n in (4, 2, 1) if nch % n == 0)
    rows = group * t_new
    qg = q.reshape(nb, t_new, KV_HEADS, group, HEAD_DIM).transpose(0, 2, 3, 1, 4).reshape(nb, KV_HEADS, rows, HEAD_DIM)
    qir = jnp.tile(qi.reshape(nb, t_new, IDX_HEADS, IDX_DIM).transpose(0, 2, 1, 3), (1, 1, group, 1))
    qir = qir.reshape(nb, IDX_HEADS * rows, IDX_DIM)
    wir = jnp.tile(wi.transpose(0, 2, 1), (1, 1, group)).reshape(nb, IDX_HEADS * rows, 1)
    padk = lambda a: jnp.pad(a, ((0, 0), (0, ts - t_new), (0, 0)))
    per_b = lambda a: pl.BlockSpec((1,) + a.shape[1:], lambda b, pt: (b,) + (0,) * (a.ndim - 1))
    ops = (qg, qir, wir, padk(ki_new).transpose(0, 2, 1), padk(k_new), padk(v_new))
    cache_kidx_t = cache_kidx.transpose(0, 1, 3, 2)
    out = pl.pallas_call(
        functools.partial(_sample_attn_kernel, n_pages=n_pages, t_new=t_new, topk=topk),
        grid_spec=pltpu.PrefetchScalarGridSpec(
            num_scalar_prefetch=1,
            grid=(nb,),
            in_specs=[per_b(a) for a in ops] + [pl.BlockSpec(memory_space=pl.ANY)] * 3,
            out_specs=pl.BlockSpec((1, KV_HEADS, rows, HEAD_DIM), lambda b, pt: (b, 0, 0, 0)),
            scratch_shapes=[
                pltpu.VMEM((IDX_DIM, n_pages * PAGE_SIZE), F32),
                pltpu.VMEM((SAMPLE_RING, KV_HEADS, ts, HEAD_DIM), F32),
                pltpu.VMEM((SAMPLE_RING, KV_HEADS, ts, HEAD_DIM), F32),
                pltpu.VMEM((nch // lanes + 1, rows, lanes * ts), F32),
                pltpu.SemaphoreType.DMA((1,)),
                pltpu.SemaphoreType.DMA((SAMPLE_RING,)),
                pltpu.SemaphoreType.DMA((SAMPLE_RING,)),
            ]),
        out_shape=jax.ShapeDtypeStruct((nb, KV_HEADS, rows, HEAD_DIM), BF16),
        compiler_params=_cparams("arbitrary"),
        name="sample_attention",
    )(page_table, *ops, cache_kidx_t, cache_k, cache_v)
    return out.reshape(nb, KV_HEADS, group, t_new, HEAD_DIM).transpose(0, 3, 1, 2, 4).reshape(nb, t_new, width)


def prompt_attention(qb, qib, wi, kib, kb, vb, kn2, *, tq, ts, topk):
    s, width = qb.shape
    heads = width // HEAD_DIM
    assert s % tq == 0 and s % ts == 0
    nc = s // ts
    kit = kib.reshape(nc, ts, IDX_DIM).transpose(0, 2, 1)
    kt = kb.reshape(nc, ts, KV_HEADS * HEAD_DIM).transpose(0, 2, 1)
    v3 = vb.reshape(nc, ts, KV_HEADS * HEAD_DIM)
    whole = lambda a: pl.BlockSpec(a.shape, lambda i: (0,) * a.ndim, pipeline_mode=pl.Buffered(1))
    return pl.pallas_call(
        functools.partial(_prompt_attn_kernel, tq=tq, ts=ts, topk=topk),
        grid=(s // tq,),
        in_specs=[
            pl.BlockSpec((tq, width), lambda i: (i, 0)),
            pl.BlockSpec((tq, IDX_HEADS * IDX_DIM), lambda i: (i, 0)),
            pl.BlockSpec((tq, IDX_HEADS), lambda i: (i, 0)),
            whole(kit), whole(kt), whole(v3), whole(kn2),
        ],
        out_specs=pl.BlockSpec((tq, width), lambda i: (i, 0)),
        out_shape=jax.ShapeDtypeStruct((s, width), BF16),
        scratch_shapes=[
            pltpu.VMEM((nc, tq, ts), F32),
            pltpu.VMEM((heads, tq, 1), F32),
            pltpu.VMEM((heads, tq, 1), F32),
            pltpu.VMEM((heads, tq, 2 * HEAD_DIM), F32),
        ],
        compiler_params=_cparams("arbitrary"),
        name="prompt_attention",
    )(qb, qib, wi, kit, kt, v3, kn2)


def _row_tile(m, cap):
    t = cap
    while m % t:
        t //= 2
    return t


def _token_stages(x, attend, gdn, lw):
    (nmw, wp, w_out, nfw, wq, sub_keys, w_u, w_v, fw, tabs) = lw
    m = x.shape[0]
    tm = _row_tile(m, 512)
    p = norm_matmul(x, nmw, wp, tm=tm, tn=wp.shape[1] // 2)
    qb, qib, k32, kb, v32, vb, ki32, kib, auxr, kn2 = rope_split(p, *tabs, tm=tm)
    wi, a_pre = auxr[:, 0:IDX_HEADS], auxr[:, 2 * IDX_HEADS:3 * IDX_HEADS]
    ya = attend(qb, qib, wi, kib, kb, vb, kn2)
    yg, new_conv, new_ssm = gdn(p, a_pre)
    h, xn, qp = out_proj(ya, yg, x, w_out, nfw, wq, tm=tm)
    i1, i2, gate = peer_topk(qp, sub_keys, tm=_row_tile(m, 256))
    tok = lambda a: a.reshape(PEER_HEADS * PEER_TOPK, m).T
    gates = peer_gates(tok(i1), tok(i2), tok(gate), tmb=64)
    f = peer_dense(xn, w_u, w_v, gates, tm=_row_tile(m, 1024), ib=8)
    y = residual_norm(h, f, fw, tm=tm)
    return y, k32, v32, ki32, new_conv, new_ssm


def kernel(x_prompt, x_sample, cache_k, cache_v, cache_kidx, page_table, state_conv, state_ssm, norm_mix_w, w_in,
           conv_w, a_log, dt_bias, gdn_norm_w, w_out, norm_ffn_w, peer_wq, peer_sub_keys, peer_u, peer_v,
           norm_final_w):
    depth = w_in.shape[0]
    b, s, d = x_prompt.shape
    nb, t, _ = x_sample.shape
    assert depth == 1 and b == 1, "single layer, single prompt sequence"
    past = page_table.shape[1] * PAGE_SIZE
    heads_g = state_ssm.shape[2]
    conv_ch = state_conv.shape[-1]

    wp = _pack_w_in(w_in[0])
    shared = (norm_mix_w[0][None], wp, w_out[0].astype(BF16), norm_ffn_w[0][None], peer_wq[0].astype(BF16),
              peer_sub_keys[0], peer_u[0].astype(BF16), peer_v[0].astype(BF16), norm_final_w[None])
    gdn_w = (conv_w[0], a_log[0], dt_bias[0], gdn_norm_w[0])

    def attend_p(qb, qib, wi, kib, kb, vb, kn2):
        return prompt_attention(qb, qib, wi, kib, kb, vb, kn2.T, tq=_row_tile(s, 256), ts=_row_tile(s, 512),
                                topk=min(TOPK_MAX, s // 4))

    def gdn_p(p, a_pre):
        conv0 = jnp.zeros((1, CONV_W - 1, conv_ch), F32)
        ssm0 = jnp.zeros((1, heads_g, GDN_DK, GDN_DV), F32)
        block = 2 * GDN_CHUNK if s % (2 * GDN_CHUNK) == 0 else GDN_CHUNK
        assert s % block == 0
        return gdn_mixer(p, a_pre.T[None], conv0, *gdn_w, ssm0, n_seq=1, t_pad=s, t_total=s, chunk=block,
                         sub=GDN_CHUNK)

    tabs_p = _rope_tables(jnp.arange(s))
    y_p, k_p, v_p, ki_p, conv_p, ssm_p = _token_stages(x_prompt.reshape(s, d), attend_p, gdn_p, shared + (tabs_p,))

    m_s = nb * t
    t8 = -(-t // 8) * 8

    def attend_s(qb, qib, wi, kib, kb, vb, kn2):
        seq = lambda a: a.reshape(nb, t, a.shape[-1])
        y = sample_attention(seq(qb), seq(qib), seq(wi), seq(kib), seq(kb), seq(vb), cache_kidx, cache_k, cache_v,
                             page_table, topk=min(TOPK_MAX, (past + t) // 4))
        return y.reshape(m_s, y.shape[-1])

    def gdn_s(p, a_pre):
        pad_t = lambda a: jnp.pad(a.reshape(nb, t, a.shape[-1]), ((0, 0), (0, t8 - t), (0, 0)))
        p8 = pad_t(p).reshape(nb * t8, p.shape[-1])
        at = pad_t(a_pre).transpose(0, 2, 1)
        yg, nconv, nssm = gdn_mixer(p8, at, state_conv[0], *gdn_w, state_ssm[0], n_seq=nb, t_pad=t8, t_total=t,
                                    chunk=t8, sub=8)
        return yg.reshape(nb, t8, yg.shape[-1])[:, :t].reshape(m_s, yg.shape[-1]), nconv, nssm

    tabs_s = tuple(jnp.tile(a, (nb, 1)) for a in _rope_tables(past + jnp.arange(t)))
    y_s, k_s, v_s, ki_s, conv_s, ssm_s = _token_stages(x_sample.reshape(m_s, d), attend_s, gdn_s, shared + (tabs_s,))

    kv = lambda a, n, tt: a.reshape(1, n, tt, KV_HEADS, HEAD_DIM)
    return (y_p.reshape(b, s, d), y_s.reshape(nb, t, d),
            kv(k_p, b, s), kv(v_p, b, s), ki_p.reshape(1, b, s, IDX_DIM), conv_p[None], ssm_p[None],
            kv(k_s, nb, t), kv(v_s, nb, t), ki_s.reshape(1, nb, t, IDX_DIM), conv_s[None], ssm_s[None])
```

```python
import functools
import math

import jax
import jax.numpy as jnp
from jax import lax
from jax.experimental import pallas as pl
from jax.experimental.pallas import tpu as pltpu

F32 = jnp.float32
BF16 = jnp.bfloat16
I32 = jnp.int32
EPS = 1e-6
NEG_INF = float("-inf")

HEAD_DIM = 128
KV_HEADS = 2
IDX_HEADS = 8
IDX_DIM = 64
TOPK_MAX = 256
ROPE_THETA = 10000.0
PAGE_SIZE = 128
GDN_DK = 128
GDN_DV = 128
CONV_W = 4
GDN_CHUNK = 64
PEER_HEADS = 8
PEER_KEYS = 128
PEER_QDIM = 128
PEER_TOPK = 16

VMEM_LIMIT_BYTES = 56 * 1024 * 1024


def _cparams(*sem):
    return pltpu.CompilerParams(dimension_semantics=sem, vmem_limit_bytes=VMEM_LIMIT_BYTES)


_PROJ_SRC = (("q", 1024), ("k", 256), ("v", 256), ("qi", 512), ("ki", 64), ("wi", 8), ("qkv", 3072), ("z", 1024),
             ("b", 8), ("a", 8))
_PROJ_DST = ("q", "qi", "k", "v", "qkv", "z", "ki", "wi", "b", "a")
PROJ_PACKED = 6400


def _pack_w_in_kernel(w_ref, o_ref):
    src, off = {}, 0
    for name, width in _PROJ_SRC:
        src[name] = (off, width)
        off += width
    dst = 0
    for name in _PROJ_DST:
        s0, width = src[name]
        o_ref[:, dst:dst + width] = w_ref[:, s0:s0 + width].astype(o_ref.dtype)
        dst += width
    o_ref[:, dst:] = jnp.zeros((o_ref.shape[0], o_ref.shape[1] - dst), o_ref.dtype)


def _pack_w_in(w_in):
    d, n = w_in.shape
    assert n == sum(w for _, w in _PROJ_SRC)
    tr = _row_tile(d, 256)
    return pl.pallas_call(
        _pack_w_in_kernel,
        grid=(d // tr,),
        in_specs=[pl.BlockSpec((tr, n), lambda i: (i, 0))],
        out_specs=pl.BlockSpec((tr, PROJ_PACKED), lambda i: (i, 0)),
        out_shape=jax.ShapeDtypeStruct((d, PROJ_PACKED), BF16),
        compiler_params=_cparams("arbitrary"),
        name="pack_w_in",
    )(w_in)


def _rope_tables(pos):
    def table(dim):
        half = dim // 2
        inv = ROPE_THETA ** (-jnp.arange(half, dtype=F32) / half)
        ang = pos.astype(F32)[:, None] * inv[None, :]
        cos = jnp.tile(jnp.cos(ang), (1, 128 // half))
        sin = jnp.tile(jnp.concatenate([-jnp.sin(ang), jnp.sin(ang)], axis=1), (1, 128 // dim))
        return cos, sin
    cq, sq = table(HEAD_DIM)
    ci, si = table(IDX_DIM)
    return cq, sq, ci, si


def _norm_matmul_kernel(x_ref, nw_ref, w_ref, o_ref):
    x = x_ref[...]
    y = x * lax.rsqrt(jnp.mean(x * x, axis=-1, keepdims=True) + EPS)
    xn = (y * nw_ref[...]).astype(BF16)
    o_ref[...] = jnp.dot(xn, w_ref[...], preferred_element_type=F32)


def norm_matmul(x, nw, w, *, tm, tn):
    m, d = x.shape
    n = w.shape[1]
    assert m % tm == 0 and n % tn == 0
    return pl.pallas_call(
        _norm_matmul_kernel,
        grid=(n // tn, m // tm),
        in_specs=[
            pl.BlockSpec((tm, d), lambda j, i: (i, 0)),
            pl.BlockSpec((1, d), lambda j, i: (0, 0)),
            pl.BlockSpec((d, tn), lambda j, i: (0, j)),
        ],
        out_specs=pl.BlockSpec((tm, tn), lambda j, i: (i, j)),
        out_shape=jax.ShapeDtypeStruct((m, n), F32),
        compiler_params=_cparams("arbitrary", "arbitrary"),
        name="norm_matmul",
    )(x, nw, w)


def _rope128(x, cos, sin_signed):
    return x * cos + pltpu.roll(x, 64, 1) * sin_signed


def _rope64(x, cos, sin_signed, first_half):
    partner = jnp.where(first_half, pltpu.roll(x, 96, 1), pltpu.roll(x, 32, 1))
    return x * cos + partner * sin_signed


def _rope_split_kernel(q_ref, qi_ref, k_ref, v_ref, aux_ref, cq_ref, sq_ref, ci_ref, si_ref,
                       qb_ref, qib_ref, k32_ref, kb_ref, v32_ref, vb_ref, ki32_ref, kib_ref, auxr_ref, kn2_ref):
    cq, sq, ci, si = cq_ref[...], sq_ref[...], ci_ref[...], si_ref[...]
    lane = lax.broadcasted_iota(I32, ci.shape, 1)
    first_half = (lane % IDX_DIM) < (IDX_DIM // 2)
    for h in range(q_ref.shape[1] // HEAD_DIM):
        sl = slice(h * HEAD_DIM, (h + 1) * HEAD_DIM)
        qb_ref[:, sl] = _rope128(q_ref[:, sl], cq, sq).astype(BF16)
    for h in range(k_ref.shape[1] // HEAD_DIM):
        sl = slice(h * HEAD_DIM, (h + 1) * HEAD_DIM)
        kr = _rope128(k_ref[:, sl], cq, sq)
        k32_ref[:, sl] = kr
        krb = kr.astype(BF16)
        kb_ref[:, sl] = krb
        kn2_ref[:, h:h + 1] = jnp.sum(krb.astype(F32) * krb.astype(F32), axis=1, keepdims=True)
    for h in range(qi_ref.shape[1] // 128):
        sl = slice(h * 128, (h + 1) * 128)
        qib_ref[:, sl] = _rope64(qi_ref[:, sl], ci, si, first_half).astype(BF16)
    v = v_ref[...]
    v32_ref[...] = v
    vb_ref[...] = v.astype(BF16)
    aux = aux_ref[...]
    kir = _rope64(aux, ci, si, first_half)[:, :IDX_DIM]
    ki32_ref[...] = kir
    kib_ref[...] = kir.astype(BF16)
    auxr_ref[...] = aux[:, IDX_DIM:]


def rope_split(p, cq, sq, ci, si, *, tm):
    m = p.shape[0]
    assert m % tm == 0
    row = lambda w, j: pl.BlockSpec((tm, w), lambda i, j=j: (i, j))
    outs = [
        ((m, 1024), BF16), ((m, 512), BF16), ((m, 256), F32), ((m, 256), BF16), ((m, 256), F32),
        ((m, 256), BF16), ((m, IDX_DIM), F32), ((m, IDX_DIM), BF16), ((m, 128 - IDX_DIM), F32),
        ((m, KV_HEADS), F32),
    ]
    return pl.pallas_call(
        _rope_split_kernel,
        grid=(m // tm,),
        in_specs=[row(1024, 0), row(512, 2), row(256, 6), row(256, 7), row(128, 48),
                  row(128, 0), row(128, 0), row(128, 0), row(128, 0)],
        out_specs=[pl.BlockSpec((tm, s[1]), lambda i: (i, 0)) for s, _ in outs],
        out_shape=[jax.ShapeDtypeStruct(s, dt) for s, dt in outs],
        compiler_params=_cparams("arbitrary"),
        name="rope_split",
    )(p, p, p, p, p, cq, sq, ci, si)


def _out_proj_kernel(ya_ref, yg_ref, res_ref, wa_ref, wg_ref, nw_ref, wq_ref, h_ref, xn_ref, qp_ref):
    h = res_ref[...] + jnp.dot(ya_ref[...], wa_ref[...], preferred_element_type=F32)
    h = h + jnp.dot(yg_ref[...], wg_ref[...], preferred_element_type=F32)
    h_ref[...] = h
    y = h * lax.rsqrt(jnp.mean(h * h, axis=-1, keepdims=True) + EPS)
    xn = (y * nw_ref[...]).astype(BF16)
    xn_ref[...] = xn
    qp_ref[...] = jnp.dot(xn, wq_ref[...], preferred_element_type=F32)


def out_proj(ya, yg, res, w_out, nw, wq, *, tm):
    m, d = res.shape
    half = ya.shape[1]
    nq = wq.shape[1]
    assert m % tm == 0
    rows = lambda wd: pl.BlockSpec((tm, wd), lambda i: (i, 0))
    const = lambda a: pl.BlockSpec(a.shape, lambda i: (0,) * a.ndim, pipeline_mode=pl.Buffered(1))
    wa, wg = w_out[:half], w_out[half:]
    return pl.pallas_call(
        _out_proj_kernel,
        grid=(m // tm,),
        in_specs=[rows(half), rows(half), rows(d), const(wa), const(wg), const(nw), const(wq)],
        out_specs=[rows(d), rows(d), rows(nq)],
        out_shape=[jax.ShapeDtypeStruct((m, d), F32), jax.ShapeDtypeStruct((m, d), BF16),
                   jax.ShapeDtypeStruct((m, nq), F32)],
        compiler_params=_cparams("arbitrary"),
        name="out_proj",
    )(ya, yg, res, wa, wg, nw, wq)


def _take_top(s, codes, count):
    big = jnp.int32(2 ** 30)
    vals, picks = [], []
    for _ in range(count):
        m = jnp.max(s, axis=0, keepdims=True)
        pick = jnp.min(jnp.where(s == m, codes, big), axis=0, keepdims=True)
        s = jnp.where(codes == pick, NEG_INF, s)
        vals.append(m)
        picks.append(pick)
    return jnp.concatenate(vals, axis=0), jnp.concatenate(picks, axis=0)


def _lookup(table, sel, count):
    out = jnp.zeros(sel.shape, table.dtype)
    for a in range(count):
        out = jnp.where(sel == a, jnp.broadcast_to(table[a:a + 1, :], sel.shape), out)
    return out


def _peer_topk_kernel(q_ref, sk_ref, i1_ref, i2_ref, g_ref):
    i1_ref[0], i2_ref[0], g_ref[0] = _peer_topk_tile(q_ref[...], sk_ref)


def _peer_topk_tile(q, sk_ref):
    tm = q.shape[0]
    kk = PEER_TOPK
    half = PEER_QDIM // 2
    key_codes = lax.broadcasted_iota(I32, (PEER_KEYS, tm), 0)
    top_v, top_i = [], []
    for c in range(2):
        qs = q[:, c * half:(c + 1) * half].astype(BF16)
        s = lax.dot_general(sk_ref[c], qs, (((1,), (1,)), ((), ())), preferred_element_type=F32)
        vals, idx = _take_top(s, key_codes, kk)
        top_v.append(vals)
        top_i.append(idx)
    pieces, codes = [], []

    def add_piece(cand, a_of_row, b_of_row):
        ok = (a_of_row + 1) * (b_of_row + 1) <= kk
        pieces.append(jnp.where(ok, cand, NEG_INF))
        codes.append(a_of_row * kk + b_of_row)

    split = 4
    for a in range(split):
        rows = -(-(kk // (a + 1)) // 8) * 8
        r = lax.broadcasted_iota(I32, (rows, tm), 0)
        add_piece(top_v[0][a:a + 1, :] + top_v[1][0:rows, :], jnp.full((rows, tm), a, I32), r)
    r8 = lax.broadcasted_iota(I32, (8, tm), 0)
    for b in range(kk // (split + 1)):
        cand = top_v[0][0:8, :] + top_v[1][b:b + 1, :]
        add_piece(jnp.where(r8 >= split, cand, NEG_INF), r8, jnp.full((8, tm), b, I32))
    add_piece(top_v[0][8:kk, :] + top_v[1][0:1, :], r8 + 8, jnp.zeros((8, tm), I32))
    best_s, best_c = _take_top(jnp.concatenate(pieces, axis=0), jnp.concatenate(codes, axis=0), kk)
    i1 = _lookup(top_i[0], lax.shift_right_logical(best_c, 4), kk)
    i2 = _lookup(top_i[1], jnp.bitwise_and(best_c, kk - 1), kk)
    e = jnp.exp(best_s - best_s[0:1, :])
    return i1, i2, e / jnp.sum(e, axis=0, keepdims=True)


def peer_topk(qp, sub_keys, *, tm):
    m = qp.shape[0]
    heads = sub_keys.shape[0]
    assert m % tm == 0 and PEER_TOPK == 16
    sk = sub_keys.reshape(heads * 2, PEER_KEYS, PEER_QDIM // 2).astype(BF16)
    out = pl.BlockSpec((1, PEER_TOPK, tm), lambda i, h: (h, 0, i))
    return pl.pallas_call(
        _peer_topk_kernel,
        grid=(m // tm, heads),
        in_specs=[pl.BlockSpec((tm, PEER_QDIM), lambda i, h: (i, h)),
                  pl.BlockSpec((2, PEER_KEYS, PEER_QDIM // 2), lambda i, h: (h, 0, 0))],
        out_specs=[out, out, out],
        out_shape=[jax.ShapeDtypeStruct((heads, PEER_TOPK, m), I32), jax.ShapeDtypeStruct((heads, PEER_TOPK, m), I32),
                   jax.ShapeDtypeStruct((heads, PEER_TOPK, m), F32)],
        compiler_params=_cparams("arbitrary", "arbitrary"),
        name="peer_topk",
    )(qp, sk)


GATE_TOKENS = 32


def _peer_gate_kernel(i1_ref, i2_ref, g_ref, o_ref):
    tmb = i1_ref.shape[0]
    nk = PEER_KEYS
    sub = lax.broadcasted_iota(I32, (nk, i1_ref.shape[1]), 0)

    def token_block(j, _):
        t0 = pl.multiple_of(j * GATE_TOKENS, GATE_TOKENS)
        per_token = []
        for u in range(GATE_TOKENS):
            wide = lambda ref: jnp.broadcast_to(ref[pl.ds(t0 + u, 1), :], sub.shape)
            p1 = jnp.where(wide(i1_ref) == sub, wide(g_ref), 0.0).astype(BF16)
            p2 = jnp.where(wide(i2_ref) == sub, 1.0, 0.0).astype(BF16)
            gm = lax.dot_general(p1, p2, (((1,), (1,)), ((), ())), preferred_element_type=F32)
            per_token.append(gm.astype(o_ref.dtype))
        block = jnp.stack(per_token, axis=0)
        o_ref[:, pl.ds(t0, GATE_TOKENS), :] = pltpu.einshape("mab->amb", block)
        return 0

    lax.fori_loop(0, tmb // GATE_TOKENS, token_block, 0)


def peer_gates(i1, i2, gate, *, tmb):
    m, slots = i1.shape
    assert m % tmb == 0 and tmb % GATE_TOKENS == 0
    rows = pl.BlockSpec((tmb, slots), lambda i: (i, 0))
    return pl.pallas_call(
        _peer_gate_kernel,
        grid=(m // tmb,),
        in_specs=[rows, rows, rows],
        out_specs=pl.BlockSpec((PEER_KEYS, tmb, PEER_KEYS), lambda i: (0, i, 0)),
        out_shape=jax.ShapeDtypeStruct((PEER_KEYS, m, PEER_KEYS), BF16),
        compiler_params=_cparams("arbitrary"),
        name="peer_gates",
    )(i1, i2, gate)


def _gelu_tanh(x):
    return 0.5 * x * (1.0 + jnp.tanh(math.sqrt(2.0 / math.pi) * (x + 0.044715 * (x * x * x))))


def _peer_dense_kernel(xn_ref, wu_ref, wv_ref, g_ref, o_ref):
    j = pl.program_id(1)
    ib = g_ref.shape[0]
    a = lax.dot_general(xn_ref[...], wu_ref[...], (((1,), (1,)), ((), ())), preferred_element_type=F32)
    act = _gelu_tanh(a)
    hm = jnp.concatenate(
        [(g_ref[u].astype(F32) * act[:, u * PEER_KEYS:(u + 1) * PEER_KEYS]).astype(BF16) for u in range(ib)], axis=1)
    part = jnp.dot(hm, wv_ref[...], preferred_element_type=F32)

    @pl.when(j == 0)
    def _():
        o_ref[...] = part

    @pl.when(j > 0)
    def _():
        o_ref[...] += part


def peer_dense(xn, w_u, w_v, gates, *, tm, ib):
    m, d = xn.shape
    assert m % tm == 0 and PEER_KEYS % ib == 0
    eb = ib * PEER_KEYS
    return pl.pallas_call(
        _peer_dense_kernel,
        grid=(m // tm, PEER_KEYS // ib),
        in_specs=[
            pl.BlockSpec((tm, d), lambda i, j: (i, 0)),
            pl.BlockSpec((eb, d), lambda i, j: (j, 0)),
            pl.BlockSpec((eb, d), lambda i, j: (j, 0)),
            pl.BlockSpec((ib, tm, PEER_KEYS), lambda i, j: (j, i, 0)),
        ],
        out_specs=pl.BlockSpec((tm, d), lambda i, j: (i, 0)),
        out_shape=jax.ShapeDtypeStruct((m, d), F32),
        compiler_params=_cparams("arbitrary", "arbitrary"),
        name="peer_dense",
    )(xn, w_u, w_v, gates)


def _residual_norm_kernel(h_ref, f_ref, w_ref, o_ref):
    y = h_ref[...] + f_ref[...]
    o_ref[...] = y * lax.rsqrt(jnp.mean(y * y, axis=-1, keepdims=True) + EPS) * w_ref[...]


def residual_norm(h, f, w, *, tm):
    m, d = h.shape
    assert m % tm == 0
    rows = pl.BlockSpec((tm, d), lambda i: (i, 0))
    return pl.pallas_call(
        _residual_norm_kernel,
        grid=(m // tm,),
        in_specs=[rows, rows, pl.BlockSpec((1, d), lambda i: (0, 0))],
        out_specs=rows,
        out_shape=jax.ShapeDtypeStruct((m, d), F32),
        compiler_params=_cparams("arbitrary"),
        name="residual_norm",
    )(h, f, w)


_HI = lax.Precision.HIGHEST


def _dot_hi(a, b):
    return jnp.dot(a, b, preferred_element_type=F32, precision=_HI)


_BNN = (((2,), (1,)), ((0,), (0,)))
_BNT = (((2,), (2,)), ((0,), (0,)))
_BTN = (((1,), (1,)), ((0,), (0,)))


def _bdot(a, b, dims=_BNN):
    return lax.dot_general(a.astype(BF16), b.astype(BF16), dims, preferred_element_type=F32)


def _split2(a):
    hi = a.astype(BF16)
    return hi, (a - hi.astype(F32)).astype(BF16)


def _bdot3(a, b, dims=_BNN):
    (ah, al), (bh, bl) = a, b
    dot = lambda x, y: lax.dot_general(x, y, dims, preferred_element_type=F32)
    return dot(ah, bh) + (dot(ah, bl) + dot(al, bh))


def _sigmoid(x):
    return 1.0 / (1.0 + jnp.exp(-x))


def _softplus(x):
    return jnp.maximum(x, 0.0) + jnp.log1p(jnp.exp(-jnp.abs(x)))


def _gdn_kernel(xq_ref, xk_ref, xv_ref, z_ref, aux_ref, at_ref, cs_ref, cw_ref, alog_ref, dtb_ref,
                alogt_ref, dtbt_ref, nw_ref, s0_ref, y_ref, nconv_ref, nssm_ref, xp_ref, st_ref,
                *, chunk, sub, t_total, b_lane, a_lane):
    c = pl.program_id(1)
    nchunks = pl.num_programs(1)
    heads = st_ref.shape[0]
    width = heads * GDN_DK
    tail = CONV_W - 1

    @pl.when(c == 0)
    def _():
        xp_ref[8 - tail:8, :] = cs_ref[0]
        st_ref[...] = s0_ref[0]

    xp_ref[8:8 + chunk, 0:width] = xq_ref[...]
    xp_ref[8:8 + chunk, width:2 * width] = xk_ref[...]
    xp_ref[8:8 + chunk, 2 * width:3 * width] = xv_ref[...]
    conv = cw_ref[tail:tail + 1, :] * xp_ref[8:8 + chunk, :]
    for j in range(tail):
        conv = conv + cw_ref[j:j + 1, :] * xp_ref[8 - tail + j:8 - tail + j + chunk, :]
    conv = conv * _sigmoid(conv)

    padded = t_total % chunk != 0
    ridx = c * chunk + lax.broadcasted_iota(I32, (chunk, 1), 0)
    rvalid = ridx < t_total
    cidx = c * chunk + lax.broadcasted_iota(I32, (1, chunk), 1)
    cvalid = cidx < t_total

    aux = aux_ref[...]
    beta = _sigmoid(aux[:, b_lane:b_lane + heads])
    g = -jnp.exp(alog_ref[...]) * _softplus(aux[:, a_lane:a_lane + heads] + dtb_ref[...])
    gt = -jnp.exp(alogt_ref[...]) * _softplus(at_ref[0] + dtbt_ref[...])
    if padded:
        beta = jnp.where(rvalid, beta, 0.0)
        g = jnp.where(rvalid, g, 0.0)
        gt = jnp.where(cvalid, gt, 0.0)
    n_sub = chunk // sub
    rf = lax.broadcasted_iota(I32, (chunk, chunk), 0)
    cf = lax.broadcasted_iota(I32, (chunk, chunk), 1)
    same = (rf // sub) == (cf // sub)
    gc = _dot_hi(jnp.where(jnp.logical_and(same, rf >= cf), 1.0, 0.0), g)
    gct = _dot_hi(gt, jnp.where(jnp.logical_and(same, rf <= cf), 1.0, 0.0))
    ri = lax.broadcasted_iota(I32, (sub, sub), 0)
    ci = lax.broadcasted_iota(I32, (sub, sub), 1)
    causal = ri >= ci
    strict = ri > ci
    eye = jnp.where(ri == ci, 1.0, 0.0)

    per_head = lambda x2d, base: jnp.stack(
        [x2d[:, base + h * GDN_DK:base + (h + 1) * GDN_DK] for h in range(heads)], axis=0)
    q = per_head(conv, 0)
    k = per_head(conv, width)
    v = per_head(conv, 2 * width)
    q = q * lax.rsqrt(jnp.sum(q * q, axis=-1, keepdims=True) + EPS) * GDN_DK ** -0.5
    k = k * lax.rsqrt(jnp.sum(k * k, axis=-1, keepdims=True) + EPS)
    if padded:
        k = jnp.where(rvalid[None], k, 0.0)
        v = jnp.where(rvalid[None], v, 0.0)
    gcol = jnp.stack([gc[:, h:h + 1] for h in range(heads)], axis=0)
    grow = jnp.stack([gct[h:h + 1, :] for h in range(heads)], axis=0)
    bcol = jnp.stack([beta[:, h:h + 1] for h in range(heads)], axis=0)
    by_rows = lambda a: jnp.concatenate([a[:, j * sub:(j + 1) * sub] for j in range(n_sub)], axis=0)
    q, k, v, gcol, bcol = (by_rows(a) for a in (q, k, v, gcol, bcol))
    grow = jnp.concatenate([grow[:, :, j * sub:(j + 1) * sub] for j in range(n_sub)], axis=0)
    decay = jnp.where(causal[None], jnp.exp(jnp.where(causal[None], gcol - grow, 0.0)), 0.0)
    kb = k * bcol
    vb = v * bcol
    x = -jnp.where(strict[None], _bdot(kb, k, _BNT) * decay, 0.0)
    tinv = eye[None] + x
    xs = _split2(x)
    span = 2
    while span < sub:
        xs = _split2(_bdot3(xs, xs))
        tinv = tinv + _bdot3(_split2(tinv), xs)
        span *= 2
    u = _bdot(tinv, vb)
    w = _bdot(tinv, kb * jnp.exp(gcol))
    intra = jnp.where(causal[None], _bdot(q, k, _BNT) * decay, 0.0)
    qg = q * jnp.exp(gcol)
    glast = gcol[:, sub - 1:sub, :]
    kd = k * jnp.exp(glast - gcol)
    state = st_ref[...]
    outs = []
    for j in range(n_sub):
        sl = slice(j * heads, (j + 1) * heads)
        v_new = u[sl] - _bdot(w[sl], state)
        outs.append(_bdot(qg[sl], state) + _bdot(intra[sl], v_new))
        state = state * jnp.exp(glast[sl]) + _bdot(kd[sl], v_new, _BTN)
    st_ref[...] = state
    out = outs[0] if n_sub == 1 else jnp.concatenate(outs, axis=1)
    on = out * lax.rsqrt(jnp.mean(out * out, axis=-1, keepdims=True) + EPS) * nw_ref[...]
    for h in range(heads):
        sl = slice(h * GDN_DV, (h + 1) * GDN_DV)
        zh = z_ref[:, sl]
        y_ref[:, sl] = (on[h] * (zh * _sigmoid(zh))).astype(y_ref.dtype)

    last_valid = t_total - (t_total - 1) // chunk * chunk
    @pl.when(c < nchunks - 1)
    def _():
        xp_ref[8 - tail:8, :] = xp_ref[8 + chunk - tail:8 + chunk, :]

    @pl.when(c == nchunks - 1)
    def _():
        nconv_ref[0] = xp_ref[8 + last_valid - tail:8 + last_valid, :]
        nssm_ref[0] = st_ref[...]


def gdn_mixer(p, at, conv_state, conv_w, a_log, dt_bias, norm_w, ssm_state, *, n_seq, t_pad, t_total, chunk, sub):
    heads = ssm_state.shape[1]
    width = heads * GDN_DK
    assert t_pad % chunk == 0 and chunk % sub == 0 and sub % 8 == 0
    nch = t_pad // chunk
    rows = lambda wd, j: pl.BlockSpec((chunk, wd), lambda n, c, j=j: (n * nch + c, j))
    const = lambda a: pl.BlockSpec(a.shape, lambda n, c: (0,) * a.ndim)
    alog = a_log.reshape(1, heads)
    dtb = dt_bias.reshape(1, heads)
    nw = norm_w.reshape(1, GDN_DV)
    at = at.reshape(n_seq, heads, nch, chunk).transpose(0, 2, 1, 3).reshape(n_seq * nch, heads, chunk)
    kern = functools.partial(_gdn_kernel, chunk=chunk, sub=sub, t_total=t_total, b_lane=IDX_DIM + 8,
                             a_lane=IDX_DIM + 16)
    return pl.pallas_call(
        kern,
        grid=(n_seq, nch),
        in_specs=[
            rows(width, 2), rows(width, 3), rows(width, 4), rows(width, 5), rows(128, 48),
            pl.BlockSpec((1, heads, chunk), lambda n, c: (n * nch + c, 0, 0)),
            pl.BlockSpec((1, CONV_W - 1, 3 * width), lambda n, c: (n, 0, 0)),
            const(conv_w), const(alog), const(dtb), const(alog.T), const(dtb.T), const(nw),
            pl.BlockSpec((1, heads, GDN_DK, GDN_DV), lambda n, c: (n, 0, 0, 0)),
        ],
        out_specs=[
            pl.BlockSpec((chunk, width), lambda n, c: (n * nch + c, 0)),
            pl.BlockSpec((1, CONV_W - 1, 3 * width), lambda n, c: (n, 0, 0)),
            pl.BlockSpec((1, heads, GDN_DK, GDN_DV), lambda n, c: (n, 0, 0, 0)),
        ],
        out_shape=[
            jax.ShapeDtypeStruct((n_seq * t_pad, width), BF16),
            jax.ShapeDtypeStruct((n_seq, CONV_W - 1, 3 * width), F32),
            jax.ShapeDtypeStruct((n_seq, heads, GDN_DK, GDN_DV), F32),
        ],
        scratch_shapes=[pltpu.VMEM((8 + chunk, 3 * width), F32), pltpu.VMEM((heads, GDN_DK, GDN_DV), F32)],
        compiler_params=_cparams("arbitrary", "arbitrary"),
        name="gdn_mixer",
    )(p, p, p, p, p, at, conv_state, conv_w, alog, dtb, alog.T, dtb.T, nw, ssm_state)


def _lane_fold(x, op):
    out = x[:, :128]
    for j in range(1, x.shape[1] // 128):
        out = op(out, x[:, j * 128:(j + 1) * 128])
    return out


TIE_UNROLL = 4


def _select_threshold(load, nch, nch_max, ts, rows, topk, lo0, hi0, skip):
    kf = float(topk)
    rb = min(rows, 128)
    assert rows % rb == 0
    assert nch_max * (ts // 128) <= 256
    lane = lax.broadcasted_iota(I32, (rb, 128), 1)
    on_lanes = rb == 128
    if on_lanes:
        state_shape = (8, rows)
        ones = jnp.ones((8, 128), BF16)
        to_state = lambda blk: jnp.transpose(jnp.broadcast_to(blk, (128, 128)))[0:8]
        to_rows = lambda st, r0: jnp.transpose(jnp.broadcast_to(st[0:1, r0:r0 + 128], (128, 128)))
        row_sums = lambda acc: lax.dot_general(ones, acc.astype(BF16), (((1,), (1,)), ((), ())),
                                               preferred_element_type=F32)
        join_axis = 1
    else:
        state_shape = (rows, 128)
        ones = jnp.ones((128, 128), BF16)
        to_state = lambda blk: jnp.broadcast_to(blk, (rb, 128))
        to_rows = lambda st, r0: st[r0:r0 + rb]
        row_sums = lambda acc: jnp.dot(acc.astype(BF16), ones, preferred_element_type=F32)
        join_axis = 0
    join = lambda parts: parts[0] if len(parts) == 1 else jnp.concatenate(parts, axis=join_axis)
    rep = lambda col: join([to_state(col[r0:r0 + rb].astype(F32)) for r0 in range(0, rows, rb)])
    lo0, hi0, skip = rep(lo0), rep(hi0), rep(skip) > 0.5

    def count(pred, *cols):
        blocks = range(0, rows, rb)
        wides = [[to_rows(col, r0) for col in cols] for r0 in blocks]
        accs = []
        for r0, wide in zip(blocks, wides):
            def body(c, acc, r0=r0, wide=wide):
                x = load(c, r0, rb)
                for j in range(ts // 128):
                    kpos = lane + (c * ts + j * 128)
                    acc = acc + jnp.where(pred(x[:, j * 128:(j + 1) * 128], kpos, *wide), 1.0, 0.0)
                return acc

            accs.append(lax.fori_loop(0, nch, body, jnp.zeros((rb, 128), F32)))
        return join([row_sums(acc) for acc in accs])

    ge_ = lambda x, kpos, t: x >= t
    gt_ = lambda x, kpos, t: x > t
    eq_ = lambda x, kpos, t: x == t

    zero = jnp.zeros(state_shape, F32)
    span = (nch * ts).astype(F32) if hasattr(nch, "astype") else float(nch * ts)

    def search(state, limit):
        def cond(st):
            _, _, done, _, it = st
            return jnp.logical_and(jnp.min(done) < 0.5, it < limit)

        def body(st):
            lo, hi, done, hit_any, it = st
            mid = jnp.minimum(jnp.maximum(lo * 0.5 + hi * 0.5, lo), hi)
            cnt = count(ge_, mid)
            active = done < 0.5
            ge = jnp.logical_and(active, cnt >= kf)
            hit = jnp.logical_and(active, cnt == kf)
            collapsed = jnp.logical_or(mid <= lo, mid >= hi)
            new_lo = jnp.where(ge, mid, lo)
            new_hi = jnp.where(jnp.logical_and(active, jnp.logical_not(jnp.logical_or(ge, collapsed))), mid, hi)
            new_hi = jnp.where(hit, mid, new_hi)
            new_done = jnp.where(jnp.logical_or(hit, jnp.logical_and(active, collapsed)), 1.0, done)
            return new_lo, new_hi, new_done, jnp.where(hit, 1.0, hit_any), it + 1

        lo, hi, done, hit_any, _ = lax.while_loop(cond, body, state + (jnp.int32(0),))
        return lo, hi, done, hit_any

    c_pos = count(gt_, zero)
    c_nn = count(ge_, zero)
    at_zero = jnp.logical_and(jnp.logical_not(skip), jnp.logical_and(c_pos < kf, c_nn >= kf))
    lo0 = jnp.where(at_zero, 0.0, jnp.where(c_pos >= kf, jnp.maximum(lo0, 0.0), lo0))
    hi0 = jnp.where(at_zero, 0.0, jnp.where(c_nn < kf, jnp.minimum(hi0, 0.0), hi0))
    lo, hi, _, hit_any = search((lo0, hi0, jnp.where(jnp.logical_or(skip, at_zero), 1.0, 0.0), zero), 4096)
    hit = hit_any > 0.5

    def count_collapsed(_):
        cnt_hi = count(ge_, hi)
        vc = jnp.where(cnt_hi >= kf, hi, lo)
        return vc, count(gt_, vc), count(eq_, vc)

    settled = jnp.logical_or(jnp.logical_or(skip, at_zero), hit)
    vc, cgt_c, ceq_c = lax.cond(jnp.min(jnp.where(settled, 1.0, 0.0)) > 0.5,
                                lambda _: (hi, zero, zero), count_collapsed, None)
    v = jnp.where(skip, NEG_INF, jnp.where(at_zero, 0.0, jnp.where(hit, hi, vc)))
    cgt = jnp.where(at_zero, c_pos, jnp.where(hit, kf, cgt_c))
    ceq = jnp.where(at_zero, c_nn - c_pos, jnp.where(hit, 0.0, ceq_c))
    need = kf - cgt
    partial = jnp.logical_and(jnp.logical_not(skip), need < ceq)
    rows_of = lambda st: jnp.concatenate([to_rows(st, r0) for r0 in range(0, rows, rb)], axis=0)
    cut_all = rows_of(jnp.where(skip, -1.0, zero + span))
    blocks = range(0, rows, rb)
    tile_ts = lambda a: jnp.concatenate([a] * (ts // 128), axis=1)

    def cut_by_prefix(_):
        nchp = -(-nch_max // 8) * 8
        crow = lax.broadcasted_iota(I32, (nchp, rb), 0)
        tables = []
        for r0 in blocks:
            vt = tile_ts(to_rows(v, r0))

            def per_chunk(i, tab, r0=r0, vt=vt):
                for u in range(TIE_UNROLL):
                    c = i * TIE_UNROLL + u
                    ties = jnp.where(load(jnp.minimum(c, nch - 1), r0, rb) == vt, 1.0, 0.0)
                    cnt = row_sums(_lane_fold(ties, jnp.add))
                    here = jnp.logical_and(crow == c, c < nch)
                    tab = jnp.where(here, jnp.concatenate([cnt] * (nchp // 8), axis=0), tab)
                return tab

            steps = (nch + TIE_UNROLL - 1) // TIE_UNROLL
            tables.append(lax.fori_loop(0, steps, per_chunk, jnp.zeros((nchp, rb), F32)))
        tab = join(tables)
        ci = lax.broadcasted_iota(I32, (nchp, nchp), 0)
        cj = lax.broadcasted_iota(I32, (nchp, nchp), 1)
        upto = _dot_hi(jnp.where(ci >= cj, 1.0, 0.0), tab)
        before = upto < jnp.broadcast_to(need[0:1], tab.shape)
        cstar = jnp.sum(jnp.where(before, 1.0, 0.0), axis=0, keepdims=True)
        left = need[0:1] - jnp.sum(jnp.where(before, tab, 0.0), axis=0, keepdims=True)
        cstar, left = jnp.broadcast_to(cstar, state_shape), jnp.broadcast_to(left, state_shape)
        ki = lax.broadcasted_iota(I32, (ts, ts), 0)
        kj = lax.broadcasted_iota(I32, (ts, ts), 1)
        upper = jnp.where(ki <= kj, 1.0, 0.0).astype(BF16)
        cuts = []
        for r0 in blocks:
            vt, ct = tile_ts(to_rows(v, r0)), tile_ts(to_rows(cstar, r0))

            def crossing(c, m, r0=r0, vt=vt, ct=ct):
                here = jnp.logical_and(load(c, r0, rb) == vt, ct == c.astype(F32))
                return m + jnp.where(here, 1.0, 0.0)

            mask = lax.fori_loop(0, nch, crossing, jnp.zeros((rb, ts), F32))
            seen = jnp.dot(mask.astype(BF16), upper, preferred_element_type=F32)
            inside = jnp.sum(jnp.where(seen < tile_ts(to_rows(left, r0)), 1.0, 0.0), axis=1, keepdims=True)
            cuts.append(to_rows(cstar, r0) * float(ts) + jnp.broadcast_to(inside, (rb, 128)))
        return jnp.where(rows_of(jnp.where(partial, 1.0, 0.0)) > 0.5, jnp.concatenate(cuts, axis=0), cut_all)

    def cut_by_bisection(_):
        def c_cond(st):
            lo_i, hi_i = st
            return jnp.max(jnp.where(partial, hi_i - lo_i, 0.0)) > 0.0

        def c_body(st):
            lo_i, hi_i = st
            mid = jnp.floor((lo_i + hi_i) * 0.5)
            cnt = count(lambda x, kpos, t, m: jnp.logical_and(x == t, kpos.astype(F32) <= m), v, mid)
            ok = cnt >= need
            return jnp.where(ok, lo_i, mid + 1.0), jnp.where(ok, mid, hi_i)

        _, cut = lax.while_loop(c_cond, c_body, (zero, zero + (span - 1.0)))
        return jnp.where(rows_of(jnp.where(partial, 1.0, 0.0)) > 0.5, rows_of(cut), cut_all)

    any_partial = jnp.max(jnp.where(partial, 1.0, 0.0)) > 0.5
    cut = lax.cond(any_partial, cut_by_prefix if on_lanes else cut_by_bisection, lambda _: cut_all, None)
    return rows_of(v), cut.astype(I32)


def _prompt_attn_kernel(q_ref, qi_ref, wi_ref, kit_ref, kt_ref, v_ref, kn2_ref, o_ref,
                        sc_ref, m_ref, l_ref, acc_ref, *, tq, ts, topk):
    i = pl.program_id(0)
    nch = ((i + 1) * tq + ts - 1) // ts
    row = i * tq + lax.broadcasted_iota(I32, (tq, 1), 0)
    heads = q_ref.shape[1] // HEAD_DIM
    wsc = (wi_ref[...] * IDX_HEADS ** -0.5) * IDX_DIM ** -0.5
    qi_h = [qi_ref[:, h * IDX_DIM:(h + 1) * IDX_DIM] for h in range(IDX_HEADS)]
    w_h = [jnp.broadcast_to(wsc[:, h:h + 1], (tq, ts)) for h in range(IDX_HEADS)]

    def scores(c, carry):
        mn, mx = carry
        kic = kit_ref[c]
        acc = jnp.zeros((tq, ts), F32)
        for h in range(IDX_HEADS):
            lg = jnp.dot(qi_h[h], kic, preferred_element_type=F32)
            acc = acc + jnp.maximum(lg, 0.0) * w_h[h]
        kpos = c * ts + lax.broadcasted_iota(I32, (tq, ts), 1)
        adm = kpos <= row
        sc_ref[c] = jnp.where(adm, acc, NEG_INF)
        mn = jnp.minimum(mn, _lane_fold(jnp.where(adm, acc, jnp.inf), jnp.minimum))
        mx = jnp.maximum(mx, _lane_fold(jnp.where(adm, acc, NEG_INF), jnp.maximum))
        return mn, mx

    mn, mx = lax.fori_loop(0, nch, scores,
                           (jnp.full((tq, 128), jnp.inf, F32), jnp.full((tq, 128), NEG_INF, F32)))
    lo0 = jnp.min(mn, axis=1, keepdims=True)
    hi0 = jnp.max(mx, axis=1, keepdims=True)
    skip = row < topk
    v, cut = _select_threshold(lambda c, r0, nr: sc_ref[c, r0:r0 + nr, :], nch, sc_ref.shape[0], ts, tq, topk,
                               lo0, hi0, skip)

    scale = HEAD_DIM ** -0.5
    group = heads // KV_HEADS
    vw = jnp.concatenate([v] * (ts // 128), axis=1)
    cutw = jnp.concatenate([cut] * (ts // 128), axis=1)
    head = lambda h: slice(h * HEAD_DIM, (h + 1) * HEAD_DIM)

    def selected(c):
        x = sc_ref[c]
        kpos = c * ts + lax.broadcasted_iota(I32, (tq, ts), 1)
        return jnp.logical_or(x > vw, jnp.logical_and(x == vw, kpos <= cutw))

    c2 = scale * math.log2(math.e)
    kmax = jnp.sqrt(jnp.max(kn2_ref[...], axis=1, keepdims=True))
    bound = []
    for h in range(heads):
        qh = q_ref[:, head(h)].astype(F32)
        qn = jnp.sqrt(jnp.sum(qh * qh, axis=1, keepdims=True))
        bound.append(jnp.broadcast_to(qn * (kmax[h // group:h // group + 1, :] * c2), (tq, ts)))
    ones = jnp.ones((ts, HEAD_DIM), BF16)
    acc_ref[...] = jnp.zeros(acc_ref.shape, F32)

    def attend_bounded(c, _):
        keep = jnp.where(selected(c), 1.0, 0.0).astype(BF16)
        kc = kt_ref[c]
        vc = v_ref[c]
        for g in range(KV_HEADS):
            v_ones = jnp.concatenate([vc[:, head(g)], ones], axis=1)
            for h in range(g * group, (g + 1) * group):
                s = jnp.dot(q_ref[:, head(h)], kc[head(g), :], preferred_element_type=F32)
                p = jnp.exp2(s * c2 - bound[h]).astype(BF16) * keep
                acc_ref[h] += jnp.dot(p, v_ones, preferred_element_type=F32)
        return 0

    lax.fori_loop(0, nch, attend_bounded, 0)
    lmin = jnp.full((tq, HEAD_DIM), jnp.inf, F32)
    for h in range(heads):
        acc = acc_ref[h]
        lsum = acc[:, HEAD_DIM:]
        lmin = jnp.minimum(lmin, lsum)
        o_ref[:, head(h)] = (acc[:, :HEAD_DIM] / lsum).astype(o_ref.dtype)

    @pl.when(jnp.logical_not(jnp.min(lmin) > 1e-30))
    def _():
        m_ref[...] = jnp.full(m_ref.shape, -1e30, F32)
        l_ref[...] = jnp.zeros(l_ref.shape, F32)
        acc_ref[...] = jnp.zeros(acc_ref.shape, F32)

        def attend_online(c, _):
            bias = jnp.where(selected(c), 0.0, NEG_INF)
            kc = kt_ref[c]
            vc = v_ref[c]
            for h in range(heads):
                g = h // group
                s = jnp.dot(q_ref[:, head(h)], kc[head(g), :], preferred_element_type=F32) * scale + bias
                m_old = m_ref[h]
                m_new = jnp.maximum(m_old, jnp.max(s, axis=1, keepdims=True))
                alpha = jnp.exp(m_old - m_new)
                p = jnp.exp(s - m_new)
                l_ref[h] = l_ref[h] * alpha + jnp.sum(p, axis=1, keepdims=True)
                acc_ref[h, :, :HEAD_DIM] = acc_ref[h, :, :HEAD_DIM] * alpha + jnp.dot(
                    p.astype(BF16), vc[:, head(g)], preferred_element_type=F32)
                m_ref[h] = m_new
            return 0

        lax.fori_loop(0, nch, attend_online, 0)
        for h in range(heads):
            o_ref[:, head(h)] = (acc_ref[h, :, :HEAD_DIM] / l_ref[h]).astype(o_ref.dtype)


SAMPLE_TS = 4 * PAGE_SIZE
SAMPLE_RING = 8


def _sample_attn_kernel(pt_ref, q_ref, qi_ref, wi_ref, kin_ref, kn_ref, vn_ref, cki_ref, ck_ref, cv_ref,
                        o_ref, kibuf, kbuf, vbuf, sc_ref, sem_i, sem_k, sem_v, *, n_pages, t_new, topk):
    b = pl.program_id(0)
    ts = SAMPLE_TS
    ppc = ts // PAGE_SIZE
    nch = n_pages // ppc
    past = n_pages * PAGE_SIZE
    rows = q_ref.shape[2]

    def ki_copy(p):
        dst = kibuf.at[:, pl.ds(pl.multiple_of(p * PAGE_SIZE, PAGE_SIZE), PAGE_SIZE)]
        return pltpu.make_async_copy(cki_ref.at[0, pt_ref[b, p]], dst, sem_i.at[0])

    def kv_copies(c, slot):
        cps = []
        for j in range(ppc):
            page = pt_ref[b, c * ppc + j]
            dst = pl.ds(j * PAGE_SIZE, PAGE_SIZE)
            for g in range(KV_HEADS):
                cps.append(pltpu.make_async_copy(ck_ref.at[0, page, :, g, :], kbuf.at[slot, g, dst], sem_k.at[slot]))
                cps.append(pltpu.make_async_copy(cv_ref.at[0, page, :, g, :], vbuf.at[slot, g, dst], sem_v.at[slot]))
        return cps

    def start_ki(p, _):
        ki_copy(p).start()
        return 0

    def wait_ki(p, _):
        ki_copy(p).wait()
        return 0

    lax.fori_loop(0, n_pages, start_ki, 0)
    ring = kbuf.shape[0]
    lanes = next(n for n in (4, 2, 1) if nch % n == 0)
    ahead = ring - lanes
    assert ahead >= 1
    for c0 in range(min(ahead, nch)):
        for cp in kv_copies(c0, c0):
            cp.start()
    lax.fori_loop(0, n_pages, wait_ki, 0)

    wcol = (wi_ref[0] * IDX_HEADS ** -0.5) * IDX_DIM ** -0.5
    qi = qi_ref[0]

    def index_scores(kct):
        width = kct.shape[1]
        lg = jnp.dot(qi, kct, preferred_element_type=F32)
        weighted = jnp.maximum(lg, 0.0) * jnp.broadcast_to(wcol, (IDX_HEADS * rows, width))
        acc = weighted[0:rows]
        for h in range(1, IDX_HEADS):
            acc = acc + weighted[h * rows:(h + 1) * rows]
        return acc

    def scores(i, carry):
        mn, mx = carry
        span = lanes * ts
        acc = index_scores(kibuf[:, pl.ds(pl.multiple_of(i * span, span), span)].astype(BF16))
        sc_ref[i] = acc
        return jnp.minimum(mn, _lane_fold(acc, jnp.minimum)), jnp.maximum(mx, _lane_fold(acc, jnp.maximum))

    mn, mx = lax.fori_loop(0, nch // lanes, scores,
                           (jnp.full((rows, 128), jnp.inf, F32), jnp.full((rows, 128), NEG_INF, F32)))
    acc = index_scores(kin_ref[0])
    tok = lax.broadcasted_iota(I32, (rows, ts), 0) % t_new
    col = lax.broadcasted_iota(I32, (rows, ts), 1)
    adm = jnp.logical_and(col <= tok, col < t_new)
    span = lanes * ts
    ngrp = nch // lanes + 1
    sc_ref[ngrp - 1] = jnp.concatenate(
        [jnp.where(adm, acc, NEG_INF), jnp.full((rows, span - ts), NEG_INF, F32)], axis=1) if lanes > 1 else jnp.where(
            adm, acc, NEG_INF)
    mn = jnp.minimum(mn, _lane_fold(jnp.where(adm, acc, jnp.inf), jnp.minimum))
    mx = jnp.maximum(mx, _lane_fold(jnp.where(adm, acc, NEG_INF), jnp.maximum))
    lo0 = jnp.min(mn, axis=1, keepdims=True)
    hi0 = jnp.max(mx, axis=1, keepdims=True)
    skip = jnp.full((rows, 1), past + 1 <= topk)
    v, cut = _select_threshold(lambda c, r0, nr: sc_ref[c], ngrp, ngrp, span, rows, topk, lo0, hi0, skip)
    vw = jnp.concatenate([v] * (span // 128), axis=1)
    cutw = jnp.concatenate([cut] * (span // 128), axis=1)
    scale = HEAD_DIM ** -0.5

    def bias_of(c):
        x = sc_ref[c]
        kpos = c * span + lax.broadcasted_iota(I32, (rows, span), 1)
        sel = jnp.logical_or(x > vw, jnp.logical_and(x == vw, kpos <= cutw))
        return jnp.where(sel, 0.0, NEG_INF)

    def flash(state, parts):
        m_old, l_old, a_old = state
        s = jnp.concatenate(
            [lax.dot_general(q_ref[0, g], kg, (((1,), (1,)), ((), ())), preferred_element_type=F32) * scale + bias
             for g, bias, kg, _ in parts], axis=0)
        m_new = jnp.maximum(m_old, jnp.max(s, axis=1, keepdims=True))
        alpha = jnp.exp(m_old - m_new)
        p = jnp.exp(s - m_new)
        l_new = l_old * alpha + jnp.sum(p, axis=1, keepdims=True)
        pb = p.astype(BF16)
        pv = jnp.concatenate(
            [jnp.dot(pb[n * rows:(n + 1) * rows], vg, preferred_element_type=F32) for n, (_, _, _, vg) in enumerate(parts)],
            axis=0)
        return m_new, l_new, a_old * alpha + pv

    def receive(c):
        for cp in kv_copies(c, lax.rem(c, ring)):
            cp.wait()

        @pl.when(c + ahead < nch)
        def _():
            for cp in kv_copies(c + ahead, lax.rem(c + ahead, ring)):
                cp.start()

    def attend(i, state):
        for u in range(lanes):
            receive(i * lanes + u)
        parts = []
        bias_grp = bias_of(i)
        for u in range(lanes):
            c = i * lanes + u
            slot = lax.rem(c, ring)
            bias = bias_grp[:, u * ts:(u + 1) * ts]
            parts += [(g, bias, kbuf[slot, g].astype(BF16), vbuf[slot, g].astype(BF16)) for g in range(KV_HEADS)]
        return flash(state, parts)

    pieces = lanes * KV_HEADS
    init = (jnp.full((pieces * rows, 1), -1e30, F32), jnp.zeros((pieces * rows, 1), F32),
            jnp.zeros((pieces * rows, HEAD_DIM), F32))
    m_all, l_all, a_all = lax.fori_loop(0, nch // lanes, attend, init)

    def merge(a, b):
        m = jnp.maximum(a[0], b[0])
        fa, fb = jnp.exp(a[0] - m), jnp.exp(b[0] - m)
        return m, a[1] * fa + b[1] * fb, a[2] * fa + b[2] * fb

    per_lane = KV_HEADS * rows
    state = tuple(x[0:per_lane] for x in (m_all, l_all, a_all))
    for u in range(1, lanes):
        state = merge(state, tuple(x[u * per_lane:(u + 1) * per_lane] for x in (m_all, l_all, a_all)))
    bias = bias_of(ngrp - 1)[:, :ts]
    new_parts = [(g, bias, kn_ref[0][:, g * HEAD_DIM:(g + 1) * HEAD_DIM], vn_ref[0][:, g * HEAD_DIM:(g + 1) * HEAD_DIM])
                 for g in range(KV_HEADS)]
    _, l_fin, a_fin = flash(state, new_parts)
    out = a_fin / l_fin
    for g in range(KV_HEADS):
        o_ref[0, g] = out[g * rows:(g + 1) * rows].astype(o_ref.dtype)


def sample_attention(q, qi, wi, ki_new, k_new, v_new, cache_kidx, cache_k, cache_v, page_table, *, topk):
    nb, t_new, width = q.shape
    heads = width // HEAD_DIM
    group = heads // KV_HEADS
    n_pages = page_table.shape[1]
    ts = SAMPLE_TS
    assert n_pages % (ts // PAGE_SIZE) == 0
    nch = n_pages // (ts // PAGE_SIZE)
    lanes = next(n for n in (4, 2, 1) if nch % n == 0)
    rows = group * t_new
    qg = q.reshape(nb, t_new, KV_HEADS, group, HEAD_DIM).transpose(0, 2, 3, 1, 4).reshape(nb, KV_HEADS, rows, HEAD_DIM)
    qir = jnp.tile(qi.reshape(nb, t_new, IDX_HEADS, IDX_DIM).transpose(0, 2, 1, 3), (1, 1, group, 1))
    qir = qir.reshape(nb, IDX_HEADS * rows, IDX_DIM)
    wir = jnp.tile(wi.transpose(0, 2, 1), (1, 1, group)).reshape(nb, IDX_HEADS * rows, 1)
    padk = lambda a: jnp.pad(a, ((0, 0), (0, ts - t_new), (0, 0)))
    per_b = lambda a: pl.BlockSpec((1,) + a.shape[1:], lambda b, pt: (b,) + (0,) * (a.ndim - 1))
    ops = (qg, qir, wir, padk(ki_new).transpose(0, 2, 1), padk(k_new), padk(v_new))
    cache_kidx_t = cache_kidx.transpose(0, 1, 3, 2)
    out = pl.pallas_call(
        functools.partial(_sample_attn_kernel, n_pages=n_pages, t_new=t_new, topk=topk),
        grid_spec=pltpu.PrefetchScalarGridSpec(
            num_scalar_prefetch=1,
            grid=(nb,),
            in_specs=[per_b(a) for a in ops] + [pl.BlockSpec(memory_space=pl.ANY)] * 3,
            out_specs=pl.BlockSpec((1, KV_HEADS, rows, HEAD_DIM), lambda b, pt: (b, 0, 0, 0)),
            scratch_shapes=[
                pltpu.VMEM((IDX_DIM, n_pages * PAGE_SIZE), F32),
                pltpu.VMEM((SAMPLE_RING, KV_HEADS, ts, HEAD_DIM), F32),
                pltpu.VMEM((SAMPLE_RING, KV_HEADS, ts, HEAD_DIM), F32),
                pltpu.VMEM((nch // lanes + 1, rows, lanes * ts), F32),
                pltpu.SemaphoreType.DMA((1,)),
                pltpu.SemaphoreType.DMA((SAMPLE_RING,)),
                pltpu.SemaphoreType.DMA((SAMPLE_RING,)),
            ]),
        out_shape=jax.ShapeDtypeStruct((nb, KV_HEADS, rows, HEAD_DIM), BF16),
        compiler_params=_cparams("arbitrary"),
        name="sample_attention",
    )(page_table, *ops, cache_kidx_t, cache_k, cache_v)
    return out.reshape(nb, KV_HEADS, group, t_new, HEAD_DIM).transpose(0, 3, 1, 2, 4).reshape(nb, t_new, width)


def prompt_attention(qb, qib, wi, kib, kb, vb, kn2, *, tq, ts, topk):
    s, width = qb.shape
    heads = width // HEAD_DIM
    assert s % tq == 0 and s % ts == 0
    nc = s // ts
    kit = kib.reshape(nc, ts, IDX_DIM).transpose(0, 2, 1)
    kt = kb.reshape(nc, ts, KV_HEADS * HEAD_DIM).transpose(0, 2, 1)
    v3 = vb.reshape(nc, ts, KV_HEADS * HEAD_DIM)
    whole = lambda a: pl.BlockSpec(a.shape, lambda i: (0,) * a.ndim, pipeline_mode=pl.Buffered(1))
    return pl.pallas_call(
        functools.partial(_prompt_attn_kernel, tq=tq, ts=ts, topk=topk),
        grid=(s // tq,),
        in_specs=[
            pl.BlockSpec((tq, width), lambda i: (i, 0)),
            pl.BlockSpec((tq, IDX_HEADS * IDX_DIM), lambda i: (i, 0)),
            pl.BlockSpec((tq, IDX_HEADS), lambda i: (i, 0)),
            whole(kit), whole(kt), whole(v3), whole(kn2),
        ],
        out_specs=pl.BlockSpec((tq, width), lambda i: (i, 0)),
        out_shape=jax.ShapeDtypeStruct((s, width), BF16),
        scratch_shapes=[
            pltpu.VMEM((nc, tq, ts), F32),
            pltpu.VMEM((heads, tq, 1), F32),
            pltpu.VMEM((heads, tq, 1), F32),
            pltpu.VMEM((heads, tq, 2 * HEAD_DIM), F32),
        ],
        compiler_params=_cparams("arbitrary"),
        name="prompt_attention",
    )(qb, qib, wi, kit, kt, v3, kn2)


def _row_tile(m, cap):
    t = cap
    while m % t:
        t //= 2
    return t


def _token_stages(x, attend, gdn, lw):
    (nmw, wp, w_out, nfw, wq, sub_keys, w_u, w_v, fw, tabs) = lw
    m = x.shape[0]
    tm = _row_tile(m, 512)
    p = norm_matmul(x, nmw, wp, tm=tm, tn=wp.shape[1] // 2)
    qb, qib, k32, kb, v32, vb, ki32, kib, auxr, kn2 = rope_split(p, *tabs, tm=tm)
    wi, a_pre = auxr[:, 0:IDX_HEADS], auxr[:, 2 * IDX_HEADS:3 * IDX_HEADS]
    ya = attend(qb, qib, wi, kib, kb, vb, kn2)
    yg, new_conv, new_ssm = gdn(p, a_pre)
    h, xn, qp = out_proj(ya, yg, x, w_out, nfw, wq, tm=tm)
    i1, i2, gate = peer_topk(qp, sub_keys, tm=_row_tile(m, 256))
    tok = lambda a: a.reshape(PEER_HEADS * PEER_TOPK, m).T
    gates = peer_gates(tok(i1), tok(i2), tok(gate), tmb=64)
    f = peer_dense(xn, w_u, w_v, gates, tm=_row_tile(m, 1024), ib=8)
    y = residual_norm(h, f, fw, tm=tm)
    return y, k32, v32, ki32, new_conv, new_ssm


def kernel(x_prompt, x_sample, cache_k, cache_v, cache_kidx, page_table, state_conv, state_ssm, norm_mix_w, w_in,
           conv_w, a_log, dt_bias, gdn_norm_w, w_out, norm_ffn_w, peer_wq, peer_sub_keys, peer_u, peer_v,
           norm_final_w):
    depth = w_in.shape[0]
    b, s, d = x_prompt.shape
    nb, t, _ = x_sample.shape
    assert depth == 1 and b == 1, "single layer, single prompt sequence"
    past = page_table.shape[1] * PAGE_SIZE
    heads_g = state_ssm.shape[2]
    conv_ch = state_conv.shape[-1]

    wp = _pack_w_in(w_in[0])
    shared = (norm_mix_w[0][None], wp, w_out[0].astype(BF16), norm_ffn_w[0][None], peer_wq[0].astype(BF16),
              peer_sub_keys[0], peer_u[0].astype(BF16), peer_v[0].astype(BF16), norm_final_w[None])
    gdn_w = (conv_w[0], a_log[0], dt_bias[0], gdn_norm_w[0])

    def attend_p(qb, qib, wi, kib, kb, vb, kn2):
        return prompt_attention(qb, qib, wi, kib, kb, vb, kn2.T, tq=_row_tile(s, 256), ts=_row_tile(s, 512),
                                topk=min(TOPK_MAX, s // 4))

    def gdn_p(p, a_pre):
        conv0 = jnp.zeros((1, CONV_W - 1, conv_ch), F32)
        ssm0 = jnp.zeros((1, heads_g, GDN_DK, GDN_DV), F32)
        block = 2 * GDN_CHUNK if s % (2 * GDN_CHUNK) == 0 else GDN_CHUNK
        assert s % block == 0
        return gdn_mixer(p, a_pre.T[None], conv0, *gdn_w, ssm0, n_seq=1, t_pad=s, t_total=s, chunk=block,
                         sub=GDN_CHUNK)

    tabs_p = _rope_tables(jnp.arange(s))
    y_p, k_p, v_p, ki_p, conv_p, ssm_p = _token_stages(x_prompt.reshape(s, d), attend_p, gdn_p, shared + (tabs_p,))

    m_s = nb * t
    t8 = -(-t // 8) * 8

    def attend_s(qb, qib, wi, kib, kb, vb, kn2):
        seq = lambda a: a.reshape(nb, t, a.shape[-1])
        y = sample_attention(seq(qb), seq(qib), seq(wi), seq(kib), seq(kb), seq(vb), cache_kidx, cache_k, cache_v,
                             page_table, topk=min(TOPK_MAX, (past + t) // 4))
        return y.reshape(m_s, y.shape[-1])

    def gdn_s(p, a_pre):
        pad_t = lambda a: jnp.pad(a.reshape(nb, t, a.shape[-1]), ((0, 0), (0, t8 - t), (0, 0)))
        p8 = pad_t(p).reshape(nb * t8, p.shape[-1])
        at = pad_t(a_pre).transpose(0, 2, 1)
        yg, nconv, nssm = gdn_mixer(p8, at, state_conv[0], *gdn_w, state_ssm[0], n_seq=nb, t_pad=t8, t_total=t,
                                    chunk=t8, sub=8)
        return yg.reshape(nb, t8, yg.shape[-1])[:, :t].reshape(m_s, yg.shape[-1]), nconv, nssm

    tabs_s = tuple(jnp.tile(a, (nb, 1)) for a in _rope_tables(past + jnp.arange(t)))
    y_s, k_s, v_s, ki_s, conv_s, ssm_s = _token_stages(x_sample.reshape(m_s, d), attend_s, gdn_s, shared + (tabs_s,))

    kv = lambda a, n, tt: a.reshape(1, n, tt, KV_HEADS, HEAD_DIM)
    return (y_p.reshape(b, s, d), y_s.reshape(nb, t, d),
            kv(k_p, b, s), kv(v_p, b, s), ki_p.reshape(1, b, s, IDX_DIM), conv_p[None], ssm_p[None],
            kv(k_s, nb, t), kv(v_s, nb, t), ki_s.reshape(1, nb, t, IDX_DIM), conv_s[None], ssm_s[None])
```

```python
import functools
import math

import jax
import jax.numpy as jnp
from jax import lax
from jax.experimental import pallas as pl
from jax.experimental.pallas import tpu as pltpu

F32 = jnp.float32
BF16 = jnp.bfloat16
I32 = jnp.int32
EPS = 1e-6
NEG_INF = float("-inf")

HEAD_DIM = 128
KV_HEADS = 2
IDX_HEADS = 8
IDX_DIM = 64
TOPK_MAX = 256
ROPE_THETA = 10000.0
PAGE_SIZE = 128
GDN_DK = 128
GDN_DV = 128
CONV_W = 4
GDN_CHUNK = 64
PEER_HEADS = 8
PEER_KEYS = 128
PEER_QDIM = 128
PEER_TOPK = 16

VMEM_LIMIT_BYTES = 56 * 1024 * 1024

TOKEN_TILE = 512
PEER_TOKEN_TILE = 1024
PEER_KEY_BLOCK = 8
PEER_TOPK_TILE = 256
GATE_TILE = 256
ATTN_Q_TILE = 256
ATTN_KEY_CHUNK = 512
MAX_SEARCH_STEPS = 4096
SOFTMAX_FLOOR = -1e30
UNDERFLOW_GUARD = 1e-30


def _cparams(*sem):
    return pltpu.CompilerParams(dimension_semantics=sem, vmem_limit_bytes=VMEM_LIMIT_BYTES)


_PROJ_SRC = (("q", 1024), ("k", 256), ("v", 256), ("qi", 512), ("ki", 64), ("wi", 8), ("qkv", 3072), ("z", 1024),
             ("b", 8), ("a", 8))
_PROJ_DST = ("q", "qi", "k", "v", "qkv", "z", "ki", "wi", "b", "a")
PROJ_PACKED = 6400


def _pack_w_in_kernel(w_ref, o_ref):
    src, off = {}, 0
    for name, width in _PROJ_SRC:
        src[name] = (off, width)
        off += width
    dst = 0
    for name in _PROJ_DST:
        s0, width = src[name]
        o_ref[:, dst:dst + width] = w_ref[:, s0:s0 + width].astype(o_ref.dtype)
        dst += width
    o_ref[:, dst:] = jnp.zeros((o_ref.shape[0], o_ref.shape[1] - dst), o_ref.dtype)


def _pack_w_in(w_in):
    d, n = w_in.shape
    assert n == sum(w for _, w in _PROJ_SRC)
    tr = _row_tile(d, 256)
    return pl.pallas_call(
        _pack_w_in_kernel,
        grid=(d // tr,),
        in_specs=[pl.BlockSpec((tr, n), lambda i: (i, 0))],
        out_specs=pl.BlockSpec((tr, PROJ_PACKED), lambda i: (i, 0)),
        out_shape=jax.ShapeDtypeStruct((d, PROJ_PACKED), BF16),
        compiler_params=_cparams("arbitrary"),
        name="pack_w_in",
    )(w_in)


def _rope_tables(pos):
    def table(dim):
        half = dim // 2
        inv = ROPE_THETA ** (-jnp.arange(half, dtype=F32) / half)
        ang = pos.astype(F32)[:, None] * inv[None, :]
        cos = jnp.tile(jnp.cos(ang), (1, 128 // half))
        sin = jnp.tile(jnp.concatenate([-jnp.sin(ang), jnp.sin(ang)], axis=1), (1, 128 // dim))
        return cos, sin
    cq, sq = table(HEAD_DIM)
    ci, si = table(IDX_DIM)
    return cq, sq, ci, si


def _norm_matmul_kernel(x_ref, nw_ref, w_ref, o_ref):
    x = x_ref[...]
    y = x * lax.rsqrt(jnp.mean(x * x, axis=-1, keepdims=True) + EPS)
    xn = (y * nw_ref[...]).astype(BF16)
    o_ref[...] = jnp.dot(xn, w_ref[...], preferred_element_type=F32)


def norm_matmul(x, nw, w, *, tm, tn):
    m, d = x.shape
    n = w.shape[1]
    assert m % tm == 0 and n % tn == 0
    return pl.pallas_call(
        _norm_matmul_kernel,
        grid=(n // tn, m // tm),
        in_specs=[
            pl.BlockSpec((tm, d), lambda j, i: (i, 0)),
            pl.BlockSpec((1, d), lambda j, i: (0, 0)),
            pl.BlockSpec((d, tn), lambda j, i: (0, j)),
        ],
        out_specs=pl.BlockSpec((tm, tn), lambda j, i: (i, j)),
        out_shape=jax.ShapeDtypeStruct((m, n), F32),
        compiler_params=_cparams("arbitrary", "arbitrary"),
        name="norm_matmul",
    )(x, nw, w)


def _rope128(x, cos, sin_signed):
    return x * cos + pltpu.roll(x, 64, 1) * sin_signed


def _rope64(x, cos, sin_signed, first_half):
    partner = jnp.where(first_half, pltpu.roll(x, 96, 1), pltpu.roll(x, 32, 1))
    return x * cos + partner * sin_signed


def _rope_split_kernel(q_ref, qi_ref, k_ref, v_ref, aux_ref, cq_ref, sq_ref, ci_ref, si_ref,
                       qb_ref, qib_ref, k32_ref, kb_ref, v32_ref, vb_ref, ki32_ref, kib_ref, auxr_ref, kn2_ref):
    cq, sq, ci, si = cq_ref[...], sq_ref[...], ci_ref[...], si_ref[...]
    lane = lax.broadcasted_iota(I32, ci.shape, 1)
    first_half = (lane % IDX_DIM) < (IDX_DIM // 2)
    for h in range(q_ref.shape[1] // HEAD_DIM):
        sl = slice(h * HEAD_DIM, (h + 1) * HEAD_DIM)
        qb_ref[:, sl] = _rope128(q_ref[:, sl], cq, sq).astype(BF16)
    for h in range(k_ref.shape[1] // HEAD_DIM):
        sl = slice(h * HEAD_DIM, (h + 1) * HEAD_DIM)
        kr = _rope128(k_ref[:, sl], cq, sq)
        k32_ref[:, sl] = kr
        krb = kr.astype(BF16)
        kb_ref[:, sl] = krb
        kn2_ref[:, h:h + 1] = jnp.sum(krb.astype(F32) * krb.astype(F32), axis=1, keepdims=True)
    for h in range(qi_ref.shape[1] // 128):
        sl = slice(h * 128, (h + 1) * 128)
        qib_ref[:, sl] = _rope64(qi_ref[:, sl], ci, si, first_half).astype(BF16)
    v = v_ref[...]
    v32_ref[...] = v
    vb_ref[...] = v.astype(BF16)
    aux = aux_ref[...]
    kir = _rope64(aux, ci, si, first_half)[:, :IDX_DIM]
    ki32_ref[...] = kir
    kib_ref[...] = kir.astype(BF16)
    auxr_ref[...] = aux[:, IDX_DIM:]


def rope_split(p, cq, sq, ci, si, *, tm):
    m = p.shape[0]
    assert m % tm == 0
    row = lambda w, j: pl.BlockSpec((tm, w), lambda i, j=j: (i, j))
    outs = [
        ((m, 1024), BF16), ((m, 512), BF16), ((m, 256), F32), ((m, 256), BF16), ((m, 256), F32),
        ((m, 256), BF16), ((m, IDX_DIM), F32), ((m, IDX_DIM), BF16), ((m, 128 - IDX_DIM), F32),
        ((m, KV_HEADS), F32),
    ]
    return pl.pallas_call(
        _rope_split_kernel,
        grid=(m // tm,),
        in_specs=[row(1024, 0), row(512, 2), row(256, 6), row(256, 7), row(128, 48),
                  row(128, 0), row(128, 0), row(128, 0), row(128, 0)],
        out_specs=[pl.BlockSpec((tm, s[1]), lambda i: (i, 0)) for s, _ in outs],
        out_shape=[jax.ShapeDtypeStruct(s, dt) for s, dt in outs],
        compiler_params=_cparams("arbitrary"),
        name="rope_split",
    )(p, p, p, p, p, cq, sq, ci, si)


def _out_proj_kernel(ya_ref, yg_ref, res_ref, wa_ref, wg_ref, nw_ref, wq_ref, h_ref, xn_ref, qp_ref):
    h = res_ref[...] + jnp.dot(ya_ref[...], wa_ref[...], preferred_element_type=F32)
    h = h + jnp.dot(yg_ref[...], wg_ref[...], preferred_element_type=F32)
    h_ref[...] = h
    y = h * lax.rsqrt(jnp.mean(h * h, axis=-1, keepdims=True) + EPS)
    xn = (y * nw_ref[...]).astype(BF16)
    xn_ref[...] = xn
    qp_ref[...] = jnp.dot(xn, wq_ref[...], preferred_element_type=F32)


def out_proj(ya, yg, res, w_out, nw, wq, *, tm):
    m, d = res.shape
    half = ya.shape[1]
    nq = wq.shape[1]
    assert m % tm == 0
    rows = lambda wd: pl.BlockSpec((tm, wd), lambda i: (i, 0))
    const = lambda a: pl.BlockSpec(a.shape, lambda i: (0,) * a.ndim, pipeline_mode=pl.Buffered(1))
    wa, wg = w_out[:half], w_out[half:]
    return pl.pallas_call(
        _out_proj_kernel,
        grid=(m // tm,),
        in_specs=[rows(half), rows(half), rows(d), const(wa), const(wg), const(nw), const(wq)],
        out_specs=[rows(d), rows(d), rows(nq)],
        out_shape=[jax.ShapeDtypeStruct((m, d), F32), jax.ShapeDtypeStruct((m, d), BF16),
                   jax.ShapeDtypeStruct((m, nq), F32)],
        compiler_params=_cparams("arbitrary"),
        name="out_proj",
    )(ya, yg, res, wa, wg, nw, wq)


def _take_top(s, codes, count):
    big = jnp.int32(2 ** 30)
    vals, picks = [], []
    for _ in range(count):
        m = jnp.max(s, axis=0, keepdims=True)
        pick = jnp.min(jnp.where(s == m, codes, big), axis=0, keepdims=True)
        s = jnp.where(codes == pick, NEG_INF, s)
        vals.append(m)
        picks.append(pick)
    return jnp.concatenate(vals, axis=0), jnp.concatenate(picks, axis=0)


def _lookup(table, sel, count):
    out = jnp.zeros(sel.shape, table.dtype)
    for a in range(count):
        out = jnp.where(sel == a, jnp.broadcast_to(table[a:a + 1, :], sel.shape), out)
    return out


def _peer_topk_kernel(q_ref, sk_ref, i1_ref, i2_ref, g_ref):
    def one_head(h, _):
        q = q_ref[:, pl.ds(pl.multiple_of(h * PEER_QDIM, PEER_QDIM), PEER_QDIM)]
        i1_ref[h], i2_ref[h], g_ref[h] = _peer_topk_tile(q, lambda c: sk_ref[2 * h + c])
        return 0

    lax.fori_loop(0, i1_ref.shape[0], one_head, 0)


def _peer_topk_tile(q, sub_keys):
    tm = q.shape[0]
    kk = PEER_TOPK
    half = PEER_QDIM // 2
    key_codes = lax.broadcasted_iota(I32, (PEER_KEYS, tm), 0)
    top_v, top_i = [], []
    for c in range(2):
        qs = q[:, c * half:(c + 1) * half].astype(BF16)
        s = lax.dot_general(sub_keys(c), qs, (((1,), (1,)), ((), ())), preferred_element_type=F32)
        vals, idx = _take_top(s, key_codes, kk)
        top_v.append(vals)
        top_i.append(idx)
    pieces, codes = [], []

    def add_piece(cand, a_of_row, b_of_row):
        ok = (a_of_row + 1) * (b_of_row + 1) <= kk
        pieces.append(jnp.where(ok, cand, NEG_INF))
        codes.append(a_of_row * kk + b_of_row)

    split = 4
    for a in range(split):
        rows = -(-(kk // (a + 1)) // 8) * 8
        r = lax.broadcasted_iota(I32, (rows, tm), 0)
        add_piece(top_v[0][a:a + 1, :] + top_v[1][0:rows, :], jnp.full((rows, tm), a, I32), r)
    r8 = lax.broadcasted_iota(I32, (8, tm), 0)
    for b in range(kk // (split + 1)):
        cand = top_v[0][0:8, :] + top_v[1][b:b + 1, :]
        add_piece(jnp.where(r8 >= split, cand, NEG_INF), r8, jnp.full((8, tm), b, I32))
    add_piece(top_v[0][8:kk, :] + top_v[1][0:1, :], r8 + 8, jnp.zeros((8, tm), I32))
    best_s, best_c = _take_top(jnp.concatenate(pieces, axis=0), jnp.concatenate(codes, axis=0), kk)
    i1 = _lookup(top_i[0], lax.shift_right_logical(best_c, 4), kk)
    i2 = _lookup(top_i[1], jnp.bitwise_and(best_c, kk - 1), kk)
    e = jnp.exp(best_s - best_s[0:1, :])
    return i1, i2, e / jnp.sum(e, axis=0, keepdims=True)


def peer_topk(qp, sub_keys, *, tm):
    m = qp.shape[0]
    heads = sub_keys.shape[0]
    assert m % tm == 0 and PEER_TOPK == 16
    sk = sub_keys.reshape(heads * 2, PEER_KEYS, PEER_QDIM // 2).astype(BF16)
    out = pl.BlockSpec((heads, PEER_TOPK, tm), lambda i: (0, 0, i))
    return pl.pallas_call(
        _peer_topk_kernel,
        grid=(m // tm,),
        in_specs=[pl.BlockSpec((tm, heads * PEER_QDIM), lambda i: (i, 0)),
                  pl.BlockSpec(sk.shape, lambda i: (0, 0, 0))],
        out_specs=[out, out, out],
        out_shape=[jax.ShapeDtypeStruct((heads, PEER_TOPK, m), I32), jax.ShapeDtypeStruct((heads, PEER_TOPK, m), I32),
                   jax.ShapeDtypeStruct((heads, PEER_TOPK, m), F32)],
        compiler_params=_cparams("arbitrary"),
        name="peer_topk",
    )(qp, sk)


GATE_TOKENS = 32


def _peer_gate_kernel(i1_ref, i2_ref, g_ref, o_ref):
    tmb = i1_ref.shape[0]
    nk = PEER_KEYS
    sub = lax.broadcasted_iota(I32, (nk, i1_ref.shape[1]), 0)

    def token_block(j, _):
        t0 = pl.multiple_of(j * GATE_TOKENS, GATE_TOKENS)
        per_token = []
        for u in range(GATE_TOKENS):
            wide = lambda ref: jnp.broadcast_to(ref[pl.ds(t0 + u, 1), :], sub.shape)
            p1 = jnp.where(wide(i1_ref) == sub, wide(g_ref), 0.0).astype(BF16)
            p2 = jnp.where(wide(i2_ref) == sub, 1.0, 0.0).astype(BF16)
            gm = lax.dot_general(p1, p2, (((1,), (1,)), ((), ())), preferred_element_type=F32)
            per_token.append(gm.astype(o_ref.dtype))
        block = jnp.stack(per_token, axis=0)
        o_ref[:, pl.ds(t0, GATE_TOKENS), :] = pltpu.einshape("mab->amb", block)
        return 0

    lax.fori_loop(0, tmb // GATE_TOKENS, token_block, 0)


def peer_gates(i1, i2, gate, *, tmb):
    m, slots = i1.shape
    assert m % tmb == 0 and tmb % GATE_TOKENS == 0
    rows = pl.BlockSpec((tmb, slots), lambda i: (i, 0))
    return pl.pallas_call(
        _peer_gate_kernel,
        grid=(m // tmb,),
        in_specs=[rows, rows, rows],
        out_specs=pl.BlockSpec((PEER_KEYS, tmb, PEER_KEYS), lambda i: (0, i, 0)),
        out_shape=jax.ShapeDtypeStruct((PEER_KEYS, m, PEER_KEYS), BF16),
        compiler_params=_cparams("arbitrary"),
        name="peer_gates",
    )(i1, i2, gate)


def _gelu_tanh(x):
    return 0.5 * x * (1.0 + jnp.tanh(math.sqrt(2.0 / math.pi) * (x + 0.044715 * (x * x * x))))


def _peer_dense_kernel(xn_ref, wu_ref, wv_ref, g_ref, o_ref):
    j = pl.program_id(1)
    ib = g_ref.shape[0]
    a = lax.dot_general(xn_ref[...], wu_ref[...], (((1,), (1,)), ((), ())), preferred_element_type=F32)
    act = _gelu_tanh(a)
    hm = jnp.concatenate(
        [(g_ref[u].astype(F32) * act[:, u * PEER_KEYS:(u + 1) * PEER_KEYS]).astype(BF16) for u in range(ib)], axis=1)
    part = jnp.dot(hm, wv_ref[...], preferred_element_type=F32)

    @pl.when(j == 0)
    def _():
        o_ref[...] = part

    @pl.when(j > 0)
    def _():
        o_ref[...] += part


def peer_dense(xn, w_u, w_v, gates, *, tm, ib):
    m, d = xn.shape
    assert m % tm == 0 and PEER_KEYS % ib == 0
    eb = ib * PEER_KEYS
    return pl.pallas_call(
        _peer_dense_kernel,
        grid=(m // tm, PEER_KEYS // ib),
        in_specs=[
            pl.BlockSpec((tm, d), lambda i, j: (i, 0)),
            pl.BlockSpec((eb, d), lambda i, j: (j, 0)),
            pl.BlockSpec((eb, d), lambda i, j: (j, 0)),
            pl.BlockSpec((ib, tm, PEER_KEYS), lambda i, j: (j, i, 0)),
        ],
        out_specs=pl.BlockSpec((tm, d), lambda i, j: (i, 0)),
        out_shape=jax.ShapeDtypeStruct((m, d), F32),
        compiler_params=_cparams("arbitrary", "arbitrary"),
        name="peer_dense",
    )(xn, w_u, w_v, gates)


def _residual_norm_kernel(h_ref, f_ref, w_ref, o_ref):
    y = h_ref[...] + f_ref[...]
    o_ref[...] = y * lax.rsqrt(jnp.mean(y * y, axis=-1, keepdims=True) + EPS) * w_ref[...]


def residual_norm(h, f, w, *, tm):
    m, d = h.shape
    assert m % tm == 0
    rows = pl.BlockSpec((tm, d), lambda i: (i, 0))
    return pl.pallas_call(
        _residual_norm_kernel,
        grid=(m // tm,),
        in_specs=[rows, rows, pl.BlockSpec((1, d), lambda i: (0, 0))],
        out_specs=rows,
        out_shape=jax.ShapeDtypeStruct((m, d), F32),
        compiler_params=_cparams("arbitrary"),
        name="residual_norm",
    )(h, f, w)


_HI = lax.Precision.HIGHEST


def _dot_hi(a, b):
    return jnp.dot(a, b, preferred_element_type=F32, precision=_HI)


_BNN = (((2,), (1,)), ((0,), (0,)))
_BNT = (((2,), (2,)), ((0,), (0,)))
_BTN = (((1,), (1,)), ((0,), (0,)))


def _bdot(a, b, dims=_BNN):
    return lax.dot_general(a.astype(BF16), b.astype(BF16), dims, preferred_element_type=F32)


def _split2(a):
    hi = a.astype(BF16)
    return hi, (a - hi.astype(F32)).astype(BF16)


def _bdot3(a, b, dims=_BNN):
    (ah, al), (bh, bl) = a, b
    dot = lambda x, y: lax.dot_general(x, y, dims, preferred_element_type=F32)
    return dot(ah, bh) + (dot(ah, bl) + dot(al, bh))


def _sigmoid(x):
    return 1.0 / (1.0 + jnp.exp(-x))


def _softplus(x):
    return jnp.maximum(x, 0.0) + jnp.log1p(jnp.exp(-jnp.abs(x)))


def _gdn_kernel(xq_ref, xk_ref, xv_ref, z_ref, aux_ref, at_ref, cs_ref, cw_ref, alog_ref, dtb_ref,
                alogt_ref, dtbt_ref, nw_ref, s0_ref, y_ref, nconv_ref, nssm_ref, xp_ref, st_ref,
                *, chunk, sub, t_total, b_lane, a_lane):
    c = pl.program_id(1)
    nchunks = pl.num_programs(1)
    heads = st_ref.shape[0]
    width = heads * GDN_DK
    tail = CONV_W - 1

    @pl.when(c == 0)
    def _():
        xp_ref[8 - tail:8, :] = cs_ref[0]
        st_ref[...] = s0_ref[0]

    xp_ref[8:8 + chunk, 0:width] = xq_ref[...]
    xp_ref[8:8 + chunk, width:2 * width] = xk_ref[...]
    xp_ref[8:8 + chunk, 2 * width:3 * width] = xv_ref[...]
    conv = cw_ref[tail:tail + 1, :] * xp_ref[8:8 + chunk, :]
    for j in range(tail):
        conv = conv + cw_ref[j:j + 1, :] * xp_ref[8 - tail + j:8 - tail + j + chunk, :]
    conv = conv * _sigmoid(conv)

    padded = t_total % chunk != 0
    ridx = c * chunk + lax.broadcasted_iota(I32, (chunk, 1), 0)
    rvalid = ridx < t_total
    cidx = c * chunk + lax.broadcasted_iota(I32, (1, chunk), 1)
    cvalid = cidx < t_total

    aux = aux_ref[...]
    beta = _sigmoid(aux[:, b_lane:b_lane + heads])
    g = -jnp.exp(alog_ref[...]) * _softplus(aux[:, a_lane:a_lane + heads] + dtb_ref[...])
    gt = -jnp.exp(alogt_ref[...]) * _softplus(at_ref[0] + dtbt_ref[...])
    if padded:
        beta = jnp.where(rvalid, beta, 0.0)
        g = jnp.where(rvalid, g, 0.0)
        gt = jnp.where(cvalid, gt, 0.0)
    n_sub = chunk // sub
    rf = lax.broadcasted_iota(I32, (chunk, chunk), 0)
    cf = lax.broadcasted_iota(I32, (chunk, chunk), 1)
    same = (rf // sub) == (cf // sub)
    gc = _dot_hi(jnp.where(jnp.logical_and(same, rf >= cf), 1.0, 0.0), g)
    gct = _dot_hi(gt, jnp.where(jnp.logical_and(same, rf <= cf), 1.0, 0.0))
    ri = lax.broadcasted_iota(I32, (sub, sub), 0)
    ci = lax.broadcasted_iota(I32, (sub, sub), 1)
    causal = ri >= ci
    strict = ri > ci
    eye = jnp.where(ri == ci, 1.0, 0.0)

    per_head = lambda x2d, base: jnp.stack(
        [x2d[:, base + h * GDN_DK:base + (h + 1) * GDN_DK] for h in range(heads)], axis=0)
    q = per_head(conv, 0)
    k = per_head(conv, width)
    v = per_head(conv, 2 * width)
    q = q * lax.rsqrt(jnp.sum(q * q, axis=-1, keepdims=True) + EPS) * GDN_DK ** -0.5
    k = k * lax.rsqrt(jnp.sum(k * k, axis=-1, keepdims=True) + EPS)
    if padded:
        k = jnp.where(rvalid[None], k, 0.0)
        v = jnp.where(rvalid[None], v, 0.0)
    gcol = jnp.stack([gc[:, h:h + 1] for h in range(heads)], axis=0)
    grow = jnp.stack([gct[h:h + 1, :] for h in range(heads)], axis=0)
    bcol = jnp.stack([beta[:, h:h + 1] for h in range(heads)], axis=0)
    by_rows = lambda a: jnp.concatenate([a[:, j * sub:(j + 1) * sub] for j in range(n_sub)], axis=0)
    q, k, v, gcol, bcol = (by_rows(a) for a in (q, k, v, gcol, bcol))
    grow = jnp.concatenate([grow[:, :, j * sub:(j + 1) * sub] for j in range(n_sub)], axis=0)
    decay = jnp.where(causal[None], jnp.exp(jnp.where(causal[None], gcol - grow, 0.0)), 0.0)
    kb = k * bcol
    vb = v * bcol
    x = -jnp.where(strict[None], _bdot(kb, k, _BNT) * decay, 0.0)
    tinv = eye[None] + x
    xs = _split2(x)
    span = 2
    while span < sub:
        xs = _split2(_bdot3(xs, xs))
        tinv = tinv + _bdot3(_split2(tinv), xs)
        span *= 2
    u = _bdot(tinv, vb)
    w = _bdot(tinv, kb * jnp.exp(gcol))
    intra = jnp.where(causal[None], _bdot(q, k, _BNT) * decay, 0.0)
    qg = q * jnp.exp(gcol)
    glast = gcol[:, sub - 1:sub, :]
    kd = k * jnp.exp(glast - gcol)
    state = st_ref[...]
    outs = []
    for j in range(n_sub):
        sl = slice(j * heads, (j + 1) * heads)
        v_new = u[sl] - _bdot(w[sl], state)
        outs.append(_bdot(qg[sl], state) + _bdot(intra[sl], v_new))
        state = state * jnp.exp(glast[sl]) + _bdot(kd[sl], v_new, _BTN)
    st_ref[...] = state
    out = outs[0] if n_sub == 1 else jnp.concatenate(outs, axis=1)
    on = out * lax.rsqrt(jnp.mean(out * out, axis=-1, keepdims=True) + EPS) * nw_ref[...]
    for h in range(heads):
        sl = slice(h * GDN_DV, (h + 1) * GDN_DV)
        zh = z_ref[:, sl]
        y_ref[:, sl] = (on[h] * (zh * _sigmoid(zh))).astype(y_ref.dtype)

    last_valid = t_total - (t_total - 1) // chunk * chunk
    @pl.when(c < nchunks - 1)
    def _():
        xp_ref[8 - tail:8, :] = xp_ref[8 + chunk - tail:8 + chunk, :]

    @pl.when(c == nchunks - 1)
    def _():
        nconv_ref[0] = xp_ref[8 + last_valid - tail:8 + last_valid, :]
        nssm_ref[0] = st_ref[...]


def gdn_mixer(p, at, conv_state, conv_w, a_log, dt_bias, norm_w, ssm_state, *, n_seq, t_pad, t_total, chunk, sub):
    heads = ssm_state.shape[1]
    width = heads * GDN_DK
    assert t_pad % chunk == 0 and chunk % sub == 0 and sub % 8 == 0
    nch = t_pad // chunk
    rows = lambda wd, j: pl.BlockSpec((chunk, wd), lambda n, c, j=j: (n * nch + c, j))
    const = lambda a: pl.BlockSpec(a.shape, lambda n, c: (0,) * a.ndim)
    alog = a_log.reshape(1, heads)
    dtb = dt_bias.reshape(1, heads)
    nw = norm_w.reshape(1, GDN_DV)
    at = at.reshape(n_seq, heads, nch, chunk).transpose(0, 2, 1, 3).reshape(n_seq * nch, heads, chunk)
    kern = functools.partial(_gdn_kernel, chunk=chunk, sub=sub, t_total=t_total, b_lane=IDX_DIM + 8,
                             a_lane=IDX_DIM + 16)
    return pl.pallas_call(
        kern,
        grid=(n_seq, nch),
        in_specs=[
            rows(width, 2), rows(width, 3), rows(width, 4), rows(width, 5), rows(128, 48),
            pl.BlockSpec((1, heads, chunk), lambda n, c: (n * nch + c, 0, 0)),
            pl.BlockSpec((1, CONV_W - 1, 3 * width), lambda n, c: (n, 0, 0)),
            const(conv_w), const(alog), const(dtb), const(alog.T), const(dtb.T), const(nw),
            pl.BlockSpec((1, heads, GDN_DK, GDN_DV), lambda n, c: (n, 0, 0, 0)),
        ],
        out_specs=[
            pl.BlockSpec((chunk, width), lambda n, c: (n * nch + c, 0)),
            pl.BlockSpec((1, CONV_W - 1, 3 * width), lambda n, c: (n, 0, 0)),
            pl.BlockSpec((1, heads, GDN_DK, GDN_DV), lambda n, c: (n, 0, 0, 0)),
        ],
        out_shape=[
            jax.ShapeDtypeStruct((n_seq * t_pad, width), BF16),
            jax.ShapeDtypeStruct((n_seq, CONV_W - 1, 3 * width), F32),
            jax.ShapeDtypeStruct((n_seq, heads, GDN_DK, GDN_DV), F32),
        ],
        scratch_shapes=[pltpu.VMEM((8 + chunk, 3 * width), F32), pltpu.VMEM((heads, GDN_DK, GDN_DV), F32)],
        compiler_params=_cparams("arbitrary", "arbitrary"),
        name="gdn_mixer",
    )(p, p, p, p, p, at, conv_state, conv_w, alog, dtb, alog.T, dtb.T, nw, ssm_state)


def _lane_fold(x, op):
    out = x[:, :128]
    for j in range(1, x.shape[1] // 128):
        out = op(out, x[:, j * 128:(j + 1) * 128])
    return out


TIE_UNROLL = 4


def _select_threshold(load, nch, nch_max, ts, rows, topk, lo0, hi0, skip):
    kf = float(topk)
    rb = min(rows, 128)
    assert rows % rb == 0
    assert nch_max * (ts // 128) <= 256
    lane = lax.broadcasted_iota(I32, (rb, 128), 1)
    on_lanes = rb == 128
    if on_lanes:
        state_shape = (8, rows)
        ones = jnp.ones((8, 128), BF16)
        to_state = lambda blk: jnp.transpose(jnp.broadcast_to(blk, (128, 128)))[0:8]
        to_rows = lambda st, r0: jnp.transpose(jnp.broadcast_to(st[0:1, r0:r0 + 128], (128, 128)))
        row_sums = lambda acc: lax.dot_general(ones, acc.astype(BF16), (((1,), (1,)), ((), ())),
                                               preferred_element_type=F32)
        join_axis = 1
    else:
        state_shape = (rows, 128)
        ones = jnp.ones((128, 128), BF16)
        to_state = lambda blk: jnp.broadcast_to(blk, (rb, 128))
        to_rows = lambda st, r0: st[r0:r0 + rb]
        row_sums = lambda acc: jnp.dot(acc.astype(BF16), ones, preferred_element_type=F32)
        join_axis = 0
    join = lambda parts: parts[0] if len(parts) == 1 else jnp.concatenate(parts, axis=join_axis)
    rep = lambda col: join([to_state(col[r0:r0 + rb].astype(F32)) for r0 in range(0, rows, rb)])
    lo0, hi0, skip = rep(lo0), rep(hi0), rep(skip) > 0.5

    def count(pred, *cols):
        blocks = range(0, rows, rb)
        wides = [[to_rows(col, r0) for col in cols] for r0 in blocks]
        accs = []
        for r0, wide in zip(blocks, wides):
            def body(c, acc, r0=r0, wide=wide):
                x = load(c, r0, rb)
                for j in range(ts // 128):
                    kpos = lane + (c * ts + j * 128)
                    acc = acc + jnp.where(pred(x[:, j * 128:(j + 1) * 128], kpos, *wide), 1.0, 0.0)
                return acc

            accs.append(lax.fori_loop(0, nch, body, jnp.zeros((rb, 128), F32)))
        return join([row_sums(acc) for acc in accs])

    ge_ = lambda x, kpos, t: x >= t
    gt_ = lambda x, kpos, t: x > t
    eq_ = lambda x, kpos, t: x == t

    zero = jnp.zeros(state_shape, F32)
    span = (nch * ts).astype(F32) if hasattr(nch, "astype") else float(nch * ts)

    def search(state, limit):
        def cond(st):
            _, _, done, _, it = st
            return jnp.logical_and(jnp.min(done) < 0.5, it < limit)

        def body(st):
            lo, hi, done, hit_any, it = st
            mid = jnp.minimum(jnp.maximum(lo * 0.5 + hi * 0.5, lo), hi)
            cnt = count(ge_, mid)
            active = done < 0.5
            ge = jnp.logical_and(active, cnt >= kf)
            hit = jnp.logical_and(active, cnt == kf)
            collapsed = jnp.logical_or(mid <= lo, mid >= hi)
            new_lo = jnp.where(ge, mid, lo)
            new_hi = jnp.where(jnp.logical_and(active, jnp.logical_not(jnp.logical_or(ge, collapsed))), mid, hi)
            new_hi = jnp.where(hit, mid, new_hi)
            new_done = jnp.where(jnp.logical_or(hit, jnp.logical_and(active, collapsed)), 1.0, done)
            return new_lo, new_hi, new_done, jnp.where(hit, 1.0, hit_any), it + 1

        lo, hi, done, hit_any, _ = lax.while_loop(cond, body, state + (jnp.int32(0),))
        return lo, hi, done, hit_any

    c_pos = count(gt_, zero)
    c_nn = count(ge_, zero)
    at_zero = jnp.logical_and(jnp.logical_not(skip), jnp.logical_and(c_pos < kf, c_nn >= kf))
    lo0 = jnp.where(at_zero, 0.0, jnp.where(c_pos >= kf, jnp.maximum(lo0, 0.0), lo0))
    hi0 = jnp.where(at_zero, 0.0, jnp.where(c_nn < kf, jnp.minimum(hi0, 0.0), hi0))
    lo, hi, _, hit_any = search((lo0, hi0, jnp.where(jnp.logical_or(skip, at_zero), 1.0, 0.0), zero), MAX_SEARCH_STEPS)
    hit = hit_any > 0.5

    def count_collapsed(_):
        cnt_hi = count(ge_, hi)
        vc = jnp.where(cnt_hi >= kf, hi, lo)
        return vc, count(gt_, vc), count(eq_, vc)

    settled = jnp.logical_or(jnp.logical_or(skip, at_zero), hit)
    vc, cgt_c, ceq_c = lax.cond(jnp.min(jnp.where(settled, 1.0, 0.0)) > 0.5,
                                lambda _: (hi, zero, zero), count_collapsed, None)
    v = jnp.where(skip, NEG_INF, jnp.where(at_zero, 0.0, jnp.where(hit, hi, vc)))
    cgt = jnp.where(at_zero, c_pos, jnp.where(hit, kf, cgt_c))
    ceq = jnp.where(at_zero, c_nn - c_pos, jnp.where(hit, 0.0, ceq_c))
    need = kf - cgt
    partial = jnp.logical_and(jnp.logical_not(skip), need < ceq)
    rows_of = lambda st: jnp.concatenate([to_rows(st, r0) for r0 in range(0, rows, rb)], axis=0)
    cut_all = rows_of(jnp.where(skip, -1.0, zero + span))
    blocks = range(0, rows, rb)
    tile_ts = lambda a: jnp.concatenate([a] * (ts // 128), axis=1)

    def cut_by_prefix(_):
        nchp = -(-nch_max // 8) * 8
        crow = lax.broadcasted_iota(I32, (nchp, rb), 0)
        tables = []
        for r0 in blocks:
            vt = tile_ts(to_rows(v, r0))

            def per_chunk(i, tab, r0=r0, vt=vt):
                for u in range(TIE_UNROLL):
                    c = i * TIE_UNROLL + u
                    ties = jnp.where(load(jnp.minimum(c, nch - 1), r0, rb) == vt, 1.0, 0.0)
                    cnt = row_sums(_lane_fold(ties, jnp.add))
                    here = jnp.logical_and(crow == c, c < nch)
                    tab = jnp.where(here, jnp.concatenate([cnt] * (nchp // 8), axis=0), tab)
                return tab

            steps = (nch + TIE_UNROLL - 1) // TIE_UNROLL
            tables.append(lax.fori_loop(0, steps, per_chunk, jnp.zeros((nchp, rb), F32)))
        tab = join(tables)
        ci = lax.broadcasted_iota(I32, (nchp, nchp), 0)
        cj = lax.broadcasted_iota(I32, (nchp, nchp), 1)
        upto = _dot_hi(jnp.where(ci >= cj, 1.0, 0.0), tab)
        before = upto < jnp.broadcast_to(need[0:1], tab.shape)
        cstar = jnp.sum(jnp.where(before, 1.0, 0.0), axis=0, keepdims=True)
        left = need[0:1] - jnp.sum(jnp.where(before, tab, 0.0), axis=0, keepdims=True)
        cstar, left = jnp.broadcast_to(cstar, state_shape), jnp.broadcast_to(left, state_shape)
        ki = lax.broadcasted_iota(I32, (ts, ts), 0)
        kj = lax.broadcasted_iota(I32, (ts, ts), 1)
        upper = jnp.where(ki <= kj, 1.0, 0.0).astype(BF16)
        cuts = []
        for r0 in blocks:
            vt, ct = tile_ts(to_rows(v, r0)), tile_ts(to_rows(cstar, r0))

            def crossing(c, m, r0=r0, vt=vt, ct=ct):
                here = jnp.logical_and(load(c, r0, rb) == vt, ct == c.astype(F32))
                return m + jnp.where(here, 1.0, 0.0)

            mask = lax.fori_loop(0, nch, crossing, jnp.zeros((rb, ts), F32))
            seen = jnp.dot(mask.astype(BF16), upper, preferred_element_type=F32)
            inside = jnp.sum(jnp.where(seen < tile_ts(to_rows(left, r0)), 1.0, 0.0), axis=1, keepdims=True)
            cuts.append(to_rows(cstar, r0) * float(ts) + jnp.broadcast_to(inside, (rb, 128)))
        return jnp.where(rows_of(jnp.where(partial, 1.0, 0.0)) > 0.5, jnp.concatenate(cuts, axis=0), cut_all)

    def cut_by_bisection(_):
        def c_cond(st):
            lo_i, hi_i = st
            return jnp.max(jnp.where(partial, hi_i - lo_i, 0.0)) > 0.0

        def c_body(st):
            lo_i, hi_i = st
            mid = jnp.floor((lo_i + hi_i) * 0.5)
            cnt = count(lambda x, kpos, t, m: jnp.logical_and(x == t, kpos.astype(F32) <= m), v, mid)
            ok = cnt >= need
            return jnp.where(ok, lo_i, mid + 1.0), jnp.where(ok, mid, hi_i)

        _, cut = lax.while_loop(c_cond, c_body, (zero, zero + (span - 1.0)))
        return jnp.where(rows_of(jnp.where(partial, 1.0, 0.0)) > 0.5, rows_of(cut), cut_all)

    any_partial = jnp.max(jnp.where(partial, 1.0, 0.0)) > 0.5
    cut = lax.cond(any_partial, cut_by_prefix if on_lanes else cut_by_bisection, lambda _: cut_all, None)
    return rows_of(v), cut.astype(I32)


def _prompt_attn_kernel(q_ref, qi_ref, wi_ref, kit_ref, kt_ref, v_ref, kn2_ref, o_ref,
                        sc_ref, m_ref, l_ref, acc_ref, *, tq, ts, topk):
    i = pl.program_id(0)
    nch = ((i + 1) * tq + ts - 1) // ts
    row = i * tq + lax.broadcasted_iota(I32, (tq, 1), 0)
    heads = q_ref.shape[1] // HEAD_DIM
    wsc = (wi_ref[...] * IDX_HEADS ** -0.5) * IDX_DIM ** -0.5
    qi_h = [qi_ref[:, h * IDX_DIM:(h + 1) * IDX_DIM] for h in range(IDX_HEADS)]
    w_h = [jnp.broadcast_to(wsc[:, h:h + 1], (tq, ts)) for h in range(IDX_HEADS)]

    def scores(c, carry):
        mn, mx = carry
        kic = kit_ref[c]
        acc = jnp.zeros((tq, ts), F32)
        for h in range(IDX_HEADS):
            lg = jnp.dot(qi_h[h], kic, preferred_element_type=F32)
            acc = acc + jnp.maximum(lg, 0.0) * w_h[h]
        kpos = c * ts + lax.broadcasted_iota(I32, (tq, ts), 1)
        adm = kpos <= row
        sc_ref[c] = jnp.where(adm, acc, NEG_INF)
        mn = jnp.minimum(mn, _lane_fold(jnp.where(adm, acc, jnp.inf), jnp.minimum))
        mx = jnp.maximum(mx, _lane_fold(jnp.where(adm, acc, NEG_INF), jnp.maximum))
        return mn, mx

    mn, mx = lax.fori_loop(0, nch, scores,
                           (jnp.full((tq, 128), jnp.inf, F32), jnp.full((tq, 128), NEG_INF, F32)))
    lo0 = jnp.min(mn, axis=1, keepdims=True)
    hi0 = jnp.max(mx, axis=1, keepdims=True)
    skip = row < topk
    v, cut = _select_threshold(lambda c, r0, nr: sc_ref[c, r0:r0 + nr, :], nch, sc_ref.shape[0], ts, tq, topk,
                               lo0, hi0, skip)

    scale = HEAD_DIM ** -0.5
    group = heads // KV_HEADS
    vw = jnp.concatenate([v] * (ts // 128), axis=1)
    cutw = jnp.concatenate([cut] * (ts // 128), axis=1)
    head = lambda h: slice(h * HEAD_DIM, (h + 1) * HEAD_DIM)

    def selected(c):
        x = sc_ref[c]
        kpos = c * ts + lax.broadcasted_iota(I32, (tq, ts), 1)
        return jnp.logical_or(x > vw, jnp.logical_and(x == vw, kpos <= cutw))

    c2 = scale * math.log2(math.e)
    kmax = jnp.sqrt(jnp.max(kn2_ref[...], axis=1, keepdims=True))
    bound = []
    for h in range(heads):
        qh = q_ref[:, head(h)].astype(F32)
        qn = jnp.sqrt(jnp.sum(qh * qh, axis=1, keepdims=True))
        bound.append(jnp.broadcast_to(qn * (kmax[h // group:h // group + 1, :] * c2), (tq, ts)))
    ones = jnp.ones((ts, HEAD_DIM), BF16)
    acc_ref[...] = jnp.zeros(acc_ref.shape, F32)

    def attend_bounded(c, _):
        keep = jnp.where(selected(c), 1.0, 0.0).astype(BF16)
        kc = kt_ref[c]
        vc = v_ref[c]
        for g in range(KV_HEADS):
            v_ones = jnp.concatenate([vc[:, head(g)], ones], axis=1)
            for h in range(g * group, (g + 1) * group):
                s = jnp.dot(q_ref[:, head(h)], kc[head(g), :], preferred_element_type=F32)
                p = jnp.exp2(s * c2 - bound[h]).astype(BF16) * keep
                acc_ref[h] += jnp.dot(p, v_ones, preferred_element_type=F32)
        return 0

    lax.fori_loop(0, nch, attend_bounded, 0)
    lmin = jnp.full((tq, HEAD_DIM), jnp.inf, F32)
    for h in range(heads):
        acc = acc_ref[h]
        lsum = acc[:, HEAD_DIM:]
        lmin = jnp.minimum(lmin, lsum)
        o_ref[:, head(h)] = (acc[:, :HEAD_DIM] / lsum).astype(o_ref.dtype)

    @pl.when(jnp.logical_not(jnp.min(lmin) > UNDERFLOW_GUARD))
    def _():
        m_ref[...] = jnp.full(m_ref.shape, SOFTMAX_FLOOR, F32)
        l_ref[...] = jnp.zeros(l_ref.shape, F32)
        acc_ref[...] = jnp.zeros(acc_ref.shape, F32)

        def attend_online(c, _):
            bias = jnp.where(selected(c), 0.0, NEG_INF)
            kc = kt_ref[c]
            vc = v_ref[c]
            for h in range(heads):
                g = h // group
                s = jnp.dot(q_ref[:, head(h)], kc[head(g), :], preferred_element_type=F32) * scale + bias
                m_old = m_ref[h]
                m_new = jnp.maximum(m_old, jnp.max(s, axis=1, keepdims=True))
                alpha = jnp.exp(m_old - m_new)
                p = jnp.exp(s - m_new)
                l_ref[h] = l_ref[h] * alpha + jnp.sum(p, axis=1, keepdims=True)
                acc_ref[h, :, :HEAD_DIM] = acc_ref[h, :, :HEAD_DIM] * alpha + jnp.dot(
                    p.astype(BF16), vc[:, head(g)], preferred_element_type=F32)
                m_ref[h] = m_new
            return 0

        lax.fori_loop(0, nch, attend_online, 0)
        for h in range(heads):
            o_ref[:, head(h)] = (acc_ref[h, :, :HEAD_DIM] / l_ref[h]).astype(o_ref.dtype)


SAMPLE_TS = 4 * PAGE_SIZE
SAMPLE_RING = 8


def _sample_attn_kernel(pt_ref, q_ref, qi_ref, wi_ref, kin_ref, kn_ref, vn_ref, cki_ref, ck_ref, cv_ref,
                        o_ref, kibuf, kbuf, vbuf, sc_ref, sem_i, sem_k, sem_v, *, n_pages, t_new, topk):
    b = pl.program_id(0)
    ts = SAMPLE_TS
    ppc = ts // PAGE_SIZE
    nch = n_pages // ppc
    past = n_pages * PAGE_SIZE
    rows = q_ref.shape[2]

    def ki_copy(p):
        dst = kibuf.at[:, pl.ds(pl.multiple_of(p * PAGE_SIZE, PAGE_SIZE), PAGE_SIZE)]
        return pltpu.make_async_copy(cki_ref.at[0, pt_ref[b, p]], dst, sem_i.at[0])

    def kv_copies(c, slot):
        cps = []
        for j in range(ppc):
            page = pt_ref[b, c * ppc + j]
            dst = pl.ds(j * PAGE_SIZE, PAGE_SIZE)
            for g in range(KV_HEADS):
                cps.append(pltpu.make_async_copy(ck_ref.at[0, page, :, g, :], kbuf.at[slot, g, dst], sem_k.at[slot]))
                cps.append(pltpu.make_async_copy(cv_ref.at[0, page, :, g, :], vbuf.at[slot, g, dst], sem_v.at[slot]))
        return cps

    def start_ki(p, _):
        ki_copy(p).start()
        return 0

    def wait_ki(p, _):
        ki_copy(p).wait()
        return 0

    lax.fori_loop(0, n_pages, start_ki, 0)
    ring = kbuf.shape[0]
    lanes = next(n for n in (4, 2, 1) if nch % n == 0)
    ahead = ring - lanes
    assert ahead >= 1
    for c0 in range(min(ahead, nch)):
        for cp in kv_copies(c0, c0):
            cp.start()
    lax.fori_loop(0, n_pages, wait_ki, 0)

    wcol = (wi_ref[0] * IDX_HEADS ** -0.5) * IDX_DIM ** -0.5
    qi = qi_ref[0]

    def index_scores(kct):
        width = kct.shape[1]
        lg = jnp.dot(qi, kct, preferred_element_type=F32)
        weighted = jnp.maximum(lg, 0.0) * jnp.broadcast_to(wcol, (IDX_HEADS * rows, width))
        acc = weighted[0:rows]
        for h in range(1, IDX_HEADS):
            acc = acc + weighted[h * rows:(h + 1) * rows]
        return acc

    def scores(i, carry):
        mn, mx = carry
        span = lanes * ts
        acc = index_scores(kibuf[:, pl.ds(pl.multiple_of(i * span, span), span)].astype(BF16))
        sc_ref[i] = acc
        return jnp.minimum(mn, _lane_fold(acc, jnp.minimum)), jnp.maximum(mx, _lane_fold(acc, jnp.maximum))

    mn, mx = lax.fori_loop(0, nch // lanes, scores,
                           (jnp.full((rows, 128), jnp.inf, F32), jnp.full((rows, 128), NEG_INF, F32)))
    acc = index_scores(kin_ref[0])
    tok = lax.broadcasted_iota(I32, (rows, ts), 0) % t_new
    col = lax.broadcasted_iota(I32, (rows, ts), 1)
    adm = jnp.logical_and(col <= tok, col < t_new)
    span = lanes * ts
    ngrp = nch // lanes + 1
    sc_ref[ngrp - 1] = jnp.concatenate(
        [jnp.where(adm, acc, NEG_INF), jnp.full((rows, span - ts), NEG_INF, F32)], axis=1) if lanes > 1 else jnp.where(
            adm, acc, NEG_INF)
    mn = jnp.minimum(mn, _lane_fold(jnp.where(adm, acc, jnp.inf), jnp.minimum))
    mx = jnp.maximum(mx, _lane_fold(jnp.where(adm, acc, NEG_INF), jnp.maximum))
    lo0 = jnp.min(mn, axis=1, keepdims=True)
    hi0 = jnp.max(mx, axis=1, keepdims=True)
    skip = jnp.full((rows, 1), past + 1 <= topk)
    v, cut = _select_threshold(lambda c, r0, nr: sc_ref[c], ngrp, ngrp, span, rows, topk, lo0, hi0, skip)
    vw = jnp.concatenate([v] * (span // 128), axis=1)
    cutw = jnp.concatenate([cut] * (span // 128), axis=1)
    scale = HEAD_DIM ** -0.5

    def bias_of(c):
        x = sc_ref[c]
        kpos = c * span + lax.broadcasted_iota(I32, (rows, span), 1)
        sel = jnp.logical_or(x > vw, jnp.logical_and(x == vw, kpos <= cutw))
        return jnp.where(sel, 0.0, NEG_INF)

    def flash(state, parts):
        m_old, l_old, a_old = state
        s = jnp.concatenate(
            [lax.dot_general(q_ref[0, g], kg, (((1,), (1,)), ((), ())), preferred_element_type=F32) * scale + bias
             for g, bias, kg, _ in parts], axis=0)
        m_new = jnp.maximum(m_old, jnp.max(s, axis=1, keepdims=True))
        alpha = jnp.exp(m_old - m_new)
        p = jnp.exp(s - m_new)
        l_new = l_old * alpha + jnp.sum(p, axis=1, keepdims=True)
        pb = p.astype(BF16)
        pv = jnp.concatenate(
            [jnp.dot(pb[n * rows:(n + 1) * rows], vg, preferred_element_type=F32) for n, (_, _, _, vg) in enumerate(parts)],
            axis=0)
        return m_new, l_new, a_old * alpha + pv

    def receive(c):
        for cp in kv_copies(c, lax.rem(c, ring)):
            cp.wait()

        @pl.when(c + ahead < nch)
        def _():
            for cp in kv_copies(c + ahead, lax.rem(c + ahead, ring)):
                cp.start()

    def attend(i, state):
        for u in range(lanes):
            receive(i * lanes + u)
        parts = []
        bias_grp = bias_of(i)
        for u in range(lanes):
            c = i * lanes + u
            slot = lax.rem(c, ring)
            bias = bias_grp[:, u * ts:(u + 1) * ts]
            parts += [(g, bias, kbuf[slot, g].astype(BF16), vbuf[slot, g].astype(BF16)) for g in range(KV_HEADS)]
        return flash(state, parts)

    pieces = lanes * KV_HEADS
    init = (jnp.full((pieces * rows, 1), SOFTMAX_FLOOR, F32), jnp.zeros((pieces * rows, 1), F32),
            jnp.zeros((pieces * rows, HEAD_DIM), F32))
    m_all, l_all, a_all = lax.fori_loop(0, nch // lanes, attend, init)

    def merge(a, b):
        m = jnp.maximum(a[0], b[0])
        fa, fb = jnp.exp(a[0] - m), jnp.exp(b[0] - m)
        return m, a[1] * fa + b[1] * fb, a[2] * fa + b[2] * fb

    per_lane = KV_HEADS * rows
    state = tuple(x[0:per_lane] for x in (m_all, l_all, a_all))
    for u in range(1, lanes):
        state = merge(state, tuple(x[u * per_lane:(u + 1) * per_lane] for x in (m_all, l_all, a_all)))
    bias = bias_of(ngrp - 1)[:, :ts]
    new_parts = [(g, bias, kn_ref[0][:, g * HEAD_DIM:(g + 1) * HEAD_DIM], vn_ref[0][:, g * HEAD_DIM:(g + 1) * HEAD_DIM])
                 for g in range(KV_HEADS)]
    _, l_fin, a_fin = flash(state, new_parts)
    out = a_fin / l_fin
    for g in range(KV_HEADS):
        o_ref[0, g] = out[g * rows:(g + 1) * rows].astype(o_ref.dtype)


def sample_attention(q, qi, wi, ki_new, k_new, v_new, cache_kidx, cache_k, cache_v, page_table, *, topk):
    nb, t_new, width = q.shape
    heads = width // HEAD_DIM
    group = heads // KV_HEADS
    n_pages = page_table.shape[1]
    ts = SAMPLE_TS
    assert n_pages % (ts // PAGE_SIZE) == 0
    nch = n_pages // (ts // PAGE_SIZE)
    lanes = next(n for n in (4, 2, 1) if nch % n == 0)
    rows = group * t_new
    qg = q.reshape(nb, t_new, KV_HEADS, group, HEAD_DIM).transpose(0, 2, 3, 1, 4).reshape(nb, KV_HEADS, rows, HEAD_DIM)
    qir = jnp.tile(qi.reshape(nb, t_new, IDX_HEADS, IDX_DIM).transpose(0, 2, 1, 3), (1, 1, group, 1))
    qir = qir.reshape(nb, IDX_HEADS * rows, IDX_DIM)
    wir = jnp.tile(wi.transpose(0, 2, 1), (1, 1, group)).reshape(nb, IDX_HEADS * rows, 1)
    padk = lambda a: jnp.pad(a, ((0, 0), (0, ts - t_new), (0, 0)))
    per_b = lambda a: pl.BlockSpec((1,) + a.shape[1:], lambda b, pt: (b,) + (0,) * (a.ndim - 1))
    ops = (qg, qir, wir, padk(ki_new).transpose(0, 2, 1), padk(k_new), padk(v_new))
    cache_kidx_t = cache_kidx.transpose(0, 1, 3, 2)
    out = pl.pallas_call(
        functools.partial(_sample_attn_kernel, n_pages=n_pages, t_new=t_new, topk=topk),
        grid_spec=pltpu.PrefetchScalarGridSpec(
            num_scalar_prefetch=1,
            grid=(nb,),
            in_specs=[per_b(a) for a in ops] + [pl.BlockSpec(memory_space=pl.ANY)] * 3,
            out_specs=pl.BlockSpec((1, KV_HEADS, rows, HEAD_DIM), lambda b, pt: (b, 0, 0, 0)),
            scratch_shapes=[
                pltpu.VMEM((IDX_DIM, n_pages * PAGE_SIZE), F32),
                pltpu.VMEM((SAMPLE_RING, KV_HEADS, ts, HEAD_DIM), F32),
                pltpu.VMEM((SAMPLE_RING, KV_HEADS, ts, HEAD_DIM), F32),
                pltpu.VMEM((nch // lanes + 1, rows, lanes * ts), F32),
                pltpu.SemaphoreType.DMA((1,)),
                pltpu.SemaphoreType.DMA((SAMPLE_RING,)),
                pltpu.SemaphoreType.DMA((SAMPLE_RING,)),
            ]),
        out_shape=jax.ShapeDtypeStruct((nb, KV_HEADS, rows, HEAD_DIM), BF16),
        compiler_params=_cparams("arbitrary"),
        name="sample_attention",
    )(page_table, *ops, cache_kidx_t, cache_k, cache_v)
    return out.reshape(nb, KV_HEADS, group, t_new, HEAD_DIM).transpose(0, 3, 1, 2, 4).reshape(nb, t_new, width)


def prompt_attention(qb, qib, wi, kib, kb, vb, kn2, *, tq, ts, topk):
    s, width = qb.shape
    heads = width // HEAD_DIM
    assert s % tq == 0 and s % ts == 0
    nc = s // ts
    kit = kib.reshape(nc, ts, IDX_DIM).transpose(0, 2, 1)
    kt = kb.reshape(nc, ts, KV_HEADS * HEAD_DIM).transpose(0, 2, 1)
    v3 = vb.reshape(nc, ts, KV_HEADS * HEAD_DIM)
    whole = lambda a: pl.BlockSpec(a.shape, lambda i: (0,) * a.ndim, pipeline_mode=pl.Buffered(1))
    return pl.pallas_call(
        functools.partial(_prompt_attn_kernel, tq=tq, ts=ts, topk=topk),
        grid=(s // tq,),
        in_specs=[
            pl.BlockSpec((tq, width), lambda i: (i, 0)),
            pl.BlockSpec((tq, IDX_HEADS * IDX_DIM), lambda i: (i, 0)),
            pl.BlockSpec((tq, IDX_HEADS), lambda i: (i, 0)),
            whole(kit), whole(kt), whole(v3), whole(kn2),
        ],
        out_specs=pl.BlockSpec((tq, width), lambda i: (i, 0)),
        out_shape=jax.ShapeDtypeStruct((s, width), BF16),
        scratch_shapes=[
            pltpu.VMEM((nc, tq, ts), F32),
            pltpu.VMEM((heads, tq, 1), F32),
            pltpu.VMEM((heads, tq, 1), F32),
            pltpu.VMEM((heads, tq, 2 * HEAD_DIM), F32),
        ],
        compiler_params=_cparams("arbitrary"),
        name="prompt_attention",
    )(qb, qib, wi, kit, kt, v3, kn2)


def _row_tile(m, cap):
    t = cap
    while m % t:
        t //= 2
    return t


def _token_stages(x, attend, gdn, lw):
    (nmw, wp, w_out, nfw, wq, sub_keys, w_u, w_v, fw, tabs) = lw
    m = x.shape[0]
    tm = _row_tile(m, TOKEN_TILE)
    p = norm_matmul(x, nmw, wp, tm=tm, tn=wp.shape[1] // 2)
    qb, qib, k32, kb, v32, vb, ki32, kib, auxr, kn2 = rope_split(p, *tabs, tm=tm)
    wi, a_pre = auxr[:, 0:IDX_HEADS], auxr[:, 2 * IDX_HEADS:3 * IDX_HEADS]
    ya = attend(qb, qib, wi, kib, kb, vb, kn2)
    yg, new_conv, new_ssm = gdn(p, a_pre)
    h, xn, qp = out_proj(ya, yg, x, w_out, nfw, wq, tm=tm)
    i1, i2, gate = peer_topk(qp, sub_keys, tm=_row_tile(m, PEER_TOPK_TILE))
    tok = lambda a: a.reshape(PEER_HEADS * PEER_TOPK, m).T
    gates = peer_gates(tok(i1), tok(i2), tok(gate), tmb=_row_tile(m, GATE_TILE))
    f = peer_dense(xn, w_u, w_v, gates, tm=_row_tile(m, PEER_TOKEN_TILE), ib=PEER_KEY_BLOCK)
    y = residual_norm(h, f, fw, tm=tm)
    return y, k32, v32, ki32, new_conv, new_ssm


def kernel(x_prompt, x_sample, cache_k, cache_v, cache_kidx, page_table, state_conv, state_ssm, norm_mix_w, w_in,
           conv_w, a_log, dt_bias, gdn_norm_w, w_out, norm_ffn_w, peer_wq, peer_sub_keys, peer_u, peer_v,
           norm_final_w):
    depth = w_in.shape[0]
    b, s, d = x_prompt.shape
    nb, t, _ = x_sample.shape
    assert depth == 1 and b == 1, "single layer, single prompt sequence"
    past = page_table.shape[1] * PAGE_SIZE
    heads_g = state_ssm.shape[2]
    conv_ch = state_conv.shape[-1]

    wp = _pack_w_in(w_in[0])
    shared = (norm_mix_w[0][None], wp, w_out[0].astype(BF16), norm_ffn_w[0][None], peer_wq[0].astype(BF16),
              peer_sub_keys[0], peer_u[0].astype(BF16), peer_v[0].astype(BF16), norm_final_w[None])
    gdn_w = (conv_w[0], a_log[0], dt_bias[0], gdn_norm_w[0])

    def attend_p(qb, qib, wi, kib, kb, vb, kn2):
        return prompt_attention(qb, qib, wi, kib, kb, vb, kn2.T, tq=_row_tile(s, ATTN_Q_TILE), ts=_row_tile(s, ATTN_KEY_CHUNK),
                                topk=min(TOPK_MAX, s // 4))

    def gdn_p(p, a_pre):
        conv0 = jnp.zeros((1, CONV_W - 1, conv_ch), F32)
        ssm0 = jnp.zeros((1, heads_g, GDN_DK, GDN_DV), F32)
        block = 2 * GDN_CHUNK if s % (2 * GDN_CHUNK) == 0 else GDN_CHUNK
        assert s % block == 0
        return gdn_mixer(p, a_pre.T[None], conv0, *gdn_w, ssm0, n_seq=1, t_pad=s, t_total=s, chunk=block,
                         sub=GDN_CHUNK)

    tabs_p = _rope_tables(jnp.arange(s))
    y_p, k_p, v_p, ki_p, conv_p, ssm_p = _token_stages(x_prompt.reshape(s, d), attend_p, gdn_p, shared + (tabs_p,))

    m_s = nb * t
    t8 = -(-t // 8) * 8

    def attend_s(qb, qib, wi, kib, kb, vb, kn2):
        seq = lambda a: a.reshape(nb, t, a.shape[-1])
        y = sample_attention(seq(qb), seq(qib), seq(wi), seq(kib), seq(kb), seq(vb), cache_kidx, cache_k, cache_v,
                             page_table, topk=min(TOPK_MAX, (past + t) // 4))
        return y.reshape(m_s, y.shape[-1])

    def gdn_s(p, a_pre):
        pad_t = lambda a: jnp.pad(a.reshape(nb, t, a.shape[-1]), ((0, 0), (0, t8 - t), (0, 0)))
        p8 = pad_t(p).reshape(nb * t8, p.shape[-1])
        at = pad_t(a_pre).transpose(0, 2, 1)
        yg, nconv, nssm = gdn_mixer(p8, at, state_conv[0], *gdn_w, state_ssm[0], n_seq=nb, t_pad=t8, t_total=t,
                                    chunk=t8, sub=8)
        return yg.reshape(nb, t8, yg.shape[-1])[:, :t].reshape(m_s, yg.shape[-1]), nconv, nssm

    tabs_s = tuple(jnp.tile(a, (nb, 1)) for a in _rope_tables(past + jnp.arange(t)))
    y_s, k_s, v_s, ki_s, conv_s, ssm_s = _token_stages(x_sample.reshape(m_s, d), attend_s, gdn_s, shared + (tabs_s,))

    kv = lambda a, n, tt: a.reshape(1, n, tt, KV_HEADS, HEAD_DIM)
    return (y_p.reshape(b, s, d), y_s.reshape(nb, t, d),
            kv(k_p, b, s), kv(v_p, b, s), ki_p.reshape(1, b, s, IDX_DIM), conv_p[None], ssm_p[None],
            kv(k_s, nb, t), kv(v_s, nb, t), ki_s.reshape(1, nb, t, IDX_DIM), conv_s[None], ssm_s[None])
```

```python
import functools
import math

import jax
import jax.numpy as jnp
from jax import lax
from jax.experimental import pallas as pl
from jax.experimental.pallas import tpu as pltpu

F32 = jnp.float32
BF16 = jnp.bfloat16
I32 = jnp.int32
EPS = 1e-6
NEG_INF = float("-inf")

HEAD_DIM = 128
KV_HEADS = 2
IDX_HEADS = 8
IDX_DIM = 64
TOPK_MAX = 256
ROPE_THETA = 10000.0
PAGE_SIZE = 128
GDN_DK = 128
GDN_DV = 128
CONV_W = 4
GDN_CHUNK = 64
PEER_HEADS = 8
PEER_KEYS = 128
PEER_QDIM = 128
PEER_TOPK = 16

VMEM_LIMIT_BYTES = 56 * 1024 * 1024

TOKEN_TILE = 512
PEER_TOKEN_TILE = 1024
PEER_KEY_BLOCK = 8
PEER_TOPK_TILE = 256
GATE_TILE = 256
ATTN_Q_TILE = 256
ATTN_KEY_CHUNK = 512
MAX_SEARCH_STEPS = 4096
UNTESTED_SEARCH_STEPS = 12
SOFTMAX_FLOOR = -1e30
UNDERFLOW_GUARD = 1e-30


def _cparams(*sem):
    return pltpu.CompilerParams(dimension_semantics=sem, vmem_limit_bytes=VMEM_LIMIT_BYTES)


_PROJ_SRC = (("q", 1024), ("k", 256), ("v", 256), ("qi", 512), ("ki", 64), ("wi", 8), ("qkv", 3072), ("z", 1024),
             ("b", 8), ("a", 8))
_PROJ_DST = ("q", "qi", "k", "v", "qkv", "z", "ki", "wi", "b", "a")
PROJ_PACKED = 6400


def _pack_w_in_kernel(w_ref, o_ref):
    src, off = {}, 0
    for name, width in _PROJ_SRC:
        src[name] = (off, width)
        off += width
    dst = 0
    for name in _PROJ_DST:
        s0, width = src[name]
        o_ref[:, dst:dst + width] = w_ref[:, s0:s0 + width].astype(o_ref.dtype)
        dst += width
    o_ref[:, dst:] = jnp.zeros((o_ref.shape[0], o_ref.shape[1] - dst), o_ref.dtype)


def _pack_w_in(w_in):
    d, n = w_in.shape
    assert n == sum(w for _, w in _PROJ_SRC)
    tr = _row_tile(d, 256)
    return pl.pallas_call(
        _pack_w_in_kernel,
        grid=(d // tr,),
        in_specs=[pl.BlockSpec((tr, n), lambda i: (i, 0))],
        out_specs=pl.BlockSpec((tr, PROJ_PACKED), lambda i: (i, 0)),
        out_shape=jax.ShapeDtypeStruct((d, PROJ_PACKED), BF16),
        compiler_params=_cparams("arbitrary"),
        name="pack_w_in",
    )(w_in)


def _rope_tables(pos):
    def table(dim):
        half = dim // 2
        inv = ROPE_THETA ** (-jnp.arange(half, dtype=F32) / half)
        ang = pos.astype(F32)[:, None] * inv[None, :]
        cos = jnp.tile(jnp.cos(ang), (1, 128 // half))
        sin = jnp.tile(jnp.concatenate([-jnp.sin(ang), jnp.sin(ang)], axis=1), (1, 128 // dim))
        return cos, sin
    cq, sq = table(HEAD_DIM)
    ci, si = table(IDX_DIM)
    return cq, sq, ci, si


def _norm_matmul_kernel(x_ref, nw_ref, w_ref, o_ref):
    x = x_ref[...]
    y = x * lax.rsqrt(jnp.mean(x * x, axis=-1, keepdims=True) + EPS)
    xn = (y * nw_ref[...]).astype(BF16)
    o_ref[...] = jnp.dot(xn, w_ref[...], preferred_element_type=F32)


def norm_matmul(x, nw, w, *, tm, tn):
    m, d = x.shape
    n = w.shape[1]
    assert m % tm == 0 and n % tn == 0
    return pl.pallas_call(
        _norm_matmul_kernel,
        grid=(n // tn, m // tm),
        in_specs=[
            pl.BlockSpec((tm, d), lambda j, i: (i, 0)),
            pl.BlockSpec((1, d), lambda j, i: (0, 0)),
            pl.BlockSpec((d, tn), lambda j, i: (0, j)),
        ],
        out_specs=pl.BlockSpec((tm, tn), lambda j, i: (i, j)),
        out_shape=jax.ShapeDtypeStruct((m, n), F32),
        compiler_params=_cparams("arbitrary", "arbitrary"),
        name="norm_matmul",
    )(x, nw, w)


def _rope128(x, cos, sin_signed):
    return x * cos + pltpu.roll(x, 64, 1) * sin_signed


def _rope64(x, cos, sin_signed, first_half):
    partner = jnp.where(first_half, pltpu.roll(x, 96, 1), pltpu.roll(x, 32, 1))
    return x * cos + partner * sin_signed


def _rope_split_kernel(q_ref, qi_ref, k_ref, v_ref, aux_ref, cq_ref, sq_ref, ci_ref, si_ref,
                       qb_ref, qib_ref, k32_ref, kb_ref, v32_ref, vb_ref, ki32_ref, kib_ref, auxr_ref, kn2_ref):
    cq, sq, ci, si = cq_ref[...], sq_ref[...], ci_ref[...], si_ref[...]
    lane = lax.broadcasted_iota(I32, ci.shape, 1)
    first_half = (lane % IDX_DIM) < (IDX_DIM // 2)
    for h in range(q_ref.shape[1] // HEAD_DIM):
        sl = slice(h * HEAD_DIM, (h + 1) * HEAD_DIM)
        qb_ref[:, sl] = _rope128(q_ref[:, sl], cq, sq).astype(BF16)
    for h in range(k_ref.shape[1] // HEAD_DIM):
        sl = slice(h * HEAD_DIM, (h + 1) * HEAD_DIM)
        kr = _rope128(k_ref[:, sl], cq, sq)
        k32_ref[:, sl] = kr
        krb = kr.astype(BF16)
        kb_ref[:, sl] = krb
        kn2_ref[:, h:h + 1] = jnp.sum(krb.astype(F32) * krb.astype(F32), axis=1, keepdims=True)
    for h in range(qi_ref.shape[1] // 128):
        sl = slice(h * 128, (h + 1) * 128)
        qib_ref[:, sl] = _rope64(qi_ref[:, sl], ci, si, first_half).astype(BF16)
    v = v_ref[...]
    v32_ref[...] = v
    vb_ref[...] = v.astype(BF16)
    aux = aux_ref[...]
    kir = _rope64(aux, ci, si, first_half)[:, :IDX_DIM]
    ki32_ref[...] = kir
    kib_ref[...] = kir.astype(BF16)
    auxr_ref[...] = aux[:, IDX_DIM:]


def rope_split(p, cq, sq, ci, si, *, tm):
    m = p.shape[0]
    assert m % tm == 0
    row = lambda w, j: pl.BlockSpec((tm, w), lambda i, j=j: (i, j))
    outs = [
        ((m, 1024), BF16), ((m, 512), BF16), ((m, 256), F32), ((m, 256), BF16), ((m, 256), F32),
        ((m, 256), BF16), ((m, IDX_DIM), F32), ((m, IDX_DIM), BF16), ((m, 128 - IDX_DIM), F32),
        ((m, KV_HEADS), F32),
    ]
    return pl.pallas_call(
        _rope_split_kernel,
        grid=(m // tm,),
        in_specs=[row(1024, 0), row(512, 2), row(256, 6), row(256, 7), row(128, 48),
                  row(128, 0), row(128, 0), row(128, 0), row(128, 0)],
        out_specs=[pl.BlockSpec((tm, s[1]), lambda i: (i, 0)) for s, _ in outs],
        out_shape=[jax.ShapeDtypeStruct(s, dt) for s, dt in outs],
        compiler_params=_cparams("arbitrary"),
        name="rope_split",
    )(p, p, p, p, p, cq, sq, ci, si)


def _out_proj_kernel(ya_ref, yg_ref, res_ref, wa_ref, wg_ref, nw_ref, wq_ref, h_ref, xn_ref, qp_ref):
    h = res_ref[...] + jnp.dot(ya_ref[...], wa_ref[...], preferred_element_type=F32)
    h = h + jnp.dot(yg_ref[...], wg_ref[...], preferred_element_type=F32)
    h_ref[...] = h
    y = h * lax.rsqrt(jnp.mean(h * h, axis=-1, keepdims=True) + EPS)
    xn = (y * nw_ref[...]).astype(BF16)
    xn_ref[...] = xn
    qp_ref[...] = jnp.dot(xn, wq_ref[...], preferred_element_type=F32)


def out_proj(ya, yg, res, w_out, nw, wq, *, tm):
    m, d = res.shape
    half = ya.shape[1]
    nq = wq.shape[1]
    assert m % tm == 0
    rows = lambda wd: pl.BlockSpec((tm, wd), lambda i: (i, 0))
    const = lambda a: pl.BlockSpec(a.shape, lambda i: (0,) * a.ndim, pipeline_mode=pl.Buffered(1))
    wa, wg = w_out[:half], w_out[half:]
    return pl.pallas_call(
        _out_proj_kernel,
        grid=(m // tm,),
        in_specs=[rows(half), rows(half), rows(d), const(wa), const(wg), const(nw), const(wq)],
        out_specs=[rows(d), rows(d), rows(nq)],
        out_shape=[jax.ShapeDtypeStruct((m, d), F32), jax.ShapeDtypeStruct((m, d), BF16),
                   jax.ShapeDtypeStruct((m, nq), F32)],
        compiler_params=_cparams("arbitrary"),
        name="out_proj",
    )(ya, yg, res, wa, wg, nw, wq)


def _take_top(s, codes, count):
    big = jnp.int32(2 ** 30)
    vals, picks = [], []
    for _ in range(count):
        m = jnp.max(s, axis=0, keepdims=True)
        pick = jnp.min(jnp.where(s == m, codes, big), axis=0, keepdims=True)
        s = jnp.where(codes == pick, NEG_INF, s)
        vals.append(m)
        picks.append(pick)
    return jnp.concatenate(vals, axis=0), jnp.concatenate(picks, axis=0)


def _lookup(table, sel, count):
    out = jnp.zeros(sel.shape, table.dtype)
    for a in range(count):
        out = jnp.where(sel == a, jnp.broadcast_to(table[a:a + 1, :], sel.shape), out)
    return out


def _peer_topk_kernel(q_ref, sk_ref, i1_ref, i2_ref, g_ref):
    def one_head(h, _):
        q = q_ref[:, pl.ds(pl.multiple_of(h * PEER_QDIM, PEER_QDIM), PEER_QDIM)]
        i1_ref[h], i2_ref[h], g_ref[h] = _peer_topk_tile(q, lambda c: sk_ref[2 * h + c])
        return 0

    lax.fori_loop(0, i1_ref.shape[0], one_head, 0)


def _peer_topk_tile(q, sub_keys):
    tm = q.shape[0]
    kk = PEER_TOPK
    half = PEER_QDIM // 2
    key_codes = lax.broadcasted_iota(I32, (PEER_KEYS, tm), 0)
    top_v, top_i = [], []
    for c in range(2):
        qs = q[:, c * half:(c + 1) * half].astype(BF16)
        s = lax.dot_general(sub_keys(c), qs, (((1,), (1,)), ((), ())), preferred_element_type=F32)
        vals, idx = _take_top(s, key_codes, kk)
        top_v.append(vals)
        top_i.append(idx)
    pieces, codes = [], []

    def add_piece(cand, a_of_row, b_of_row):
        ok = (a_of_row + 1) * (b_of_row + 1) <= kk
        pieces.append(jnp.where(ok, cand, NEG_INF))
        codes.append(a_of_row * kk + b_of_row)

    split = 4
    for a in range(split):
        rows = -(-(kk // (a + 1)) // 8) * 8
        r = lax.broadcasted_iota(I32, (rows, tm), 0)
        add_piece(top_v[0][a:a + 1, :] + top_v[1][0:rows, :], jnp.full((rows, tm), a, I32), r)
    r8 = lax.broadcasted_iota(I32, (8, tm), 0)
    for b in range(kk // (split + 1)):
        cand = top_v[0][0:8, :] + top_v[1][b:b + 1, :]
        add_piece(jnp.where(r8 >= split, cand, NEG_INF), r8, jnp.full((8, tm), b, I32))
    add_piece(top_v[0][8:kk, :] + top_v[1][0:1, :], r8 + 8, jnp.zeros((8, tm), I32))
    best_s, best_c = _take_top(jnp.concatenate(pieces, axis=0), jnp.concatenate(codes, axis=0), kk)
    i1 = _lookup(top_i[0], lax.shift_right_logical(best_c, 4), kk)
    i2 = _lookup(top_i[1], jnp.bitwise_and(best_c, kk - 1), kk)
    e = jnp.exp(best_s - best_s[0:1, :])
    return i1, i2, e / jnp.sum(e, axis=0, keepdims=True)


def peer_topk(qp, sub_keys, *, tm):
    m = qp.shape[0]
    heads = sub_keys.shape[0]
    assert m % tm == 0 and PEER_TOPK == 16
    sk = sub_keys.reshape(heads * 2, PEER_KEYS, PEER_QDIM // 2).astype(BF16)
    out = pl.BlockSpec((heads, PEER_TOPK, tm), lambda i: (0, 0, i))
    return pl.pallas_call(
        _peer_topk_kernel,
        grid=(m // tm,),
        in_specs=[pl.BlockSpec((tm, heads * PEER_QDIM), lambda i: (i, 0)),
                  pl.BlockSpec(sk.shape, lambda i: (0, 0, 0))],
        out_specs=[out, out, out],
        out_shape=[jax.ShapeDtypeStruct((heads, PEER_TOPK, m), I32), jax.ShapeDtypeStruct((heads, PEER_TOPK, m), I32),
                   jax.ShapeDtypeStruct((heads, PEER_TOPK, m), F32)],
        compiler_params=_cparams("arbitrary"),
        name="peer_topk",
    )(qp, sk)


GATE_TOKENS = 32


def _peer_gate_kernel(i1_ref, i2_ref, g_ref, o_ref):
    tmb = i1_ref.shape[0]
    nk = PEER_KEYS
    sub = lax.broadcasted_iota(I32, (nk, i1_ref.shape[1]), 0)

    def token_block(j, _):
        t0 = pl.multiple_of(j * GATE_TOKENS, GATE_TOKENS)
        per_token = []
        for u in range(GATE_TOKENS):
            wide = lambda ref: jnp.broadcast_to(ref[pl.ds(t0 + u, 1), :], sub.shape)
            p1 = jnp.where(wide(i1_ref) == sub, wide(g_ref), 0.0).astype(BF16)
            p2 = jnp.where(wide(i2_ref) == sub, 1.0, 0.0).astype(BF16)
            gm = lax.dot_general(p1, p2, (((1,), (1,)), ((), ())), preferred_element_type=F32)
            per_token.append(gm.astype(o_ref.dtype))
        block = jnp.stack(per_token, axis=0)
        o_ref[:, pl.ds(t0, GATE_TOKENS), :] = pltpu.einshape("mab->amb", block)
        return 0

    lax.fori_loop(0, tmb // GATE_TOKENS, token_block, 0)


def peer_gates(i1, i2, gate, *, tmb):
    m, slots = i1.shape
    assert m % tmb == 0 and tmb % GATE_TOKENS == 0
    rows = pl.BlockSpec((tmb, slots), lambda i: (i, 0))
    return pl.pallas_call(
        _peer_gate_kernel,
        grid=(m // tmb,),
        in_specs=[rows, rows, rows],
        out_specs=pl.BlockSpec((PEER_KEYS, tmb, PEER_KEYS), lambda i: (0, i, 0)),
        out_shape=jax.ShapeDtypeStruct((PEER_KEYS, m, PEER_KEYS), BF16),
        compiler_params=_cparams("arbitrary"),
        name="peer_gates",
    )(i1, i2, gate)


def _gelu_tanh(x):
    return 0.5 * x * (1.0 + jnp.tanh(math.sqrt(2.0 / math.pi) * (x + 0.044715 * (x * x * x))))


def _peer_dense_kernel(xn_ref, wu_ref, wv_ref, g_ref, o_ref):
    j = pl.program_id(1)
    ib = g_ref.shape[0]
    a = lax.dot_general(xn_ref[...], wu_ref[...], (((1,), (1,)), ((), ())), preferred_element_type=F32)
    act = _gelu_tanh(a)
    hm = jnp.concatenate(
        [(g_ref[u].astype(F32) * act[:, u * PEER_KEYS:(u + 1) * PEER_KEYS]).astype(BF16) for u in range(ib)], axis=1)
    part = jnp.dot(hm, wv_ref[...], preferred_element_type=F32)

    @pl.when(j == 0)
    def _():
        o_ref[...] = part

    @pl.when(j > 0)
    def _():
        o_ref[...] += part


def peer_dense(xn, w_u, w_v, gates, *, tm, ib):
    m, d = xn.shape
    assert m % tm == 0 and PEER_KEYS % ib == 0
    eb = ib * PEER_KEYS
    return pl.pallas_call(
        _peer_dense_kernel,
        grid=(m // tm, PEER_KEYS // ib),
        in_specs=[
            pl.BlockSpec((tm, d), lambda i, j: (i, 0)),
            pl.BlockSpec((eb, d), lambda i, j: (j, 0)),
            pl.BlockSpec((eb, d), lambda i, j: (j, 0)),
            pl.BlockSpec((ib, tm, PEER_KEYS), lambda i, j: (j, i, 0)),
        ],
        out_specs=pl.BlockSpec((tm, d), lambda i, j: (i, 0)),
        out_shape=jax.ShapeDtypeStruct((m, d), F32),
        compiler_params=_cparams("arbitrary", "arbitrary"),
        name="peer_dense",
    )(xn, w_u, w_v, gates)


def _residual_norm_kernel(h_ref, f_ref, w_ref, o_ref):
    y = h_ref[...] + f_ref[...]
    o_ref[...] = y * lax.rsqrt(jnp.mean(y * y, axis=-1, keepdims=True) + EPS) * w_ref[...]


def residual_norm(h, f, w, *, tm):
    m, d = h.shape
    assert m % tm == 0
    rows = pl.BlockSpec((tm, d), lambda i: (i, 0))
    return pl.pallas_call(
        _residual_norm_kernel,
        grid=(m // tm,),
        in_specs=[rows, rows, pl.BlockSpec((1, d), lambda i: (0, 0))],
        out_specs=rows,
        out_shape=jax.ShapeDtypeStruct((m, d), F32),
        compiler_params=_cparams("arbitrary"),
        name="residual_norm",
    )(h, f, w)


_HI = lax.Precision.HIGHEST


def _dot_hi(a, b):
    return jnp.dot(a, b, preferred_element_type=F32, precision=_HI)


_BNN = (((2,), (1,)), ((0,), (0,)))
_BNT = (((2,), (2,)), ((0,), (0,)))
_BTN = (((1,), (1,)), ((0,), (0,)))


def _bdot(a, b, dims=_BNN):
    return lax.dot_general(a.astype(BF16), b.astype(BF16), dims, preferred_element_type=F32)


def _split2(a):
    hi = a.astype(BF16)
    return hi, (a - hi.astype(F32)).astype(BF16)


def _bdot3(a, b, dims=_BNN):
    (ah, al), (bh, bl) = a, b
    dot = lambda x, y: lax.dot_general(x, y, dims, preferred_element_type=F32)
    return dot(ah, bh) + (dot(ah, bl) + dot(al, bh))


def _sigmoid(x):
    return 1.0 / (1.0 + jnp.exp(-x))


def _softplus(x):
    return jnp.maximum(x, 0.0) + jnp.log1p(jnp.exp(-jnp.abs(x)))


def _gdn_kernel(xq_ref, xk_ref, xv_ref, z_ref, aux_ref, at_ref, cs_ref, cw_ref, alog_ref, dtb_ref,
                alogt_ref, dtbt_ref, nw_ref, s0_ref, y_ref, nconv_ref, nssm_ref, xp_ref, st_ref,
                *, chunk, sub, t_total, b_lane, a_lane):
    c = pl.program_id(1)
    nchunks = pl.num_programs(1)
    heads = st_ref.shape[0]
    width = heads * GDN_DK
    tail = CONV_W - 1

    @pl.when(c == 0)
    def _():
        xp_ref[8 - tail:8, :] = cs_ref[0]
        st_ref[...] = s0_ref[0]

    xp_ref[8:8 + chunk, 0:width] = xq_ref[...]
    xp_ref[8:8 + chunk, width:2 * width] = xk_ref[...]
    xp_ref[8:8 + chunk, 2 * width:3 * width] = xv_ref[...]
    conv = cw_ref[tail:tail + 1, :] * xp_ref[8:8 + chunk, :]
    for j in range(tail):
        conv = conv + cw_ref[j:j + 1, :] * xp_ref[8 - tail + j:8 - tail + j + chunk, :]
    conv = conv * _sigmoid(conv)

    padded = t_total % chunk != 0
    ridx = c * chunk + lax.broadcasted_iota(I32, (chunk, 1), 0)
    rvalid = ridx < t_total
    cidx = c * chunk + lax.broadcasted_iota(I32, (1, chunk), 1)
    cvalid = cidx < t_total

    aux = aux_ref[...]
    beta = _sigmoid(aux[:, b_lane:b_lane + heads])
    g = -jnp.exp(alog_ref[...]) * _softplus(aux[:, a_lane:a_lane + heads] + dtb_ref[...])
    gt = -jnp.exp(alogt_ref[...]) * _softplus(at_ref[0] + dtbt_ref[...])
    if padded:
        beta = jnp.where(rvalid, beta, 0.0)
        g = jnp.where(rvalid, g, 0.0)
        gt = jnp.where(cvalid, gt, 0.0)
    n_sub = chunk // sub
    rf = lax.broadcasted_iota(I32, (chunk, chunk), 0)
    cf = lax.broadcasted_iota(I32, (chunk, chunk), 1)
    same = (rf // sub) == (cf // sub)
    gc = _dot_hi(jnp.where(jnp.logical_and(same, rf >= cf), 1.0, 0.0), g)
    gct = _dot_hi(gt, jnp.where(jnp.logical_and(same, rf <= cf), 1.0, 0.0))
    ri = lax.broadcasted_iota(I32, (sub, sub), 0)
    ci = lax.broadcasted_iota(I32, (sub, sub), 1)
    causal = ri >= ci
    strict = ri > ci
    eye = jnp.where(ri == ci, 1.0, 0.0)

    per_head = lambda x2d, base: jnp.stack(
        [x2d[:, base + h * GDN_DK:base + (h + 1) * GDN_DK] for h in range(heads)], axis=0)
    q = per_head(conv, 0)
    k = per_head(conv, width)
    v = per_head(conv, 2 * width)
    q = q * lax.rsqrt(jnp.sum(q * q, axis=-1, keepdims=True) + EPS) * GDN_DK ** -0.5
    k = k * lax.rsqrt(jnp.sum(k * k, axis=-1, keepdims=True) + EPS)
    if padded:
        k = jnp.where(rvalid[None], k, 0.0)
        v = jnp.where(rvalid[None], v, 0.0)
    gcol = jnp.stack([gc[:, h:h + 1] for h in range(heads)], axis=0)
    grow = jnp.stack([gct[h:h + 1, :] for h in range(heads)], axis=0)
    bcol = jnp.stack([beta[:, h:h + 1] for h in range(heads)], axis=0)
    by_rows = lambda a: jnp.concatenate([a[:, j * sub:(j + 1) * sub] for j in range(n_sub)], axis=0)
    q, k, v, gcol, bcol = (by_rows(a) for a in (q, k, v, gcol, bcol))
    grow = jnp.concatenate([grow[:, :, j * sub:(j + 1) * sub] for j in range(n_sub)], axis=0)
    decay = jnp.where(causal[None], jnp.exp(jnp.where(causal[None], gcol - grow, 0.0)), 0.0)
    kb = k * bcol
    vb = v * bcol
    x = -jnp.where(strict[None], _bdot(kb, k, _BNT) * decay, 0.0)
    tinv = eye[None] + x
    xs = _split2(x)
    span = 2
    while span < sub:
        xs = _split2(_bdot3(xs, xs))
        tinv = tinv + _bdot3(_split2(tinv), xs)
        span *= 2
    u = _bdot(tinv, vb)
    w = _bdot(tinv, kb * jnp.exp(gcol))
    intra = jnp.where(causal[None], _bdot(q, k, _BNT) * decay, 0.0)
    qg = q * jnp.exp(gcol)
    glast = gcol[:, sub - 1:sub, :]
    kd = k * jnp.exp(glast - gcol)
    state = st_ref[...]
    outs = []
    for j in range(n_sub):
        sl = slice(j * heads, (j + 1) * heads)
        v_new = u[sl] - _bdot(w[sl], state)
        outs.append(_bdot(qg[sl], state) + _bdot(intra[sl], v_new))
        state = state * jnp.exp(glast[sl]) + _bdot(kd[sl], v_new, _BTN)
    st_ref[...] = state
    out = outs[0] if n_sub == 1 else jnp.concatenate(outs, axis=1)
    on = out * lax.rsqrt(jnp.mean(out * out, axis=-1, keepdims=True) + EPS) * nw_ref[...]
    for h in range(heads):
        sl = slice(h * GDN_DV, (h + 1) * GDN_DV)
        zh = z_ref[:, sl]
        y_ref[:, sl] = (on[h] * (zh * _sigmoid(zh))).astype(y_ref.dtype)

    last_valid = t_total - (t_total - 1) // chunk * chunk
    @pl.when(c < nchunks - 1)
    def _():
        xp_ref[8 - tail:8, :] = xp_ref[8 + chunk - tail:8 + chunk, :]

    @pl.when(c == nchunks - 1)
    def _():
        nconv_ref[0] = xp_ref[8 + last_valid - tail:8 + last_valid, :]
        nssm_ref[0] = st_ref[...]


def gdn_mixer(p, at, conv_state, conv_w, a_log, dt_bias, norm_w, ssm_state, *, n_seq, t_pad, t_total, chunk, sub):
    heads = ssm_state.shape[1]
    width = heads * GDN_DK
    assert t_pad % chunk == 0 and chunk % sub == 0 and sub % 8 == 0
    nch = t_pad // chunk
    rows = lambda wd, j: pl.BlockSpec((chunk, wd), lambda n, c, j=j: (n * nch + c, j))
    const = lambda a: pl.BlockSpec(a.shape, lambda n, c: (0,) * a.ndim)
    alog = a_log.reshape(1, heads)
    dtb = dt_bias.reshape(1, heads)
    nw = norm_w.reshape(1, GDN_DV)
    at = at.reshape(n_seq, heads, nch, chunk).transpose(0, 2, 1, 3).reshape(n_seq * nch, heads, chunk)
    kern = functools.partial(_gdn_kernel, chunk=chunk, sub=sub, t_total=t_total, b_lane=IDX_DIM + 8,
                             a_lane=IDX_DIM + 16)
    return pl.pallas_call(
        kern,
        grid=(n_seq, nch),
        in_specs=[
            rows(width, 2), rows(width, 3), rows(width, 4), rows(width, 5), rows(128, 48),
            pl.BlockSpec((1, heads, chunk), lambda n, c: (n * nch + c, 0, 0)),
            pl.BlockSpec((1, CONV_W - 1, 3 * width), lambda n, c: (n, 0, 0)),
            const(conv_w), const(alog), const(dtb), const(alog.T), const(dtb.T), const(nw),
            pl.BlockSpec((1, heads, GDN_DK, GDN_DV), lambda n, c: (n, 0, 0, 0)),
        ],
        out_specs=[
            pl.BlockSpec((chunk, width), lambda n, c: (n * nch + c, 0)),
            pl.BlockSpec((1, CONV_W - 1, 3 * width), lambda n, c: (n, 0, 0)),
            pl.BlockSpec((1, heads, GDN_DK, GDN_DV), lambda n, c: (n, 0, 0, 0)),
        ],
        out_shape=[
            jax.ShapeDtypeStruct((n_seq * t_pad, width), BF16),
            jax.ShapeDtypeStruct((n_seq, CONV_W - 1, 3 * width), F32),
            jax.ShapeDtypeStruct((n_seq, heads, GDN_DK, GDN_DV), F32),
        ],
        scratch_shapes=[pltpu.VMEM((8 + chunk, 3 * width), F32), pltpu.VMEM((heads, GDN_DK, GDN_DV), F32)],
        compiler_params=_cparams("arbitrary", "arbitrary"),
        name="gdn_mixer",
    )(p, p, p, p, p, at, conv_state, conv_w, alog, dtb, alog.T, dtb.T, nw, ssm_state)


def _lane_fold(x, op):
    out = x[:, :128]
    for j in range(1, x.shape[1] // 128):
        out = op(out, x[:, j * 128:(j + 1) * 128])
    return out


TIE_UNROLL = 4


def _select_threshold(load, nch, nch_max, ts, rows, topk, lo0, hi0, skip):
    kf = float(topk)
    rb = min(rows, 128)
    assert rows % rb == 0
    assert nch_max * (ts // 128) <= 256
    lane = lax.broadcasted_iota(I32, (rb, 128), 1)
    on_lanes = rb == 128
    if on_lanes:
        state_shape = (8, rows)
        ones = jnp.ones((8, 128), BF16)
        to_state = lambda blk: jnp.transpose(jnp.broadcast_to(blk, (128, 128)))[0:8]
        to_rows = lambda st, r0: jnp.transpose(jnp.broadcast_to(st[0:1, r0:r0 + 128], (128, 128)))
        row_sums = lambda acc: lax.dot_general(ones, acc.astype(BF16), (((1,), (1,)), ((), ())),
                                               preferred_element_type=F32)
        join_axis = 1
    else:
        state_shape = (rows, 128)
        ones = jnp.ones((128, 128), BF16)
        to_state = lambda blk: jnp.broadcast_to(blk, (rb, 128))
        to_rows = lambda st, r0: st[r0:r0 + rb]
        row_sums = lambda acc: jnp.dot(acc.astype(BF16), ones, preferred_element_type=F32)
        join_axis = 0
    join = lambda parts: parts[0] if len(parts) == 1 else jnp.concatenate(parts, axis=join_axis)
    rep = lambda col: join([to_state(col[r0:r0 + rb].astype(F32)) for r0 in range(0, rows, rb)])
    lo0, hi0, skip = rep(lo0), rep(hi0), rep(skip) > 0.5

    def count(pred, *cols):
        blocks = range(0, rows, rb)
        wides = [[to_rows(col, r0) for col in cols] for r0 in blocks]
        accs = []
        for r0, wide in zip(blocks, wides):
            def body(c, acc, r0=r0, wide=wide):
                x = load(c, r0, rb)
                for j in range(ts // 128):
                    kpos = lane + (c * ts + j * 128)
                    acc = acc + jnp.where(pred(x[:, j * 128:(j + 1) * 128], kpos, *wide), 1.0, 0.0)
                return acc

            accs.append(lax.fori_loop(0, nch, body, jnp.zeros((rb, 128), F32)))
        return join([row_sums(acc) for acc in accs])

    ge_ = lambda x, kpos, t: x >= t
    gt_ = lambda x, kpos, t: x > t
    eq_ = lambda x, kpos, t: x == t

    zero = jnp.zeros(state_shape, F32)
    span = (nch * ts).astype(F32) if hasattr(nch, "astype") else float(nch * ts)

    def search(state, limit):
        def cond(st):
            _, _, done, _, it = st
            return jnp.logical_and(jnp.min(done) < 0.5, it < limit)

        def body(st):
            lo, hi, done, hit_any, it = st
            mid = jnp.minimum(jnp.maximum(lo * 0.5 + hi * 0.5, lo), hi)
            cnt = count(ge_, mid)
            active = done < 0.5
            ge = jnp.logical_and(active, cnt >= kf)
            hit = jnp.logical_and(active, cnt == kf)
            collapsed = jnp.logical_or(mid <= lo, mid >= hi)
            new_lo = jnp.where(ge, mid, lo)
            new_hi = jnp.where(jnp.logical_and(active, jnp.logical_not(jnp.logical_or(ge, collapsed))), mid, hi)
            new_hi = jnp.where(hit, mid, new_hi)
            new_done = jnp.where(jnp.logical_or(hit, jnp.logical_and(active, collapsed)), 1.0, done)
            return new_lo, new_hi, new_done, jnp.where(hit, 1.0, hit_any), it + 1

        st = lax.fori_loop(0, UNTESTED_SEARCH_STEPS, lambda _, st: body(st), state + (jnp.int32(0),))
        lo, hi, done, hit_any, _ = lax.while_loop(cond, body, st)
        return lo, hi, done, hit_any

    c_pos = count(gt_, zero)
    c_nn = count(ge_, zero)
    at_zero = jnp.logical_and(jnp.logical_not(skip), jnp.logical_and(c_pos < kf, c_nn >= kf))
    lo0 = jnp.where(at_zero, 0.0, jnp.where(c_pos >= kf, jnp.maximum(lo0, 0.0), lo0))
    hi0 = jnp.where(at_zero, 0.0, jnp.where(c_nn < kf, jnp.minimum(hi0, 0.0), hi0))
    lo, hi, _, hit_any = search((lo0, hi0, jnp.where(jnp.logical_or(skip, at_zero), 1.0, 0.0), zero), MAX_SEARCH_STEPS)
    hit = hit_any > 0.5

    def count_collapsed(_):
        cnt_hi = count(ge_, hi)
        vc = jnp.where(cnt_hi >= kf, hi, lo)
        return vc, count(gt_, vc), count(eq_, vc)

    settled = jnp.logical_or(jnp.logical_or(skip, at_zero), hit)
    vc, cgt_c, ceq_c = lax.cond(jnp.min(jnp.where(settled, 1.0, 0.0)) > 0.5,
                                lambda _: (hi, zero, zero), count_collapsed, None)
    v = jnp.where(skip, NEG_INF, jnp.where(at_zero, 0.0, jnp.where(hit, hi, vc)))
    cgt = jnp.where(at_zero, c_pos, jnp.where(hit, kf, cgt_c))
    ceq = jnp.where(at_zero, c_nn - c_pos, jnp.where(hit, 0.0, ceq_c))
    need = kf - cgt
    partial = jnp.logical_and(jnp.logical_not(skip), need < ceq)
    rows_of = lambda st: jnp.concatenate([to_rows(st, r0) for r0 in range(0, rows, rb)], axis=0)
    cut_all = rows_of(jnp.where(skip, -1.0, zero + span))
    blocks = range(0, rows, rb)
    tile_ts = lambda a: jnp.concatenate([a] * (ts // 128), axis=1)

    def cut_by_prefix(_):
        nchp = -(-nch_max // 8) * 8
        crow = lax.broadcasted_iota(I32, (nchp, rb), 0)
        tables = []
        for r0 in blocks:
            vt = tile_ts(to_rows(v, r0))

            def per_chunk(i, tab, r0=r0, vt=vt):
                for u in range(TIE_UNROLL):
                    c = i * TIE_UNROLL + u
                    ties = jnp.where(load(jnp.minimum(c, nch - 1), r0, rb) == vt, 1.0, 0.0)
                    cnt = row_sums(_lane_fold(ties, jnp.add))
                    here = jnp.logical_and(crow == c, c < nch)
                    tab = jnp.where(here, jnp.concatenate([cnt] * (nchp // 8), axis=0), tab)
                return tab

            steps = (nch + TIE_UNROLL - 1) // TIE_UNROLL
            tables.append(lax.fori_loop(0, steps, per_chunk, jnp.zeros((nchp, rb), F32)))
        tab = join(tables)
        ci = lax.broadcasted_iota(I32, (nchp, nchp), 0)
        cj = lax.broadcasted_iota(I32, (nchp, nchp), 1)
        upto = _dot_hi(jnp.where(ci >= cj, 1.0, 0.0), tab)
        before = upto < jnp.broadcast_to(need[0:1], tab.shape)
        cstar = jnp.sum(jnp.where(before, 1.0, 0.0), axis=0, keepdims=True)
        left = need[0:1] - jnp.sum(jnp.where(before, tab, 0.0), axis=0, keepdims=True)
        cstar, left = jnp.broadcast_to(cstar, state_shape), jnp.broadcast_to(left, state_shape)
        ki = lax.broadcasted_iota(I32, (ts, ts), 0)
        kj = lax.broadcasted_iota(I32, (ts, ts), 1)
        upper = jnp.where(ki <= kj, 1.0, 0.0).astype(BF16)
        cuts = []
        for r0 in blocks:
            vt, ct = tile_ts(to_rows(v, r0)), tile_ts(to_rows(cstar, r0))

            def crossing(c, m, r0=r0, vt=vt, ct=ct):
                here = jnp.logical_and(load(c, r0, rb) == vt, ct == c.astype(F32))
                return m + jnp.where(here, 1.0, 0.0)

            mask = lax.fori_loop(0, nch, crossing, jnp.zeros((rb, ts), F32))
            seen = jnp.dot(mask.astype(BF16), upper, preferred_element_type=F32)
            inside = jnp.sum(jnp.where(seen < tile_ts(to_rows(left, r0)), 1.0, 0.0), axis=1, keepdims=True)
            cuts.append(to_rows(cstar, r0) * float(ts) + jnp.broadcast_to(inside, (rb, 128)))
        return jnp.where(rows_of(jnp.where(partial, 1.0, 0.0)) > 0.5, jnp.concatenate(cuts, axis=0), cut_all)

    def cut_by_bisection(_):
        def c_cond(st):
            lo_i, hi_i = st
            return jnp.max(jnp.where(partial, hi_i - lo_i, 0.0)) > 0.0

        def c_body(st):
            lo_i, hi_i = st
            mid = jnp.floor((lo_i + hi_i) * 0.5)
            cnt = count(lambda x, kpos, t, m: jnp.logical_and(x == t, kpos.astype(F32) <= m), v, mid)
            ok = cnt >= need
            return jnp.where(ok, lo_i, mid + 1.0), jnp.where(ok, mid, hi_i)

        _, cut = lax.while_loop(c_cond, c_body, (zero, zero + (span - 1.0)))
        return jnp.where(rows_of(jnp.where(partial, 1.0, 0.0)) > 0.5, rows_of(cut), cut_all)

    any_partial = jnp.max(jnp.where(partial, 1.0, 0.0)) > 0.5
    cut = lax.cond(any_partial, cut_by_prefix if on_lanes else cut_by_bisection, lambda _: cut_all, None)
    return rows_of(v), cut.astype(I32)


def _prompt_attn_kernel(q_ref, qi_ref, wi_ref, kit_ref, kt_ref, v_ref, kn2_ref, o_ref,
                        sc_ref, m_ref, l_ref, acc_ref, *, tq, ts, topk):
    i = pl.program_id(0)
    nch = ((i + 1) * tq + ts - 1) // ts
    row = i * tq + lax.broadcasted_iota(I32, (tq, 1), 0)
    heads = q_ref.shape[1] // HEAD_DIM
    wsc = (wi_ref[...] * IDX_HEADS ** -0.5) * IDX_DIM ** -0.5
    qi_h = [qi_ref[:, h * IDX_DIM:(h + 1) * IDX_DIM] for h in range(IDX_HEADS)]
    w_h = [jnp.broadcast_to(wsc[:, h:h + 1], (tq, ts)) for h in range(IDX_HEADS)]

    def scores(c, carry):
        mn, mx = carry
        kic = kit_ref[c]
        acc = jnp.zeros((tq, ts), F32)
        for h in range(IDX_HEADS):
            lg = jnp.dot(qi_h[h], kic, preferred_element_type=F32)
            acc = acc + jnp.maximum(lg, 0.0) * w_h[h]
        kpos = c * ts + lax.broadcasted_iota(I32, (tq, ts), 1)
        adm = kpos <= row
        sc_ref[c] = jnp.where(adm, acc, NEG_INF)
        mn = jnp.minimum(mn, _lane_fold(jnp.where(adm, acc, jnp.inf), jnp.minimum))
        mx = jnp.maximum(mx, _lane_fold(jnp.where(adm, acc, NEG_INF), jnp.maximum))
        return mn, mx

    mn, mx = lax.fori_loop(0, nch, scores,
                           (jnp.full((tq, 128), jnp.inf, F32), jnp.full((tq, 128), NEG_INF, F32)))
    lo0 = jnp.min(mn, axis=1, keepdims=True)
    hi0 = jnp.max(mx, axis=1, keepdims=True)
    skip = row < topk
    v, cut = _select_threshold(lambda c, r0, nr: sc_ref[c, r0:r0 + nr, :], nch, sc_ref.shape[0], ts, tq, topk,
                               lo0, hi0, skip)

    scale = HEAD_DIM ** -0.5
    group = heads // KV_HEADS
    vw = jnp.concatenate([v] * (ts // 128), axis=1)
    cutw = jnp.concatenate([cut] * (ts // 128), axis=1)
    head = lambda h: slice(h * HEAD_DIM, (h + 1) * HEAD_DIM)

    def selected(c):
        x = sc_ref[c]
        kpos = c * ts + lax.broadcasted_iota(I32, (tq, ts), 1)
        return jnp.logical_or(x > vw, jnp.logical_and(x == vw, kpos <= cutw))

    c2 = scale * math.log2(math.e)
    kmax = jnp.sqrt(jnp.max(kn2_ref[...], axis=1, keepdims=True))
    bound = []
    for h in range(heads):
        qh = q_ref[:, head(h)].astype(F32)
        qn = jnp.sqrt(jnp.sum(qh * qh, axis=1, keepdims=True))
        bound.append(jnp.broadcast_to(qn * (kmax[h // group:h // group + 1, :] * c2), (tq, ts)))
    ones = jnp.ones((ts, HEAD_DIM), BF16)
    acc_ref[...] = jnp.zeros(acc_ref.shape, F32)

    def attend_bounded(c, _):
        keep = jnp.where(selected(c), 1.0, 0.0).astype(BF16)
        kc = kt_ref[c]
        vc = v_ref[c]
        for g in range(KV_HEADS):
            v_ones = jnp.concatenate([vc[:, head(g)], ones], axis=1)
            for h in range(g * group, (g + 1) * group):
                s = jnp.dot(q_ref[:, head(h)], kc[head(g), :], preferred_element_type=F32)
                p = jnp.exp2(s * c2 - bound[h]).astype(BF16) * keep
                acc_ref[h] += jnp.dot(p, v_ones, preferred_element_type=F32)
        return 0

    lax.fori_loop(0, nch, attend_bounded, 0)
    lmin = jnp.full((tq, HEAD_DIM), jnp.inf, F32)
    for h in range(heads):
        acc = acc_ref[h]
        lsum = acc[:, HEAD_DIM:]
        lmin = jnp.minimum(lmin, lsum)
        o_ref[:, head(h)] = (acc[:, :HEAD_DIM] / lsum).astype(o_ref.dtype)

    @pl.when(jnp.logical_not(jnp.min(lmin) > UNDERFLOW_GUARD))
    def _():
        m_ref[...] = jnp.full(m_ref.shape, SOFTMAX_FLOOR, F32)
        l_ref[...] = jnp.zeros(l_ref.shape, F32)
        acc_ref[...] = jnp.zeros(acc_ref.shape, F32)

        def attend_online(c, _):
            bias = jnp.where(selected(c), 0.0, NEG_INF)
            kc = kt_ref[c]
            vc = v_ref[c]
            for h in range(heads):
                g = h // group
                s = jnp.dot(q_ref[:, head(h)], kc[head(g), :], preferred_element_type=F32) * scale + bias
                m_old = m_ref[h]
                m_new = jnp.maximum(m_old, jnp.max(s, axis=1, keepdims=True))
                alpha = jnp.exp(m_old - m_new)
                p = jnp.exp(s - m_new)
                l_ref[h] = l_ref[h] * alpha + jnp.sum(p, axis=1, keepdims=True)
                acc_ref[h, :, :HEAD_DIM] = acc_ref[h, :, :HEAD_DIM] * alpha + jnp.dot(
                    p.astype(BF16), vc[:, head(g)], preferred_element_type=F32)
                m_ref[h] = m_new
            return 0

        lax.fori_loop(0, nch, attend_online, 0)
        for h in range(heads):
            o_ref[:, head(h)] = (acc_ref[h, :, :HEAD_DIM] / l_ref[h]).astype(o_ref.dtype)


SAMPLE_TS = 4 * PAGE_SIZE
SAMPLE_RING = 8


def _sample_attn_kernel(pt_ref, q_ref, qi_ref, wi_ref, kin_ref, kn_ref, vn_ref, cki_ref, ck_ref, cv_ref,
                        o_ref, kibuf, kbuf, vbuf, sc_ref, sem_i, sem_k, sem_v, *, n_pages, t_new, topk):
    b = pl.program_id(0)
    ts = SAMPLE_TS
    ppc = ts // PAGE_SIZE
    nch = n_pages // ppc
    past = n_pages * PAGE_SIZE
    rows = q_ref.shape[2]

    def ki_copy(p):
        dst = kibuf.at[:, pl.ds(pl.multiple_of(p * PAGE_SIZE, PAGE_SIZE), PAGE_SIZE)]
        return pltpu.make_async_copy(cki_ref.at[0, pt_ref[b, p]], dst, sem_i.at[0])

    def kv_copies(c, slot):
        cps = []
        for j in range(ppc):
            page = pt_ref[b, c * ppc + j]
            dst = pl.ds(j * PAGE_SIZE, PAGE_SIZE)
            for g in range(KV_HEADS):
                cps.append(pltpu.make_async_copy(ck_ref.at[0, page, :, g, :], kbuf.at[slot, g, dst], sem_k.at[slot]))
                cps.append(pltpu.make_async_copy(cv_ref.at[0, page, :, g, :], vbuf.at[slot, g, dst], sem_v.at[slot]))
        return cps

    def start_ki(p, _):
        ki_copy(p).start()
        return 0

    def wait_ki(p, _):
        ki_copy(p).wait()
        return 0

    lax.fori_loop(0, n_pages, start_ki, 0)
    ring = kbuf.shape[0]
    lanes = next(n for n in (4, 2, 1) if nch % n == 0)
    ahead = ring - lanes
    assert ahead >= 1
    for c0 in range(min(ahead, nch)):
        for cp in kv_copies(c0, c0):
            cp.start()
    lax.fori_loop(0, n_pages, wait_ki, 0)

    wcol = (wi_ref[0] * IDX_HEADS ** -0.5) * IDX_DIM ** -0.5
    qi = qi_ref[0]

    def index_scores(kct):
        width = kct.shape[1]
        lg = jnp.dot(qi, kct, preferred_element_type=F32)
        weighted = jnp.maximum(lg, 0.0) * jnp.broadcast_to(wcol, (IDX_HEADS * rows, width))
        acc = weighted[0:rows]
        for h in range(1, IDX_HEADS):
            acc = acc + weighted[h * rows:(h + 1) * rows]
        return acc

    def scores(i, carry):
        mn, mx = carry
        span = lanes * ts
        acc = index_scores(kibuf[:, pl.ds(pl.multiple_of(i * span, span), span)].astype(BF16))
        sc_ref[i] = acc
        return jnp.minimum(mn, _lane_fold(acc, jnp.minimum)), jnp.maximum(mx, _lane_fold(acc, jnp.maximum))

    mn, mx = lax.fori_loop(0, nch // lanes, scores,
                           (jnp.full((rows, 128), jnp.inf, F32), jnp.full((rows, 128), NEG_INF, F32)))
    acc = index_scores(kin_ref[0])
    tok = lax.broadcasted_iota(I32, (rows, ts), 0) % t_new
    col = lax.broadcasted_iota(I32, (rows, ts), 1)
    adm = jnp.logical_and(col <= tok, col < t_new)
    span = lanes * ts
    ngrp = nch // lanes + 1
    sc_ref[ngrp - 1] = jnp.concatenate(
        [jnp.where(adm, acc, NEG_INF), jnp.full((rows, span - ts), NEG_INF, F32)], axis=1) if lanes > 1 else jnp.where(
            adm, acc, NEG_INF)
    mn = jnp.minimum(mn, _lane_fold(jnp.where(adm, acc, jnp.inf), jnp.minimum))
    mx = jnp.maximum(mx, _lane_fold(jnp.where(adm, acc, NEG_INF), jnp.maximum))
    lo0 = jnp.min(mn, axis=1, keepdims=True)
    hi0 = jnp.max(mx, axis=1, keepdims=True)
    skip = jnp.full((rows, 1), past + 1 <= topk)
    v, cut = _select_threshold(lambda c, r0, nr: sc_ref[c], ngrp, ngrp, span, rows, topk, lo0, hi0, skip)
    vw = jnp.concatenate([v] * (span // 128), axis=1)
    cutw = jnp.concatenate([cut] * (span // 128), axis=1)
    scale = HEAD_DIM ** -0.5

    def bias_of(c):
        x = sc_ref[c]
        kpos = c * span + lax.broadcasted_iota(I32, (rows, span), 1)
        sel = jnp.logical_or(x > vw, jnp.logical_and(x == vw, kpos <= cutw))
        return jnp.where(sel, 0.0, NEG_INF)

    def flash(state, parts):
        m_old, l_old, a_old = state
        s = jnp.concatenate(
            [lax.dot_general(q_ref[0, g], kg, (((1,), (1,)), ((), ())), preferred_element_type=F32) * scale + bias
             for g, bias, kg, _ in parts], axis=0)
        m_new = jnp.maximum(m_old, jnp.max(s, axis=1, keepdims=True))
        alpha = jnp.exp(m_old - m_new)
        p = jnp.exp(s - m_new)
        l_new = l_old * alpha + jnp.sum(p, axis=1, keepdims=True)
        pb = p.astype(BF16)
        pv = jnp.concatenate(
            [jnp.dot(pb[n * rows:(n + 1) * rows], vg, preferred_element_type=F32) for n, (_, _, _, vg) in enumerate(parts)],
            axis=0)
        return m_new, l_new, a_old * alpha + pv

    def receive(c):
        for cp in kv_copies(c, lax.rem(c, ring)):
            cp.wait()

        @pl.when(c + ahead < nch)
        def _():
            for cp in kv_copies(c + ahead, lax.rem(c + ahead, ring)):
                cp.start()

    def attend(i, state):
        for u in range(lanes):
            receive(i * lanes + u)
        parts = []
        bias_grp = bias_of(i)
        for u in range(lanes):
            c = i * lanes + u
            slot = lax.rem(c, ring)
            bias = bias_grp[:, u * ts:(u + 1) * ts]
            parts += [(g, bias, kbuf[slot, g].astype(BF16), vbuf[slot, g].astype(BF16)) for g in range(KV_HEADS)]
        return flash(state, parts)

    pieces = lanes * KV_HEADS
    init = (jnp.full((pieces * rows, 1), SOFTMAX_FLOOR, F32), jnp.zeros((pieces * rows, 1), F32),
            jnp.zeros((pieces * rows, HEAD_DIM), F32))
    m_all, l_all, a_all = lax.fori_loop(0, nch // lanes, attend, init)

    def merge(a, b):
        m = jnp.maximum(a[0], b[0])
        fa, fb = jnp.exp(a[0] - m), jnp.exp(b[0] - m)
        return m, a[1] * fa + b[1] * fb, a[2] * fa + b[2] * fb

    per_lane = KV_HEADS * rows
    state = tuple(x[0:per_lane] for x in (m_all, l_all, a_all))
    for u in range(1, lanes):
        state = merge(state, tuple(x[u * per_lane:(u + 1) * per_lane] for x in (m_all, l_all, a_all)))
    bias = bias_of(ngrp - 1)[:, :ts]
    new_parts = [(g, bias, kn_ref[0][:, g * HEAD_DIM:(g + 1) * HEAD_DIM], vn_ref[0][:, g * HEAD_DIM:(g + 1) * HEAD_DIM])
                 for g in range(KV_HEADS)]
    _, l_fin, a_fin = flash(state, new_parts)
    out = a_fin / l_fin
    for g in range(KV_HEADS):
        o_ref[0, g] = out[g * rows:(g + 1) * rows].astype(o_ref.dtype)


def sample_attention(q, qi, wi, ki_new, k_new, v_new, cache_kidx, cache_k, cache_v, page_table, *, topk):
    nb, t_new, width = q.shape
    heads = width // HEAD_DIM
    group = heads // KV_HEADS
    n_pages = page_table.shape[1]
    ts = SAMPLE_TS
    assert n_pages % (ts // PAGE_SIZE) == 0
    nch = n_pages // (ts // PAGE_SIZE)
    lanes = next(n for n in (4, 2, 1) if nch % n == 0)
    rows = group * t_new
    qg = q.reshape(nb, t_new, KV_HEADS, group, HEAD_DIM).transpose(0, 2, 3, 1, 4).reshape(nb, KV_HEADS, rows, HEAD_DIM)
    qir = jnp.tile(qi.reshape(nb, t_new, IDX_HEADS, IDX_DIM).transpose(0, 2, 1, 3), (1, 1, group, 1))
    qir = qir.reshape(nb, IDX_HEADS * rows, IDX_DIM)
    wir = jnp.tile(wi.transpose(0, 2, 1), (1, 1, group)).reshape(nb, IDX_HEADS * rows, 1)
    padk = lambda a: jnp.pad(a, ((0, 0), (0, ts - t_new), (0, 0)))
    per_b = lambda a: pl.BlockSpec((1,) + a.shape[1:], lambda b, pt: (b,) + (0,) * (a.ndim - 1))
    ops = (qg, qir, wir, padk(ki_new).transpose(0, 2, 1), padk(k_new), padk(v_new))
    cache_kidx_t = cache_kidx.transpose(0, 1, 3, 2)
    out = pl.pallas_call(
        functools.partial(_sample_attn_kernel, n_pages=n_pages, t_new=t_new, topk=topk),
        grid_spec=pltpu.PrefetchScalarGridSpec(
            num_scalar_prefetch=1,
            grid=(nb,),
            in_specs=[per_b(a) for a in ops] + [pl.BlockSpec(memory_space=pl.ANY)] * 3,
            out_specs=pl.BlockSpec((1, KV_HEADS, rows, HEAD_DIM), lambda b, pt: (b, 0, 0, 0)),
            scratch_shapes=[
                pltpu.VMEM((IDX_DIM, n_pages * PAGE_SIZE), F32),
                pltpu.VMEM((SAMPLE_RING, KV_HEADS, ts, HEAD_DIM), F32),
                pltpu.VMEM((SAMPLE_RING, KV_HEADS, ts, HEAD_DIM), F32),
                pltpu.VMEM((nch // lanes + 1, rows, lanes * ts), F32),
                pltpu.SemaphoreType.DMA((1,)),
                pltpu.SemaphoreType.DMA((SAMPLE_RING,)),
                pltpu.SemaphoreType.DMA((SAMPLE_RING,)),
            ]),
        out_shape=jax.ShapeDtypeStruct((nb, KV_HEADS, rows, HEAD_DIM), BF16),
        compiler_params=_cparams("arbitrary"),
        name="sample_attention",
    )(page_table, *ops, cache_kidx_t, cache_k, cache_v)
    return out.reshape(nb, KV_HEADS, group, t_new, HEAD_DIM).transpose(0, 3, 1, 2, 4).reshape(nb, t_new, width)


def prompt_attention(qb, qib, wi, kib, kb, vb, kn2, *, tq, ts, topk):
    s, width = qb.shape
    heads = width // HEAD_DIM
    assert s % tq == 0 and s % ts == 0
    nc = s // ts
    kit = kib.reshape(nc, ts, IDX_DIM).transpose(0, 2, 1)
    kt = kb.reshape(nc, ts, KV_HEADS * HEAD_DIM).transpose(0, 2, 1)
    v3 = vb.reshape(nc, ts, KV_HEADS * HEAD_DIM)
    whole = lambda a: pl.BlockSpec(a.shape, lambda i: (0,) * a.ndim, pipeline_mode=pl.Buffered(1))
    return pl.pallas_call(
        functools.partial(_prompt_attn_kernel, tq=tq, ts=ts, topk=topk),
        grid=(s // tq,),
        in_specs=[
            pl.BlockSpec((tq, width), lambda i: (i, 0)),
            pl.BlockSpec((tq, IDX_HEADS * IDX_DIM), lambda i: (i, 0)),
            pl.BlockSpec((tq, IDX_HEADS), lambda i: (i, 0)),
            whole(kit), whole(kt), whole(v3), whole(kn2),
        ],
        out_specs=pl.BlockSpec((tq, width), lambda i: (i, 0)),
        out_shape=jax.ShapeDtypeStruct((s, width), BF16),
        scratch_shapes=[
            pltpu.VMEM((nc, tq, ts), F32),
            pltpu.VMEM((heads, tq, 1), F32),
            pltpu.VMEM((heads, tq, 1), F32),
            pltpu.VMEM((heads, tq, 2 * HEAD_DIM), F32),
        ],
        compiler_params=_cparams("arbitrary"),
        name="prompt_attention",
    )(qb, qib, wi, kit, kt, v3, kn2)


def _row_tile(m, cap):
    t = cap
    while m % t:
        t //= 2
    return t


def _token_stages(x, attend, gdn, lw):
    (nmw, wp, w_out, nfw, wq, sub_keys, w_u, w_v, fw, tabs) = lw
    m = x.shape[0]
    tm = _row_tile(m, TOKEN_TILE)
    p = norm_matmul(x, nmw, wp, tm=tm, tn=wp.shape[1] // 2)
    qb, qib, k32, kb, v32, vb, ki32, kib, auxr, kn2 = rope_split(p, *tabs, tm=tm)
    wi, a_pre = auxr[:, 0:IDX_HEADS], auxr[:, 2 * IDX_HEADS:3 * IDX_HEADS]
    ya = attend(qb, qib, wi, kib, kb, vb, kn2)
    yg, new_conv, new_ssm = gdn(p, a_pre)
    h, xn, qp = out_proj(ya, yg, x, w_out, nfw, wq, tm=tm)
    i1, i2, gate = peer_topk(qp, sub_keys, tm=_row_tile(m, PEER_TOPK_TILE))
    tok = lambda a: a.reshape(PEER_HEADS * PEER_TOPK, m).T
    gates = peer_gates(tok(i1), tok(i2), tok(gate), tmb=_row_tile(m, GATE_TILE))
    f = peer_dense(xn, w_u, w_v, gates, tm=_row_tile(m, PEER_TOKEN_TILE), ib=PEER_KEY_BLOCK)
    y = residual_norm(h, f, fw, tm=tm)
    return y, k32, v32, ki32, new_conv, new_ssm


def kernel(x_prompt, x_sample, cache_k, cache_v, cache_kidx, page_table, state_conv, state_ssm, norm_mix_w, w_in,
           conv_w, a_log, dt_bias, gdn_norm_w, w_out, norm_ffn_w, peer_wq, peer_sub_keys, peer_u, peer_v,
           norm_final_w):
    depth = w_in.shape[0]
    b, s, d = x_prompt.shape
    nb, t, _ = x_sample.shape
    assert depth == 1 and b == 1, "single layer, single prompt sequence"
    past = page_table.shape[1] * PAGE_SIZE
    heads_g = state_ssm.shape[2]
    conv_ch = state_conv.shape[-1]

    wp = _pack_w_in(w_in[0])
    shared = (norm_mix_w[0][None], wp, w_out[0].astype(BF16), norm_ffn_w[0][None], peer_wq[0].astype(BF16),
              peer_sub_keys[0], peer_u[0].astype(BF16), peer_v[0].astype(BF16), norm_final_w[None])
    gdn_w = (conv_w[0], a_log[0], dt_bias[0], gdn_norm_w[0])

    def attend_p(qb, qib, wi, kib, kb, vb, kn2):
        return prompt_attention(qb, qib, wi, kib, kb, vb, kn2.T, tq=_row_tile(s, ATTN_Q_TILE), ts=_row_tile(s, ATTN_KEY_CHUNK),
                                topk=min(TOPK_MAX, s // 4))

    def gdn_p(p, a_pre):
        conv0 = jnp.zeros((1, CONV_W - 1, conv_ch), F32)
        ssm0 = jnp.zeros((1, heads_g, GDN_DK, GDN_DV), F32)
        block = 2 * GDN_CHUNK if s % (2 * GDN_CHUNK) == 0 else GDN_CHUNK
        assert s % block == 0
        return gdn_mixer(p, a_pre.T[None], conv0, *gdn_w, ssm0, n_seq=1, t_pad=s, t_total=s, chunk=block,
                         sub=GDN_CHUNK)

    tabs_p = _rope_tables(jnp.arange(s))
    y_p, k_p, v_p, ki_p, conv_p, ssm_p = _token_stages(x_prompt.reshape(s, d), attend_p, gdn_p, shared + (tabs_p,))

    m_s = nb * t
    t8 = -(-t // 8) * 8

    def attend_s(qb, qib, wi, kib, kb, vb, kn2):
        seq = lambda a: a.reshape(nb, t, a.shape[-1])
        y = sample_attention(seq(qb), seq(qib), seq(wi), seq(kib), seq(kb), seq(vb), cache_kidx, cache_k, cache_v,
                             page_table, topk=min(TOPK_MAX, (past + t) // 4))
        return y.reshape(m_s, y.shape[-1])

    def gdn_s(p, a_pre):
        pad_t = lambda a: jnp.pad(a.reshape(nb, t, a.shape[-1]), ((0, 0), (0, t8 - t), (0, 0)))
        p8 = pad_t(p).reshape(nb * t8, p.shape[-1])
        at = pad_t(a_pre).transpose(0, 2, 1)
        yg, nconv, nssm = gdn_mixer(p8, at, state_conv[0], *gdn_w, state_ssm[0], n_seq=nb, t_pad=t8, t_total=t,
                                    chunk=t8, sub=8)
        return yg.reshape(nb, t8, yg.shape[-1])[:, :t].reshape(m_s, yg.shape[-1]), nconv, nssm

    tabs_s = tuple(jnp.tile(a, (nb, 1)) for a in _rope_tables(past + jnp.arange(t)))
    y_s, k_s, v_s, ki_s, conv_s, ssm_s = _token_stages(x_sample.reshape(m_s, d), attend_s, gdn_s, shared + (tabs_s,))

    kv = lambda a, n, tt: a.reshape(1, n, tt, KV_HEADS, HEAD_DIM)
    return (y_p.reshape(b, s, d), y_s.reshape(nb, t, d),
            kv(k_p, b, s), kv(v_p, b, s), ki_p.reshape(1, b, s, IDX_DIM), conv_p[None], ssm_p[None],
            kv(k_s, nb, t), kv(v_s, nb, t), ki_s.reshape(1, nb, t, IDX_DIM), conv_s[None], ssm_s[None])
```

```python
import functools
import math

import jax
import jax.numpy as jnp
from jax import lax
from jax.experimental import pallas as pl
from jax.experimental.pallas import tpu as pltpu

F32 = jnp.float32
BF16 = jnp.bfloat16
I32 = jnp.int32
EPS = 1e-6
NEG_INF = float("-inf")

HEAD_DIM = 128
KV_HEADS = 2
IDX_HEADS = 8
IDX_DIM = 64
TOPK_MAX = 256
ROPE_THETA = 10000.0
PAGE_SIZE = 128
GDN_DK = 128
GDN_DV = 128
CONV_W = 4
GDN_CHUNK = 64
PEER_HEADS = 8
PEER_KEYS = 128
PEER_QDIM = 128
PEER_TOPK = 16

VMEM_LIMIT_BYTES = 56 * 1024 * 1024

TOKEN_TILE = 512
PEER_TOKEN_TILE = 1024
PEER_KEY_BLOCK = 8
PEER_TOPK_TILE = 512
GATE_TILE = 256
ATTN_Q_TILE = 256
ATTN_KEY_CHUNK = 512
MAX_SEARCH_STEPS = 4096
UNTESTED_SEARCH_STEPS = 15
SOFTMAX_FLOOR = -1e30
UNDERFLOW_GUARD = 1e-30


def _cparams(*sem):
    return pltpu.CompilerParams(dimension_semantics=sem, vmem_limit_bytes=VMEM_LIMIT_BYTES)


_PROJ_SRC = (("q", 1024), ("k", 256), ("v", 256), ("qi", 512), ("ki", 64), ("wi", 8), ("qkv", 3072), ("z", 1024),
             ("b", 8), ("a", 8))
_PROJ_DST = ("q", "qi", "k", "v", "qkv", "z", "ki", "wi", "b", "a")
PROJ_PACKED = 6400


def _pack_w_in_kernel(w_ref, o_ref):
    src, off = {}, 0
    for name, width in _PROJ_SRC:
        src[name] = (off, width)
        off += width
    dst = 0
    for name in _PROJ_DST:
        s0, width = src[name]
        o_ref[:, dst:dst + width] = w_ref[:, s0:s0 + width].astype(o_ref.dtype)
        dst += width
    o_ref[:, dst:] = jnp.zeros((o_ref.shape[0], o_ref.shape[1] - dst), o_ref.dtype)


def _pack_w_in(w_in):
    d, n = w_in.shape
    assert n == sum(w for _, w in _PROJ_SRC)
    tr = _row_tile(d, 256)
    return pl.pallas_call(
        _pack_w_in_kernel,
        grid=(d // tr,),
        in_specs=[pl.BlockSpec((tr, n), lambda i: (i, 0))],
        out_specs=pl.BlockSpec((tr, PROJ_PACKED), lambda i: (i, 0)),
        out_shape=jax.ShapeDtypeStruct((d, PROJ_PACKED), BF16),
        compiler_params=_cparams("arbitrary"),
        name="pack_w_in",
    )(w_in)


def _rope_tables(pos):
    def table(dim):
        half = dim // 2
        inv = ROPE_THETA ** (-jnp.arange(half, dtype=F32) / half)
        ang = pos.astype(F32)[:, None] * inv[None, :]
        cos = jnp.tile(jnp.cos(ang), (1, 128 // half))
        sin = jnp.tile(jnp.concatenate([-jnp.sin(ang), jnp.sin(ang)], axis=1), (1, 128 // dim))
        return cos, sin
    cq, sq = table(HEAD_DIM)
    ci, si = table(IDX_DIM)
    return cq, sq, ci, si


def _norm_matmul_kernel(x_ref, nw_ref, w_ref, o_ref):
    x = x_ref[...]
    y = x * lax.rsqrt(jnp.mean(x * x, axis=-1, keepdims=True) + EPS)
    xn = (y * nw_ref[...]).astype(BF16)
    o_ref[...] = jnp.dot(xn, w_ref[...], preferred_element_type=F32)


def norm_matmul(x, nw, w, *, tm, tn):
    m, d = x.shape
    n = w.shape[1]
    assert m % tm == 0 and n % tn == 0
    return pl.pallas_call(
        _norm_matmul_kernel,
        grid=(n // tn, m // tm),
        in_specs=[
            pl.BlockSpec((tm, d), lambda j, i: (i, 0)),
            pl.BlockSpec((1, d), lambda j, i: (0, 0)),
            pl.BlockSpec((d, tn), lambda j, i: (0, j)),
        ],
        out_specs=pl.BlockSpec((tm, tn), lambda j, i: (i, j)),
        out_shape=jax.ShapeDtypeStruct((m, n), F32),
        compiler_params=_cparams("arbitrary", "arbitrary"),
        name="norm_matmul",
    )(x, nw, w)


def _rope128(x, cos, sin_signed):
    return x * cos + pltpu.roll(x, 64, 1) * sin_signed


def _rope64(x, cos, sin_signed, first_half):
    partner = jnp.where(first_half, pltpu.roll(x, 96, 1), pltpu.roll(x, 32, 1))
    return x * cos + partner * sin_signed


def _rope_split_kernel(q_ref, qi_ref, k_ref, v_ref, aux_ref, cq_ref, sq_ref, ci_ref, si_ref,
                       qb_ref, qib_ref, k32_ref, kb_ref, v32_ref, vb_ref, ki32_ref, kib_ref, auxr_ref, kn2_ref):
    cq, sq, ci, si = cq_ref[...], sq_ref[...], ci_ref[...], si_ref[...]
    lane = lax.broadcasted_iota(I32, ci.shape, 1)
    first_half = (lane % IDX_DIM) < (IDX_DIM // 2)
    for h in range(q_ref.shape[1] // HEAD_DIM):
        sl = slice(h * HEAD_DIM, (h + 1) * HEAD_DIM)
        qb_ref[:, sl] = _rope128(q_ref[:, sl], cq, sq).astype(BF16)
    for h in range(k_ref.shape[1] // HEAD_DIM):
        sl = slice(h * HEAD_DIM, (h + 1) * HEAD_DIM)
        kr = _rope128(k_ref[:, sl], cq, sq)
        k32_ref[:, sl] = kr
        krb = kr.astype(BF16)
        kb_ref[:, sl] = krb
        kn2_ref[:, h:h + 1] = jnp.sum(krb.astype(F32) * krb.astype(F32), axis=1, keepdims=True)
    for h in range(qi_ref.shape[1] // 128):
        sl = slice(h * 128, (h + 1) * 128)
        qib_ref[:, sl] = _rope64(qi_ref[:, sl], ci, si, first_half).astype(BF16)
    v = v_ref[...]
    v32_ref[...] = v
    vb_ref[...] = v.astype(BF16)
    aux = aux_ref[...]
    kir = _rope64(aux, ci, si, first_half)[:, :IDX_DIM]
    ki32_ref[...] = kir
    kib_ref[...] = kir.astype(BF16)
    auxr_ref[...] = aux[:, IDX_DIM:]


def rope_split(p, cq, sq, ci, si, *, tm):
    m = p.shape[0]
    assert m % tm == 0
    row = lambda w, j: pl.BlockSpec((tm, w), lambda i, j=j: (i, j))
    outs = [
        ((m, 1024), BF16), ((m, 512), BF16), ((m, 256), F32), ((m, 256), BF16), ((m, 256), F32),
        ((m, 256), BF16), ((m, IDX_DIM), F32), ((m, IDX_DIM), BF16), ((m, 128 - IDX_DIM), F32),
        ((m, KV_HEADS), F32),
    ]
    return pl.pallas_call(
        _rope_split_kernel,
        grid=(m // tm,),
        in_specs=[row(1024, 0), row(512, 2), row(256, 6), row(256, 7), row(128, 48),
                  row(128, 0), row(128, 0), row(128, 0), row(128, 0)],
        out_specs=[pl.BlockSpec((tm, s[1]), lambda i: (i, 0)) for s, _ in outs],
        out_shape=[jax.ShapeDtypeStruct(s, dt) for s, dt in outs],
        compiler_params=_cparams("arbitrary"),
        name="rope_split",
    )(p, p, p, p, p, cq, sq, ci, si)


def _out_proj_kernel(ya_ref, yg_ref, res_ref, wa_ref, wg_ref, nw_ref, wq_ref, h_ref, xn_ref, qp_ref):
    h = res_ref[...] + jnp.dot(ya_ref[...], wa_ref[...], preferred_element_type=F32)
    h = h + jnp.dot(yg_ref[...], wg_ref[...], preferred_element_type=F32)
    h_ref[...] = h
    y = h * lax.rsqrt(jnp.mean(h * h, axis=-1, keepdims=True) + EPS)
    xn = (y * nw_ref[...]).astype(BF16)
    xn_ref[...] = xn
    qp_ref[...] = jnp.dot(xn, wq_ref[...], preferred_element_type=F32)


def out_proj(ya, yg, res, w_out, nw, wq, *, tm):
    m, d = res.shape
    half = ya.shape[1]
    nq = wq.shape[1]
    assert m % tm == 0
    rows = lambda wd: pl.BlockSpec((tm, wd), lambda i: (i, 0))
    const = lambda a: pl.BlockSpec(a.shape, lambda i: (0,) * a.ndim, pipeline_mode=pl.Buffered(1))
    wa, wg = w_out[:half], w_out[half:]
    return pl.pallas_call(
        _out_proj_kernel,
        grid=(m // tm,),
        in_specs=[rows(half), rows(half), rows(d), const(wa), const(wg), const(nw), const(wq)],
        out_specs=[rows(d), rows(d), rows(nq)],
        out_shape=[jax.ShapeDtypeStruct((m, d), F32), jax.ShapeDtypeStruct((m, d), BF16),
                   jax.ShapeDtypeStruct((m, nq), F32)],
        compiler_params=_cparams("arbitrary"),
        name="out_proj",
    )(ya, yg, res, wa, wg, nw, wq)


def _take_top(s, codes, count):
    big = jnp.int32(2 ** 30)
    vals, picks = [], []
    for _ in range(count):
        m = jnp.max(s, axis=0, keepdims=True)
        pick = jnp.min(jnp.where(s == m, codes, big), axis=0, keepdims=True)
        s = jnp.where(codes == pick, NEG_INF, s)
        vals.append(m)
        picks.append(pick)
    return jnp.concatenate(vals, axis=0), jnp.concatenate(picks, axis=0)


def _lookup(table, sel, count):
    out = jnp.zeros(sel.shape, table.dtype)
    for a in range(count):
        out = jnp.where(sel == a, jnp.broadcast_to(table[a:a + 1, :], sel.shape), out)
    return out


def _peer_topk_kernel(q_ref, sk_ref, i1_ref, i2_ref, g_ref):
    def one_head(h, _):
        q = q_ref[:, pl.ds(pl.multiple_of(h * PEER_QDIM, PEER_QDIM), PEER_QDIM)]
        i1_ref[h], i2_ref[h], g_ref[h] = _peer_topk_tile(q, lambda c: sk_ref[2 * h + c])
        return 0

    lax.fori_loop(0, i1_ref.shape[0], one_head, 0)


def _peer_topk_tile(q, sub_keys):
    tm = q.shape[0]
    kk = PEER_TOPK
    half = PEER_QDIM // 2
    key_codes = lax.broadcasted_iota(I32, (PEER_KEYS, tm), 0)
    top_v, top_i = [], []
    for c in range(2):
        qs = q[:, c * half:(c + 1) * half].astype(BF16)
        s = lax.dot_general(sub_keys(c), qs, (((1,), (1,)), ((), ())), preferred_element_type=F32)
        vals, idx = _take_top(s, key_codes, kk)
        top_v.append(vals)
        top_i.append(idx)
    pieces, codes = [], []

    def add_piece(cand, a_of_row, b_of_row):
        ok = (a_of_row + 1) * (b_of_row + 1) <= kk
        pieces.append(jnp.where(ok, cand, NEG_INF))
        codes.append(a_of_row * kk + b_of_row)

    split = 4
    for a in range(split):
        rows = -(-(kk // (a + 1)) // 8) * 8
        r = lax.broadcasted_iota(I32, (rows, tm), 0)
        add_piece(top_v[0][a:a + 1, :] + top_v[1][0:rows, :], jnp.full((rows, tm), a, I32), r)
    r8 = lax.broadcasted_iota(I32, (8, tm), 0)
    for b in range(kk // (split + 1)):
        cand = top_v[0][0:8, :] + top_v[1][b:b + 1, :]
        add_piece(jnp.where(r8 >= split, cand, NEG_INF), r8, jnp.full((8, tm), b, I32))
    add_piece(top_v[0][8:kk, :] + top_v[1][0:1, :], r8 + 8, jnp.zeros((8, tm), I32))
    best_s, best_c = _take_top(jnp.concatenate(pieces, axis=0), jnp.concatenate(codes, axis=0), kk)
    i1 = _lookup(top_i[0], lax.shift_right_logical(best_c, 4), kk)
    i2 = _lookup(top_i[1], jnp.bitwise_and(best_c, kk - 1), kk)
    e = jnp.exp(best_s - best_s[0:1, :])
    return i1, i2, e / jnp.sum(e, axis=0, keepdims=True)


def peer_topk(qp, sub_keys, *, tm):
    m = qp.shape[0]
    heads = sub_keys.shape[0]
    assert m % tm == 0 and PEER_TOPK == 16
    sk = sub_keys.reshape(heads * 2, PEER_KEYS, PEER_QDIM // 2).astype(BF16)
    out = pl.BlockSpec((heads, PEER_TOPK, tm), lambda i: (0, 0, i))
    return pl.pallas_call(
        _peer_topk_kernel,
        grid=(m // tm,),
        in_specs=[pl.BlockSpec((tm, heads * PEER_QDIM), lambda i: (i, 0)),
                  pl.BlockSpec(sk.shape, lambda i: (0, 0, 0))],
        out_specs=[out, out, out],
        out_shape=[jax.ShapeDtypeStruct((heads, PEER_TOPK, m), I32), jax.ShapeDtypeStruct((heads, PEER_TOPK, m), I32),
                   jax.ShapeDtypeStruct((heads, PEER_TOPK, m), F32)],
        compiler_params=_cparams("arbitrary"),
        name="peer_topk",
    )(qp, sk)


GATE_TOKENS = 32


def _peer_gate_kernel(i1_ref, i2_ref, g_ref, o_ref):
    tmb = i1_ref.shape[0]
    nk = PEER_KEYS
    sub = lax.broadcasted_iota(I32, (nk, i1_ref.shape[1]), 0)

    def token_block(j, _):
        t0 = pl.multiple_of(j * GATE_TOKENS, GATE_TOKENS)
        per_token = []
        for u in range(GATE_TOKENS):
            wide = lambda ref: jnp.broadcast_to(ref[pl.ds(t0 + u, 1), :], sub.shape)
            p1 = jnp.where(wide(i1_ref) == sub, wide(g_ref), 0.0).astype(BF16)
            p2 = jnp.where(wide(i2_ref) == sub, 1.0, 0.0).astype(BF16)
            gm = lax.dot_general(p1, p2, (((1,), (1,)), ((), ())), preferred_element_type=F32)
            per_token.append(gm.astype(o_ref.dtype))
        block = jnp.stack(per_token, axis=0)
        o_ref[:, pl.ds(t0, GATE_TOKENS), :] = pltpu.einshape("mab->amb", block)
        return 0

    lax.fori_loop(0, tmb // GATE_TOKENS, token_block, 0)


def peer_gates(i1, i2, gate, *, tmb):
    m, slots = i1.shape
    assert m % tmb == 0 and tmb % GATE_TOKENS == 0
    rows = pl.BlockSpec((tmb, slots), lambda i: (i, 0))
    return pl.pallas_call(
        _peer_gate_kernel,
        grid=(m // tmb,),
        in_specs=[rows, rows, rows],
        out_specs=pl.BlockSpec((PEER_KEYS, tmb, PEER_KEYS), lambda i: (0, i, 0)),
        out_shape=jax.ShapeDtypeStruct((PEER_KEYS, m, PEER_KEYS), BF16),
        compiler_params=_cparams("arbitrary"),
        name="peer_gates",
    )(i1, i2, gate)


def _gelu_tanh(x):
    return 0.5 * x * (1.0 + jnp.tanh(math.sqrt(2.0 / math.pi) * (x + 0.044715 * (x * x * x))))


def _peer_dense_kernel(xn_ref, wu_ref, wv_ref, g_ref, o_ref):
    j = pl.program_id(1)
    ib = g_ref.shape[0]
    a = lax.dot_general(xn_ref[...], wu_ref[...], (((1,), (1,)), ((), ())), preferred_element_type=F32)
    act = _gelu_tanh(a)
    hm = jnp.concatenate(
        [(g_ref[u].astype(F32) * act[:, u * PEER_KEYS:(u + 1) * PEER_KEYS]).astype(BF16) for u in range(ib)], axis=1)
    part = jnp.dot(hm, wv_ref[...], preferred_element_type=F32)

    @pl.when(j == 0)
    def _():
        o_ref[...] = part

    @pl.when(j > 0)
    def _():
        o_ref[...] += part


def peer_dense(xn, w_u, w_v, gates, *, tm, ib):
    m, d = xn.shape
    assert m % tm == 0 and PEER_KEYS % ib == 0
    eb = ib * PEER_KEYS
    return pl.pallas_call(
        _peer_dense_kernel,
        grid=(m // tm, PEER_KEYS // ib),
        in_specs=[
            pl.BlockSpec((tm, d), lambda i, j: (i, 0)),
            pl.BlockSpec((eb, d), lambda i, j: (j, 0)),
            pl.BlockSpec((eb, d), lambda i, j: (j, 0)),
            pl.BlockSpec((ib, tm, PEER_KEYS), lambda i, j: (j, i, 0)),
        ],
        out_specs=pl.BlockSpec((tm, d), lambda i, j: (i, 0)),
        out_shape=jax.ShapeDtypeStruct((m, d), F32),
        compiler_params=_cparams("arbitrary", "arbitrary"),
        name="peer_dense",
    )(xn, w_u, w_v, gates)


def _residual_norm_kernel(h_ref, f_ref, w_ref, o_ref):
    y = h_ref[...] + f_ref[...]
    o_ref[...] = y * lax.rsqrt(jnp.mean(y * y, axis=-1, keepdims=True) + EPS) * w_ref[...]


def residual_norm(h, f, w, *, tm):
    m, d = h.shape
    assert m % tm == 0
    rows = pl.BlockSpec((tm, d), lambda i: (i, 0))
    return pl.pallas_call(
        _residual_norm_kernel,
        grid=(m // tm,),
        in_specs=[rows, rows, pl.BlockSpec((1, d), lambda i: (0, 0))],
        out_specs=rows,
        out_shape=jax.ShapeDtypeStruct((m, d), F32),
        compiler_params=_cparams("arbitrary"),
        name="residual_norm",
    )(h, f, w)


_HI = lax.Precision.HIGHEST


def _dot_hi(a, b):
    return jnp.dot(a, b, preferred_element_type=F32, precision=_HI)


_BNN = (((2,), (1,)), ((0,), (0,)))
_BNT = (((2,), (2,)), ((0,), (0,)))
_BTN = (((1,), (1,)), ((0,), (0,)))


def _bdot(a, b, dims=_BNN):
    return lax.dot_general(a.astype(BF16), b.astype(BF16), dims, preferred_element_type=F32)


def _split2(a):
    hi = a.astype(BF16)
    return hi, (a - hi.astype(F32)).astype(BF16)


def _bdot3(a, b, dims=_BNN):
    (ah, al), (bh, bl) = a, b
    dot = lambda x, y: lax.dot_general(x, y, dims, preferred_element_type=F32)
    return dot(ah, bh) + (dot(ah, bl) + dot(al, bh))


def _sigmoid(x):
    return 1.0 / (1.0 + jnp.exp(-x))


def _softplus(x):
    return jnp.maximum(x, 0.0) + jnp.log1p(jnp.exp(-jnp.abs(x)))


def _gdn_kernel(xq_ref, xk_ref, xv_ref, z_ref, aux_ref, at_ref, cs_ref, cw_ref, alog_ref, dtb_ref,
                alogt_ref, dtbt_ref, nw_ref, s0_ref, y_ref, nconv_ref, nssm_ref, xp_ref, st_ref,
                *, chunk, sub, t_total, b_lane, a_lane):
    c = pl.program_id(1)
    nchunks = pl.num_programs(1)
    heads = st_ref.shape[0]
    width = heads * GDN_DK
    tail = CONV_W - 1

    @pl.when(c == 0)
    def _():
        xp_ref[8 - tail:8, :] = cs_ref[0]
        st_ref[...] = s0_ref[0]

    xp_ref[8:8 + chunk, 0:width] = xq_ref[...]
    xp_ref[8:8 + chunk, width:2 * width] = xk_ref[...]
    xp_ref[8:8 + chunk, 2 * width:3 * width] = xv_ref[...]
    conv = cw_ref[tail:tail + 1, :] * xp_ref[8:8 + chunk, :]
    for j in range(tail):
        conv = conv + cw_ref[j:j + 1, :] * xp_ref[8 - tail + j:8 - tail + j + chunk, :]
    conv = conv * _sigmoid(conv)

    padded = t_total % chunk != 0
    ridx = c * chunk + lax.broadcasted_iota(I32, (chunk, 1), 0)
    rvalid = ridx < t_total
    cidx = c * chunk + lax.broadcasted_iota(I32, (1, chunk), 1)
    cvalid = cidx < t_total

    aux = aux_ref[...]
    beta = _sigmoid(aux[:, b_lane:b_lane + heads])
    g = -jnp.exp(alog_ref[...]) * _softplus(aux[:, a_lane:a_lane + heads] + dtb_ref[...])
    gt = -jnp.exp(alogt_ref[...]) * _softplus(at_ref[0] + dtbt_ref[...])
    if padded:
        beta = jnp.where(rvalid, beta, 0.0)
        g = jnp.where(rvalid, g, 0.0)
        gt = jnp.where(cvalid, gt, 0.0)
    n_sub = chunk // sub
    rf = lax.broadcasted_iota(I32, (chunk, chunk), 0)
    cf = lax.broadcasted_iota(I32, (chunk, chunk), 1)
    same = (rf // sub) == (cf // sub)
    gc = _dot_hi(jnp.where(jnp.logical_and(same, rf >= cf), 1.0, 0.0), g)
    gct = _dot_hi(gt, jnp.where(jnp.logical_and(same, rf <= cf), 1.0, 0.0))
    ri = lax.broadcasted_iota(I32, (sub, sub), 0)
    ci = lax.broadcasted_iota(I32, (sub, sub), 1)
    causal = ri >= ci
    strict = ri > ci
    eye = jnp.where(ri == ci, 1.0, 0.0)

    per_head = lambda x2d, base: jnp.stack(
        [x2d[:, base + h * GDN_DK:base + (h + 1) * GDN_DK] for h in range(heads)], axis=0)
    q = per_head(conv, 0)
    k = per_head(conv, width)
    v = per_head(conv, 2 * width)
    q = q * lax.rsqrt(jnp.sum(q * q, axis=-1, keepdims=True) + EPS) * GDN_DK ** -0.5
    k = k * lax.rsqrt(jnp.sum(k * k, axis=-1, keepdims=True) + EPS)
    if padded:
        k = jnp.where(rvalid[None], k, 0.0)
        v = jnp.where(rvalid[None], v, 0.0)
    gcol = jnp.stack([gc[:, h:h + 1] for h in range(heads)], axis=0)
    grow = jnp.stack([gct[h:h + 1, :] for h in range(heads)], axis=0)
    bcol = jnp.stack([beta[:, h:h + 1] for h in range(heads)], axis=0)
    by_rows = lambda a: jnp.concatenate([a[:, j * sub:(j + 1) * sub] for j in range(n_sub)], axis=0)
    q, k, v, gcol, bcol = (by_rows(a) for a in (q, k, v, gcol, bcol))
    grow = jnp.concatenate([grow[:, :, j * sub:(j + 1) * sub] for j in range(n_sub)], axis=0)
    decay = jnp.where(causal[None], jnp.exp(jnp.where(causal[None], gcol - grow, 0.0)), 0.0)
    kb = k * bcol
    vb = v * bcol
    x = -jnp.where(strict[None], _bdot(kb, k, _BNT) * decay, 0.0)
    tinv = eye[None] + x
    xs = _split2(x)
    span = 2
    while span < sub:
        xs = _split2(_bdot3(xs, xs))
        tinv = tinv + _bdot3(_split2(tinv), xs)
        span *= 2
    u = _bdot(tinv, vb)
    w = _bdot(tinv, kb * jnp.exp(gcol))
    intra = jnp.where(causal[None], _bdot(q, k, _BNT) * decay, 0.0)
    qg = q * jnp.exp(gcol)
    glast = gcol[:, sub - 1:sub, :]
    kd = k * jnp.exp(glast - gcol)
    state = st_ref[...]
    outs = []
    for j in range(n_sub):
        sl = slice(j * heads, (j + 1) * heads)
        v_new = u[sl] - _bdot(w[sl], state)
        outs.append(_bdot(qg[sl], state) + _bdot(intra[sl], v_new))
        state = state * jnp.exp(glast[sl]) + _bdot(kd[sl], v_new, _BTN)
    st_ref[...] = state
    out = outs[0] if n_sub == 1 else jnp.concatenate(outs, axis=1)
    on = out * lax.rsqrt(jnp.mean(out * out, axis=-1, keepdims=True) + EPS) * nw_ref[...]
    for h in range(heads):
        sl = slice(h * GDN_DV, (h + 1) * GDN_DV)
        zh = z_ref[:, sl]
        y_ref[:, sl] = (on[h] * (zh * _sigmoid(zh))).astype(y_ref.dtype)

    last_valid = t_total - (t_total - 1) // chunk * chunk
    @pl.when(c < nchunks - 1)
    def _():
        xp_ref[8 - tail:8, :] = xp_ref[8 + chunk - tail:8 + chunk, :]

    @pl.when(c == nchunks - 1)
    def _():
        nconv_ref[0] = xp_ref[8 + last_valid - tail:8 + last_valid, :]
        nssm_ref[0] = st_ref[...]


def gdn_mixer(p, at, conv_state, conv_w, a_log, dt_bias, norm_w, ssm_state, *, n_seq, t_pad, t_total, chunk, sub):
    heads = ssm_state.shape[1]
    width = heads * GDN_DK
    assert t_pad % chunk == 0 and chunk % sub == 0 and sub % 8 == 0
    nch = t_pad // chunk
    rows = lambda wd, j: pl.BlockSpec((chunk, wd), lambda n, c, j=j: (n * nch + c, j))
    const = lambda a: pl.BlockSpec(a.shape, lambda n, c: (0,) * a.ndim)
    alog = a_log.reshape(1, heads)
    dtb = dt_bias.reshape(1, heads)
    nw = norm_w.reshape(1, GDN_DV)
    at = at.reshape(n_seq, heads, nch, chunk).transpose(0, 2, 1, 3).reshape(n_seq * nch, heads, chunk)
    kern = functools.partial(_gdn_kernel, chunk=chunk, sub=sub, t_total=t_total, b_lane=IDX_DIM + 8,
                             a_lane=IDX_DIM + 16)
    return pl.pallas_call(
        kern,
        grid=(n_seq, nch),
        in_specs=[
            rows(width, 2), rows(width, 3), rows(width, 4), rows(width, 5), rows(128, 48),
            pl.BlockSpec((1, heads, chunk), lambda n, c: (n * nch + c, 0, 0)),
            pl.BlockSpec((1, CONV_W - 1, 3 * width), lambda n, c: (n, 0, 0)),
            const(conv_w), const(alog), const(dtb), const(alog.T), const(dtb.T), const(nw),
            pl.BlockSpec((1, heads, GDN_DK, GDN_DV), lambda n, c: (n, 0, 0, 0)),
        ],
        out_specs=[
            pl.BlockSpec((chunk, width), lambda n, c: (n * nch + c, 0)),
            pl.BlockSpec((1, CONV_W - 1, 3 * width), lambda n, c: (n, 0, 0)),
            pl.BlockSpec((1, heads, GDN_DK, GDN_DV), lambda n, c: (n, 0, 0, 0)),
        ],
        out_shape=[
            jax.ShapeDtypeStruct((n_seq * t_pad, width), BF16),
            jax.ShapeDtypeStruct((n_seq, CONV_W - 1, 3 * width), F32),
            jax.ShapeDtypeStruct((n_seq, heads, GDN_DK, GDN_DV), F32),
        ],
        scratch_shapes=[pltpu.VMEM((8 + chunk, 3 * width), F32), pltpu.VMEM((heads, GDN_DK, GDN_DV), F32)],
        compiler_params=_cparams("arbitrary", "arbitrary"),
        name="gdn_mixer",
    )(p, p, p, p, p, at, conv_state, conv_w, alog, dtb, alog.T, dtb.T, nw, ssm_state)


def _lane_fold(x, op):
    out = x[:, :128]
    for j in range(1, x.shape[1] // 128):
        out = op(out, x[:, j * 128:(j + 1) * 128])
    return out


TIE_UNROLL = 4


def _select_threshold(load, nch, nch_max, ts, rows, topk, lo0, hi0, skip):
    kf = float(topk)
    rb = min(rows, 128)
    assert rows % rb == 0
    assert nch_max * (ts // 128) <= 256
    lane = lax.broadcasted_iota(I32, (rb, 128), 1)
    on_lanes = rb == 128
    if on_lanes:
        state_shape = (8, rows)
        ones = jnp.ones((8, 128), BF16)
        to_state = lambda blk: jnp.transpose(jnp.broadcast_to(blk, (128, 128)))[0:8]
        to_rows = lambda st, r0: jnp.transpose(jnp.broadcast_to(st[0:1, r0:r0 + 128], (128, 128)))
        row_sums = lambda acc: lax.dot_general(ones, acc.astype(BF16), (((1,), (1,)), ((), ())),
                                               preferred_element_type=F32)
        join_axis = 1
    else:
        state_shape = (rows, 128)
        ones = jnp.ones((128, 128), BF16)
        to_state = lambda blk: jnp.broadcast_to(blk, (rb, 128))
        to_rows = lambda st, r0: st[r0:r0 + rb]
        row_sums = lambda acc: jnp.dot(acc.astype(BF16), ones, preferred_element_type=F32)
        join_axis = 0
    join = lambda parts: parts[0] if len(parts) == 1 else jnp.concatenate(parts, axis=join_axis)
    rep = lambda col: join([to_state(col[r0:r0 + rb].astype(F32)) for r0 in range(0, rows, rb)])
    lo0, hi0, skip = rep(lo0), rep(hi0), rep(skip) > 0.5

    def count(pred, *cols):
        blocks = range(0, rows, rb)
        wides = [[to_rows(col, r0) for col in cols] for r0 in blocks]
        accs = []
        for r0, wide in zip(blocks, wides):
            def body(c, acc, r0=r0, wide=wide):
                x = load(c, r0, rb)
                for j in range(ts // 128):
                    kpos = lane + (c * ts + j * 128)
                    acc = acc + jnp.where(pred(x[:, j * 128:(j + 1) * 128], kpos, *wide), 1.0, 0.0)
                return acc

            accs.append(lax.fori_loop(0, nch, body, jnp.zeros((rb, 128), F32)))
        return join([row_sums(acc) for acc in accs])

    ge_ = lambda x, kpos, t: x >= t
    gt_ = lambda x, kpos, t: x > t
    eq_ = lambda x, kpos, t: x == t

    zero = jnp.zeros(state_shape, F32)
    span = (nch * ts).astype(F32) if hasattr(nch, "astype") else float(nch * ts)

    def search(state, limit):
        def cond(st):
            _, _, done, _, it = st
            return jnp.logical_and(jnp.min(done) < 0.5, it < limit)

        def body(st):
            lo, hi, done, hit_any, it = st
            mid = jnp.minimum(jnp.maximum(lo * 0.5 + hi * 0.5, lo), hi)
            cnt = count(ge_, mid)
            active = done < 0.5
            ge = jnp.logical_and(active, cnt >= kf)
            hit = jnp.logical_and(active, cnt == kf)
            collapsed = jnp.logical_or(mid <= lo, mid >= hi)
            new_lo = jnp.where(ge, mid, lo)
            new_hi = jnp.where(jnp.logical_and(active, jnp.logical_not(jnp.logical_or(ge, collapsed))), mid, hi)
            new_hi = jnp.where(hit, mid, new_hi)
            new_done = jnp.where(jnp.logical_or(hit, jnp.logical_and(active, collapsed)), 1.0, done)
            return new_lo, new_hi, new_done, jnp.where(hit, 1.0, hit_any), it + 1

        st = lax.fori_loop(0, UNTESTED_SEARCH_STEPS, lambda _, st: body(st), state + (jnp.int32(0),))
        lo, hi, done, hit_any, _ = lax.while_loop(cond, body, st)
        return lo, hi, done, hit_any

    c_pos = count(gt_, zero)
    c_nn = count(ge_, zero)
    at_zero = jnp.logical_and(jnp.logical_not(skip), jnp.logical_and(c_pos < kf, c_nn >= kf))
    lo0 = jnp.where(at_zero, 0.0, jnp.where(c_pos >= kf, jnp.maximum(lo0, 0.0), lo0))
    hi0 = jnp.where(at_zero, 0.0, jnp.where(c_nn < kf, jnp.minimum(hi0, 0.0), hi0))
    lo, hi, _, hit_any = search((lo0, hi0, jnp.where(jnp.logical_or(skip, at_zero), 1.0, 0.0), zero), MAX_SEARCH_STEPS)
    hit = hit_any > 0.5

    def count_collapsed(_):
        cnt_hi = count(ge_, hi)
        vc = jnp.where(cnt_hi >= kf, hi, lo)
        return vc, count(gt_, vc), count(eq_, vc)

    settled = jnp.logical_or(jnp.logical_or(skip, at_zero), hit)
    vc, cgt_c, ceq_c = lax.cond(jnp.min(jnp.where(settled, 1.0, 0.0)) > 0.5,
                                lambda _: (hi, zero, zero), count_collapsed, None)
    v = jnp.where(skip, NEG_INF, jnp.where(at_zero, 0.0, jnp.where(hit, hi, vc)))
    cgt = jnp.where(at_zero, c_pos, jnp.where(hit, kf, cgt_c))
    ceq = jnp.where(at_zero, c_nn - c_pos, jnp.where(hit, 0.0, ceq_c))
    need = kf - cgt
    partial = jnp.logical_and(jnp.logical_not(skip), need < ceq)
    rows_of = lambda st: jnp.concatenate([to_rows(st, r0) for r0 in range(0, rows, rb)], axis=0)
    cut_all = rows_of(jnp.where(skip, -1.0, zero + span))
    blocks = range(0, rows, rb)
    tile_ts = lambda a: jnp.concatenate([a] * (ts // 128), axis=1)

    def cut_by_prefix(_):
        nchp = -(-nch_max // 8) * 8
        crow = lax.broadcasted_iota(I32, (nchp, rb), 0)
        tables = []
        for r0 in blocks:
            vt = tile_ts(to_rows(v, r0))

            def per_chunk(i, tab, r0=r0, vt=vt):
                for u in range(TIE_UNROLL):
                    c = i * TIE_UNROLL + u
                    ties = jnp.where(load(jnp.minimum(c, nch - 1), r0, rb) == vt, 1.0, 0.0)
                    cnt = row_sums(_lane_fold(ties, jnp.add))
                    here = jnp.logical_and(crow == c, c < nch)
                    tab = jnp.where(here, jnp.concatenate([cnt] * (nchp // 8), axis=0), tab)
                return tab

            steps = (nch + TIE_UNROLL - 1) // TIE_UNROLL
            tables.append(lax.fori_loop(0, steps, per_chunk, jnp.zeros((nchp, rb), F32)))
        tab = join(tables)
        ci = lax.broadcasted_iota(I32, (nchp, nchp), 0)
        cj = lax.broadcasted_iota(I32, (nchp, nchp), 1)
        upto = _dot_hi(jnp.where(ci >= cj, 1.0, 0.0), tab)
        before = upto < jnp.broadcast_to(need[0:1], tab.shape)
        cstar = jnp.sum(jnp.where(before, 1.0, 0.0), axis=0, keepdims=True)
        left = need[0:1] - jnp.sum(jnp.where(before, tab, 0.0), axis=0, keepdims=True)
        cstar, left = jnp.broadcast_to(cstar, state_shape), jnp.broadcast_to(left, state_shape)
        ki = lax.broadcasted_iota(I32, (ts, ts), 0)
        kj = lax.broadcasted_iota(I32, (ts, ts), 1)
        upper = jnp.where(ki <= kj, 1.0, 0.0).astype(BF16)
        cuts = []
        for r0 in blocks:
            vt, ct = tile_ts(to_rows(v, r0)), tile_ts(to_rows(cstar, r0))

            def crossing(c, m, r0=r0, vt=vt, ct=ct):
                here = jnp.logical_and(load(c, r0, rb) == vt, ct == c.astype(F32))
                return m + jnp.where(here, 1.0, 0.0)

            mask = lax.fori_loop(0, nch, crossing, jnp.zeros((rb, ts), F32))
            seen = jnp.dot(mask.astype(BF16), upper, preferred_element_type=F32)
            inside = jnp.sum(jnp.where(seen < tile_ts(to_rows(left, r0)), 1.0, 0.0), axis=1, keepdims=True)
            cuts.append(to_rows(cstar, r0) * float(ts) + jnp.broadcast_to(inside, (rb, 128)))
        return jnp.where(rows_of(jnp.where(partial, 1.0, 0.0)) > 0.5, jnp.concatenate(cuts, axis=0), cut_all)

    def cut_by_bisection(_):
        def c_cond(st):
            lo_i, hi_i = st
            return jnp.max(jnp.where(partial, hi_i - lo_i, 0.0)) > 0.0

        def c_body(st):
            lo_i, hi_i = st
            mid = jnp.floor((lo_i + hi_i) * 0.5)
            cnt = count(lambda x, kpos, t, m: jnp.logical_and(x == t, kpos.astype(F32) <= m), v, mid)
            ok = cnt >= need
            return jnp.where(ok, lo_i, mid + 1.0), jnp.where(ok, mid, hi_i)

        _, cut = lax.while_loop(c_cond, c_body, (zero, zero + (span - 1.0)))
        return jnp.where(rows_of(jnp.where(partial, 1.0, 0.0)) > 0.5, rows_of(cut), cut_all)

    any_partial = jnp.max(jnp.where(partial, 1.0, 0.0)) > 0.5
    cut = lax.cond(any_partial, cut_by_prefix if on_lanes else cut_by_bisection, lambda _: cut_all, None)
    return rows_of(v), cut.astype(I32)


def _prompt_attn_kernel(q_ref, qi_ref, wi_ref, kit_ref, kt_ref, v_ref, kn2_ref, o_ref,
                        sc_ref, m_ref, l_ref, acc_ref, *, tq, ts, topk):
    i = pl.program_id(0)
    nch = ((i + 1) * tq + ts - 1) // ts
    row = i * tq + lax.broadcasted_iota(I32, (tq, 1), 0)
    heads = q_ref.shape[1] // HEAD_DIM
    wsc = (wi_ref[...] * IDX_HEADS ** -0.5) * IDX_DIM ** -0.5
    qi_h = [qi_ref[:, h * IDX_DIM:(h + 1) * IDX_DIM] for h in range(IDX_HEADS)]
    w_h = [jnp.broadcast_to(wsc[:, h:h + 1], (tq, ts)) for h in range(IDX_HEADS)]

    def scores(c, carry):
        mn, mx = carry
        kic = kit_ref[c]
        acc = jnp.zeros((tq, ts), F32)
        for h in range(IDX_HEADS):
            lg = jnp.dot(qi_h[h], kic, preferred_element_type=F32)
            acc = acc + jnp.maximum(lg, 0.0) * w_h[h]
        kpos = c * ts + lax.broadcasted_iota(I32, (tq, ts), 1)
        adm = kpos <= row
        sc_ref[c] = jnp.where(adm, acc, NEG_INF)
        mn = jnp.minimum(mn, _lane_fold(jnp.where(adm, acc, jnp.inf), jnp.minimum))
        mx = jnp.maximum(mx, _lane_fold(jnp.where(adm, acc, NEG_INF), jnp.maximum))
        return mn, mx

    mn, mx = lax.fori_loop(0, nch, scores,
                           (jnp.full((tq, 128), jnp.inf, F32), jnp.full((tq, 128), NEG_INF, F32)))
    lo0 = jnp.min(mn, axis=1, keepdims=True)
    hi0 = jnp.max(mx, axis=1, keepdims=True)
    skip = row < topk
    v, cut = _select_threshold(lambda c, r0, nr: sc_ref[c, r0:r0 + nr, :], nch, sc_ref.shape[0], ts, tq, topk,
                               lo0, hi0, skip)

    scale = HEAD_DIM ** -0.5
    group = heads // KV_HEADS
    vw = jnp.concatenate([v] * (ts // 128), axis=1)
    cutw = jnp.concatenate([cut] * (ts // 128), axis=1)
    head = lambda h: slice(h * HEAD_DIM, (h + 1) * HEAD_DIM)

    def selected(c):
        x = sc_ref[c]
        kpos = c * ts + lax.broadcasted_iota(I32, (tq, ts), 1)
        return jnp.logical_or(x > vw, jnp.logical_and(x == vw, kpos <= cutw))

    c2 = scale * math.log2(math.e)
    kmax = jnp.sqrt(jnp.max(kn2_ref[...], axis=1, keepdims=True))
    bound = []
    for h in range(heads):
        qh = q_ref[:, head(h)].astype(F32)
        qn = jnp.sqrt(jnp.sum(qh * qh, axis=1, keepdims=True))
        bound.append(jnp.broadcast_to(qn * (kmax[h // group:h // group + 1, :] * c2), (tq, ts)))
    ones = jnp.ones((ts, HEAD_DIM), BF16)
    acc_ref[...] = jnp.zeros(acc_ref.shape, F32)

    def attend_bounded(c, _):
        keep = jnp.where(selected(c), 1.0, 0.0).astype(BF16)
        kc = kt_ref[c]
        vc = v_ref[c]
        for g in range(KV_HEADS):
            v_ones = jnp.concatenate([vc[:, head(g)], ones], axis=1)
            for h in range(g * group, (g + 1) * group):
                s = jnp.dot(q_ref[:, head(h)], kc[head(g), :], preferred_element_type=F32)
                p = jnp.exp2(s * c2 - bound[h]).astype(BF16) * keep
                acc_ref[h] += jnp.dot(p, v_ones, preferred_element_type=F32)
        return 0

    lax.fori_loop(0, nch, attend_bounded, 0)
    lmin = jnp.full((tq, HEAD_DIM), jnp.inf, F32)
    for h in range(heads):
        acc = acc_ref[h]
        lsum = acc[:, HEAD_DIM:]
        lmin = jnp.minimum(lmin, lsum)
        o_ref[:, head(h)] = (acc[:, :HEAD_DIM] / lsum).astype(o_ref.dtype)

    @pl.when(jnp.logical_not(jnp.min(lmin) > UNDERFLOW_GUARD))
    def _():
        m_ref[...] = jnp.full(m_ref.shape, SOFTMAX_FLOOR, F32)
        l_ref[...] = jnp.zeros(l_ref.shape, F32)
        acc_ref[...] = jnp.zeros(acc_ref.shape, F32)

        def attend_online(c, _):
            bias = jnp.where(selected(c), 0.0, NEG_INF)
            kc = kt_ref[c]
            vc = v_ref[c]
            for h in range(heads):
                g = h // group
                s = jnp.dot(q_ref[:, head(h)], kc[head(g), :], preferred_element_type=F32) * scale + bias
                m_old = m_ref[h]
                m_new = jnp.maximum(m_old, jnp.max(s, axis=1, keepdims=True))
                alpha = jnp.exp(m_old - m_new)
                p = jnp.exp(s - m_new)
                l_ref[h] = l_ref[h] * alpha + jnp.sum(p, axis=1, keepdims=True)
                acc_ref[h, :, :HEAD_DIM] = acc_ref[h, :, :HEAD_DIM] * alpha + jnp.dot(
                    p.astype(BF16), vc[:, head(g)], preferred_element_type=F32)
                m_ref[h] = m_new
            return 0

        lax.fori_loop(0, nch, attend_online, 0)
        for h in range(heads):
            o_ref[:, head(h)] = (acc_ref[h, :, :HEAD_DIM] / l_ref[h]).astype(o_ref.dtype)


SAMPLE_TS = 4 * PAGE_SIZE
SAMPLE_RING = 8


def _sample_attn_kernel(pt_ref, q_ref, qi_ref, wi_ref, kin_ref, kn_ref, vn_ref, cki_ref, ck_ref, cv_ref,
                        o_ref, kibuf, kbuf, vbuf, sc_ref, sem_i, sem_k, sem_v, *, n_pages, t_new, topk):
    b = pl.program_id(0)
    ts = SAMPLE_TS
    ppc = ts // PAGE_SIZE
    nch = n_pages // ppc
    past = n_pages * PAGE_SIZE
    rows = q_ref.shape[2]

    def ki_copy(p):
        dst = kibuf.at[:, pl.ds(pl.multiple_of(p * PAGE_SIZE, PAGE_SIZE), PAGE_SIZE)]
        return pltpu.make_async_copy(cki_ref.at[0, pt_ref[b, p]], dst, sem_i.at[0])

    def kv_copies(c, slot):
        cps = []
        for j in range(ppc):
            page = pt_ref[b, c * ppc + j]
            dst = pl.ds(j * PAGE_SIZE, PAGE_SIZE)
            for g in range(KV_HEADS):
                cps.append(pltpu.make_async_copy(ck_ref.at[0, page, :, g, :], kbuf.at[slot, g, dst], sem_k.at[slot]))
                cps.append(pltpu.make_async_copy(cv_ref.at[0, page, :, g, :], vbuf.at[slot, g, dst], sem_v.at[slot]))
        return cps

    def start_ki(p, _):
        ki_copy(p).start()
        return 0

    def wait_ki(p, _):
        ki_copy(p).wait()
        return 0

    lax.fori_loop(0, n_pages, start_ki, 0)
    ring = kbuf.shape[0]
    lanes = next(n for n in (4, 2, 1) if nch % n == 0)
    ahead = ring - lanes
    assert ahead >= 1
    for c0 in range(min(ahead, nch)):
        for cp in kv_copies(c0, c0):
            cp.start()
    lax.fori_loop(0, n_pages, wait_ki, 0)

    wcol = (wi_ref[0] * IDX_HEADS ** -0.5) * IDX_DIM ** -0.5
    qi = qi_ref[0]

    def index_scores(kct):
        width = kct.shape[1]
        lg = jnp.dot(qi, kct, preferred_element_type=F32)
        weighted = jnp.maximum(lg, 0.0) * jnp.broadcast_to(wcol, (IDX_HEADS * rows, width))
        acc = weighted[0:rows]
        for h in range(1, IDX_HEADS):
            acc = acc + weighted[h * rows:(h + 1) * rows]
        return acc

    def scores(i, carry):
        mn, mx = carry
        span = lanes * ts
        acc = index_scores(kibuf[:, pl.ds(pl.multiple_of(i * span, span), span)].astype(BF16))
        sc_ref[i] = acc
        return jnp.minimum(mn, _lane_fold(acc, jnp.minimum)), jnp.maximum(mx, _lane_fold(acc, jnp.maximum))

    mn, mx = lax.fori_loop(0, nch // lanes, scores,
                           (jnp.full((rows, 128), jnp.inf, F32), jnp.full((rows, 128), NEG_INF, F32)))
    acc = index_scores(kin_ref[0])
    tok = lax.broadcasted_iota(I32, (rows, ts), 0) % t_new
    col = lax.broadcasted_iota(I32, (rows, ts), 1)
    adm = jnp.logical_and(col <= tok, col < t_new)
    span = lanes * ts
    ngrp = nch // lanes + 1
    sc_ref[ngrp - 1] = jnp.concatenate(
        [jnp.where(adm, acc, NEG_INF), jnp.full((rows, span - ts), NEG_INF, F32)], axis=1) if lanes > 1 else jnp.where(
            adm, acc, NEG_INF)
    mn = jnp.minimum(mn, _lane_fold(jnp.where(adm, acc, jnp.inf), jnp.minimum))
    mx = jnp.maximum(mx, _lane_fold(jnp.where(adm, acc, NEG_INF), jnp.maximum))
    lo0 = jnp.min(mn, axis=1, keepdims=True)
    hi0 = jnp.max(mx, axis=1, keepdims=True)
    skip = jnp.full((rows, 1), past + 1 <= topk)
    v, cut = _select_threshold(lambda c, r0, nr: sc_ref[c], ngrp, ngrp, span, rows, topk, lo0, hi0, skip)
    vw = jnp.concatenate([v] * (span // 128), axis=1)
    cutw = jnp.concatenate([cut] * (span // 128), axis=1)
    scale = HEAD_DIM ** -0.5

    def bias_of(c):
        x = sc_ref[c]
        kpos = c * span + lax.broadcasted_iota(I32, (rows, span), 1)
        sel = jnp.logical_or(x > vw, jnp.logical_and(x == vw, kpos <= cutw))
        return jnp.where(sel, 0.0, NEG_INF)

    def flash(state, parts):
        m_old, l_old, a_old = state
        s = jnp.concatenate(
            [lax.dot_general(q_ref[0, g], kg, (((1,), (1,)), ((), ())), preferred_element_type=F32) * scale + bias
             for g, bias, kg, _ in parts], axis=0)
        m_new = jnp.maximum(m_old, jnp.max(s, axis=1, keepdims=True))
        alpha = jnp.exp(m_old - m_new)
        p = jnp.exp(s - m_new)
        l_new = l_old * alpha + jnp.sum(p, axis=1, keepdims=True)
        pb = p.astype(BF16)
        pv = jnp.concatenate(
            [jnp.dot(pb[n * rows:(n + 1) * rows], vg, preferred_element_type=F32) for n, (_, _, _, vg) in enumerate(parts)],
            axis=0)
        return m_new, l_new, a_old * alpha + pv

    def receive(c):
        for cp in kv_copies(c, lax.rem(c, ring)):
            cp.wait()

        @pl.when(c + ahead < nch)
        def _():
            for cp in kv_copies(c + ahead, lax.rem(c + ahead, ring)):
                cp.start()

    def attend(i, state):
        for u in range(lanes):
            receive(i * lanes + u)
        parts = []
        bias_grp = bias_of(i)
        for u in range(lanes):
            c = i * lanes + u
            slot = lax.rem(c, ring)
            bias = bias_grp[:, u * ts:(u + 1) * ts]
            parts += [(g, bias, kbuf[slot, g].astype(BF16), vbuf[slot, g].astype(BF16)) for g in range(KV_HEADS)]
        return flash(state, parts)

    pieces = lanes * KV_HEADS
    init = (jnp.full((pieces * rows, 1), SOFTMAX_FLOOR, F32), jnp.zeros((pieces * rows, 1), F32),
            jnp.zeros((pieces * rows, HEAD_DIM), F32))
    m_all, l_all, a_all = lax.fori_loop(0, nch // lanes, attend, init)

    def merge(a, b):
        m = jnp.maximum(a[0], b[0])
        fa, fb = jnp.exp(a[0] - m), jnp.exp(b[0] - m)
        return m, a[1] * fa + b[1] * fb, a[2] * fa + b[2] * fb

    per_lane = KV_HEADS * rows
    state = tuple(x[0:per_lane] for x in (m_all, l_all, a_all))
    for u in range(1, lanes):
        state = merge(state, tuple(x[u * per_lane:(u + 1) * per_lane] for x in (m_all, l_all, a_all)))
    bias = bias_of(ngrp - 1)[:, :ts]
    new_parts = [(g, bias, kn_ref[0][:, g * HEAD_DIM:(g + 1) * HEAD_DIM], vn_ref[0][:, g * HEAD_DIM:(g + 1) * HEAD_DIM])
                 for g in range(KV_HEADS)]
    _, l_fin, a_fin = flash(state, new_parts)
    out = a_fin / l_fin
    for g in range(KV_HEADS):
        o_ref[0, g] = out[g * rows:(g + 1) * rows].astype(o_ref.dtype)


def sample_attention(q, qi, wi, ki_new, k_new, v_new, cache_kidx, cache_k, cache_v, page_table, *, topk):
    nb, t_new, width = q.shape
    heads = width // HEAD_DIM
    group = heads // KV_HEADS
    n_pages = page_table.shape[1]
    ts = SAMPLE_TS
    assert n_pages % (ts // PAGE_SIZE) == 0
    nch = n_pages // (ts // PAGE_SIZE)
    lanes = next(n for n in (4, 2, 1) if nch % n == 0)
    rows = group * t_new
    qg = q.reshape(nb, t_new, KV_HEADS, group, HEAD_DIM).transpose(0, 2, 3, 1, 4).reshape(nb, KV_HEADS, rows, HEAD_DIM)
    qir = jnp.tile(qi.reshape(nb, t_new, IDX_HEADS, IDX_DIM).transpose(0, 2, 1, 3), (1, 1, group, 1))
    qir = qir.reshape(nb, IDX_HEADS * rows, IDX_DIM)
    wir = jnp.tile(wi.transpose(0, 2, 1), (1, 1, group)).reshape(nb, IDX_HEADS * rows, 1)
    padk = lambda a: jnp.pad(a, ((0, 0), (0, ts - t_new), (0, 0)))
    per_b = lambda a: pl.BlockSpec((1,) + a.shape[1:], lambda b, pt: (b,) + (0,) * (a.ndim - 1))
    ops = (qg, qir, wir, padk(ki_new).transpose(0, 2, 1), padk(k_new), padk(v_new))
    cache_kidx_t = cache_kidx.transpose(0, 1, 3, 2)
    out = pl.pallas_call(
        functools.partial(_sample_attn_kernel, n_pages=n_pages, t_new=t_new, topk=topk),
        grid_spec=pltpu.PrefetchScalarGridSpec(
            num_scalar_prefetch=1,
            grid=(nb,),
            in_specs=[per_b(a) for a in ops] + [pl.BlockSpec(memory_space=pl.ANY)] * 3,
            out_specs=pl.BlockSpec((1, KV_HEADS, rows, HEAD_DIM), lambda b, pt: (b, 0, 0, 0)),
            scratch_shapes=[
                pltpu.VMEM((IDX_DIM, n_pages * PAGE_SIZE), F32),
                pltpu.VMEM((SAMPLE_RING, KV_HEADS, ts, HEAD_DIM), F32),
                pltpu.VMEM((SAMPLE_RING, KV_HEADS, ts, HEAD_DIM), F32),
                pltpu.VMEM((nch // lanes + 1, rows, lanes * ts), F32),
                pltpu.SemaphoreType.DMA((1,)),
                pltpu.SemaphoreType.DMA((SAMPLE_RING,)),
                pltpu.SemaphoreType.DMA((SAMPLE_RING,)),
            ]),
        out_shape=jax.ShapeDtypeStruct((nb, KV_HEADS, rows, HEAD_DIM), BF16),
        compiler_params=_cparams("arbitrary"),
        name="sample_attention",
    )(page_table, *ops, cache_kidx_t, cache_k, cache_v)
    return out.reshape(nb, KV_HEADS, group, t_new, HEAD_DIM).transpose(0, 3, 1, 2, 4).reshape(nb, t_new, width)


def prompt_attention(qb, qib, wi, kib, kb, vb, kn2, *, tq, ts, topk):
    s, width = qb.shape
    heads = width // HEAD_DIM
    assert s % tq == 0 and s % ts == 0
    nc = s // ts
    kit = kib.reshape(nc, ts, IDX_DIM).transpose(0, 2, 1)
    kt = kb.reshape(nc, ts, KV_HEADS * HEAD_DIM).transpose(0, 2, 1)
    v3 = vb.reshape(nc, ts, KV_HEADS * HEAD_DIM)
    whole = lambda a: pl.BlockSpec(a.shape, lambda i: (0,) * a.ndim, pipeline_mode=pl.Buffered(1))
    return pl.pallas_call(
        functools.partial(_prompt_attn_kernel, tq=tq, ts=ts, topk=topk),
        grid=(s // tq,),
        in_specs=[
            pl.BlockSpec((tq, width), lambda i: (i, 0)),
            pl.BlockSpec((tq, IDX_HEADS * IDX_DIM), lambda i: (i, 0)),
            pl.BlockSpec((tq, IDX_HEADS), lambda i: (i, 0)),
            whole(kit), whole(kt), whole(v3), whole(kn2),
        ],
        out_specs=pl.BlockSpec((tq, width), lambda i: (i, 0)),
        out_shape=jax.ShapeDtypeStruct((s, width), BF16),
        scratch_shapes=[
            pltpu.VMEM((nc, tq, ts), F32),
            pltpu.VMEM((heads, tq, 1), F32),
            pltpu.VMEM((heads, tq, 1), F32),
            pltpu.VMEM((heads, tq, 2 * HEAD_DIM), F32),
        ],
        compiler_params=_cparams("arbitrary"),
        name="prompt_attention",
    )(qb, qib, wi, kit, kt, v3, kn2)


def _row_tile(m, cap):
    t = cap
    while m % t:
        t //= 2
    return t


def _token_stages(x, attend, gdn, lw):
    (nmw, wp, w_out, nfw, wq, sub_keys, w_u, w_v, fw, tabs) = lw
    m = x.shape[0]
    tm = _row_tile(m, TOKEN_TILE)
    p = norm_matmul(x, nmw, wp, tm=tm, tn=wp.shape[1] // 2)
    qb, qib, k32, kb, v32, vb, ki32, kib, auxr, kn2 = rope_split(p, *tabs, tm=tm)
    wi, a_pre = auxr[:, 0:IDX_HEADS], auxr[:, 2 * IDX_HEADS:3 * IDX_HEADS]
    ya = attend(qb, qib, wi, kib, kb, vb, kn2)
    yg, new_conv, new_ssm = gdn(p, a_pre)
    h, xn, qp = out_proj(ya, yg, x, w_out, nfw, wq, tm=tm)
    i1, i2, gate = peer_topk(qp, sub_keys, tm=_row_tile(m, PEER_TOPK_TILE))
    tok = lambda a: a.reshape(PEER_HEADS * PEER_TOPK, m).T
    gates = peer_gates(tok(i1), tok(i2), tok(gate), tmb=_row_tile(m, GATE_TILE))
    f = peer_dense(xn, w_u, w_v, gates, tm=_row_tile(m, PEER_TOKEN_TILE), ib=PEER_KEY_BLOCK)
    y = residual_norm(h, f, fw, tm=tm)
    return y, k32, v32, ki32, new_conv, new_ssm


def kernel(x_prompt, x_sample, cache_k, cache_v, cache_kidx, page_table, state_conv, state_ssm, norm_mix_w, w_in,
           conv_w, a_log, dt_bias, gdn_norm_w, w_out, norm_ffn_w, peer_wq, peer_sub_keys, peer_u, peer_v,
           norm_final_w):
    depth = w_in.shape[0]
    b, s, d = x_prompt.shape
    nb, t, _ = x_sample.shape
    assert depth == 1 and b == 1, "single layer, single prompt sequence"
    past = page_table.shape[1] * PAGE_SIZE
    heads_g = state_ssm.shape[2]
    conv_ch = state_conv.shape[-1]

    wp = _pack_w_in(w_in[0])
    shared = (norm_mix_w[0][None], wp, w_out[0].astype(BF16), norm_ffn_w[0][None], peer_wq[0].astype(BF16),
              peer_sub_keys[0], peer_u[0].astype(BF16), peer_v[0].astype(BF16), norm_final_w[None])
    gdn_w = (conv_w[0], a_log[0], dt_bias[0], gdn_norm_w[0])

    def attend_p(qb, qib, wi, kib, kb, vb, kn2):
        return prompt_attention(qb, qib, wi, kib, kb, vb, kn2.T, tq=_row_tile(s, ATTN_Q_TILE), ts=_row_tile(s, ATTN_KEY_CHUNK),
                                topk=min(TOPK_MAX, s // 4))

    def gdn_p(p, a_pre):
        conv0 = jnp.zeros((1, CONV_W - 1, conv_ch), F32)
        ssm0 = jnp.zeros((1, heads_g, GDN_DK, GDN_DV), F32)
        block = 2 * GDN_CHUNK if s % (2 * GDN_CHUNK) == 0 else GDN_CHUNK
        assert s % block == 0
        return gdn_mixer(p, a_pre.T[None], conv0, *gdn_w, ssm0, n_seq=1, t_pad=s, t_total=s, chunk=block,
                         sub=GDN_CHUNK)

    tabs_p = _rope_tables(jnp.arange(s))
    y_p, k_p, v_p, ki_p, conv_p, ssm_p = _token_stages(x_prompt.reshape(s, d), attend_p, gdn_p, shared + (tabs_p,))

    m_s = nb * t
    t8 = -(-t // 8) * 8

    def attend_s(qb, qib, wi, kib, kb, vb, kn2):
        seq = lambda a: a.reshape(nb, t, a.shape[-1])
        y = sample_attention(seq(qb), seq(qib), seq(wi), seq(kib), seq(kb), seq(vb), cache_kidx, cache_k, cache_v,
                             page_table, topk=min(TOPK_MAX, (past + t) // 4))
        return y.reshape(m_s, y.shape[-1])

    def gdn_s(p, a_pre):
        pad_t = lambda a: jnp.pad(a.reshape(nb, t, a.shape[-1]), ((0, 0), (0, t8 - t), (0, 0)))
        p8 = pad_t(p).reshape(nb * t8, p.shape[-1])
        at = pad_t(a_pre).transpose(0, 2, 1)
        yg, nconv, nssm = gdn_mixer(p8, at, state_conv[0], *gdn_w, state_ssm[0], n_seq=nb, t_pad=t8, t_total=t,
                                    chunk=t8, sub=8)
        return yg.reshape(nb, t8, yg.shape[-1])[:, :t].reshape(m_s, yg.shape[-1]), nconv, nssm

    tabs_s = tuple(jnp.tile(a, (nb, 1)) for a in _rope_tables(past + jnp.arange(t)))
    y_s, k_s, v_s, ki_s, conv_s, ssm_s = _token_stages(x_sample.reshape(m_s, d), attend_s, gdn_s, shared + (tabs_s,))

    kv = lambda a, n, tt: a.reshape(1, n, tt, KV_HEADS, HEAD_DIM)
    return (y_p.reshape(b, s, d), y_s.reshape(nb, t, d),
            kv(k_p, b, s), kv(v_p, b, s), ki_p.reshape(1, b, s, IDX_DIM), conv_p[None], ssm_p[None],
            kv(k_s, nb, t), kv(v_s, nb, t), ki_s.reshape(1, nb, t, IDX_DIM), conv_s[None], ssm_s[None])
```
